```python
import math
import jax, jax.numpy as jnp
from jax import lax
import numpy as np

D_MODEL = 1024
BATCH = 8
SEQ = 8192
DEPTH = 4

CHUNK = 64
MIX_WIDTH = D_MODEL
CONV_WIDTH = MIX_WIDTH // 2
CONV_GROUPS = 8
CONV_K = 31
RET_HEADS = 4
RET_DIM = (MIX_WIDTH - CONV_WIDTH) // RET_HEADS
RET_WIDTH = RET_HEADS * RET_DIM
IN_WIDTH = 2 * CONV_WIDTH + 4 * RET_WIDTH
D_FF = int(math.ceil(8 * D_MODEL / 3 / 256) * 256)
ROPE_BASE = 10000.0
EPS = 1e-6

kernel_name = 'hybrid_conformer_retention_encoder'


def rms_norm(x, g):
    xf = x.astype(jnp.float32)
    y = xf * lax.rsqrt(jnp.mean(xf * xf, axis=-1, keepdims=True) + EPS)
    return y.astype(x.dtype) * g


def layer_norm(x, g, b):
    xf = x.astype(jnp.float32)
    mu = jnp.mean(xf, axis=-1, keepdims=True)
    var = jnp.mean(jnp.square(xf - mu), axis=-1, keepdims=True)
    return ((xf - mu) * lax.rsqrt(var + EPS)).astype(x.dtype) * g + b


def causal_depthwise_conv(u, w, b):
    C = u.shape[-1]
    up = jnp.pad(u, ((0, 0), (CONV_K - 1, 0), (0, 0)))
    y = lax.conv_general_dilated(up, w[:, None, :].astype(u.dtype), window_strides=(1,),
                                 padding='VALID', dimension_numbers=('NWC', 'WIO', 'NWC'),
                                 feature_group_count=C)
    return y + b


def rotary(t, pos):
    half = t.shape[-1] // 2
    freqs = ROPE_BASE ** (-jnp.arange(half, dtype=jnp.float32) / half)
    ang = pos[:, None] * freqs[None, :]
    cos = jnp.cos(ang)[None, :, None, :]
    sin = jnp.sin(ang)[None, :, None, :]
    t1, t2 = t[..., :half], t[..., half:]
    return jnp.concatenate([t1 * cos - t2 * sin, t1 * sin + t2 * cos], axis=-1)


def chunk_retention(q, k, v):
    Bsz, S, H, Dk = q.shape
    Dv = v.shape[-1]
    NC = S // CHUNK

    def blk(t):
        return t.reshape(Bsz, NC, CHUNK, H, t.shape[-1]).transpose(0, 3, 1, 2, 4)

    q, k, v = blk(q), blk(k), blk(v)
    log_g = jnp.log(1.0 - 2.0 ** (-5.0 - jnp.arange(H, dtype=jnp.float32)))
    idx = jnp.arange(CHUNK, dtype=jnp.float32)
    intra_decay = jnp.exp(log_g[:, None, None] * jnp.abs(idx[:, None] - idx[None, :]))
    scores = jnp.einsum('bhnid,bhnjd->bhnij', q, k) * intra_decay[None, :, None]
    intra = jnp.einsum('bhnij,bhnje->bhnie', scores, v)

    k_dec = k * jnp.exp(log_g[:, None] * (CHUNK - 1 - idx))[None, :, None, :, None]
    kv = jnp.einsum('bhnjd,bhnje->nbhde', k_dec, v)
    chunk_decay = jnp.exp(log_g * CHUNK)[None, :, None, None]

    def step(state, kv_n):
        return chunk_decay * state + kv_n, state

    _, prev = lax.scan(step, jnp.zeros((Bsz, H, Dk, Dv), jnp.float32), kv)
    q_dec = q * jnp.exp(log_g[:, None] * (idx + 1.0))[None, :, None, :, None]
    cross = jnp.einsum('bhnid,nbhde->bhnie', q_dec, prev)
    return (intra + cross).transpose(0, 2, 3, 1, 4).reshape(Bsz, S, H, Dv)


def _fwd_setup_inputs(seed: int = 0) -> dict:
    key = jax.random.key(seed)
    ks = jax.random.split(key, 16)
    f32 = jnp.float32
    n = lambda k, shape, s: jax.random.normal(k, shape, f32) * s
    return {
        'x': n(ks[0], (BATCH, SEQ, D_MODEL), 1.0),
        'norm1_g': 1.0 + n(ks[1], (DEPTH, D_MODEL), 0.02),
        'w_in': n(ks[2], (DEPTH, D_MODEL, IN_WIDTH), D_MODEL ** -0.5),
        'conv_w': n(ks[3], (DEPTH, CONV_K, CONV_WIDTH), CONV_K ** -0.5),
        'conv_b': n(ks[4], (DEPTH, CONV_WIDTH), 0.01),
        'conv_ln_g': 1.0 + n(ks[5], (DEPTH, CONV_WIDTH), 0.02),
        'conv_ln_b': n(ks[6], (DEPTH, CONV_WIDTH), 0.01),
        'ret_gn_g': 1.0 + n(ks[7], (DEPTH, RET_WIDTH), 0.02),
        'w_out': n(ks[8], (DEPTH, MIX_WIDTH, D_MODEL), (MIX_WIDTH * 2 * DEPTH) ** -0.5),
        'norm2_g': 1.0 + n(ks[9], (DEPTH, D_MODEL), 0.02),
        'w_gate': n(ks[10], (DEPTH, D_MODEL, D_FF), D_MODEL ** -0.5),
        'w_up': n(ks[11], (DEPTH, D_MODEL, D_FF), D_MODEL ** -0.5),
        'w_down': n(ks[12], (DEPTH, D_FF, D_MODEL), (D_FF * 2 * DEPTH) ** -0.5),
        'final_g': 1.0 + n(ks[13], (D_MODEL,), 0.02),
    }


def _fwd_reference(x, norm1_g, w_in, conv_w, conv_b, conv_ln_g, conv_ln_b, ret_gn_g, w_out,
              norm2_g, w_gate, w_up, w_down, final_g):
    Bsz, S, _ = x.shape
    pos = jnp.arange(S, dtype=jnp.float32)
    cw, rw = CONV_WIDTH, RET_WIDTH
    for l in range(DEPTH):
        h = rms_norm(x, norm1_g[l])
        proj = h @ w_in[l]
        a = proj[..., :cw]
        b = proj[..., cw:2 * cw]
        q = proj[..., 2 * cw:2 * cw + rw]
        k = proj[..., 2 * cw + rw:2 * cw + 2 * rw]
        v = proj[..., 2 * cw + 2 * rw:2 * cw + 3 * rw]
        g = proj[..., 2 * cw + 3 * rw:]

        u = a * jax.nn.sigmoid(b)
        u = causal_depthwise_conv(u, conv_w[l], conv_b[l])
        u = jax.nn.silu(layer_norm(u, conv_ln_g[l], conv_ln_b[l]))

        qh = rotary(q.reshape(Bsz, S, RET_HEADS, RET_DIM).astype(jnp.float32), pos)
        kh = rotary(k.reshape(Bsz, S, RET_HEADS, RET_DIM).astype(jnp.float32), pos) * (RET_DIM ** -0.5)
        vh = v.reshape(Bsz, S, RET_HEADS, RET_DIM).astype(jnp.float32)
        r = chunk_retention(qh, kh, vh)
        mu = jnp.mean(r, axis=-1, keepdims=True)
        var = jnp.mean(jnp.square(r - mu), axis=-1, keepdims=True)
        r = ((r - mu) * lax.rsqrt(var + EPS)).reshape(Bsz, S, rw).astype(x.dtype)
        r = r * ret_gn_g[l] * jax.nn.silu(g)

        mixed = jnp.concatenate([u, r], axis=-1)
        x = x + mixed @ w_out[l]

        h2 = rms_norm(x, norm2_g[l])
        x = x + (jax.nn.silu(h2 @ w_gate[l]) * (h2 @ w_up[l])) @ w_down[l]
    return rms_norm(x, final_g)


import jax as _jax
import jax.numpy as _jnp

TWIN_FORMAT = 'train_step'
FWD_PARAMS = ['x', 'norm1_g', 'w_in', 'conv_w', 'conv_b', 'conv_ln_g', 'conv_ln_b', 'ret_gn_g', 'w_out', 'norm2_g', 'w_gate', 'w_up', 'w_down', 'final_g']
TWIN_WEIGHTS = ['norm1_g', 'w_in', 'conv_w', 'conv_b', 'conv_ln_g', 'conv_ln_b', 'ret_gn_g', 'w_out', 'norm2_g', 'w_gate', 'w_up', 'w_down', 'final_g']
TWIN_DIFF_INPUT = 'x'
TWIN_INPUTS = ['x', 'norm1_g', 'w_in', 'conv_w', 'conv_b', 'conv_ln_g', 'conv_ln_b', 'ret_gn_g', 'w_out', 'norm2_g', 'w_gate', 'w_up', 'w_down', 'final_g', 'loss_target', 'm_norm1_g', 'm_w_in', 'm_conv_w', 'm_conv_b', 'm_conv_ln_g', 'm_conv_ln_b', 'm_ret_gn_g', 'm_w_out', 'm_norm2_g', 'm_w_gate', 'm_w_up', 'm_w_down', 'm_final_g', 'v_norm1_g', 'v_w_in', 'v_conv_w', 'v_conv_b', 'v_conv_ln_g', 'v_conv_ln_b', 'v_ret_gn_g', 'v_w_out', 'v_norm2_g', 'v_w_gate', 'v_w_up', 'v_w_down', 'v_final_g']
TWIN_OUTPUTS = ['loss', 'grad_x', 'grad_norm1_g', 'grad_w_in', 'grad_conv_w', 'grad_conv_b', 'grad_conv_ln_g', 'grad_conv_ln_b', 'grad_ret_gn_g', 'grad_w_out', 'grad_norm2_g', 'grad_w_gate', 'grad_w_up', 'grad_w_down', 'grad_final_g', 'delta_norm1_g', 'delta_w_in', 'delta_conv_w', 'delta_conv_b', 'delta_conv_ln_g', 'delta_conv_ln_b', 'delta_ret_gn_g', 'delta_w_out', 'delta_norm2_g', 'delta_w_gate', 'delta_w_up', 'delta_w_down', 'delta_final_g', 'new_m_norm1_g', 'new_m_w_in', 'new_m_conv_w', 'new_m_conv_b', 'new_m_conv_ln_g', 'new_m_conv_ln_b', 'new_m_ret_gn_g', 'new_m_w_out', 'new_m_norm2_g', 'new_m_w_gate', 'new_m_w_up', 'new_m_w_down', 'new_m_final_g', 'new_v_norm1_g', 'new_v_w_in', 'new_v_conv_w', 'new_v_conv_b', 'new_v_conv_ln_g', 'new_v_conv_ln_b', 'new_v_ret_gn_g', 'new_v_w_out', 'new_v_norm2_g', 'new_v_w_gate', 'new_v_w_up', 'new_v_w_down', 'new_v_final_g']
TWIN_LEAF_KINDS = {'loss': 'loss', 'grad_x': 'grad_x', 'grad_norm1_g': 'grad_w', 'grad_w_in': 'grad_w', 'grad_conv_w': 'grad_w', 'grad_conv_b': 'grad_w', 'grad_conv_ln_g': 'grad_w', 'grad_conv_ln_b': 'grad_w', 'grad_ret_gn_g': 'grad_w', 'grad_w_out': 'grad_w', 'grad_norm2_g': 'grad_w', 'grad_w_gate': 'grad_w', 'grad_w_up': 'grad_w', 'grad_w_down': 'grad_w', 'grad_final_g': 'grad_w', 'delta_norm1_g': 'delta_w', 'delta_w_in': 'delta_w', 'delta_conv_w': 'delta_w', 'delta_conv_b': 'delta_w', 'delta_conv_ln_g': 'delta_w', 'delta_conv_ln_b': 'delta_w', 'delta_ret_gn_g': 'delta_w', 'delta_w_out': 'delta_w', 'delta_norm2_g': 'delta_w', 'delta_w_gate': 'delta_w', 'delta_w_up': 'delta_w', 'delta_w_down': 'delta_w', 'delta_final_g': 'delta_w', 'new_m_norm1_g': 'new_m', 'new_m_w_in': 'new_m', 'new_m_conv_w': 'new_m', 'new_m_conv_b': 'new_m', 'new_m_conv_ln_g': 'new_m', 'new_m_conv_ln_b': 'new_m', 'new_m_ret_gn_g': 'new_m', 'new_m_w_out': 'new_m', 'new_m_norm2_g': 'new_m', 'new_m_w_gate': 'new_m', 'new_m_w_up': 'new_m', 'new_m_w_down': 'new_m', 'new_m_final_g': 'new_m', 'new_v_norm1_g': 'new_v', 'new_v_w_in': 'new_v', 'new_v_conv_w': 'new_v', 'new_v_conv_b': 'new_v', 'new_v_conv_ln_g': 'new_v', 'new_v_conv_ln_b': 'new_v', 'new_v_ret_gn_g': 'new_v', 'new_v_w_out': 'new_v', 'new_v_norm2_g': 'new_v', 'new_v_w_gate': 'new_v', 'new_v_w_up': 'new_v', 'new_v_w_down': 'new_v', 'new_v_final_g': 'new_v'}


def _forward(args):
    return _fwd_reference(*[args[k] for k in FWD_PARAMS])


def _output_shape():
    def fwd():
        inp = _fwd_setup_inputs(0)
        return _fwd_reference(*[inp[k] for k in FWD_PARAMS])
    out = _jax.eval_shape(fwd)
    return out.shape, out.dtype

N_MICROBATCH = 1
ADAM_LR = 0.001
ADAM_B1 = 0.9
ADAM_B2 = 0.999
ADAM_EPS = 1e-08
ADAM_WD = 0.01
ADAM_STEP = 10
PER_EXAMPLE_BATCH_AXIS = {'x': 0, 'loss_target': 0}
SHARED_INPUTS = []
_WEIGHT_DTYPES = {'norm1_g': _jnp.float32, 'w_in': _jnp.float32, 'conv_w': _jnp.float32, 'conv_b': _jnp.float32, 'conv_ln_g': _jnp.float32, 'conv_ln_b': _jnp.float32, 'ret_gn_g': _jnp.float32, 'w_out': _jnp.float32, 'norm2_g': _jnp.float32, 'w_gate': _jnp.float32, 'w_up': _jnp.float32, 'w_down': _jnp.float32, 'final_g': _jnp.float32}
MOMENT_SCALE = {'norm1_g': 8.954249e-02, 'w_in': 5.089248e-02, 'conv_w': 5.609157e-02, 'conv_b': 1.176944e-01, 'conv_ln_g': 6.614617e-02, 'conv_ln_b': 6.263035e-02, 'ret_gn_g': 5.418226e-02, 'w_out': 1.536790e-01, 'norm2_g': 7.669208e-02, 'w_gate': 3.224475e-02, 'w_up': 3.122976e-02, 'w_down': 1.465245e-01, 'final_g': 6.402398e+01}


def _to_microbatches(a, axis):
    t = _jnp.moveaxis(a, axis, 0)
    t = t.reshape((N_MICROBATCH, t.shape[0] // N_MICROBATCH) + t.shape[1:])
    return _jnp.moveaxis(t, 1, axis + 1)


def setup_inputs(seed: int = 0) -> dict:
    inp = _fwd_setup_inputs(seed)
    key = _jax.random.fold_in(_jax.random.key(seed), 7919)
    shape, _ = _output_shape()
    out = dict(inp)
    out["loss_target"] = _jax.random.normal(_jax.random.fold_in(key, 0), shape, _jnp.float32)
    for i, name in enumerate(TWIN_WEIGHTS):
        w = inp[name].astype(_jnp.float32)
        if MOMENT_SCALE is None:
            s = _jnp.sqrt(_jnp.mean(_jnp.square(w)) + 1e-30)
        else:
            s = MOMENT_SCALE[name]
        km, kv = _jax.random.split(_jax.random.fold_in(key, i + 1))
        out[name] = w
        out["m_" + name] = s * _jax.random.normal(km, w.shape, _jnp.float32)
        out["v_" + name] = (s * s) * _jax.random.uniform(kv, w.shape, _jnp.float32, 0.5, 1.5)
    if N_MICROBATCH > 1:
        for name, axis in PER_EXAMPLE_BATCH_AXIS.items():
            out[name] = _to_microbatches(out[name], axis)
    return {'x': out['x'], 'norm1_g': out['norm1_g'], 'w_in': out['w_in'], 'conv_w': out['conv_w'], 'conv_b': out['conv_b'], 'conv_ln_g': out['conv_ln_g'], 'conv_ln_b': out['conv_ln_b'], 'ret_gn_g': out['ret_gn_g'], 'w_out': out['w_out'], 'norm2_g': out['norm2_g'], 'w_gate': out['w_gate'], 'w_up': out['w_up'], 'w_down': out['w_down'], 'final_g': out['final_g'], 'loss_target': out['loss_target'], 'm_norm1_g': out['m_norm1_g'], 'm_w_in': out['m_w_in'], 'm_conv_w': out['m_conv_w'], 'm_conv_b': out['m_conv_b'], 'm_conv_ln_g': out['m_conv_ln_g'], 'm_conv_ln_b': out['m_conv_ln_b'], 'm_ret_gn_g': out['m_ret_gn_g'], 'm_w_out': out['m_w_out'], 'm_norm2_g': out['m_norm2_g'], 'm_w_gate': out['m_w_gate'], 'm_w_up': out['m_w_up'], 'm_w_down': out['m_w_down'], 'm_final_g': out['m_final_g'], 'v_norm1_g': out['v_norm1_g'], 'v_w_in': out['v_w_in'], 'v_conv_w': out['v_conv_w'], 'v_conv_b': out['v_conv_b'], 'v_conv_ln_g': out['v_conv_ln_g'], 'v_conv_ln_b': out['v_conv_ln_b'], 'v_ret_gn_g': out['v_ret_gn_g'], 'v_w_out': out['v_w_out'], 'v_norm2_g': out['v_norm2_g'], 'v_w_gate': out['v_w_gate'], 'v_w_up': out['v_w_up'], 'v_w_down': out['v_w_down'], 'v_final_g': out['v_final_g']}


def _loss(weights, diff, rest, loss_target):
    with _jax.named_scope("forward"):
        args = {**rest, TWIN_DIFF_INPUT: diff, **{k: w.astype(_WEIGHT_DTYPES[k]) for k, w in weights.items()}}
        y = _forward(args)
    with _jax.named_scope("loss_head"):
        err = _jnp.square(y.astype(_jnp.float32) - loss_target)
        return 0.5 * _jnp.sum(_jnp.mean(err, axis=-1)) if err.ndim else 0.5 * err


def _adamw(w, g, m, v):
    m = ADAM_B1 * m + (1.0 - ADAM_B1) * g
    v = ADAM_B2 * v + (1.0 - ADAM_B2) * _jnp.square(g)
    m_hat = m / (1.0 - ADAM_B1 ** ADAM_STEP)
    v_hat = v / (1.0 - ADAM_B2 ** ADAM_STEP)
    delta = -ADAM_LR * (m_hat / (_jnp.sqrt(v_hat) + ADAM_EPS) + ADAM_WD * w)
    return delta, m, v


def reference(x, norm1_g, w_in, conv_w, conv_b, conv_ln_g, conv_ln_b, ret_gn_g, w_out, norm2_g, w_gate, w_up, w_down, final_g, loss_target, m_norm1_g, m_w_in, m_conv_w, m_conv_b, m_conv_ln_g, m_conv_ln_b, m_ret_gn_g, m_w_out, m_norm2_g, m_w_gate, m_w_up, m_w_down, m_final_g, v_norm1_g, v_w_in, v_conv_w, v_conv_b, v_conv_ln_g, v_conv_ln_b, v_ret_gn_g, v_w_out, v_norm2_g, v_w_gate, v_w_up, v_w_down, v_final_g):
    given = dict(x=x, norm1_g=norm1_g, w_in=w_in, conv_w=conv_w, conv_b=conv_b, conv_ln_g=conv_ln_g, conv_ln_b=conv_ln_b, ret_gn_g=ret_gn_g, w_out=w_out, norm2_g=norm2_g, w_gate=w_gate, w_up=w_up, w_down=w_down, final_g=final_g, loss_target=loss_target, m_norm1_g=m_norm1_g, m_w_in=m_w_in, m_conv_w=m_conv_w, m_conv_b=m_conv_b, m_conv_ln_g=m_conv_ln_g, m_conv_ln_b=m_conv_ln_b, m_ret_gn_g=m_ret_gn_g, m_w_out=m_w_out, m_norm2_g=m_norm2_g, m_w_gate=m_w_gate, m_w_up=m_w_up, m_w_down=m_w_down, m_final_g=m_final_g, v_norm1_g=v_norm1_g, v_w_in=v_w_in, v_conv_w=v_conv_w, v_conv_b=v_conv_b, v_conv_ln_g=v_conv_ln_g, v_conv_ln_b=v_conv_ln_b, v_ret_gn_g=v_ret_gn_g, v_w_out=v_w_out, v_norm2_g=v_norm2_g, v_w_gate=v_w_gate, v_w_up=v_w_up, v_w_down=v_w_down, v_final_g=v_final_g)
    weights = {n: given[n] for n in TWIN_WEIGHTS}
    shared = {n: given[n] for n in SHARED_INPUTS}
    per_example = {n: given[n] for n in ['x']}
    grad_fn = _jax.value_and_grad(_loss, argnums=(0, 1))

    def one_microbatch(ex, loss_target):
        ex = dict(ex)
        diff = ex.pop(TWIN_DIFF_INPUT)
        return grad_fn(weights, diff, {**shared, **ex}, loss_target)

    if N_MICROBATCH == 1:
        loss, (grad_w, grad_x) = one_microbatch(per_example, given["loss_target"])
    else:
        def body(carry, xs):
            loss_sum, grad_sum = carry
            l_k, (gw_k, gx_k) = one_microbatch(xs[0], xs[1])
            with _jax.named_scope("update"):
                return (loss_sum + l_k, _jax.tree.map(_jnp.add, grad_sum, gw_k)), gx_k

        init = (_jnp.zeros((), _jnp.float32), _jax.tree.map(_jnp.zeros_like, weights))
        (loss, grad_w), grad_x = _jax.lax.scan(body, init, (per_example, given["loss_target"]))
    with _jax.named_scope("update"):
        delta_w, new_m, new_v = {}, {}, {}
        for n in TWIN_WEIGHTS:
            delta_w[n], new_m[n], new_v[n] = _adamw(weights[n], grad_w[n], given["m_" + n], given["v_" + n])
    return (loss, grad_x, *[grad_w[n] for n in TWIN_WEIGHTS], *[delta_w[n] for n in TWIN_WEIGHTS],
            *[new_m[n] for n in TWIN_WEIGHTS], *[new_v[n] for n in TWIN_WEIGHTS])
```

```python
import functools
import math

import jax
import jax.numpy as jnp
from jax import lax
from jax.experimental import pallas as pl
from jax.experimental.pallas import tpu as pltpu

F32 = jnp.float32
CDT = jnp.bfloat16
EPS = 1e-6
CHUNK = 64
SC = 256
HEADS = 4
HD = 128
CONV_K = 31
HALO = 32
ROPE_BASE = 10000.0
ADAM_LR = 0.001
ADAM_B1 = 0.9
ADAM_B2 = 0.999
ADAM_EPS = 1e-08
ADAM_WD = 0.01
ADAM_STEP = 10
N_DEV = 8
MESH = pl.DeviceIdType.MESH
VMEM_LIMIT = 60 * 1024 * 1024


def _call(body, **kw):
    return pl.pallas_call(body, **kw)


def _cp(sem=None, vmem=VMEM_LIMIT):
    return pltpu.CompilerParams(dimension_semantics=sem, vmem_limit_bytes=vmem)


def _resident(shape):
    nd = len(shape)
    return pl.BlockSpec(shape, lambda *_: (0,) * nd, pipeline_mode=pl.Buffered(1))


def _dot(a, b):
    return jnp.dot(a, b, preferred_element_type=F32)


def _dot_nt(a, b):
    return lax.dot_general(a, b, (((1,), (1,)), ((), ())), preferred_element_type=F32)


def _dot_tn(a, b):
    return lax.dot_general(a, b, (((0,), (0,)), ((), ())), preferred_element_type=F32)


def _sigmoid(x):
    return 1.0 / (1.0 + jnp.exp(-x))


def _rms_bwd(x, r, g, dy):
    xh = x * r
    dyg = dy * g
    dx = r * (dyg - xh * jnp.mean(dyg * xh, axis=-1, keepdims=True))
    return dx, dy * xh


def _f_in(x, g, w, tm):
    S, D = x.shape
    N = w.shape[1]

    def body(x_ref, g_ref, w_ref, h_ref, p_ref):
        xv = x_ref[...]
        r = lax.rsqrt(jnp.mean(xv * xv, axis=-1, keepdims=True) + EPS)
        h = ((xv * r) * g_ref[...]).astype(CDT)
        h_ref[...] = h
        p_ref[...] = _dot(h, w_ref[...])

    return _call(
        body, name="f_in", grid=(S // tm,),
        in_specs=[pl.BlockSpec((tm, D), lambda i: (i, 0)), _resident((1, D)), _resident((D, N))],
        out_specs=[pl.BlockSpec((tm, D), lambda i: (i, 0)), pl.BlockSpec((tm, N), lambda i: (i, 0))],
        out_shape=[jax.ShapeDtypeStruct((S, D), CDT), jax.ShapeDtypeStruct((S, N), F32)],
        compiler_params=_cp(("parallel",)),
    )(x, g, w)


def _f_conv(proj, cw, cb, lg, lb, tm):
    S = proj.shape[0]
    CW = cw.shape[1]
    RB = 64
    hb = tm // HALO

    def body(ab_ref, halo_ref, cw_ref, cb_ref, lg_ref, lb_ref, c_ref, u_ref, ext):
        i = pl.program_id(0)
        hv = halo_ref[...]
        u0h = hv[:, :CW] * _sigmoid(hv[:, CW:])
        ext[0:HALO, :] = jnp.where(i > 0, u0h, 0.0)
        av = ab_ref[...]
        ext[HALO:, :] = av[:, :CW] * _sigmoid(av[:, CW:])
        for rb in range(tm // RB):
            acc = jnp.zeros((RB, CW), F32) + cb_ref[...]
            for k in range(CONV_K):
                o = rb * RB + HALO - (CONV_K - 1) + k
                acc = acc + cw_ref[k:k + 1, :] * ext[o:o + RB, :]
            c_ref[rb * RB:(rb + 1) * RB, :] = acc
            mu = jnp.mean(acc, axis=-1, keepdims=True)
            var = jnp.mean(jnp.square(acc - mu), axis=-1, keepdims=True)
            z = ((acc - mu) * lax.rsqrt(var + EPS)) * lg_ref[...] + lb_ref[...]
            u_ref[rb * RB:(rb + 1) * RB, :] = (z * _sigmoid(z)).astype(CDT)

    return _call(
        body, name="f_conv", grid=(S // tm,),
        in_specs=[pl.BlockSpec((tm, 2 * CW), lambda i: (i, 0)),
                  pl.BlockSpec((HALO, 2 * CW), lambda i: (jnp.maximum(i * hb - 1, 0), 0)),
                  _resident((HALO, CW)), _resident((1, CW)), _resident((1, CW)), _resident((1, CW))],
        out_specs=[pl.BlockSpec((tm, CW), lambda i: (i, 0)), pl.BlockSpec((tm, CW), lambda i: (i, 0))],
        out_shape=[jax.ShapeDtypeStruct((S, CW), F32), jax.ShapeDtypeStruct((S, CW), CDT)],
        scratch_shapes=[pltpu.VMEM((tm + HALO, CW), F32)],
        compiler_params=_cp(("parallel",)),
    )(proj, proj, cw, cb, lg, lb)


def _rot(t, c, s):
    return t * c + pltpu.roll(t, HD // 2, 1) * s


def _f_ret(proj, u, cosT, sinT, Mt, qd, kd, gs, gn):
    S = proj.shape[0]
    RW = HEADS * HD
    NB = S // SC
    scale = HD ** -0.5

    def body(q_ref, k_ref, v_ref, g_ref, u_ref, c_ref, s_ref, m_ref, qd_ref, kd_ref, gs_ref, gn_ref,
             rraw_ref, st_ref, mix_ref, state):
        @pl.when(pl.program_id(0) == 0)
        def _():
            state[...] = jnp.zeros_like(state)

        mix_ref[:, 0:RW] = u_ref[...]
        cv = c_ref[...]
        sv = s_ref[...]
        for h in range(HEADS):
            cs = slice(h * HD, (h + 1) * HD)
            q = _rot(q_ref[:, cs], cv, sv)
            k = _rot(k_ref[:, cs], cv, sv) * scale
            vb = v_ref[:, cs].astype(CDT)
            qb = q.astype(CDT)
            kb = k.astype(CDT)
            a = _dot_nt(qb, kb) * m_ref[h]
            sp = state[h]
            spb = sp.astype(CDT)
            st_ref[0, h] = spb
            r = _dot(a.astype(CDT), vb) + _dot((q * qd_ref[h]).astype(CDT), spb)
            kv = _dot_tn((k * kd_ref[h]).astype(CDT), vb)
            state[h] = gs_ref[h, 0:1, :] * sp + kv
            rraw_ref[:, cs] = r
            mu = jnp.mean(r, axis=-1, keepdims=True)
            var = jnp.mean(jnp.square(r - mu), axis=-1, keepdims=True)
            n = (r - mu) * lax.rsqrt(var + EPS)
            gv = g_ref[:, cs]
            mix_ref[:, RW + h * HD:RW + (h + 1) * HD] = ((n * gn_ref[:, cs]) * (gv * _sigmoid(gv))).astype(CDT)

    col = lambda j: pl.BlockSpec((SC, RW), lambda i: (i, j))
    return _call(
        body, name="f_ret", grid=(NB,),
        in_specs=[col(2), col(3), col(4), col(5),
                  pl.BlockSpec((SC, RW), lambda i: (i, 0)),
                  pl.BlockSpec((SC, HD), lambda i: (i, 0)), pl.BlockSpec((SC, HD), lambda i: (i, 0)),
                  _resident((HEADS, SC, SC)), _resident((HEADS, SC, HD)), _resident((HEADS, SC, HD)),
                  _resident((HEADS, 8, HD)), _resident((1, RW))],
        out_specs=[pl.BlockSpec((SC, RW), lambda i: (i, 0)),
                   pl.BlockSpec((1, HEADS, HD, HD), lambda i: (i, 0, 0, 0)),
                   pl.BlockSpec((SC, 2 * RW), lambda i: (i, 0))],
        out_shape=[jax.ShapeDtypeStruct((S, RW), F32),
                   jax.ShapeDtypeStruct((NB, HEADS, HD, HD), CDT),
                   jax.ShapeDtypeStruct((S, 2 * RW), CDT)],
        scratch_shapes=[pltpu.VMEM((HEADS, HD, HD), F32)],
        compiler_params=_cp(("arbitrary",)),
    )(proj, proj, proj, proj, u, cosT, sinT, Mt, qd, kd, gs, gn)


def _f_mlp(x, mixed, wo, g2, wg, wu, wd, tm):
    S, D = x.shape
    FF = wg.shape[1]

    def body(x_ref, m_ref, wo_ref, g_ref, wg_ref, wu_ref, wd_ref,
             xm_ref, h2_ref, gate_ref, up_ref, act_ref, xo_ref):
        xm = x_ref[...] + _dot(m_ref[...], wo_ref[...])
        xm_ref[...] = xm
        r = lax.rsqrt(jnp.mean(xm * xm, axis=-1, keepdims=True) + EPS)
        h2 = ((xm * r) * g_ref[...]).astype(CDT)
        h2_ref[...] = h2
        gate = _dot(h2, wg_ref[...])
        up = _dot(h2, wu_ref[...])
        gate_ref[...] = gate
        up_ref[...] = up
        act = ((gate * _sigmoid(gate)) * up).astype(CDT)
        act_ref[...] = act
        xo_ref[...] = xm + _dot(act, wd_ref[...])

    row = lambda n: pl.BlockSpec((tm, n), lambda i: (i, 0))
    return _call(
        body, name="f_mlp", grid=(S // tm,),
        in_specs=[row(D), row(D), _resident((D, D)), _resident((1, D)),
                  _resident((D, FF)), _resident((D, FF)), _resident((FF, D))],
        out_specs=[row(D), row(D), row(FF), row(FF), row(FF), row(D)],
        out_shape=[jax.ShapeDtypeStruct((S, D), F32), jax.ShapeDtypeStruct((S, D), CDT),
                   jax.ShapeDtypeStruct((S, FF), F32), jax.ShapeDtypeStruct((S, FF), F32),
                   jax.ShapeDtypeStruct((S, FF), CDT), jax.ShapeDtypeStruct((S, D), F32)],
        compiler_params=_cp(("parallel",)),
    )(x, mixed, wo, g2, wg, wu, wd)


def _f_loss(x, fg, tgt, tm):
    S, D = x.shape

    def body(x_ref, g_ref, t_ref, dx_ref, dxb_ref, loss_ref, dg_ref):
        @pl.when(pl.program_id(0) == 0)
        def _():
            loss_ref[...] = jnp.zeros_like(loss_ref)
            dg_ref[...] = jnp.zeros_like(dg_ref)

        xv = x_ref[...]
        r = lax.rsqrt(jnp.mean(xv * xv, axis=-1, keepdims=True) + EPS)
        y = (xv * r) * g_ref[...]
        e = y - t_ref[...]
        loss_ref[...] += 0.5 * jnp.sum(jnp.mean(e * e, axis=-1, keepdims=True))
        dy = e * (1.0 / D)
        dx, dgx = _rms_bwd(xv, r, g_ref[...], dy)
        dg_ref[...] += jnp.sum(dgx, axis=0, keepdims=True)
        dx_ref[...] = dx
        dxb_ref[...] = dx.astype(CDT)

    row = pl.BlockSpec((tm, D), lambda i: (i, 0))
    return _call(
        body, name="f_loss", grid=(S // tm,),
        in_specs=[row, _resident((1, D)), row],
        out_specs=[row, row, pl.BlockSpec((1, 128), lambda i: (0, 0)), pl.BlockSpec((1, D), lambda i: (0, 0))],
        out_shape=[jax.ShapeDtypeStruct((S, D), F32), jax.ShapeDtypeStruct((S, D), CDT),
                   jax.ShapeDtypeStruct((1, 128), F32), jax.ShapeDtypeStruct((1, D), F32)],
        compiler_params=_cp(("arbitrary",)),
    )(x, fg, tgt)


def _b_mlp(dx, xm, gate, up, g2, wdT, wgT, wuT, woT, tm):
    S, D = dx.shape
    FF = gate.shape[1]

    def body(dx_ref, xm_ref, gate_ref, up_ref, g_ref, wdT_ref, wgT_ref, wuT_ref, woT_ref,
             dgate_ref, dup_ref, dxm_ref, dxmb_ref, dmix_ref, dg_ref):
        @pl.when(pl.program_id(0) == 0)
        def _():
            dg_ref[...] = jnp.zeros_like(dg_ref)

        dxv = dx_ref[...]
        dact = _dot(dxv.astype(CDT), wdT_ref[...])
        gate = gate_ref[...]
        up = up_ref[...]
        sg = _sigmoid(gate)
        sil = gate * sg
        dgate = ((dact * up) * (sg * (1.0 + gate * (1.0 - sg)))).astype(CDT)
        dup = (dact * sil).astype(CDT)
        dgate_ref[...] = dgate
        dup_ref[...] = dup
        dh2 = _dot(dgate, wgT_ref[...]) + _dot(dup, wuT_ref[...])
        xm = xm_ref[...]
        r = lax.rsqrt(jnp.mean(xm * xm, axis=-1, keepdims=True) + EPS)
        dxn, dgx = _rms_bwd(xm, r, g_ref[...], dh2)
        dg_ref[...] += jnp.sum(dgx, axis=0, keepdims=True)
        dxm = dxv + dxn
        dxm_ref[...] = dxm
        dxmb = dxm.astype(CDT)
        dxmb_ref[...] = dxmb
        dmix_ref[...] = _dot(dxmb, woT_ref[...])

    row = lambda n: pl.BlockSpec((tm, n), lambda i: (i, 0))
    return _call(
        body, name="b_mlp", grid=(S // tm,),
        in_specs=[row(D), row(D), row(FF), row(FF), _resident((1, D)),
                  _resident((D, FF)), _resident((FF, D)), _resident((FF, D)), _resident((D, D))],
        out_specs=[row(FF), row(FF), row(D), row(D), row(D), pl.BlockSpec((1, D), lambda i: (0, 0))],
        out_shape=[jax.ShapeDtypeStruct((S, FF), CDT), jax.ShapeDtypeStruct((S, FF), CDT),
                   jax.ShapeDtypeStruct((S, D), F32), jax.ShapeDtypeStruct((S, D), CDT),
                   jax.ShapeDtypeStruct((S, D), F32), jax.ShapeDtypeStruct((1, D), F32)],
        compiler_params=_cp(("arbitrary",)),
    )(dx, xm, gate, up, g2, wdT, wgT, wuT, woT)


def _b_conv(dmix, c, proj, cw, lg, lb, tm):
    S = proj.shape[0]
    CW = cw.shape[1]
    RB = 64
    hb = tm // HALO
    nt = S // tm
    last_h = S // HALO - 1

    def body(du_ref, duh_ref, c_ref, ch_ref, ab_ref, abh_ref, cw_ref, lg_ref, lb_ref,
             dab_ref, dcw_ref, dcb_ref, dlg_ref, dlb_ref, ext_u, ext_dc):
        i = pl.program_id(0)

        @pl.when(i == 0)
        def _():
            dcw_ref[...] = jnp.zeros_like(dcw_ref)
            dcb_ref[...] = jnp.zeros_like(dcb_ref)
            dlg_ref[...] = jnp.zeros_like(dlg_ref)
            dlb_ref[...] = jnp.zeros_like(dlb_ref)

        def ln_bwd(cv, du):
            mu = jnp.mean(cv, axis=-1, keepdims=True)
            var = jnp.mean(jnp.square(cv - mu), axis=-1, keepdims=True)
            rstd = lax.rsqrt(var + EPS)
            n = (cv - mu) * rstd
            z = n * lg_ref[...] + lb_ref[...]
            sz = _sigmoid(z)
            dz = du * (sz * (1.0 + z * (1.0 - sz)))
            dn = dz * lg_ref[...]
            dc = rstd * (dn - jnp.mean(dn, axis=-1, keepdims=True)
                         - n * jnp.mean(dn * n, axis=-1, keepdims=True))
            return dc, dz, n

        hv = abh_ref[...]
        ext_u[0:HALO, :] = jnp.where(i > 0, hv[:, :CW] * _sigmoid(hv[:, CW:]), 0.0)
        av = ab_ref[...]
        sb = _sigmoid(av[:, CW:])
        ext_u[HALO:, :] = av[:, :CW] * sb
        dc, dz, n = ln_bwd(c_ref[...], du_ref[...])
        ext_dc[0:tm, :] = dc
        dch, _, _ = ln_bwd(ch_ref[...], duh_ref[...])
        ext_dc[tm:, :] = jnp.where(i < nt - 1, dch, 0.0)
        dlg_ref[...] += jnp.sum(dz * n, axis=0, keepdims=True)
        dlb_ref[...] += jnp.sum(dz, axis=0, keepdims=True)
        dcb_ref[...] += jnp.sum(dc, axis=0, keepdims=True)

        for k in range(CONV_K):
            o = HALO - (CONV_K - 1) + k
            dcw_ref[k:k + 1, :] += jnp.sum(ext_dc[0:tm, :] * ext_u[o:o + tm, :], axis=0, keepdims=True)

        for rb in range(tm // RB):
            acc = jnp.zeros((RB, CW), F32)
            for k in range(CONV_K):
                o = rb * RB + (CONV_K - 1) - k
                acc = acc + cw_ref[k:k + 1, :] * ext_dc[o:o + RB, :]
            rs = slice(rb * RB, (rb + 1) * RB)
            a_r = ab_ref[rs, 0:CW]
            s_r = _sigmoid(ab_ref[rs, CW:2 * CW])
            dab_ref[rs, 0:CW] = (acc * s_r).astype(CDT)
            dab_ref[rs, CW:2 * CW] = (acc * a_r * (s_r * (1.0 - s_r))).astype(CDT)

    tile = lambda n, j: pl.BlockSpec((tm, n), lambda i: (i, j))
    nxt = lambda n, j: pl.BlockSpec((HALO, n), lambda i: (jnp.minimum((i + 1) * hb, last_h), j))
    return _call(
        body, name="b_conv", grid=(nt,),
        in_specs=[tile(CW, 0), nxt(CW, 0), tile(CW, 0), nxt(CW, 0),
                  tile(2 * CW, 0),
                  pl.BlockSpec((HALO, 2 * CW), lambda i: (jnp.maximum(i * hb - 1, 0), 0)),
                  _resident((HALO, CW)), _resident((1, CW)), _resident((1, CW))],
        out_specs=[tile(2 * CW, 0),
                   pl.BlockSpec((HALO, CW), lambda i: (0, 0)), pl.BlockSpec((1, CW), lambda i: (0, 0)),
                   pl.BlockSpec((1, CW), lambda i: (0, 0)), pl.BlockSpec((1, CW), lambda i: (0, 0))],
        out_shape=[jax.ShapeDtypeStruct((S, 2 * CW), CDT),
                   jax.ShapeDtypeStruct((HALO, CW), F32), jax.ShapeDtypeStruct((1, CW), F32),
                   jax.ShapeDtypeStruct((1, CW), F32), jax.ShapeDtypeStruct((1, CW), F32)],
        scratch_shapes=[pltpu.VMEM((tm + HALO, CW), F32), pltpu.VMEM((tm + HALO, CW), F32)],
        compiler_params=_cp(("arbitrary",)),
    )(dmix, dmix, c, c, proj, proj, cw, lg, lb)


def _b_ret(dmix, dab, proj, rraw, states, cosT, sinT, Mt, qd, kd, gs, gn):
    S = proj.shape[0]
    RW = HEADS * HD
    NB = S // SC
    scale = HD ** -0.5

    def body(dro_ref, dab_ref, q_ref, k_ref, v_ref, g_ref, rraw_ref, st_ref, c_ref, s_ref,
             m_ref, qd_ref, kd_ref, gs_ref, gn_ref, dp_ref, dgn_ref, G):
        @pl.when(pl.program_id(0) == 0)
        def _():
            G[...] = jnp.zeros_like(G)
            dgn_ref[...] = jnp.zeros_like(dgn_ref)

        dp_ref[:, 0:2 * RW] = dab_ref[...]
        cv = c_ref[...]
        sv = s_ref[...]
        for h in range(HEADS):
            cs = slice(h * HD, (h + 1) * HD)
            q = _rot(q_ref[:, cs], cv, sv)
            k = _rot(k_ref[:, cs], cv, sv) * scale
            qb = q.astype(CDT)
            kb = k.astype(CDT)
            vb = v_ref[:, cs].astype(CDT)
            spb = st_ref[0, h]
            r = rraw_ref[:, cs]
            mu = jnp.mean(r, axis=-1, keepdims=True)
            var = jnp.mean(jnp.square(r - mu), axis=-1, keepdims=True)
            rstd = lax.rsqrt(var + EPS)
            n = (r - mu) * rstd
            gv = g_ref[:, cs]
            sg = _sigmoid(gv)
            sil = gv * sg
            dro = dro_ref[:, cs]
            gnv = gn_ref[:, cs]
            dgn_ref[:, cs] += jnp.sum(dro * n * sil, axis=0, keepdims=True)
            dgate = dro * (n * gnv) * (sg * (1.0 + gv * (1.0 - sg)))
            dn = dro * gnv * sil
            dr = rstd * (dn - jnp.mean(dn, axis=-1, keepdims=True)
                         - n * jnp.mean(dn * n, axis=-1, keepdims=True))
            drb = dr.astype(CDT)
            mh = m_ref[h]
            ab = (_dot_nt(qb, kb) * mh).astype(CDT)
            dab_ = (_dot_nt(drb, vb) * mh).astype(CDT)
            qdb = (q * qd_ref[h]).astype(CDT)
            kdb = (k * kd_ref[h]).astype(CDT)
            gc = G[h]
            gb = gc.astype(CDT)
            dq = _dot(dab_, kb) + _dot_nt(drb, spb) * qd_ref[h]
            dk = _dot_tn(dab_, qb) + _dot_nt(vb, gb) * kd_ref[h]
            dv = _dot_tn(ab, drb) + _dot(kdb, gb)
            G[h] = gs_ref[h, 0:1, :] * gc + _dot_tn(qdb, drb)
            dk = dk * scale
            dqp = dq * cv + pltpu.roll(dq * sv, HD // 2, 1)
            dkp = dk * cv + pltpu.roll(dk * sv, HD // 2, 1)
            base = 2 * RW
            dp_ref[:, base + h * HD:base + (h + 1) * HD] = dqp.astype(CDT)
            dp_ref[:, base + RW + h * HD:base + RW + (h + 1) * HD] = dkp.astype(CDT)
            dp_ref[:, base + 2 * RW + h * HD:base + 2 * RW + (h + 1) * HD] = dv.astype(CDT)
            dp_ref[:, base + 3 * RW + h * HD:base + 3 * RW + (h + 1) * HD] = dgate.astype(CDT)

    rev = lambda n, j: pl.BlockSpec((SC, n), lambda i: (NB - 1 - i, j))
    return _call(
        body, name="b_ret", grid=(NB,),
        in_specs=[rev(RW, 1), rev(2 * RW, 0), rev(RW, 2), rev(RW, 3), rev(RW, 4), rev(RW, 5), rev(RW, 0),
                  pl.BlockSpec((1, HEADS, HD, HD), lambda i: (NB - 1 - i, 0, 0, 0)),
                  rev(HD, 0), rev(HD, 0),
                  _resident((HEADS, SC, SC)), _resident((HEADS, SC, HD)), _resident((HEADS, SC, HD)),
                  _resident((HEADS, 8, HD)), _resident((1, RW))],
        out_specs=[rev(6 * RW, 0), pl.BlockSpec((1, RW), lambda i: (0, 0))],
        out_shape=[jax.ShapeDtypeStruct((S, 6 * RW), CDT), jax.ShapeDtypeStruct((1, RW), F32)],
        scratch_shapes=[pltpu.VMEM((HEADS, HD, HD), F32)],
        compiler_params=_cp(("arbitrary",)),
    )(dmix, dab, proj, proj, proj, proj, rraw, states, cosT, sinT, Mt, qd, kd, gs, gn)


def _b_in(dproj, winT, x, g1, dxm, tm):
    S, D = x.shape
    N = dproj.shape[1]

    def body(dp_ref, w_ref, x_ref, g_ref, dxm_ref, dx_ref, dxb_ref, dg_ref):
        @pl.when(pl.program_id(0) == 0)
        def _():
            dg_ref[...] = jnp.zeros_like(dg_ref)

        dh = _dot(dp_ref[...], w_ref[...])
        xv = x_ref[...]
        r = lax.rsqrt(jnp.mean(xv * xv, axis=-1, keepdims=True) + EPS)
        dxn, dgx = _rms_bwd(xv, r, g_ref[...], dh)
        dg_ref[...] += jnp.sum(dgx, axis=0, keepdims=True)
        dx = dxm_ref[...] + dxn
        dx_ref[...] = dx
        dxb_ref[...] = dx.astype(CDT)

    row = lambda n: pl.BlockSpec((tm, n), lambda i: (i, 0))
    return _call(
        body, name="b_in", grid=(S // tm,),
        in_specs=[row(N), _resident((N, D)), row(D), _resident((1, D)), row(D)],
        out_specs=[row(D), row(D), pl.BlockSpec((1, D), lambda i: (0, 0))],
        out_shape=[jax.ShapeDtypeStruct((S, D), F32), jax.ShapeDtypeStruct((S, D), CDT),
                   jax.ShapeDtypeStruct((1, D), F32)],
        compiler_params=_cp(("arbitrary",)),
    )(dproj, winT, x, g1, dxm)


def _dw_tn(a, b, tm, tk):
    S, M = a.shape
    N = b.shape[1]

    def body(a_ref, b_ref, o_ref):
        @pl.when(pl.program_id(1) == 0)
        def _():
            o_ref[...] = jnp.zeros_like(o_ref)

        o_ref[...] += _dot_tn(a_ref[...], b_ref[...])

    return _call(
        body, name="dw_tn", grid=(M // tm, S // tk),
        in_specs=[pl.BlockSpec((tk, tm), lambda m, k: (k, m)), pl.BlockSpec((tk, N), lambda m, k: (k, 0))],
        out_specs=pl.BlockSpec((tm, N), lambda m, k: (m, 0)),
        out_shape=jax.ShapeDtypeStruct((M, N), F32),
        compiler_params=_cp(("parallel", "arbitrary")),
    )(a, b)


def _sum8(recv, tr):
    _, R, C = recv.shape

    def body(r_ref, o_ref):
        acc = r_ref[0]
        for s in range(1, N_DEV):
            acc = acc + r_ref[s]
        o_ref[...] = acc

    return _call(
        body, name="sum8", grid=(R // tr,),
        in_specs=[pl.BlockSpec((N_DEV, tr, C), lambda i: (0, i, 0))],
        out_specs=pl.BlockSpec((tr, C), lambda i: (i, 0)),
        out_shape=jax.ShapeDtypeStruct((R, C), F32),
        compiler_params=_cp(("parallel",)),
    )(recv)


def _adamw(w, g, m, v, tr):
    R, C = w.shape
    c1 = 1.0 - ADAM_B1 ** ADAM_STEP
    c2 = 1.0 - ADAM_B2 ** ADAM_STEP

    def body(w_ref, g_ref, m_ref, v_ref, d_ref, mo_ref, vo_ref):
        gv = g_ref[...]
        mn = ADAM_B1 * m_ref[...] + (1.0 - ADAM_B1) * gv
        vn = ADAM_B2 * v_ref[...] + (1.0 - ADAM_B2) * jnp.square(gv)
        mo_ref[...] = mn
        vo_ref[...] = vn
        d_ref[...] = -ADAM_LR * ((mn / c1) / (jnp.sqrt(vn / c2) + ADAM_EPS) + ADAM_WD * w_ref[...])

    blk = pl.BlockSpec((tr, C), lambda i: (i, 0))
    sh = jax.ShapeDtypeStruct((R, C), F32)
    return _call(
        body, name="adamw", grid=(R // tr,),
        in_specs=[blk, blk, blk, blk], out_specs=[blk, blk, blk], out_shape=[sh, sh, sh],
        compiler_params=_cp(("parallel",)),
    )(w, g, m, v)


def _coords():
    return lax.axis_index("x"), lax.axis_index("y"), lax.axis_index("c")


def _peer(x, y, c, d):
    return (x ^ (d >> 2), y ^ ((d >> 1) & 1), c ^ (d & 1))


def _ag_big(p):
    R, C = p.shape

    def body(x_ref, out_ref, send_sems, recv_sems, local_sem):
        x, y, c = _coords()
        me, sibling = (x, y, c), (x, y, 1 - c)
        chips = [(1 - x, y), (x, 1 - y), (1 - x, 1 - y)]

        def rows(px, py, pc):
            return out_ref.at[pl.ds((4 * px + 2 * py + pc) * R, R), :]

        def copy(k, block, to, src=None):
            return pltpu.make_async_remote_copy(
                src_ref=rows(*block) if src is None else src, dst_ref=rows(*block),
                send_sem=send_sems.at[k], recv_sem=recv_sems.at[k], device_id=to, device_id_type=MESH)

        mine = pltpu.make_async_copy(x_ref, rows(*me), local_sem)
        mine.start()
        first = [copy(0, me, sibling, src=x_ref)]
        first += [copy(1 + j, me, (*chip, c), src=x_ref) for j, chip in enumerate(chips)]
        for cp in first:
            cp.start()
        passed = [copy(4 + j, (*chip, c), sibling) for j, chip in enumerate(chips)]
        for j, chip in enumerate(chips):
            copy(1 + j, (*chip, c), me).wait_recv()
            passed[j].start()
        copy(0, sibling, me).wait_recv()
        for j, chip in enumerate(chips):
            copy(4 + j, (*chip, 1 - c), me).wait_recv()
        for cp in first + passed:
            cp.wait_send()
        mine.wait()

    return _call(
        body, name="ag_big",
        in_specs=[pl.BlockSpec(memory_space=pl.ANY)],
        out_specs=pl.BlockSpec(memory_space=pl.ANY),
        out_shape=jax.ShapeDtypeStruct((N_DEV * R, C), p.dtype),
        scratch_shapes=[pltpu.SemaphoreType.DMA((7,)), pltpu.SemaphoreType.DMA((7,)), pltpu.SemaphoreType.DMA],
    )(p)


def _rs_big(parts, offs, R):
    C = parts[0].shape[2]
    npart = len(parts)

    def body(*refs):
        srcs = refs[:npart]
        recv = refs[npart]
        send_sems, recv_sems, local_sem = refs[npart + 1:]
        x, y, c = _coords()
        me = 4 * x + 2 * y + c
        for i in range(npart):
            n = srcs[i].shape[1]
            pltpu.make_async_copy(srcs[i].at[me], recv.at[me, pl.ds(offs[i], n), :], local_sem).start()
        for d in range(1, N_DEV):
            px, py, pc = _peer(x, y, c, d)
            p = 4 * px + 2 * py + pc
            for i in range(npart):
                n = srcs[i].shape[1]
                pltpu.make_async_remote_copy(
                    src_ref=srcs[i].at[p], dst_ref=recv.at[me, pl.ds(offs[i], n), :],
                    send_sem=send_sems.at[d - 1], recv_sem=recv_sems.at[d - 1],
                    device_id=(px, py, pc), device_id_type=MESH).start()
        for d in range(1, N_DEV):
            px, py, pc = _peer(x, y, c, d)
            p = 4 * px + 2 * py + pc
            whole = pltpu.make_async_remote_copy(
                src_ref=recv.at[p], dst_ref=recv.at[p],
                send_sem=send_sems.at[d - 1], recv_sem=recv_sems.at[d - 1],
                device_id=(px, py, pc), device_id_type=MESH)
            whole.wait_recv()
            whole.wait_send()
        pltpu.make_async_copy(recv.at[me], recv.at[me], local_sem).wait()

    return _call(
        body, name="rs_big",
        in_specs=[pl.BlockSpec(memory_space=pl.ANY)] * npart,
        out_specs=pl.BlockSpec(memory_space=pl.ANY),
        out_shape=jax.ShapeDtypeStruct((N_DEV, R, C), F32),
        scratch_shapes=[pltpu.SemaphoreType.DMA((7,)), pltpu.SemaphoreType.DMA((7,)), pltpu.SemaphoreType.DMA],
    )(*parts)


def _gather_small(v, reduce):
    R, C = v.shape

    def exchange(v_ref, buf, send_sems, recv_sems):
        x, y, c = _coords()
        me = 4 * x + 2 * y + c
        buf[me] = v_ref[...]
        cps = []
        for d in range(1, N_DEV):
            cp = pltpu.make_async_remote_copy(
                src_ref=v_ref, dst_ref=buf.at[me], send_sem=send_sems.at[d - 1], recv_sem=recv_sems.at[d - 1],
                device_id=_peer(x, y, c, d), device_id_type=MESH)
            cp.start()
            cps.append(cp)
        for cp in cps:
            cp.wait_recv()
        for cp in cps:
            cp.wait_send()

    sems = [pltpu.SemaphoreType.DMA((7,)), pltpu.SemaphoreType.DMA((7,))]
    vm = pl.BlockSpec(memory_space=pltpu.VMEM)
    if reduce:
        def body(v_ref, o_ref, buf, send_sems, recv_sems):
            exchange(v_ref, buf, send_sems, recv_sems)
            acc = buf[0]
            for s in range(1, N_DEV):
                acc = acc + buf[s]
            o_ref[...] = acc

        return _call(body, name="allreduce_small", in_specs=[vm], out_specs=vm,
                     out_shape=jax.ShapeDtypeStruct((R, C), F32),
                     scratch_shapes=[pltpu.VMEM((N_DEV, R, C), F32)] + sems)(v)

    def body(v_ref, o_ref, send_sems, recv_sems):
        exchange(v_ref, o_ref, send_sems, recv_sems)

    return _call(body, name="allgather_small", in_specs=[vm], out_specs=vm,
                 out_shape=jax.ShapeDtypeStruct((N_DEV, R, C), F32), scratch_shapes=sems)(v)


def _tables(S):
    half = HD // 2
    pos = jnp.arange(S, dtype=F32)
    freqs = ROPE_BASE ** (-jnp.arange(half, dtype=F32) / half)
    ang = pos[:, None] * freqs[None, :]
    cos, sin = jnp.cos(ang), jnp.sin(ang)
    cosT = jnp.concatenate([cos, cos], axis=-1)
    sinT = jnp.concatenate([-sin, sin], axis=-1)
    log_g = jnp.log(1.0 - 2.0 ** (-5.0 - jnp.arange(HEADS, dtype=F32)))
    idx = jnp.arange(SC, dtype=F32)
    ci = jnp.arange(SC) // CHUNK
    diff = idx[:, None] - idx[None, :]
    same = ci[:, None] == ci[None, :]
    earlier = ci[None, :] < ci[:, None]
    expo = jnp.where(same, jnp.abs(diff), diff)
    Mt = jnp.where((same | earlier)[None], jnp.exp(log_g[:, None, None] * expo[None]), 0.0)
    ones = jnp.ones((1, 1, HD), F32)
    qd = jnp.exp(log_g[:, None] * (idx + 1.0)[None, :])[:, :, None] * ones
    kd = jnp.exp(log_g[:, None] * (SC - 1.0 - idx)[None, :])[:, :, None] * ones
    gs = jnp.exp(log_g * SC)[:, None, None] * jnp.ones((1, 8, HD), F32)
    return cosT, sinT, Mt, qd, kd, gs


def _pad_rows(a, rows):
    return jnp.pad(a, ((0, rows - a.shape[0]), (0, 0)))


def kernel(x, norm1_g, w_in, conv_w, conv_b, conv_ln_g, conv_ln_b, ret_gn_g, w_out, norm2_g, w_gate, w_up, w_down, final_g, loss_target, m_norm1_g, m_w_in, m_conv_w, m_conv_b, m_conv_ln_g, m_conv_ln_b, m_ret_gn_g, m_w_out, m_norm2_g, m_w_gate, m_w_up, m_w_down, m_final_g, v_norm1_g, v_w_in, v_conv_w, v_conv_b, v_conv_ln_g, v_conv_ln_b, v_ret_gn_g, v_w_out, v_norm2_g, v_w_gate, v_w_up, v_w_down, v_final_g):
    L, D, n_in = w_in.shape
    n_out = w_out.shape[1]
    n_ff = w_gate.shape[2]
    S = x.shape[1]
    CW = conv_b.shape[1]
    IN, FF = N_DEV * n_in, N_DEV * n_ff
    ncw = conv_w.shape[2]
    x0 = x.reshape(S, D)
    tgt = loss_target.reshape(S, D)
    TM = min(512, S)
    TMM = min(256, S)

    sizes = [n_in, n_out, n_ff, n_ff, n_ff]
    offs = [0]
    for n in sizes[:-1]:
        offs.append(offs[-1] + n)
    RL = sum(sizes)
    pack = jnp.concatenate([jnp.swapaxes(w_in, 1, 2), w_out, jnp.swapaxes(w_gate, 1, 2),
                            jnp.swapaxes(w_up, 1, 2), w_down], axis=1).astype(CDT)
    G = _ag_big(pack.reshape(L * RL, D)).reshape(N_DEV, L, RL, D)

    def full(l, k):
        return G[:, l, offs[k]:offs[k] + sizes[k], :].reshape(N_DEV * sizes[k], D)

    cwp = conv_w.reshape(L * CONV_K * ncw // 128, 128)
    cw_rows = -(-cwp.shape[0] // 8) * 8
    cwg = _gather_small(_pad_rows(cwp, cw_rows), reduce=False)[:, :cwp.shape[0], :]
    conv_w_full = jnp.moveaxis(cwg.reshape(N_DEV, L, CONV_K, ncw), 0, 2).reshape(L, CONV_K, CW)

    cosT, sinT, Mt, qd, kd, gs = _tables(S)

    saved = []
    xl = x0
    for l in range(L):
        winT, wo, wgT, wuT, wd = (full(l, k) for k in range(5))
        win, woT, wg, wu, wdT = winT.T, wo.T, wgT.T, wuT.T, wd.T
        cw = _pad_rows(conv_w_full[l], HALO)
        h, proj = _f_in(xl, norm1_g[l][None], win, TM)
        c, u = _f_conv(proj, cw, conv_b[l][None], conv_ln_g[l][None], conv_ln_b[l][None], TM)
        rraw, states, mixed = _f_ret(proj, u, cosT, sinT, Mt, qd, kd, gs, ret_gn_g[l][None])
        xm, h2, gate, up, act, xo = _f_mlp(xl, mixed, wo, norm2_g[l][None], wg, wu, wd, TMM)
        saved.append(dict(x=xl, h=h, proj=proj, c=c, rraw=rraw, states=states, mixed=mixed, xm=xm, h2=h2,
                          gate=gate, up=up, act=act, cw=cw, winT=winT, woT=woT, wgT=wgT, wuT=wuT, wdT=wdT))
        xl = xo

    dx, dxb, loss_p, dfg = _f_loss(xl, final_g[None], tgt, TM)

    parts = []
    small = []
    for l in reversed(range(L)):
        sv = saved[l]
        dgate, dup, dxm, dxmb, dmix, dg2 = _b_mlp(dx, sv["xm"], sv["gate"], sv["up"], norm2_g[l][None],
                                                  sv["wdT"], sv["wgT"], sv["wuT"], sv["woT"], TMM)
        d_wd = _dw_tn(sv["act"], dxb, FF // 2, TM)
        d_wgT = _dw_tn(dgate, sv["h2"], FF // 2, TM)
        d_wuT = _dw_tn(dup, sv["h2"], FF // 2, TM)
        d_wo = _dw_tn(sv["mixed"], dxmb, D // 2, TM)
        dab, dcw, dcb, dlg, dlb = _b_conv(dmix, sv["c"], sv["proj"], sv["cw"], conv_ln_g[l][None],
                                          conv_ln_b[l][None], TM)
        dproj, dgn = _b_ret(dmix, dab, sv["proj"], sv["rraw"], sv["states"], cosT, sinT, Mt, qd, kd, gs,
                            ret_gn_g[l][None])
        d_winT = _dw_tn(dproj, sv["h"], IN // 2, TM)
        dx, dxb, dg1 = _b_in(dproj, sv["winT"], sv["x"], norm1_g[l][None], dxm, TM)
        parts.append([d.reshape(N_DEV, -1, D) for d in (d_winT, d_wo, d_wgT, d_wuT, d_wd)])
        small.append(jnp.concatenate([dcw, dcb, dlg, dlb, dgn, dg1.reshape(2, CW), dg2.reshape(2, CW)], axis=0))
    parts = parts[::-1]
    small = small[::-1]
    grad_x = dx.reshape(1, S, D)

    rows_l = HALO + 8
    loss_row = jnp.zeros((1, CW), F32).at[0, 0].set(loss_p[0, 0])
    sm = jnp.concatenate(small + [dfg.reshape(2, CW), loss_row], axis=0)
    sm_rows = -(-sm.shape[0] // 8) * 8
    sm = _gather_small(_pad_rows(sm, sm_rows), reduce=True)
    loss = sm[L * rows_l + 2, 0]
    g_final = sm[L * rows_l:L * rows_l + 2].reshape(D)
    per = sm[:L * rows_l].reshape(L, rows_l, CW)
    me = 4 * lax.axis_index("x") + 2 * lax.axis_index("y") + lax.axis_index("c")
    g_conv_w = lax.dynamic_slice_in_dim(per[:, :CONV_K, :], me * ncw, ncw, axis=2)
    g_conv_b, g_ln_g, g_ln_b, g_gn = per[:, HALO], per[:, HALO + 1], per[:, HALO + 2], per[:, HALO + 3]
    g_n1 = per[:, HALO + 4:HALO + 6].reshape(L, D)
    g_n2 = per[:, HALO + 6:HALO + 8].reshape(L, D)

    flat_parts = [parts[l][k] for l in range(L) for k in range(5)]
    flat_offs = [l * RL + offs[k] for l in range(L) for k in range(5)]
    recv = _rs_big(flat_parts, flat_offs, L * RL)
    gsum = _sum8(recv, RL // 4).reshape(L, RL, D)
    g_w_in = jnp.swapaxes(gsum[:, offs[0]:offs[0] + n_in], 1, 2)
    g_w_out = gsum[:, offs[1]:offs[1] + n_out]
    g_w_gate = jnp.swapaxes(gsum[:, offs[2]:offs[2] + n_ff], 1, 2)
    g_w_up = jnp.swapaxes(gsum[:, offs[3]:offs[3] + n_ff], 1, 2)
    g_w_down = gsum[:, offs[4]:offs[4] + n_ff]

    def big(w, g, m, v):
        sh = w.shape
        two = lambda a: a.reshape(-1, sh[-1])
        rows = two(w).shape[0]
        d, mn, vn = _adamw(two(w), two(g), two(m), two(v), rows // 8)
        return d.reshape(sh), mn.reshape(sh), vn.reshape(sh)

    names = ["norm1_g", "conv_w", "conv_b", "conv_ln_g", "conv_ln_b", "ret_gn_g", "norm2_g", "final_g"]
    sw = dict(norm1_g=(norm1_g, g_n1, m_norm1_g, v_norm1_g), conv_w=(conv_w, g_conv_w, m_conv_w, v_conv_w),
              conv_b=(conv_b, g_conv_b, m_conv_b, v_conv_b), conv_ln_g=(conv_ln_g, g_ln_g, m_conv_ln_g, v_conv_ln_g),
              conv_ln_b=(conv_ln_b, g_ln_b, m_conv_ln_b, v_conv_ln_b), ret_gn_g=(ret_gn_g, g_gn, m_ret_gn_g, v_ret_gn_g),
              norm2_g=(norm2_g, g_n2, m_norm2_g, v_norm2_g), final_g=(final_g, g_final, m_final_g, v_final_g))
    lens = [int(math.prod(sw[n][0].shape)) for n in names]
    tot = sum(lens)
    prow = -(-tot // (8 * CW)) * 8

    def packs(j):
        flat = jnp.concatenate([sw[n][j].reshape(-1) for n in names])
        return jnp.pad(flat, (0, prow * CW - tot)).reshape(prow, CW)

    sd, smn, svn = _adamw(packs(0), packs(1), packs(2), packs(3), prow)

    def unpack(a):
        flat = a.reshape(-1)
        out, o = {}, 0
        for n, ln in zip(names, lens):
            out[n] = flat[o:o + ln].reshape(sw[n][0].shape)
            o += ln
        return out

    sd, smn, svn = unpack(sd), unpack(smn), unpack(svn)
    res = {n: (sw[n][1], sd[n], smn[n], svn[n]) for n in names}
    res["w_in"] = (g_w_in,) + big(w_in, g_w_in, m_w_in, v_w_in)
    res["w_out"] = (g_w_out,) + big(w_out, g_w_out, m_w_out, v_w_out)
    res["w_gate"] = (g_w_gate,) + big(w_gate, g_w_gate, m_w_gate, v_w_gate)
    res["w_up"] = (g_w_up,) + big(w_up, g_w_up, m_w_up, v_w_up)
    res["w_down"] = (g_w_down,) + big(w_down, g_w_down, m_w_down, v_w_down)

    order = ["norm1_g", "w_in", "conv_w", "conv_b", "conv_ln_g", "conv_ln_b", "ret_gn_g", "w_out", "norm2_g",
             "w_gate", "w_up", "w_down", "final_g"]
    return (loss, grad_x, *[res[n][0] for n in order], *[res[n][1] for n in order],
            *[res[n][2] for n in order], *[res[n][3] for n in order])
```

```python
import functools
import math

import jax
import jax.numpy as jnp
from jax import lax
from jax.experimental import pallas as pl
from jax.experimental.pallas import tpu as pltpu

F32 = jnp.float32
CDT = jnp.bfloat16
EPS = 1e-6
CHUNK = 64
SC = 256
HEADS = 4
HD = 128
CONV_K = 31
HALO = 32
ROPE_BASE = 10000.0
ADAM_LR = 0.001
ADAM_B1 = 0.9
ADAM_B2 = 0.999
ADAM_EPS = 1e-08
ADAM_WD = 0.01
ADAM_STEP = 10
N_DEV = 8
MESH = pl.DeviceIdType.MESH
VMEM_LIMIT = 60 * 1024 * 1024


def _call(body, **kw):
    return pl.pallas_call(body, **kw)


def _cp(sem=None, vmem=VMEM_LIMIT):
    return pltpu.CompilerParams(dimension_semantics=sem, vmem_limit_bytes=vmem)


def _resident(shape):
    nd = len(shape)
    return pl.BlockSpec(shape, lambda *_: (0,) * nd, pipeline_mode=pl.Buffered(1))


def _dot(a, b):
    return jnp.dot(a, b, preferred_element_type=F32)


def _dot_nt(a, b):
    return lax.dot_general(a, b, (((1,), (1,)), ((), ())), preferred_element_type=F32)


def _dot_tn(a, b):
    return lax.dot_general(a, b, (((0,), (0,)), ((), ())), preferred_element_type=F32)


def _sigmoid(x):
    return 1.0 / (1.0 + jnp.exp(-x))


def _rms_bwd(x, r, g, dy):
    xh = x * r
    dyg = dy * g
    dx = r * (dyg - xh * jnp.mean(dyg * xh, axis=-1, keepdims=True))
    return dx, dy * xh


class _Comm:
    def __init__(self, ins, out_shapes, sems, start, mid, finish):
        self.ins, self.out_shapes, self.sems = list(ins), list(out_shapes), list(sems)
        self.start, self.mid, self.finish = start, mid, finish


_ANY = pl.BlockSpec(memory_space=pl.ANY)


def _launch(name, compute, grid, in_specs, out_specs, out_shape, scratch, operands, sem, comm=None):
    n_in, n_out, n_sc = len(in_specs), len(out_specs), len(scratch)
    if comm is None:
        res = _call(compute, name=name, grid=grid, in_specs=in_specs, out_specs=out_specs, out_shape=out_shape,
                    scratch_shapes=scratch, compiler_params=_cp(sem))(*operands)
        return list(res), []
    c_in, c_out = len(comm.ins), len(comm.out_shapes)
    steps = grid[0]
    mid_step = (3 * steps) // 4

    def body(*refs):
        ins = refs[:n_in]
        cins = refs[n_in:n_in + c_in]
        o = n_in + c_in
        outs = refs[o:o + n_out]
        couts = refs[o + n_out:o + n_out + c_out]
        o += n_out + c_out
        sc = refs[o:o + n_sc]
        csem = refs[o + n_sc:]
        i = pl.program_id(0)

        @pl.when(i == 0)
        def _():
            comm.start(cins, couts, csem)

        if comm.mid is not None:
            @pl.when(i == mid_step)
            def _():
                comm.mid(cins, couts, csem)

        compute(*ins, *outs, *sc)

        @pl.when(i == steps - 1)
        def _():
            comm.finish(cins, couts, csem)

    res = _call(body, name=name, grid=grid, in_specs=list(in_specs) + [_ANY] * c_in,
                out_specs=list(out_specs) + [_ANY] * c_out, out_shape=list(out_shape) + comm.out_shapes,
                scratch_shapes=list(scratch) + comm.sems,
                compiler_params=_cp(("arbitrary",) * len(grid)))(*operands, *comm.ins)
    return list(res[:n_out]), list(res[n_out:])


def _comm_only(comm, name):
    c_in, c_out = len(comm.ins), len(comm.out_shapes)

    def body(*refs):
        cins, couts, csem = refs[:c_in], refs[c_in:c_in + c_out], refs[c_in + c_out:]
        comm.start(cins, couts, csem)
        if comm.mid is not None:
            comm.mid(cins, couts, csem)
        comm.finish(cins, couts, csem)

    res = _call(body, name=name, in_specs=[_ANY] * c_in, out_specs=[_ANY] * c_out, out_shape=comm.out_shapes,
                scratch_shapes=comm.sems)(*comm.ins)
    return list(res)


def _f_in(x, g, w, tm):
    S, D = x.shape
    N = w.shape[1]

    def body(x_ref, g_ref, w_ref, h_ref, p_ref):
        xv = x_ref[...]
        r = lax.rsqrt(jnp.mean(xv * xv, axis=-1, keepdims=True) + EPS)
        h = ((xv * r) * g_ref[...]).astype(CDT)
        h_ref[...] = h
        p_ref[...] = _dot(h, w_ref[...])

    return _call(
        body, name="f_in", grid=(S // tm,),
        in_specs=[pl.BlockSpec((tm, D), lambda i: (i, 0)), _resident((1, D)), _resident((D, N))],
        out_specs=[pl.BlockSpec((tm, D), lambda i: (i, 0)), pl.BlockSpec((tm, N), lambda i: (i, 0))],
        out_shape=[jax.ShapeDtypeStruct((S, D), CDT), jax.ShapeDtypeStruct((S, N), F32)],
        compiler_params=_cp(("parallel",)),
    )(x, g, w)


def _f_conv(proj, cw, cb, lg, lb, tm):
    S = proj.shape[0]
    CW = cw.shape[1]
    RB = 64
    hb = tm // HALO

    def body(ab_ref, halo_ref, cw_ref, cb_ref, lg_ref, lb_ref, c_ref, u_ref, ext):
        i = pl.program_id(0)
        hv = halo_ref[...]
        u0h = hv[:, :CW] * _sigmoid(hv[:, CW:])
        ext[0:HALO, :] = jnp.where(i > 0, u0h, 0.0)
        av = ab_ref[...]
        ext[HALO:, :] = av[:, :CW] * _sigmoid(av[:, CW:])
        for rb in range(tm // RB):
            acc = jnp.zeros((RB, CW), F32) + cb_ref[...]
            for k in range(CONV_K):
                o = rb * RB + HALO - (CONV_K - 1) + k
                acc = acc + cw_ref[k:k + 1, :] * ext[o:o + RB, :]
            c_ref[rb * RB:(rb + 1) * RB, :] = acc
            mu = jnp.mean(acc, axis=-1, keepdims=True)
            var = jnp.mean(jnp.square(acc - mu), axis=-1, keepdims=True)
            z = ((acc - mu) * lax.rsqrt(var + EPS)) * lg_ref[...] + lb_ref[...]
            u_ref[rb * RB:(rb + 1) * RB, :] = (z * _sigmoid(z)).astype(CDT)

    return _call(
        body, name="f_conv", grid=(S // tm,),
        in_specs=[pl.BlockSpec((tm, 2 * CW), lambda i: (i, 0)),
                  pl.BlockSpec((HALO, 2 * CW), lambda i: (jnp.maximum(i * hb - 1, 0), 0)),
                  _resident((HALO, CW)), _resident((1, CW)), _resident((1, CW)), _resident((1, CW))],
        out_specs=[pl.BlockSpec((tm, CW), lambda i: (i, 0)), pl.BlockSpec((tm, CW), lambda i: (i, 0))],
        out_shape=[jax.ShapeDtypeStruct((S, CW), F32), jax.ShapeDtypeStruct((S, CW), CDT)],
        scratch_shapes=[pltpu.VMEM((tm + HALO, CW), F32)],
        compiler_params=_cp(("parallel",)),
    )(proj, proj, cw, cb, lg, lb)


def _rot(t, c, s):
    return t * c + pltpu.roll(t, HD // 2, 1) * s


def _f_ret(proj, u, cosT, sinT, Mt, qd, kd, gs, gn):
    S = proj.shape[0]
    RW = HEADS * HD
    NB = S // SC
    scale = HD ** -0.5

    def body(q_ref, k_ref, v_ref, g_ref, u_ref, c_ref, s_ref, m_ref, qd_ref, kd_ref, gs_ref, gn_ref,
             rraw_ref, st_ref, mix_ref, state):
        @pl.when(pl.program_id(0) == 0)
        def _():
            state[...] = jnp.zeros_like(state)

        mix_ref[:, 0:RW] = u_ref[...]
        cv = c_ref[...]
        sv = s_ref[...]
        for h in range(HEADS):
            cs = slice(h * HD, (h + 1) * HD)
            q = _rot(q_ref[:, cs], cv, sv)
            k = _rot(k_ref[:, cs], cv, sv) * scale
            vb = v_ref[:, cs].astype(CDT)
            qb = q.astype(CDT)
            kb = k.astype(CDT)
            a = _dot_nt(qb, kb) * m_ref[h]
            sp = state[h]
            spb = sp.astype(CDT)
            st_ref[0, h] = spb
            r = _dot(a.astype(CDT), vb) + _dot((q * qd_ref[h]).astype(CDT), spb)
            kv = _dot_tn((k * kd_ref[h]).astype(CDT), vb)
            state[h] = gs_ref[h, 0:1, :] * sp + kv
            rraw_ref[:, cs] = r
            mu = jnp.mean(r, axis=-1, keepdims=True)
            var = jnp.mean(jnp.square(r - mu), axis=-1, keepdims=True)
            n = (r - mu) * lax.rsqrt(var + EPS)
            gv = g_ref[:, cs]
            mix_ref[:, RW + h * HD:RW + (h + 1) * HD] = ((n * gn_ref[:, cs]) * (gv * _sigmoid(gv))).astype(CDT)

    col = lambda j: pl.BlockSpec((SC, RW), lambda i: (i, j))
    return _call(
        body, name="f_ret", grid=(NB,),
        in_specs=[col(2), col(3), col(4), col(5),
                  pl.BlockSpec((SC, RW), lambda i: (i, 0)),
                  pl.BlockSpec((SC, HD), lambda i: (i, 0)), pl.BlockSpec((SC, HD), lambda i: (i, 0)),
                  _resident((HEADS, SC, SC)), _resident((HEADS, SC, HD)), _resident((HEADS, SC, HD)),
                  _resident((HEADS, 8, HD)), _resident((1, RW))],
        out_specs=[pl.BlockSpec((SC, RW), lambda i: (i, 0)),
                   pl.BlockSpec((1, HEADS, HD, HD), lambda i: (i, 0, 0, 0)),
                   pl.BlockSpec((SC, 2 * RW), lambda i: (i, 0))],
        out_shape=[jax.ShapeDtypeStruct((S, RW), F32),
                   jax.ShapeDtypeStruct((NB, HEADS, HD, HD), CDT),
                   jax.ShapeDtypeStruct((S, 2 * RW), CDT)],
        scratch_shapes=[pltpu.VMEM((HEADS, HD, HD), F32)],
        compiler_params=_cp(("arbitrary",)),
    )(proj, proj, proj, proj, u, cosT, sinT, Mt, qd, kd, gs, gn)


def _f_mlp(x, mixed, wo, g2, wg, wu, wd, tm, comm=None):
    S, D = x.shape
    FF = wg.shape[1]

    def body(x_ref, m_ref, wo_ref, g_ref, wg_ref, wu_ref, wd_ref,
             xm_ref, h2_ref, gate_ref, up_ref, act_ref, xo_ref):
        xm = x_ref[...] + _dot(m_ref[...], wo_ref[...])
        xm_ref[...] = xm
        r = lax.rsqrt(jnp.mean(xm * xm, axis=-1, keepdims=True) + EPS)
        h2 = ((xm * r) * g_ref[...]).astype(CDT)
        h2_ref[...] = h2
        gate = _dot(h2, wg_ref[...])
        up = _dot(h2, wu_ref[...])
        gate_ref[...] = gate
        up_ref[...] = up
        act = ((gate * _sigmoid(gate)) * up).astype(CDT)
        act_ref[...] = act
        xo_ref[...] = xm + _dot(act, wd_ref[...])

    row = lambda n: pl.BlockSpec((tm, n), lambda i: (i, 0))
    return _launch(
        "f_mlp" if comm is None else "f_mlp_ag", body, (S // tm,),
        [row(D), row(D), _resident((D, D)), _resident((1, D)),
         _resident((D, FF)), _resident((D, FF)), _resident((FF, D))],
        [row(D), row(D), row(FF), row(FF), row(FF), row(D)],
        [jax.ShapeDtypeStruct((S, D), F32), jax.ShapeDtypeStruct((S, D), CDT),
         jax.ShapeDtypeStruct((S, FF), F32), jax.ShapeDtypeStruct((S, FF), F32),
         jax.ShapeDtypeStruct((S, FF), CDT), jax.ShapeDtypeStruct((S, D), F32)],
        [], (x, mixed, wo, g2, wg, wu, wd), ("parallel",), comm)


def _f_loss(x, fg, tgt, tm):
    S, D = x.shape

    def body(x_ref, g_ref, t_ref, dx_ref, dxb_ref, loss_ref, dg_ref):
        @pl.when(pl.program_id(0) == 0)
        def _():
            loss_ref[...] = jnp.zeros_like(loss_ref)
            dg_ref[...] = jnp.zeros_like(dg_ref)

        xv = x_ref[...]
        r = lax.rsqrt(jnp.mean(xv * xv, axis=-1, keepdims=True) + EPS)
        y = (xv * r) * g_ref[...]
        e = y - t_ref[...]
        loss_ref[...] += 0.5 * jnp.sum(jnp.mean(e * e, axis=-1, keepdims=True))
        dy = e * (1.0 / D)
        dx, dgx = _rms_bwd(xv, r, g_ref[...], dy)
        dg_ref[...] += jnp.sum(dgx, axis=0, keepdims=True)
        dx_ref[...] = dx
        dxb_ref[...] = dx.astype(CDT)

    row = pl.BlockSpec((tm, D), lambda i: (i, 0))
    return _call(
        body, name="f_loss", grid=(S // tm,),
        in_specs=[row, _resident((1, D)), row],
        out_specs=[row, row, pl.BlockSpec((1, 128), lambda i: (0, 0)), pl.BlockSpec((1, D), lambda i: (0, 0))],
        out_shape=[jax.ShapeDtypeStruct((S, D), F32), jax.ShapeDtypeStruct((S, D), CDT),
                   jax.ShapeDtypeStruct((1, 128), F32), jax.ShapeDtypeStruct((1, D), F32)],
        compiler_params=_cp(("arbitrary",)),
    )(x, fg, tgt)


def _b_mlp(dx, xm, gate, up, g2, wdT, wgT, wuT, woT, tm, comm=None):
    S, D = dx.shape
    FF = gate.shape[1]

    def body(dx_ref, xm_ref, gate_ref, up_ref, g_ref, wdT_ref, wgT_ref, wuT_ref, woT_ref,
             dgate_ref, dup_ref, dxm_ref, dxmb_ref, dmix_ref, dg_ref):
        @pl.when(pl.program_id(0) == 0)
        def _():
            dg_ref[...] = jnp.zeros_like(dg_ref)

        dxv = dx_ref[...]
        dact = _dot(dxv.astype(CDT), wdT_ref[...])
        gate = gate_ref[...]
        up = up_ref[...]
        sg = _sigmoid(gate)
        sil = gate * sg
        dgate = ((dact * up) * (sg * (1.0 + gate * (1.0 - sg)))).astype(CDT)
        dup = (dact * sil).astype(CDT)
        dgate_ref[...] = dgate
        dup_ref[...] = dup
        dh2 = _dot(dgate, wgT_ref[...]) + _dot(dup, wuT_ref[...])
        xm = xm_ref[...]
        r = lax.rsqrt(jnp.mean(xm * xm, axis=-1, keepdims=True) + EPS)
        dxn, dgx = _rms_bwd(xm, r, g_ref[...], dh2)
        dg_ref[...] += jnp.sum(dgx, axis=0, keepdims=True)
        dxm = dxv + dxn
        dxm_ref[...] = dxm
        dxmb = dxm.astype(CDT)
        dxmb_ref[...] = dxmb
        dmix_ref[...] = _dot(dxmb, woT_ref[...])

    row = lambda n: pl.BlockSpec((tm, n), lambda i: (i, 0))
    return _launch(
        "b_mlp" if comm is None else "b_mlp_rs", body, (S // tm,),
        [row(D), row(D), row(FF), row(FF), _resident((1, D)),
         _resident((D, FF)), _resident((FF, D)), _resident((FF, D)), _resident((D, D))],
        [row(FF), row(FF), row(D), row(D), row(D), pl.BlockSpec((1, D), lambda i: (0, 0))],
        [jax.ShapeDtypeStruct((S, FF), CDT), jax.ShapeDtypeStruct((S, FF), CDT),
         jax.ShapeDtypeStruct((S, D), F32), jax.ShapeDtypeStruct((S, D), CDT),
         jax.ShapeDtypeStruct((S, D), F32), jax.ShapeDtypeStruct((1, D), F32)],
        [], (dx, xm, gate, up, g2, wdT, wgT, wuT, woT), ("arbitrary",), comm)


def _b_conv(dmix, c, proj, cw, lg, lb, tm):
    S = proj.shape[0]
    CW = cw.shape[1]
    RB = 64
    hb = tm // HALO
    nt = S // tm
    last_h = S // HALO - 1

    def body(du_ref, duh_ref, c_ref, ch_ref, ab_ref, abh_ref, cw_ref, lg_ref, lb_ref,
             dab_ref, dcw_ref, dcb_ref, dlg_ref, dlb_ref, ext_u, ext_dc):
        i = pl.program_id(0)

        @pl.when(i == 0)
        def _():
            dcw_ref[...] = jnp.zeros_like(dcw_ref)
            dcb_ref[...] = jnp.zeros_like(dcb_ref)
            dlg_ref[...] = jnp.zeros_like(dlg_ref)
            dlb_ref[...] = jnp.zeros_like(dlb_ref)

        def ln_bwd(cv, du):
            mu = jnp.mean(cv, axis=-1, keepdims=True)
            var = jnp.mean(jnp.square(cv - mu), axis=-1, keepdims=True)
            rstd = lax.rsqrt(var + EPS)
            n = (cv - mu) * rstd
            z = n * lg_ref[...] + lb_ref[...]
            sz = _sigmoid(z)
            dz = du * (sz * (1.0 + z * (1.0 - sz)))
            dn = dz * lg_ref[...]
            dc = rstd * (dn - jnp.mean(dn, axis=-1, keepdims=True)
                         - n * jnp.mean(dn * n, axis=-1, keepdims=True))
            return dc, dz, n

        hv = abh_ref[...]
        ext_u[0:HALO, :] = jnp.where(i > 0, hv[:, :CW] * _sigmoid(hv[:, CW:]), 0.0)
        av = ab_ref[...]
        sb = _sigmoid(av[:, CW:])
        ext_u[HALO:, :] = av[:, :CW] * sb
        dc, dz, n = ln_bwd(c_ref[...], du_ref[...])
        ext_dc[0:tm, :] = dc
        dch, _, _ = ln_bwd(ch_ref[...], duh_ref[...])
        ext_dc[tm:, :] = jnp.where(i < nt - 1, dch, 0.0)
        dlg_ref[...] += jnp.sum(dz * n, axis=0, keepdims=True)
        dlb_ref[...] += jnp.sum(dz, axis=0, keepdims=True)
        dcb_ref[...] += jnp.sum(dc, axis=0, keepdims=True)

        for k in range(CONV_K):
            o = HALO - (CONV_K - 1) + k
            dcw_ref[k:k + 1, :] += jnp.sum(ext_dc[0:tm, :] * ext_u[o:o + tm, :], axis=0, keepdims=True)

        for rb in range(tm // RB):
            acc = jnp.zeros((RB, CW), F32)
            for k in range(CONV_K):
                o = rb * RB + (CONV_K - 1) - k
                acc = acc + cw_ref[k:k + 1, :] * ext_dc[o:o + RB, :]
            rs = slice(rb * RB, (rb + 1) * RB)
            a_r = ab_ref[rs, 0:CW]
            s_r = _sigmoid(ab_ref[rs, CW:2 * CW])
            dab_ref[rs, 0:CW] = (acc * s_r).astype(CDT)
            dab_ref[rs, CW:2 * CW] = (acc * a_r * (s_r * (1.0 - s_r))).astype(CDT)

    tile = lambda n, j: pl.BlockSpec((tm, n), lambda i: (i, j))
    nxt = lambda n, j: pl.BlockSpec((HALO, n), lambda i: (jnp.minimum((i + 1) * hb, last_h), j))
    return _call(
        body, name="b_conv", grid=(nt,),
        in_specs=[tile(CW, 0), nxt(CW, 0), tile(CW, 0), nxt(CW, 0),
                  tile(2 * CW, 0),
                  pl.BlockSpec((HALO, 2 * CW), lambda i: (jnp.maximum(i * hb - 1, 0), 0)),
                  _resident((HALO, CW)), _resident((1, CW)), _resident((1, CW))],
        out_specs=[tile(2 * CW, 0),
                   pl.BlockSpec((HALO, CW), lambda i: (0, 0)), pl.BlockSpec((1, CW), lambda i: (0, 0)),
                   pl.BlockSpec((1, CW), lambda i: (0, 0)), pl.BlockSpec((1, CW), lambda i: (0, 0))],
        out_shape=[jax.ShapeDtypeStruct((S, 2 * CW), CDT),
                   jax.ShapeDtypeStruct((HALO, CW), F32), jax.ShapeDtypeStruct((1, CW), F32),
                   jax.ShapeDtypeStruct((1, CW), F32), jax.ShapeDtypeStruct((1, CW), F32)],
        scratch_shapes=[pltpu.VMEM((tm + HALO, CW), F32), pltpu.VMEM((tm + HALO, CW), F32)],
        compiler_params=_cp(("arbitrary",)),
    )(dmix, dmix, c, c, proj, proj, cw, lg, lb)


def _b_ret(dmix, dab, proj, rraw, states, cosT, sinT, Mt, qd, kd, gs, gn):
    S = proj.shape[0]
    RW = HEADS * HD
    NB = S // SC
    scale = HD ** -0.5

    def body(dro_ref, dab_ref, q_ref, k_ref, v_ref, g_ref, rraw_ref, st_ref, c_ref, s_ref,
             m_ref, qd_ref, kd_ref, gs_ref, gn_ref, dp_ref, dgn_ref, G):
        @pl.when(pl.program_id(0) == 0)
        def _():
            G[...] = jnp.zeros_like(G)
            dgn_ref[...] = jnp.zeros_like(dgn_ref)

        dp_ref[:, 0:2 * RW] = dab_ref[...]
        cv = c_ref[...]
        sv = s_ref[...]
        for h in range(HEADS):
            cs = slice(h * HD, (h + 1) * HD)
            q = _rot(q_ref[:, cs], cv, sv)
            k = _rot(k_ref[:, cs], cv, sv) * scale
            qb = q.astype(CDT)
            kb = k.astype(CDT)
            vb = v_ref[:, cs].astype(CDT)
            spb = st_ref[0, h]
            r = rraw_ref[:, cs]
            mu = jnp.mean(r, axis=-1, keepdims=True)
            var = jnp.mean(jnp.square(r - mu), axis=-1, keepdims=True)
            rstd = lax.rsqrt(var + EPS)
            n = (r - mu) * rstd
            gv = g_ref[:, cs]
            sg = _sigmoid(gv)
            sil = gv * sg
            dro = dro_ref[:, cs]
            gnv = gn_ref[:, cs]
            dgn_ref[:, cs] += jnp.sum(dro * n * sil, axis=0, keepdims=True)
            dgate = dro * (n * gnv) * (sg * (1.0 + gv * (1.0 - sg)))
            dn = dro * gnv * sil
            dr = rstd * (dn - jnp.mean(dn, axis=-1, keepdims=True)
                         - n * jnp.mean(dn * n, axis=-1, keepdims=True))
            drb = dr.astype(CDT)
            mh = m_ref[h]
            ab = (_dot_nt(qb, kb) * mh).astype(CDT)
            dab_ = (_dot_nt(drb, vb) * mh).astype(CDT)
            qdb = (q * qd_ref[h]).astype(CDT)
            kdb = (k * kd_ref[h]).astype(CDT)
            gc = G[h]
            gb = gc.astype(CDT)
            dq = _dot(dab_, kb) + _dot_nt(drb, spb) * qd_ref[h]
            dk = _dot_tn(dab_, qb) + _dot_nt(vb, gb) * kd_ref[h]
            dv = _dot_tn(ab, drb) + _dot(kdb, gb)
            G[h] = gs_ref[h, 0:1, :] * gc + _dot_tn(qdb, drb)
            dk = dk * scale
            dqp = dq * cv + pltpu.roll(dq * sv, HD // 2, 1)
            dkp = dk * cv + pltpu.roll(dk * sv, HD // 2, 1)
            base = 2 * RW
            dp_ref[:, base + h * HD:base + (h + 1) * HD] = dqp.astype(CDT)
            dp_ref[:, base + RW + h * HD:base + RW + (h + 1) * HD] = dkp.astype(CDT)
            dp_ref[:, base + 2 * RW + h * HD:base + 2 * RW + (h + 1) * HD] = dv.astype(CDT)
            dp_ref[:, base + 3 * RW + h * HD:base + 3 * RW + (h + 1) * HD] = dgate.astype(CDT)

    rev = lambda n, j: pl.BlockSpec((SC, n), lambda i: (NB - 1 - i, j))
    return _call(
        body, name="b_ret", grid=(NB,),
        in_specs=[rev(RW, 1), rev(2 * RW, 0), rev(RW, 2), rev(RW, 3), rev(RW, 4), rev(RW, 5), rev(RW, 0),
                  pl.BlockSpec((1, HEADS, HD, HD), lambda i: (NB - 1 - i, 0, 0, 0)),
                  rev(HD, 0), rev(HD, 0),
                  _resident((HEADS, SC, SC)), _resident((HEADS, SC, HD)), _resident((HEADS, SC, HD)),
                  _resident((HEADS, 8, HD)), _resident((1, RW))],
        out_specs=[rev(6 * RW, 0), pl.BlockSpec((1, RW), lambda i: (0, 0))],
        out_shape=[jax.ShapeDtypeStruct((S, 6 * RW), CDT), jax.ShapeDtypeStruct((1, RW), F32)],
        scratch_shapes=[pltpu.VMEM((HEADS, HD, HD), F32)],
        compiler_params=_cp(("arbitrary",)),
    )(dmix, dab, proj, proj, proj, proj, rraw, states, cosT, sinT, Mt, qd, kd, gs, gn)


def _b_in(dproj, winT, x, g1, dxm, tm, comm=None):
    S, D = x.shape
    N = dproj.shape[1]

    def body(dp_ref, w_ref, x_ref, g_ref, dxm_ref, dx_ref, dxb_ref, dg_ref):
        @pl.when(pl.program_id(0) == 0)
        def _():
            dg_ref[...] = jnp.zeros_like(dg_ref)

        dh = _dot(dp_ref[...], w_ref[...])
        xv = x_ref[...]
        r = lax.rsqrt(jnp.mean(xv * xv, axis=-1, keepdims=True) + EPS)
        dxn, dgx = _rms_bwd(xv, r, g_ref[...], dh)
        dg_ref[...] += jnp.sum(dgx, axis=0, keepdims=True)
        dx = dxm_ref[...] + dxn
        dx_ref[...] = dx
        dxb_ref[...] = dx.astype(CDT)

    row = lambda n: pl.BlockSpec((tm, n), lambda i: (i, 0))
    return _launch(
        "b_in" if comm is None else "b_in_rs", body, (S // tm,),
        [row(N), _resident((N, D)), row(D), _resident((1, D)), row(D)],
        [row(D), row(D), pl.BlockSpec((1, D), lambda i: (0, 0))],
        [jax.ShapeDtypeStruct((S, D), F32), jax.ShapeDtypeStruct((S, D), CDT),
         jax.ShapeDtypeStruct((1, D), F32)],
        [], (dproj, winT, x, g1, dxm), ("arbitrary",), comm)


def _dw_tn(a, b, tm, tk):
    S, M = a.shape
    N = b.shape[1]

    def body(a_ref, b_ref, o_ref):
        @pl.when(pl.program_id(1) == 0)
        def _():
            o_ref[...] = jnp.zeros_like(o_ref)

        o_ref[...] += _dot_tn(a_ref[...], b_ref[...])

    return _call(
        body, name="dw_tn", grid=(M // tm, S // tk),
        in_specs=[pl.BlockSpec((tk, tm), lambda m, k: (k, m)), pl.BlockSpec((tk, N), lambda m, k: (k, 0))],
        out_specs=pl.BlockSpec((tm, N), lambda m, k: (m, 0)),
        out_shape=jax.ShapeDtypeStruct((M, N), F32),
        compiler_params=_cp(("parallel", "arbitrary")),
    )(a, b)


def _add_pair(parts, recv):
    K = len(parts)
    C = parts[0].shape[2]
    halves = 2

    def body(*refs):
        cc = lax.axis_index("c")
        for k in range(K):
            s = refs[k][cc] + refs[K + k][...]
            refs[2 * K + k][...] = s
            refs[3 * K + k][...] = s.astype(CDT)

    ns = [p.shape[1] for p in parts]
    in_specs = [pl.BlockSpec((None, 2, n // halves, C), lambda q, r: (q, 0, r, 0)) for n in ns]
    in_specs += [pl.BlockSpec((None, n // halves, C), lambda q, r: (q, r, 0)) for n in ns]
    outb = [pl.BlockSpec((None, n // halves, C), lambda q, r: (q, r, 0)) for n in ns]
    res = _call(
        body, name="add_pair", grid=(N_DEV // 2, halves),
        in_specs=in_specs, out_specs=outb + outb,
        out_shape=[jax.ShapeDtypeStruct((N_DEV // 2, n, C), F32) for n in ns]
        + [jax.ShapeDtypeStruct((N_DEV // 2, n, C), CDT) for n in ns],
        compiler_params=_cp(("parallel", "parallel")),
    )(*[p.reshape(N_DEV // 2, 2, p.shape[1], C) for p in parts], *recv)
    return list(res[:K]), list(res[K:])


def _sum_chips(sums, recv):
    K = len(sums)

    def body(*refs):
        chip = 2 * lax.axis_index("x") + lax.axis_index("y")
        for k in range(K):
            r = refs[K + k]
            refs[2 * K + k][...] = ((refs[k][chip] + r[0].astype(F32)) + r[1].astype(F32)) + r[2].astype(F32)

    vm = pl.BlockSpec(memory_space=pltpu.VMEM)
    res = _call(
        body, name="sum_chips", in_specs=[vm] * (2 * K), out_specs=[vm] * K,
        out_shape=[jax.ShapeDtypeStruct(s.shape[1:], F32) for s in sums],
        compiler_params=_cp(),
    )(*sums, *recv)
    return list(res)


def _adamw(w, g, m, v, tr):
    R, C = w.shape
    c1 = 1.0 - ADAM_B1 ** ADAM_STEP
    c2 = 1.0 - ADAM_B2 ** ADAM_STEP

    def body(w_ref, g_ref, m_ref, v_ref, d_ref, mo_ref, vo_ref):
        gv = g_ref[...]
        mn = ADAM_B1 * m_ref[...] + (1.0 - ADAM_B1) * gv
        vn = ADAM_B2 * v_ref[...] + (1.0 - ADAM_B2) * jnp.square(gv)
        mo_ref[...] = mn
        vo_ref[...] = vn
        d_ref[...] = -ADAM_LR * ((mn / c1) / (jnp.sqrt(vn / c2) + ADAM_EPS) + ADAM_WD * w_ref[...])

    blk = pl.BlockSpec((tr, C), lambda i: (i, 0))
    sh = jax.ShapeDtypeStruct((R, C), F32)
    return _call(
        body, name="adamw", grid=(R // tr,),
        in_specs=[blk, blk, blk, blk], out_specs=[blk, blk, blk], out_shape=[sh, sh, sh],
        compiler_params=_cp(("parallel",)),
    )(w, g, m, v)


def _coords():
    return lax.axis_index("x"), lax.axis_index("y"), lax.axis_index("c")


def _peer(x, y, c, d):
    return (x ^ (d >> 2), y ^ ((d >> 1) & 1), c ^ (d & 1))


def _ag_comm(p):
    R, C = p.shape

    def plan(cins, couts, sems):
        x_ref, out_ref = cins[0], couts[0]
        send_sems, recv_sems, local_sem = sems
        x, y, c = _coords()
        me, sibling = (x, y, c), (x, y, 1 - c)
        chips = [(1 - x, y), (x, 1 - y), (1 - x, 1 - y)]

        def rows(px, py, pc):
            return out_ref.at[pl.ds((4 * px + 2 * py + pc) * R, R), :]

        def copy(k, block, to, src=None):
            return pltpu.make_async_remote_copy(
                src_ref=rows(*block) if src is None else src, dst_ref=rows(*block),
                send_sem=send_sems.at[k], recv_sem=recv_sems.at[k], device_id=to, device_id_type=MESH)

        mine = pltpu.make_async_copy(x_ref, rows(*me), local_sem)
        first = [copy(0, me, sibling, src=x_ref)]
        first += [copy(1 + j, me, (*chip, c), src=x_ref) for j, chip in enumerate(chips)]
        passed = [copy(4 + j, (*chip, c), sibling) for j, chip in enumerate(chips)]
        got_ici = [copy(1 + j, (*chip, c), me) for j, chip in enumerate(chips)]
        got_d2d = [copy(0, sibling, me)] + [copy(4 + j, (*chip, 1 - c), me) for j, chip in enumerate(chips)]
        return mine, first, passed, got_ici, got_d2d

    def start(*a):
        mine, first, _, _, _ = plan(*a)
        mine.start()
        for cp in first:
            cp.start()

    def mid(*a):
        _, _, passed, got_ici, _ = plan(*a)
        for got, fwd in zip(got_ici, passed):
            got.wait_recv()
            fwd.start()

    def finish(*a):
        mine, first, passed, _, got_d2d = plan(*a)
        for got in got_d2d:
            got.wait_recv()
        for cp in first + passed:
            cp.wait_send()
        mine.wait()

    return _Comm([p], [jax.ShapeDtypeStruct((N_DEV * R, C), p.dtype)],
                 [pltpu.SemaphoreType.DMA((7,)), pltpu.SemaphoreType.DMA((7,)), pltpu.SemaphoreType.DMA],
                 start, mid, finish)


def _rs_pair_comm(parts):
    K = len(parts)

    def plan(cins, couts, sems):
        send_sems, recv_sems = sems
        x, y, c = _coords()
        sibling = (x, y, 1 - c)
        cps = []
        for k in range(K):
            for q in range(N_DEV // 2):
                cps.append(pltpu.make_async_remote_copy(
                    src_ref=cins[k].at[2 * q + (1 - c)], dst_ref=couts[k].at[q],
                    send_sem=send_sems.at[k], recv_sem=recv_sems.at[k], device_id=sibling, device_id_type=MESH))
        whole = [pltpu.make_async_remote_copy(
            src_ref=couts[k], dst_ref=couts[k], send_sem=send_sems.at[k], recv_sem=recv_sems.at[k],
            device_id=sibling, device_id_type=MESH) for k in range(K)]
        return cps, whole

    def start(*a):
        for cp in plan(*a)[0]:
            cp.start()

    def finish(*a):
        for w in plan(*a)[1]:
            w.wait_recv()
            w.wait_send()

    return _Comm(parts, [jax.ShapeDtypeStruct((N_DEV // 2,) + p.shape[1:], p.dtype) for p in parts],
                 [pltpu.SemaphoreType.DMA((K,)), pltpu.SemaphoreType.DMA((K,))], start, None, finish)


def _rs_chip_comm(sums):
    K = len(sums)

    def plan(cins, couts, sems):
        send_sems, recv_sems = sems
        x, y, c = _coords()
        cps = []
        for d in range(1, N_DEV // 2):
            px, py = x ^ (d >> 1), y ^ (d & 1)
            for k in range(K):
                s = (d - 1) * K + k
                cps.append(pltpu.make_async_remote_copy(
                    src_ref=cins[k].at[2 * px + py], dst_ref=couts[k].at[d - 1],
                    send_sem=send_sems.at[s], recv_sem=recv_sems.at[s], device_id=(px, py, c), device_id_type=MESH))
        return cps

    def start(*a):
        for cp in plan(*a):
            cp.start()

    def finish(*a):
        cps = plan(*a)
        for cp in cps:
            cp.wait_recv()
        for cp in cps:
            cp.wait_send()

    n_sem = (N_DEV // 2 - 1) * K
    return _Comm(sums, [jax.ShapeDtypeStruct((N_DEV // 2 - 1,) + s.shape[1:], s.dtype) for s in sums],
                 [pltpu.SemaphoreType.DMA((n_sem,)), pltpu.SemaphoreType.DMA((n_sem,))], start, None, finish)


def _gather_small(v, reduce):
    R, C = v.shape

    def exchange(v_ref, buf, send_sems, recv_sems):
        x, y, c = _coords()
        me = 4 * x + 2 * y + c
        buf[me] = v_ref[...]
        cps = []
        for d in range(1, N_DEV):
            cp = pltpu.make_async_remote_copy(
                src_ref=v_ref, dst_ref=buf.at[me], send_sem=send_sems.at[d - 1], recv_sem=recv_sems.at[d - 1],
                device_id=_peer(x, y, c, d), device_id_type=MESH)
            cp.start()
            cps.append(cp)
        for cp in cps:
            cp.wait_recv()
        for cp in cps:
            cp.wait_send()

    sems = [pltpu.SemaphoreType.DMA((7,)), pltpu.SemaphoreType.DMA((7,))]
    vm = pl.BlockSpec(memory_space=pltpu.VMEM)
    if reduce:
        def body(v_ref, o_ref, buf, send_sems, recv_sems):
            exchange(v_ref, buf, send_sems, recv_sems)
            acc = buf[0]
            for s in range(1, N_DEV):
                acc = acc + buf[s]
            o_ref[...] = acc

        return _call(body, name="allreduce_small", in_specs=[vm], out_specs=vm,
                     out_shape=jax.ShapeDtypeStruct((R, C), F32),
                     scratch_shapes=[pltpu.VMEM((N_DEV, R, C), F32)] + sems)(v)

    def body(v_ref, o_ref, send_sems, recv_sems):
        exchange(v_ref, o_ref, send_sems, recv_sems)

    return _call(body, name="allgather_small", in_specs=[vm], out_specs=vm,
                 out_shape=jax.ShapeDtypeStruct((N_DEV, R, C), F32), scratch_shapes=sems)(v)


def _tables(S):
    half = HD // 2
    pos = jnp.arange(S, dtype=F32)
    freqs = ROPE_BASE ** (-jnp.arange(half, dtype=F32) / half)
    ang = pos[:, None] * freqs[None, :]
    cos, sin = jnp.cos(ang), jnp.sin(ang)
    cosT = jnp.concatenate([cos, cos], axis=-1)
    sinT = jnp.concatenate([-sin, sin], axis=-1)
    log_g = jnp.log(1.0 - 2.0 ** (-5.0 - jnp.arange(HEADS, dtype=F32)))
    idx = jnp.arange(SC, dtype=F32)
    ci = jnp.arange(SC) // CHUNK
    diff = idx[:, None] - idx[None, :]
    same = ci[:, None] == ci[None, :]
    earlier = ci[None, :] < ci[:, None]
    expo = jnp.where(same, jnp.abs(diff), diff)
    Mt = jnp.where((same | earlier)[None], jnp.exp(log_g[:, None, None] * expo[None]), 0.0)
    ones = jnp.ones((1, 1, HD), F32)
    qd = jnp.exp(log_g[:, None] * (idx + 1.0)[None, :])[:, :, None] * ones
    kd = jnp.exp(log_g[:, None] * (SC - 1.0 - idx)[None, :])[:, :, None] * ones
    gs = jnp.exp(log_g * SC)[:, None, None] * jnp.ones((1, 8, HD), F32)
    return cosT, sinT, Mt, qd, kd, gs


def _pad_rows(a, rows):
    return jnp.pad(a, ((0, rows - a.shape[0]), (0, 0)))


def kernel(x, norm1_g, w_in, conv_w, conv_b, conv_ln_g, conv_ln_b, ret_gn_g, w_out, norm2_g, w_gate, w_up, w_down, final_g, loss_target, m_norm1_g, m_w_in, m_conv_w, m_conv_b, m_conv_ln_g, m_conv_ln_b, m_ret_gn_g, m_w_out, m_norm2_g, m_w_gate, m_w_up, m_w_down, m_final_g, v_norm1_g, v_w_in, v_conv_w, v_conv_b, v_conv_ln_g, v_conv_ln_b, v_ret_gn_g, v_w_out, v_norm2_g, v_w_gate, v_w_up, v_w_down, v_final_g):
    L, D, n_in = w_in.shape
    n_out = w_out.shape[1]
    n_ff = w_gate.shape[2]
    S = x.shape[1]
    CW = conv_b.shape[1]
    IN, FF = N_DEV * n_in, N_DEV * n_ff
    ncw = conv_w.shape[2]
    x0 = x.reshape(S, D)
    tgt = loss_target.reshape(S, D)
    TM = min(512, S)
    TMM = min(256, S)

    sizes = [n_in, n_out, n_ff, n_ff, n_ff]
    offs = [0]
    for n in sizes[:-1]:
        offs.append(offs[-1] + n)
    RL = sum(sizes)
    pack = jnp.concatenate([jnp.swapaxes(w_in, 1, 2), w_out, jnp.swapaxes(w_gate, 1, 2),
                            jnp.swapaxes(w_up, 1, 2), w_down], axis=1).astype(CDT)
    Gl = _comm_only(_ag_comm(pack[0]), "ag_first")[0]

    def full(g, k):
        return g.reshape(N_DEV, RL, D)[:, offs[k]:offs[k] + sizes[k], :].reshape(N_DEV * sizes[k], D)

    cwp = conv_w.reshape(L * CONV_K * ncw // 128, 128)
    cw_rows = -(-cwp.shape[0] // 8) * 8
    cwg = _gather_small(_pad_rows(cwp, cw_rows), reduce=False)[:, :cwp.shape[0], :]
    conv_w_full = jnp.moveaxis(cwg.reshape(N_DEV, L, CONV_K, ncw), 0, 2).reshape(L, CONV_K, CW)

    cosT, sinT, Mt, qd, kd, gs = _tables(S)

    saved = []
    xl = x0
    for l in range(L):
        winT, wo, wgT, wuT, wd = (full(Gl, k) for k in range(5))
        win, woT, wg, wu, wdT = winT.T, wo.T, wgT.T, wuT.T, wd.T
        cw = _pad_rows(conv_w_full[l], HALO)
        h, proj = _f_in(xl, norm1_g[l][None], win, TM)
        c, u = _f_conv(proj, cw, conv_b[l][None], conv_ln_g[l][None], conv_ln_b[l][None], TM)
        rraw, states, mixed = _f_ret(proj, u, cosT, sinT, Mt, qd, kd, gs, ret_gn_g[l][None])
        (xm, h2, gate, up, act, xo), nxt = _f_mlp(xl, mixed, wo, norm2_g[l][None], wg, wu, wd, TMM,
                                                 _ag_comm(pack[l + 1]) if l + 1 < L else None)
        if nxt:
            Gl = nxt[0]
        saved.append(dict(x=xl, h=h, proj=proj, c=c, rraw=rraw, states=states, mixed=mixed, xm=xm, h2=h2,
                          gate=gate, up=up, act=act, cw=cw, winT=winT, woT=woT, wgT=wgT, wuT=wuT, wdT=wdT))
        xl = xo

    dx, dxb, loss_p, dfg = _f_loss(xl, final_g[None], tgt, TM)

    small = []
    chip_sums = [None] * L
    chip_recv = [None] * L
    in_flight = None
    for l in reversed(range(L)):
        sv = saved[l]
        (dgate, dup, dxm, dxmb, dmix, dg2), got = _b_mlp(
            dx, sv["xm"], sv["gate"], sv["up"], norm2_g[l][None], sv["wdT"], sv["wgT"], sv["wuT"], sv["woT"], TMM,
            in_flight)
        if got:
            chip_recv[l + 1] = got
        d_wd = _dw_tn(sv["act"], dxb, FF // 2, TM)
        d_wgT = _dw_tn(dgate, sv["h2"], FF // 2, TM)
        d_wuT = _dw_tn(dup, sv["h2"], FF // 2, TM)
        d_wo = _dw_tn(sv["mixed"], dxmb, D // 2, TM)
        dab, dcw, dcb, dlg, dlb = _b_conv(dmix, sv["c"], sv["proj"], sv["cw"], conv_ln_g[l][None],
                                          conv_ln_b[l][None], TM)
        dproj, dgn = _b_ret(dmix, dab, sv["proj"], sv["rraw"], sv["states"], cosT, sinT, Mt, qd, kd, gs,
                            ret_gn_g[l][None])
        d_winT = _dw_tn(dproj, sv["h"], IN // 2, TM)
        parts = [d.reshape(N_DEV, -1, D) for d in (d_winT, d_wo, d_wgT, d_wuT, d_wd)]
        (dx, dxb, dg1), pair_recv = _b_in(dproj, sv["winT"], sv["x"], norm1_g[l][None], dxm, TM,
                                          _rs_pair_comm(parts))
        chip_sums[l], sums_b = _add_pair(parts, pair_recv)
        in_flight = _rs_chip_comm(sums_b)
        small.append(jnp.concatenate([dcw, dcb, dlg, dlb, dgn, dg1.reshape(2, CW), dg2.reshape(2, CW)], axis=0))
    chip_recv[0] = _comm_only(in_flight, "rs_last")
    small = small[::-1]
    grad_x = dx.reshape(1, S, D)

    rows_l = HALO + 8
    loss_row = jnp.zeros((1, CW), F32).at[0, 0].set(loss_p[0, 0])
    sm = jnp.concatenate(small + [dfg.reshape(2, CW), loss_row], axis=0)
    sm_rows = -(-sm.shape[0] // 8) * 8
    sm = _gather_small(_pad_rows(sm, sm_rows), reduce=True)
    loss = sm[L * rows_l + 2, 0]
    g_final = sm[L * rows_l:L * rows_l + 2].reshape(D)
    per = sm[:L * rows_l].reshape(L, rows_l, CW)
    me = 4 * lax.axis_index("x") + 2 * lax.axis_index("y") + lax.axis_index("c")
    g_conv_w = lax.dynamic_slice_in_dim(per[:, :CONV_K, :], me * ncw, ncw, axis=2)
    g_conv_b, g_ln_g, g_ln_b, g_gn = per[:, HALO], per[:, HALO + 1], per[:, HALO + 2], per[:, HALO + 3]
    g_n1 = per[:, HALO + 4:HALO + 6].reshape(L, D)
    g_n2 = per[:, HALO + 6:HALO + 8].reshape(L, D)

    gl = [_sum_chips(chip_sums[l], chip_recv[l]) for l in range(L)]
    g_w_in = jnp.stack([gl[l][0].T for l in range(L)])
    g_w_out = jnp.stack([gl[l][1] for l in range(L)])
    g_w_gate = jnp.stack([gl[l][2].T for l in range(L)])
    g_w_up = jnp.stack([gl[l][3].T for l in range(L)])
    g_w_down = jnp.stack([gl[l][4] for l in range(L)])

    def big(w, g, m, v):
        sh = w.shape
        two = lambda a: a.reshape(-1, sh[-1])
        rows = two(w).shape[0]
        d, mn, vn = _adamw(two(w), two(g), two(m), two(v), rows // 8)
        return d.reshape(sh), mn.reshape(sh), vn.reshape(sh)

    names = ["norm1_g", "conv_w", "conv_b", "conv_ln_g", "conv_ln_b", "ret_gn_g", "norm2_g", "final_g"]
    sw = dict(norm1_g=(norm1_g, g_n1, m_norm1_g, v_norm1_g), conv_w=(conv_w, g_conv_w, m_conv_w, v_conv_w),
              conv_b=(conv_b, g_conv_b, m_conv_b, v_conv_b), conv_ln_g=(conv_ln_g, g_ln_g, m_conv_ln_g, v_conv_ln_g),
              conv_ln_b=(conv_ln_b, g_ln_b, m_conv_ln_b, v_conv_ln_b), ret_gn_g=(ret_gn_g, g_gn, m_ret_gn_g, v_ret_gn_g),
              norm2_g=(norm2_g, g_n2, m_norm2_g, v_norm2_g), final_g=(final_g, g_final, m_final_g, v_final_g))
    lens = [int(math.prod(sw[n][0].shape)) for n in names]
    tot = sum(lens)
    prow = -(-tot // (8 * CW)) * 8

    def packs(j):
        flat = jnp.concatenate([sw[n][j].reshape(-1) for n in names])
        return jnp.pad(flat, (0, prow * CW - tot)).reshape(prow, CW)

    sd, smn, svn = _adamw(packs(0), packs(1), packs(2), packs(3), prow)

    def unpack(a):
        flat = a.reshape(-1)
        out, o = {}, 0
        for n, ln in zip(names, lens):
            out[n] = flat[o:o + ln].reshape(sw[n][0].shape)
            o += ln
        return out

    sd, smn, svn = unpack(sd), unpack(smn), unpack(svn)
    res = {n: (sw[n][1], sd[n], smn[n], svn[n]) for n in names}
    res["w_in"] = (g_w_in,) + big(w_in, g_w_in, m_w_in, v_w_in)
    res["w_out"] = (g_w_out,) + big(w_out, g_w_out, m_w_out, v_w_out)
    res["w_gate"] = (g_w_gate,) + big(w_gate, g_w_gate, m_w_gate, v_w_gate)
    res["w_up"] = (g_w_up,) + big(w_up, g_w_up, m_w_up, v_w_up)
    res["w_down"] = (g_w_down,) + big(w_down, g_w_down, m_w_down, v_w_down)

    order = ["norm1_g", "w_in", "conv_w", "conv_b", "conv_ln_g", "conv_ln_b", "ret_gn_g", "w_out", "norm2_g",
             "w_gate", "w_up", "w_down", "final_g"]
    return (loss, grad_x, *[res[n][0] for n in order], *[res[n][1] for n in order],
            *[res[n][2] for n in order], *[res[n][3] for n in order])
```

```python
import functools
import math

import jax
import jax.numpy as jnp
from jax import lax
from jax.experimental import pallas as pl
from jax.experimental.pallas import tpu as pltpu

F32 = jnp.float32
CDT = jnp.bfloat16
EPS = 1e-6
CHUNK = 64
SC = 256
HEADS = 4
HD = 128
CONV_K = 31
HALO = 32
ROPE_BASE = 10000.0
ADAM_LR = 0.001
ADAM_B1 = 0.9
ADAM_B2 = 0.999
ADAM_EPS = 1e-08
ADAM_WD = 0.01
ADAM_STEP = 10
N_DEV = 8
MESH = pl.DeviceIdType.MESH
VMEM_LIMIT = 60 * 1024 * 1024


def _call(body, **kw):
    return pl.pallas_call(body, **kw)


def _cp(sem=None, vmem=VMEM_LIMIT):
    return pltpu.CompilerParams(dimension_semantics=sem, vmem_limit_bytes=vmem)


def _resident(shape):
    nd = len(shape)
    return pl.BlockSpec(shape, lambda *_: (0,) * nd, pipeline_mode=pl.Buffered(1))


def _dot(a, b):
    return jnp.dot(a, b, preferred_element_type=F32)


def _dot_nt(a, b):
    return lax.dot_general(a, b, (((1,), (1,)), ((), ())), preferred_element_type=F32)


def _dot_tn(a, b):
    return lax.dot_general(a, b, (((0,), (0,)), ((), ())), preferred_element_type=F32)


def _sigmoid(x):
    return 1.0 / (1.0 + jnp.exp(-x))


def _rms_bwd(x, r, g, dy):
    xh = x * r
    dyg = dy * g
    dx = r * (dyg - xh * jnp.mean(dyg * xh, axis=-1, keepdims=True))
    return dx, dy * xh


class _Comm:
    def __init__(self, ins, out_shapes, sems, start, mid, finish):
        self.ins, self.out_shapes, self.sems = list(ins), list(out_shapes), list(sems)
        self.start, self.mid, self.finish = start, mid, finish


_ANY = pl.BlockSpec(memory_space=pl.ANY)


def _launch(name, compute, grid, in_specs, out_specs, out_shape, scratch, operands, sem, comm=None):
    n_in, n_out, n_sc = len(in_specs), len(out_specs), len(scratch)
    if comm is None:
        res = _call(compute, name=name, grid=grid, in_specs=in_specs, out_specs=out_specs, out_shape=out_shape,
                    scratch_shapes=scratch, compiler_params=_cp(sem))(*operands)
        return list(res), []
    c_in, c_out = len(comm.ins), len(comm.out_shapes)
    inner = grid[1] if len(grid) > 1 else 1
    steps = grid[0] * inner
    mid_step = (3 * steps) // 4

    def body(*refs):
        ins = refs[:n_in]
        cins = refs[n_in:n_in + c_in]
        o = n_in + c_in
        outs = refs[o:o + n_out]
        couts = refs[o + n_out:o + n_out + c_out]
        o += n_out + c_out
        sc = refs[o:o + n_sc]
        csem = refs[o + n_sc:]
        i = pl.program_id(0)
        if len(grid) > 1:
            i = i * inner + pl.program_id(1)

        @pl.when(i == 0)
        def _():
            comm.start(cins, couts, csem)

        if comm.mid is not None:
            @pl.when(i == mid_step)
            def _():
                comm.mid(cins, couts, csem)

        compute(*ins, *outs, *sc)

        @pl.when(i == steps - 1)
        def _():
            comm.finish(cins, couts, csem)

    res = _call(body, name=name, grid=grid, in_specs=list(in_specs) + [_ANY] * c_in,
                out_specs=list(out_specs) + [_ANY] * c_out, out_shape=list(out_shape) + comm.out_shapes,
                scratch_shapes=list(scratch) + comm.sems,
                compiler_params=_cp(("arbitrary",) * len(grid)))(*operands, *comm.ins)
    return list(res[:n_out]), list(res[n_out:])


def _comm_only(comm, name):
    c_in, c_out = len(comm.ins), len(comm.out_shapes)

    def body(*refs):
        cins, couts, csem = refs[:c_in], refs[c_in:c_in + c_out], refs[c_in + c_out:]
        comm.start(cins, couts, csem)
        if comm.mid is not None:
            comm.mid(cins, couts, csem)
        comm.finish(cins, couts, csem)

    res = _call(body, name=name, in_specs=[_ANY] * c_in, out_specs=[_ANY] * c_out, out_shape=comm.out_shapes,
                scratch_shapes=comm.sems)(*comm.ins)
    return list(res)


def _wspec(ndev, n, D, part):
    return pl.BlockSpec((ndev, n, D), lambda i, c: (c, part, 0))


def _f_in(x, g, G, n_in, tm):
    S, D = x.shape
    nd = 2
    nc = N_DEV // nd

    def body(x_ref, g_ref, w_ref, h_ref, p_ref):
        @pl.when(pl.program_id(1) == 0)
        def _():
            xv = x_ref[...]
            r = lax.rsqrt(jnp.mean(xv * xv, axis=-1, keepdims=True) + EPS)
            h_ref[...] = ((xv * r) * g_ref[...]).astype(CDT)

        p_ref[...] = _dot_nt(h_ref[...], w_ref[...].reshape(nd * n_in, D))

    return _call(
        body, name="f_in", grid=(S // tm, nc),
        in_specs=[pl.BlockSpec((tm, D), lambda i, c: (i, 0)), pl.BlockSpec((1, D), lambda i, c: (0, 0)),
                  _wspec(nd, n_in, D, 0)],
        out_specs=[pl.BlockSpec((tm, D), lambda i, c: (i, 0)), pl.BlockSpec((tm, nd * n_in), lambda i, c: (i, c))],
        out_shape=[jax.ShapeDtypeStruct((S, D), CDT), jax.ShapeDtypeStruct((S, N_DEV * n_in), F32)],
        compiler_params=_cp(("parallel", "arbitrary")),
    )(x, g, G)


def _phases(ext, ph, rows):
    for p in range(8):
        ph[p, :, :] = ext[p:p + rows, :]


def _f_conv(proj, cw, cb, lg, lb, tm):
    S = proj.shape[0]
    CW = cw.shape[1]
    RB = 64
    hb = tm // HALO

    def body(ab_ref, halo_ref, cw_ref, cb_ref, lg_ref, lb_ref, c_ref, u_ref, ext, ph):
        i = pl.program_id(0)
        hv = halo_ref[...]
        u0h = hv[:, :CW] * _sigmoid(hv[:, CW:])
        ext[0:HALO, :] = jnp.where(i > 0, u0h, 0.0)
        av = ab_ref[...]
        ext[HALO:HALO + tm, :] = av[:, :CW] * _sigmoid(av[:, CW:])
        ext[HALO + tm:, :] = jnp.zeros((8, CW), F32)
        _phases(ext, ph, tm + HALO)
        for rb in range(tm // RB):
            acc = jnp.zeros((RB, CW), F32) + cb_ref[...]
            for k in range(CONV_K):
                o = rb * RB + HALO - (CONV_K - 1) + k
                acc = acc + cw_ref[k:k + 1, :] * ph[o % 8, o - o % 8:o - o % 8 + RB, :]
            c_ref[rb * RB:(rb + 1) * RB, :] = acc
            mu = jnp.mean(acc, axis=-1, keepdims=True)
            var = jnp.mean(jnp.square(acc - mu), axis=-1, keepdims=True)
            z = ((acc - mu) * lax.rsqrt(var + EPS)) * lg_ref[...] + lb_ref[...]
            u_ref[rb * RB:(rb + 1) * RB, :] = (z * _sigmoid(z)).astype(CDT)

    return _call(
        body, name="f_conv", grid=(S // tm,),
        in_specs=[pl.BlockSpec((tm, 2 * CW), lambda i: (i, 0)),
                  pl.BlockSpec((HALO, 2 * CW), lambda i: (jnp.maximum(i * hb - 1, 0), 0)),
                  _resident((HALO, CW)), _resident((1, CW)), _resident((1, CW)), _resident((1, CW))],
        out_specs=[pl.BlockSpec((tm, CW), lambda i: (i, 0)), pl.BlockSpec((tm, CW), lambda i: (i, 0))],
        out_shape=[jax.ShapeDtypeStruct((S, CW), F32), jax.ShapeDtypeStruct((S, CW), CDT)],
        scratch_shapes=[pltpu.VMEM((tm + HALO + 8, CW), F32), pltpu.VMEM((8, tm + HALO, CW), F32)],
        compiler_params=_cp(("parallel",)),
    )(proj, proj, cw, cb, lg, lb)


def _rot(t, c, s):
    return t * c + pltpu.roll(t, HD // 2, 1) * s


def _f_ret(proj, u, cosT, sinT, Mt, qd, kd, gs, gn):
    S = proj.shape[0]
    RW = HEADS * HD
    NB = S // SC
    scale = HD ** -0.5

    def body(q_ref, k_ref, v_ref, g_ref, u_ref, c_ref, s_ref, m_ref, qd_ref, kd_ref, gs_ref, gn_ref,
             rraw_ref, st_ref, mix_ref, state):
        @pl.when(pl.program_id(0) == 0)
        def _():
            state[...] = jnp.zeros_like(state)

        mix_ref[:, 0:RW] = u_ref[...]
        cv = c_ref[...]
        sv = s_ref[...]
        for h in range(HEADS):
            cs = slice(h * HD, (h + 1) * HD)
            q = _rot(q_ref[:, cs], cv, sv)
            k = _rot(k_ref[:, cs], cv, sv) * scale
            vb = v_ref[:, cs].astype(CDT)
            qb = q.astype(CDT)
            kb = k.astype(CDT)
            a = _dot_nt(qb, kb) * m_ref[h]
            sp = state[h]
            spb = sp.astype(CDT)
            st_ref[0, h] = spb
            r = _dot(a.astype(CDT), vb) + _dot((q * qd_ref[h]).astype(CDT), spb)
            kv = _dot_tn((k * kd_ref[h]).astype(CDT), vb)
            state[h] = gs_ref[h, 0:1, :] * sp + kv
            rraw_ref[:, cs] = r
            mu = jnp.mean(r, axis=-1, keepdims=True)
            var = jnp.mean(jnp.square(r - mu), axis=-1, keepdims=True)
            n = (r - mu) * lax.rsqrt(var + EPS)
            gv = g_ref[:, cs]
            mix_ref[:, RW + h * HD:RW + (h + 1) * HD] = ((n * gn_ref[:, cs]) * (gv * _sigmoid(gv))).astype(CDT)

    col = lambda j: pl.BlockSpec((SC, RW), lambda i: (i, j))
    return _call(
        body, name="f_ret", grid=(NB,),
        in_specs=[col(2), col(3), col(4), col(5),
                  pl.BlockSpec((SC, RW), lambda i: (i, 0)),
                  pl.BlockSpec((SC, HD), lambda i: (i, 0)), pl.BlockSpec((SC, HD), lambda i: (i, 0)),
                  _resident((HEADS, SC, SC)), _resident((HEADS, SC, HD)), _resident((HEADS, SC, HD)),
                  _resident((HEADS, 8, HD)), _resident((1, RW))],
        out_specs=[pl.BlockSpec((SC, RW), lambda i: (i, 0)),
                   pl.BlockSpec((1, HEADS, HD, HD), lambda i: (i, 0, 0, 0)),
                   pl.BlockSpec((SC, 2 * RW), lambda i: (i, 0))],
        out_shape=[jax.ShapeDtypeStruct((S, RW), F32),
                   jax.ShapeDtypeStruct((NB, HEADS, HD, HD), CDT),
                   jax.ShapeDtypeStruct((S, 2 * RW), CDT)],
        scratch_shapes=[pltpu.VMEM((HEADS, HD, HD), F32)],
        compiler_params=_cp(("arbitrary",)),
    )(proj, proj, proj, proj, u, cosT, sinT, Mt, qd, kd, gs, gn)


def _f_mlp(x, mixed, g2, G, n_out, n_ff, parts, tm, comm=None):
    S, D = x.shape
    nd = 4
    nc = N_DEV // nd
    FC = nd * n_ff
    p_wo, p_wg, p_wu, p_wd = parts

    def body(x_ref, m_ref, wo_ref, g_ref, wg_ref, wu_ref, wd_ref,
             xm_ref, h2_ref, gate_ref, up_ref, act_ref, xo_ref):
        c = pl.program_id(1)

        @pl.when(c == 0)
        def _():
            xm = x_ref[...] + _dot(m_ref[...], wo_ref[...].reshape(N_DEV * n_out, D))
            xm_ref[...] = xm
            r = lax.rsqrt(jnp.mean(xm * xm, axis=-1, keepdims=True) + EPS)
            h2_ref[...] = ((xm * r) * g_ref[...]).astype(CDT)
            xo_ref[...] = xm

        h2 = h2_ref[...]
        gate = _dot_nt(h2, wg_ref[...].reshape(FC, D))
        up = _dot_nt(h2, wu_ref[...].reshape(FC, D))
        gate_ref[...] = gate.astype(CDT)
        up_ref[...] = up.astype(CDT)
        act = ((gate * _sigmoid(gate)) * up).astype(CDT)
        act_ref[...] = act
        xo_ref[...] += _dot(act, wd_ref[...].reshape(FC, D))

    row = lambda n: pl.BlockSpec((tm, n), lambda i, c: (i, 0))
    row1 = lambda n: pl.BlockSpec((tm, n), lambda i, c: (i, 0), pipeline_mode=pl.Buffered(1))
    colb = pl.BlockSpec((tm, FC), lambda i, c: (i, c))
    return _launch(
        "f_mlp" if comm is None else "f_mlp_ag", body, (S // tm, nc),
        [row1(D), row1(D),
         pl.BlockSpec((N_DEV, n_out, D), lambda i, c: (0, p_wo, 0), pipeline_mode=pl.Buffered(1)),
         pl.BlockSpec((1, D), lambda i, c: (0, 0)),
         _wspec(nd, n_ff, D, p_wg), _wspec(nd, n_ff, D, p_wu), _wspec(nd, n_ff, D, p_wd)],
        [row(D), row(D), colb, colb, colb, row(D)],
        [jax.ShapeDtypeStruct((S, D), F32), jax.ShapeDtypeStruct((S, D), CDT),
         jax.ShapeDtypeStruct((S, N_DEV * n_ff), CDT), jax.ShapeDtypeStruct((S, N_DEV * n_ff), CDT),
         jax.ShapeDtypeStruct((S, N_DEV * n_ff), CDT), jax.ShapeDtypeStruct((S, D), F32)],
        [], (x, mixed, G, g2, G, G, G), ("parallel", "arbitrary"), comm)


def _f_loss(x, fg, tgt, tm):
    S, D = x.shape

    def body(x_ref, g_ref, t_ref, dx_ref, dxb_ref, loss_ref, dg_ref):
        @pl.when(pl.program_id(0) == 0)
        def _():
            loss_ref[...] = jnp.zeros_like(loss_ref)
            dg_ref[...] = jnp.zeros_like(dg_ref)

        xv = x_ref[...]
        r = lax.rsqrt(jnp.mean(xv * xv, axis=-1, keepdims=True) + EPS)
        y = (xv * r) * g_ref[...]
        e = y - t_ref[...]
        loss_ref[...] += 0.5 * jnp.sum(jnp.mean(e * e, axis=-1, keepdims=True))
        dy = e * (1.0 / D)
        dx, dgx = _rms_bwd(xv, r, g_ref[...], dy)
        dg_ref[...] += jnp.sum(dgx, axis=0, keepdims=True)
        dx_ref[...] = dx
        dxb_ref[...] = dx.astype(CDT)

    row = pl.BlockSpec((tm, D), lambda i: (i, 0))
    return _call(
        body, name="f_loss", grid=(S // tm,),
        in_specs=[row, _resident((1, D)), row],
        out_specs=[row, row, pl.BlockSpec((1, 128), lambda i: (0, 0)), pl.BlockSpec((1, D), lambda i: (0, 0))],
        out_shape=[jax.ShapeDtypeStruct((S, D), F32), jax.ShapeDtypeStruct((S, D), CDT),
                   jax.ShapeDtypeStruct((1, 128), F32), jax.ShapeDtypeStruct((1, D), F32)],
        compiler_params=_cp(("arbitrary",)),
    )(x, fg, tgt)


def _b_mlp(dxb, dx, xm, gate, up, g2, G, n_out, n_ff, parts, tm, comm=None):
    S, D = dx.shape
    nd = 4
    nc = N_DEV // nd
    FC = nd * n_ff
    p_wo, p_wg, p_wu, p_wd = parts

    def body(dxb_ref, dx_ref, xm_ref, gate_ref, up_ref, g_ref, wd_ref, wg_ref, wu_ref, wo_ref,
             dgate_ref, dup_ref, dxm_ref, dxmb_ref, dmix_ref, dg_ref, acc):
        c = pl.program_id(1)

        @pl.when((pl.program_id(0) == 0) & (c == 0))
        def _():
            dg_ref[...] = jnp.zeros_like(dg_ref)

        dact = _dot_nt(dxb_ref[...], wd_ref[...].reshape(FC, D))
        gate = gate_ref[...].astype(F32)
        up = up_ref[...].astype(F32)
        sg = _sigmoid(gate)
        sil = gate * sg
        dgate = ((dact * up) * (sg * (1.0 + gate * (1.0 - sg)))).astype(CDT)
        dup = (dact * sil).astype(CDT)
        dgate_ref[...] = dgate
        dup_ref[...] = dup
        dh2 = _dot(dgate, wg_ref[...].reshape(FC, D)) + _dot(dup, wu_ref[...].reshape(FC, D))

        @pl.when(c == 0)
        def _():
            acc[...] = dh2

        @pl.when(c > 0)
        def _():
            acc[...] += dh2

        @pl.when(c == nc - 1)
        def _():
            xm = xm_ref[...]
            r = lax.rsqrt(jnp.mean(xm * xm, axis=-1, keepdims=True) + EPS)
            dxn, dgx = _rms_bwd(xm, r, g_ref[...], acc[...])
            dg_ref[...] += jnp.sum(dgx, axis=0, keepdims=True)
            dxm = dx_ref[...] + dxn
            dxm_ref[...] = dxm
            dxmb = dxm.astype(CDT)
            dxmb_ref[...] = dxmb
            dmix_ref[...] = _dot_nt(dxmb, wo_ref[...].reshape(N_DEV * n_out, D))

    row = lambda n: pl.BlockSpec((tm, n), lambda i, c: (i, 0))
    row1 = lambda n: pl.BlockSpec((tm, n), lambda i, c: (i, 0), pipeline_mode=pl.Buffered(1))
    colb = pl.BlockSpec((tm, FC), lambda i, c: (i, c))
    FF = N_DEV * n_ff
    return _launch(
        "b_mlp" if comm is None else "b_mlp_rs", body, (S // tm, nc),
        [row1(D), row1(D), row1(D), colb, colb, pl.BlockSpec((1, D), lambda i, c: (0, 0)),
         _wspec(nd, n_ff, D, p_wd), _wspec(nd, n_ff, D, p_wg), _wspec(nd, n_ff, D, p_wu),
         pl.BlockSpec((N_DEV, n_out, D), lambda i, c: (0, p_wo, 0), pipeline_mode=pl.Buffered(1))],
        [colb, colb, row(D), row(D), row(D), pl.BlockSpec((1, D), lambda i, c: (0, 0))],
        [jax.ShapeDtypeStruct((S, FF), CDT), jax.ShapeDtypeStruct((S, FF), CDT),
         jax.ShapeDtypeStruct((S, D), F32), jax.ShapeDtypeStruct((S, D), CDT),
         jax.ShapeDtypeStruct((S, D), F32), jax.ShapeDtypeStruct((1, D), F32)],
        [pltpu.VMEM((tm, D), F32)], (dxb, dx, xm, gate, up, g2, G, G, G, G), ("arbitrary", "arbitrary"), comm)


def _b_conv(dmix, c, proj, cw, lg, lb, tm):
    S = proj.shape[0]
    CW = cw.shape[1]
    RB = 64
    hb = tm // HALO
    nt = S // tm
    last_h = S // HALO - 1

    def body(du_ref, duh_ref, c_ref, ch_ref, ab_ref, abh_ref, cw_ref, lg_ref, lb_ref,
             dab_ref, dcw_ref, dcb_ref, dlg_ref, dlb_ref, ext_u, ext_dc, ph_u, ph_dc, wacc):
        i = pl.program_id(0)

        @pl.when(i == 0)
        def _():
            wacc[...] = jnp.zeros_like(wacc)
            dcb_ref[...] = jnp.zeros_like(dcb_ref)
            dlg_ref[...] = jnp.zeros_like(dlg_ref)
            dlb_ref[...] = jnp.zeros_like(dlb_ref)

        def ln_bwd(cv, du):
            mu = jnp.mean(cv, axis=-1, keepdims=True)
            var = jnp.mean(jnp.square(cv - mu), axis=-1, keepdims=True)
            rstd = lax.rsqrt(var + EPS)
            n = (cv - mu) * rstd
            z = n * lg_ref[...] + lb_ref[...]
            sz = _sigmoid(z)
            dz = du * (sz * (1.0 + z * (1.0 - sz)))
            dn = dz * lg_ref[...]
            dc = rstd * (dn - jnp.mean(dn, axis=-1, keepdims=True)
                         - n * jnp.mean(dn * n, axis=-1, keepdims=True))
            return dc, dz, n

        hv = abh_ref[...]
        ext_u[0:HALO, :] = jnp.where(i > 0, hv[:, :CW] * _sigmoid(hv[:, CW:]), 0.0)
        av = ab_ref[...]
        sb = _sigmoid(av[:, CW:])
        ext_u[HALO:HALO + tm, :] = av[:, :CW] * sb
        ext_u[HALO + tm:, :] = jnp.zeros((8, CW), F32)
        dc, dz, n = ln_bwd(c_ref[...], du_ref[...])
        ext_dc[0:tm, :] = dc
        dch, _, _ = ln_bwd(ch_ref[...], duh_ref[...])
        ext_dc[tm:tm + HALO, :] = jnp.where(i < nt - 1, dch, 0.0)
        ext_dc[tm + HALO:, :] = jnp.zeros((8, CW), F32)
        dlg_ref[...] += jnp.sum(dz * n, axis=0, keepdims=True)
        dlb_ref[...] += jnp.sum(dz, axis=0, keepdims=True)
        dcb_ref[...] += jnp.sum(dc, axis=0, keepdims=True)
        _phases(ext_u, ph_u, tm + HALO)
        _phases(ext_dc, ph_dc, tm + HALO)

        for rb in range(tm // RB):
            rs = slice(rb * RB, (rb + 1) * RB)
            dcb = ext_dc[rs, :]
            for k in range(CONV_K):
                o = rb * RB + HALO - (CONV_K - 1) + k
                prod = dcb * ph_u[o % 8, o - o % 8:o - o % 8 + RB, :]
                part = prod[0:8]
                for j in range(1, RB // 8):
                    part = part + prod[8 * j:8 * j + 8]
                wacc[k] += part
            acc = jnp.zeros((RB, CW), F32)
            for k in range(CONV_K):
                o = rb * RB + (CONV_K - 1) - k
                acc = acc + cw_ref[k:k + 1, :] * ph_dc[o % 8, o - o % 8:o - o % 8 + RB, :]
            a_r = ab_ref[rs, 0:CW]
            s_r = _sigmoid(ab_ref[rs, CW:2 * CW])
            dab_ref[rs, 0:CW] = (acc * s_r).astype(CDT)
            dab_ref[rs, CW:2 * CW] = (acc * a_r * (s_r * (1.0 - s_r))).astype(CDT)

        @pl.when(i == nt - 1)
        def _():
            for k in range(CONV_K):
                dcw_ref[k:k + 1, :] = jnp.sum(wacc[k], axis=0, keepdims=True)
            dcw_ref[CONV_K:, :] = jnp.zeros((HALO - CONV_K, CW), F32)

    tile = lambda n, j: pl.BlockSpec((tm, n), lambda i: (i, j))
    nxt = lambda n, j: pl.BlockSpec((HALO, n), lambda i: (jnp.minimum((i + 1) * hb, last_h), j))
    return _call(
        body, name="b_conv", grid=(nt,),
        in_specs=[tile(CW, 0), nxt(CW, 0), tile(CW, 0), nxt(CW, 0),
                  tile(2 * CW, 0),
                  pl.BlockSpec((HALO, 2 * CW), lambda i: (jnp.maximum(i * hb - 1, 0), 0)),
                  _resident((HALO, CW)), _resident((1, CW)), _resident((1, CW))],
        out_specs=[tile(2 * CW, 0),
                   pl.BlockSpec((HALO, CW), lambda i: (0, 0)), pl.BlockSpec((1, CW), lambda i: (0, 0)),
                   pl.BlockSpec((1, CW), lambda i: (0, 0)), pl.BlockSpec((1, CW), lambda i: (0, 0))],
        out_shape=[jax.ShapeDtypeStruct((S, 2 * CW), CDT),
                   jax.ShapeDtypeStruct((HALO, CW), F32), jax.ShapeDtypeStruct((1, CW), F32),
                   jax.ShapeDtypeStruct((1, CW), F32), jax.ShapeDtypeStruct((1, CW), F32)],
        scratch_shapes=[pltpu.VMEM((tm + HALO + 8, CW), F32), pltpu.VMEM((tm + HALO + 8, CW), F32),
                        pltpu.VMEM((8, tm + HALO, CW), F32), pltpu.VMEM((8, tm + HALO, CW), F32),
                        pltpu.VMEM((HALO, 8, CW), F32)],
        compiler_params=_cp(("arbitrary",)),
    )(dmix, dmix, c, c, proj, proj, cw, lg, lb)


def _b_ret(dmix, dab, proj, rraw, states, cosT, sinT, Mt, qd, kd, gs, gn):
    S = proj.shape[0]
    RW = HEADS * HD
    NB = S // SC
    scale = HD ** -0.5

    def body(dro_ref, dab_ref, q_ref, k_ref, v_ref, g_ref, rraw_ref, st_ref, c_ref, s_ref,
             m_ref, qd_ref, kd_ref, gs_ref, gn_ref, dp_ref, dgn_ref, G):
        @pl.when(pl.program_id(0) == 0)
        def _():
            G[...] = jnp.zeros_like(G)
            dgn_ref[...] = jnp.zeros_like(dgn_ref)

        dp_ref[:, 0:2 * RW] = dab_ref[...]
        cv = c_ref[...]
        sv = s_ref[...]
        for h in range(HEADS):
            cs = slice(h * HD, (h + 1) * HD)
            q = _rot(q_ref[:, cs], cv, sv)
            k = _rot(k_ref[:, cs], cv, sv) * scale
            qb = q.astype(CDT)
            kb = k.astype(CDT)
            vb = v_ref[:, cs].astype(CDT)
            spb = st_ref[0, h]
            r = rraw_ref[:, cs]
            mu = jnp.mean(r, axis=-1, keepdims=True)
            var = jnp.mean(jnp.square(r - mu), axis=-1, keepdims=True)
            rstd = lax.rsqrt(var + EPS)
            n = (r - mu) * rstd
            gv = g_ref[:, cs]
            sg = _sigmoid(gv)
            sil = gv * sg
            dro = dro_ref[:, cs]
            gnv = gn_ref[:, cs]
            dgn_ref[:, cs] += jnp.sum(dro * n * sil, axis=0, keepdims=True)
            dgate = dro * (n * gnv) * (sg * (1.0 + gv * (1.0 - sg)))
            dn = dro * gnv * sil
            dr = rstd * (dn - jnp.mean(dn, axis=-1, keepdims=True)
                         - n * jnp.mean(dn * n, axis=-1, keepdims=True))
            drb = dr.astype(CDT)
            mh = m_ref[h]
            ab = (_dot_nt(qb, kb) * mh).astype(CDT)
            dab_ = (_dot_nt(drb, vb) * mh).astype(CDT)
            qdb = (q * qd_ref[h]).astype(CDT)
            kdb = (k * kd_ref[h]).astype(CDT)
            gc = G[h]
            gb = gc.astype(CDT)
            dq = _dot(dab_, kb) + _dot_nt(drb, spb) * qd_ref[h]
            dk = _dot_tn(dab_, qb) + _dot_nt(vb, gb) * kd_ref[h]
            dv = _dot_tn(ab, drb) + _dot(kdb, gb)
            G[h] = gs_ref[h, 0:1, :] * gc + _dot_tn(qdb, drb)
            dk = dk * scale
            dqp = dq * cv + pltpu.roll(dq * sv, HD // 2, 1)
            dkp = dk * cv + pltpu.roll(dk * sv, HD // 2, 1)
            base = 2 * RW
            dp_ref[:, base + h * HD:base + (h + 1) * HD] = dqp.astype(CDT)
            dp_ref[:, base + RW + h * HD:base + RW + (h + 1) * HD] = dkp.astype(CDT)
            dp_ref[:, base + 2 * RW + h * HD:base + 2 * RW + (h + 1) * HD] = dv.astype(CDT)
            dp_ref[:, base + 3 * RW + h * HD:base + 3 * RW + (h + 1) * HD] = dgate.astype(CDT)

    rev = lambda n, j: pl.BlockSpec((SC, n), lambda i: (NB - 1 - i, j))
    return _call(
        body, name="b_ret", grid=(NB,),
        in_specs=[rev(RW, 1), rev(2 * RW, 0), rev(RW, 2), rev(RW, 3), rev(RW, 4), rev(RW, 5), rev(RW, 0),
                  pl.BlockSpec((1, HEADS, HD, HD), lambda i: (NB - 1 - i, 0, 0, 0)),
                  rev(HD, 0), rev(HD, 0),
                  _resident((HEADS, SC, SC)), _resident((HEADS, SC, HD)), _resident((HEADS, SC, HD)),
                  _resident((HEADS, 8, HD)), _resident((1, RW))],
        out_specs=[rev(6 * RW, 0), pl.BlockSpec((1, RW), lambda i: (0, 0))],
        out_shape=[jax.ShapeDtypeStruct((S, 6 * RW), CDT), jax.ShapeDtypeStruct((1, RW), F32)],
        scratch_shapes=[pltpu.VMEM((HEADS, HD, HD), F32)],
        compiler_params=_cp(("arbitrary",)),
    )(dmix, dab, proj, proj, proj, proj, rraw, states, cosT, sinT, Mt, qd, kd, gs, gn)


def _b_in(dproj, G, n_in, x, g1, dxm, tm, comm=None):
    S, D = x.shape
    nd = 2
    nc = N_DEV // nd

    def body(dp_ref, w_ref, x_ref, g_ref, dxm_ref, dx_ref, dxb_ref, dg_ref, acc):
        c = pl.program_id(1)

        @pl.when((pl.program_id(0) == 0) & (c == 0))
        def _():
            dg_ref[...] = jnp.zeros_like(dg_ref)

        dh = _dot(dp_ref[...], w_ref[...].reshape(nd * n_in, D))

        @pl.when(c == 0)
        def _():
            acc[...] = dh

        @pl.when(c > 0)
        def _():
            acc[...] += dh

        @pl.when(c == nc - 1)
        def _():
            xv = x_ref[...]
            r = lax.rsqrt(jnp.mean(xv * xv, axis=-1, keepdims=True) + EPS)
            dxn, dgx = _rms_bwd(xv, r, g_ref[...], acc[...])
            dg_ref[...] += jnp.sum(dgx, axis=0, keepdims=True)
            dx = dxm_ref[...] + dxn
            dx_ref[...] = dx
            dxb_ref[...] = dx.astype(CDT)

    row = lambda n: pl.BlockSpec((tm, n), lambda i, c: (i, 0))
    return _launch(
        "b_in" if comm is None else "b_in_rs", body, (S // tm, nc),
        [pl.BlockSpec((tm, nd * n_in), lambda i, c: (i, c)), _wspec(nd, n_in, D, 0), row(D),
         pl.BlockSpec((1, D), lambda i, c: (0, 0)), row(D)],
        [row(D), row(D), pl.BlockSpec((1, D), lambda i, c: (0, 0))],
        [jax.ShapeDtypeStruct((S, D), F32), jax.ShapeDtypeStruct((S, D), CDT),
         jax.ShapeDtypeStruct((1, D), F32)],
        [pltpu.VMEM((tm, D), F32)], (dproj, G, x, g1, dxm), ("arbitrary", "arbitrary"), comm)


def _dw_tn(a, b, tm, tk):
    S, M = a.shape
    N = b.shape[1]

    def body(a_ref, b_ref, o_ref):
        @pl.when(pl.program_id(1) == 0)
        def _():
            o_ref[...] = jnp.zeros_like(o_ref)

        o_ref[...] += _dot_tn(a_ref[...], b_ref[...])

    return _call(
        body, name="dw_tn", grid=(M // tm, S // tk),
        in_specs=[pl.BlockSpec((tk, tm), lambda m, k: (k, m)), pl.BlockSpec((tk, N), lambda m, k: (k, 0))],
        out_specs=pl.BlockSpec((tm, N), lambda m, k: (m, 0)),
        out_shape=jax.ShapeDtypeStruct((M, N), F32),
        compiler_params=_cp(("parallel", "arbitrary")),
    )(a, b)


def _add_pair(parts, recv):
    K = len(parts)
    C = parts[0].shape[2]
    halves = 2

    def body(*refs):
        cc = lax.axis_index("c")
        for k in range(K):
            s = refs[k][cc] + refs[K + k][...]
            refs[2 * K + k][...] = s
            refs[3 * K + k][...] = s.astype(CDT)

    ns = [p.shape[1] for p in parts]
    in_specs = [pl.BlockSpec((None, 2, n // halves, C), lambda q, r: (q, 0, r, 0)) for n in ns]
    in_specs += [pl.BlockSpec((None, n // halves, C), lambda q, r: (q, r, 0)) for n in ns]
    outb = [pl.BlockSpec((None, n // halves, C), lambda q, r: (q, r, 0)) for n in ns]
    res = _call(
        body, name="add_pair", grid=(N_DEV // 2, halves),
        in_specs=in_specs, out_specs=outb + outb,
        out_shape=[jax.ShapeDtypeStruct((N_DEV // 2, n, C), F32) for n in ns]
        + [jax.ShapeDtypeStruct((N_DEV // 2, n, C), CDT) for n in ns],
        compiler_params=_cp(("parallel", "parallel")),
    )(*[p.reshape(N_DEV // 2, 2, p.shape[1], C) for p in parts], *recv)
    return list(res[:K]), list(res[K:])


def _sum_chips(sums, recv):
    K = len(sums)

    def body(*refs):
        chip = 2 * lax.axis_index("x") + lax.axis_index("y")
        for k in range(K):
            r = refs[K + k]
            refs[2 * K + k][...] = ((refs[k][chip] + r[0].astype(F32)) + r[1].astype(F32)) + r[2].astype(F32)

    vm = pl.BlockSpec(memory_space=pltpu.VMEM)
    res = _call(
        body, name="sum_chips", in_specs=[vm] * (2 * K), out_specs=[vm] * K,
        out_shape=[jax.ShapeDtypeStruct(s.shape[1:], F32) for s in sums],
        compiler_params=_cp(),
    )(*sums, *recv)
    return list(res)


def _adamw(w, g, m, v, tr):
    R, C = w.shape
    c1 = 1.0 - ADAM_B1 ** ADAM_STEP
    c2 = 1.0 - ADAM_B2 ** ADAM_STEP

    def body(w_ref, g_ref, m_ref, v_ref, d_ref, mo_ref, vo_ref):
        gv = g_ref[...]
        mn = ADAM_B1 * m_ref[...] + (1.0 - ADAM_B1) * gv
        vn = ADAM_B2 * v_ref[...] + (1.0 - ADAM_B2) * jnp.square(gv)
        mo_ref[...] = mn
        vo_ref[...] = vn
        d_ref[...] = -ADAM_LR * ((mn / c1) / (jnp.sqrt(vn / c2) + ADAM_EPS) + ADAM_WD * w_ref[...])

    blk = pl.BlockSpec((tr, C), lambda i: (i, 0))
    sh = jax.ShapeDtypeStruct((R, C), F32)
    return _call(
        body, name="adamw", grid=(R // tr,),
        in_specs=[blk, blk, blk, blk], out_specs=[blk, blk, blk], out_shape=[sh, sh, sh],
        compiler_params=_cp(("parallel",)),
    )(w, g, m, v)


def _coords():
    return lax.axis_index("x"), lax.axis_index("y"), lax.axis_index("c")


def _peer(x, y, c, d):
    return (x ^ (d >> 2), y ^ ((d >> 1) & 1), c ^ (d & 1))


def _ag_comm(p):
    R, C = p.shape

    def plan(cins, couts, sems):
        x_ref, out_ref = cins[0], couts[0]
        send_sems, recv_sems, local_sem = sems
        x, y, c = _coords()
        me, sibling = (x, y, c), (x, y, 1 - c)
        chips = [(1 - x, y), (x, 1 - y), (1 - x, 1 - y)]

        def rows(px, py, pc):
            return out_ref.at[pl.ds((4 * px + 2 * py + pc) * R, R), :]

        def copy(k, block, to, src=None):
            return pltpu.make_async_remote_copy(
                src_ref=rows(*block) if src is None else src, dst_ref=rows(*block),
                send_sem=send_sems.at[k], recv_sem=recv_sems.at[k], device_id=to, device_id_type=MESH)

        mine = pltpu.make_async_copy(x_ref, rows(*me), local_sem)
        first = [copy(0, me, sibling, src=x_ref)]
        first += [copy(1 + j, me, (*chip, c), src=x_ref) for j, chip in enumerate(chips)]
        passed = [copy(4 + j, (*chip, c), sibling) for j, chip in enumerate(chips)]
        got_ici = [copy(1 + j, (*chip, c), me) for j, chip in enumerate(chips)]
        got_d2d = [copy(0, sibling, me)] + [copy(4 + j, (*chip, 1 - c), me) for j, chip in enumerate(chips)]
        return mine, first, passed, got_ici, got_d2d

    def start(*a):
        mine, first, _, _, _ = plan(*a)
        mine.start()
        for cp in first:
            cp.start()

    def mid(*a):
        _, _, passed, got_ici, _ = plan(*a)
        for got, fwd in zip(got_ici, passed):
            got.wait_recv()
            fwd.start()

    def finish(*a):
        mine, first, passed, _, got_d2d = plan(*a)
        for got in got_d2d:
            got.wait_recv()
        for cp in first + passed:
            cp.wait_send()
        mine.wait()

    return _Comm([p], [jax.ShapeDtypeStruct((N_DEV * R, C), p.dtype)],
                 [pltpu.SemaphoreType.DMA((7,)), pltpu.SemaphoreType.DMA((7,)), pltpu.SemaphoreType.DMA],
                 start, mid, finish)


def _rs_pair_comm(parts):
    K = len(parts)

    def plan(cins, couts, sems):
        send_sems, recv_sems = sems
        x, y, c = _coords()
        sibling = (x, y, 1 - c)
        cps = []
        for k in range(K):
            for q in range(N_DEV // 2):
                cps.append(pltpu.make_async_remote_copy(
                    src_ref=cins[k].at[2 * q + (1 - c)], dst_ref=couts[k].at[q],
                    send_sem=send_sems.at[k], recv_sem=recv_sems.at[k], device_id=sibling, device_id_type=MESH))
        whole = [pltpu.make_async_remote_copy(
            src_ref=couts[k], dst_ref=couts[k], send_sem=send_sems.at[k], recv_sem=recv_sems.at[k],
            device_id=sibling, device_id_type=MESH) for k in range(K)]
        return cps, whole

    def start(*a):
        for cp in plan(*a)[0]:
            cp.start()

    def finish(*a):
        for w in plan(*a)[1]:
            w.wait_recv()
            w.wait_send()

    return _Comm(parts, [jax.ShapeDtypeStruct((N_DEV // 2,) + p.shape[1:], p.dtype) for p in parts],
                 [pltpu.SemaphoreType.DMA((K,)), pltpu.SemaphoreType.DMA((K,))], start, None, finish)


def _rs_chip_comm(sums):
    K = len(sums)

    def plan(cins, couts, sems):
        send_sems, recv_sems = sems
        x, y, c = _coords()
        cps = []
        for d in range(1, N_DEV // 2):
            px, py = x ^ (d >> 1), y ^ (d & 1)
            for k in range(K):
                s = (d - 1) * K + k
                cps.append(pltpu.make_async_remote_copy(
                    src_ref=cins[k].at[2 * px + py], dst_ref=couts[k].at[d - 1],
                    send_sem=send_sems.at[s], recv_sem=recv_sems.at[s], device_id=(px, py, c), device_id_type=MESH))
        return cps

    def start(*a):
        for cp in plan(*a):
            cp.start()

    def finish(*a):
        cps = plan(*a)
        for cp in cps:
            cp.wait_recv()
        for cp in cps:
            cp.wait_send()

    n_sem = (N_DEV // 2 - 1) * K
    return _Comm(sums, [jax.ShapeDtypeStruct((N_DEV // 2 - 1,) + s.shape[1:], s.dtype) for s in sums],
                 [pltpu.SemaphoreType.DMA((n_sem,)), pltpu.SemaphoreType.DMA((n_sem,))], start, None, finish)


def _gather_small(v, reduce):
    R, C = v.shape

    def exchange(v_ref, buf, send_sems, recv_sems):
        x, y, c = _coords()
        me = 4 * x + 2 * y + c
        buf[me] = v_ref[...]
        cps = []
        for d in range(1, N_DEV):
            cp = pltpu.make_async_remote_copy(
                src_ref=v_ref, dst_ref=buf.at[me], send_sem=send_sems.at[d - 1], recv_sem=recv_sems.at[d - 1],
                device_id=_peer(x, y, c, d), device_id_type=MESH)
            cp.start()
            cps.append(cp)
        for cp in cps:
            cp.wait_recv()
        for cp in cps:
            cp.wait_send()

    sems = [pltpu.SemaphoreType.DMA((7,)), pltpu.SemaphoreType.DMA((7,))]
    vm = pl.BlockSpec(memory_space=pltpu.VMEM)
    if reduce:
        def body(v_ref, o_ref, buf, send_sems, recv_sems):
            exchange(v_ref, buf, send_sems, recv_sems)
            acc = buf[0]
            for s in range(1, N_DEV):
                acc = acc + buf[s]
            o_ref[...] = acc

        return _call(body, name="allreduce_small", in_specs=[vm], out_specs=vm,
                     out_shape=jax.ShapeDtypeStruct((R, C), F32),
                     scratch_shapes=[pltpu.VMEM((N_DEV, R, C), F32)] + sems)(v)

    def body(v_ref, o_ref, send_sems, recv_sems):
        exchange(v_ref, o_ref, send_sems, recv_sems)

    return _call(body, name="allgather_small", in_specs=[vm], out_specs=vm,
                 out_shape=jax.ShapeDtypeStruct((N_DEV, R, C), F32), scratch_shapes=sems)(v)


def _tables(S):
    half = HD // 2
    pos = jnp.arange(S, dtype=F32)
    freqs = ROPE_BASE ** (-jnp.arange(half, dtype=F32) / half)
    ang = pos[:, None] * freqs[None, :]
    cos, sin = jnp.cos(ang), jnp.sin(ang)
    cosT = jnp.concatenate([cos, cos], axis=-1)
    sinT = jnp.concatenate([-sin, sin], axis=-1)
    log_g = jnp.log(1.0 - 2.0 ** (-5.0 - jnp.arange(HEADS, dtype=F32)))
    idx = jnp.arange(SC, dtype=F32)
    ci = jnp.arange(SC) // CHUNK
    diff = idx[:, None] - idx[None, :]
    same = ci[:, None] == ci[None, :]
    earlier = ci[None, :] < ci[:, None]
    expo = jnp.where(same, jnp.abs(diff), diff)
    Mt = jnp.where((same | earlier)[None], jnp.exp(log_g[:, None, None] * expo[None]), 0.0)
    ones = jnp.ones((1, 1, HD), F32)
    qd = jnp.exp(log_g[:, None] * (idx + 1.0)[None, :])[:, :, None] * ones
    kd = jnp.exp(log_g[:, None] * (SC - 1.0 - idx)[None, :])[:, :, None] * ones
    gs = jnp.exp(log_g * SC)[:, None, None] * jnp.ones((1, 8, HD), F32)
    return cosT, sinT, Mt, qd, kd, gs


def _pad_rows(a, rows):
    return jnp.pad(a, ((0, rows - a.shape[0]), (0, 0)))


def kernel(x, norm1_g, w_in, conv_w, conv_b, conv_ln_g, conv_ln_b, ret_gn_g, w_out, norm2_g, w_gate, w_up, w_down, final_g, loss_target, m_norm1_g, m_w_in, m_conv_w, m_conv_b, m_conv_ln_g, m_conv_ln_b, m_ret_gn_g, m_w_out, m_norm2_g, m_w_gate, m_w_up, m_w_down, m_final_g, v_norm1_g, v_w_in, v_conv_w, v_conv_b, v_conv_ln_g, v_conv_ln_b, v_ret_gn_g, v_w_out, v_norm2_g, v_w_gate, v_w_up, v_w_down, v_final_g):
    L, D, n_in = w_in.shape
    n_out = w_out.shape[1]
    n_ff = w_gate.shape[2]
    S = x.shape[1]
    CW = conv_b.shape[1]
    IN, FF = N_DEV * n_in, N_DEV * n_ff
    ncw = conv_w.shape[2]
    x0 = x.reshape(S, D)
    tgt = loss_target.reshape(S, D)
    TM = min(512, S)
    TMI = min(1024, S)
    TMM = min(512, S)

    assert n_in % n_out == 0
    o_ff = -(-(n_in + n_out) // n_ff) * n_ff
    RL = o_ff + 3 * n_ff
    wparts = (n_in // n_out, o_ff // n_ff, o_ff // n_ff + 1, o_ff // n_ff + 2)
    pack = jnp.concatenate([jnp.swapaxes(w_in, 1, 2), w_out, jnp.zeros((L, o_ff - n_in - n_out, D), F32),
                            jnp.swapaxes(w_gate, 1, 2), jnp.swapaxes(w_up, 1, 2), w_down],
                           axis=1).astype(CDT)
    Gl = _comm_only(_ag_comm(pack[0]), "ag_first")[0].reshape(N_DEV, RL, D)

    cwp = conv_w.reshape(L * CONV_K * ncw // 128, 128)
    cw_rows = -(-cwp.shape[0] // 8) * 8
    cwg = _gather_small(_pad_rows(cwp, cw_rows), reduce=False)[:, :cwp.shape[0], :]
    conv_w_full = jnp.moveaxis(cwg.reshape(N_DEV, L, CONV_K, ncw), 0, 2).reshape(L, CONV_K, CW)

    cosT, sinT, Mt, qd, kd, gs = _tables(S)

    saved = []
    xl = x0
    for l in range(L):
        cw = _pad_rows(conv_w_full[l], HALO)
        h, proj = _f_in(xl, norm1_g[l][None], Gl, n_in, TMI)
        c, u = _f_conv(proj, cw, conv_b[l][None], conv_ln_g[l][None], conv_ln_b[l][None], TM)
        rraw, states, mixed = _f_ret(proj, u, cosT, sinT, Mt, qd, kd, gs, ret_gn_g[l][None])
        (xm, h2, gate, up, act, xo), nxt = _f_mlp(xl, mixed, norm2_g[l][None], Gl, n_out, n_ff, wparts, TMM,
                                                 _ag_comm(pack[l + 1]) if l + 1 < L else None)
        saved.append(dict(x=xl, h=h, proj=proj, c=c, rraw=rraw, states=states, mixed=mixed, xm=xm, h2=h2,
                          gate=gate, up=up, act=act, cw=cw, G=Gl))
        if nxt:
            Gl = nxt[0].reshape(N_DEV, RL, D)
        xl = xo

    dx, dxb, loss_p, dfg = _f_loss(xl, final_g[None], tgt, TM)

    small = []
    chip_sums = [None] * L
    chip_recv = [None] * L
    in_flight = None
    for l in reversed(range(L)):
        sv = saved[l]
        (dgate, dup, dxm, dxmb, dmix, dg2), got = _b_mlp(
            dxb, dx, sv["xm"], sv["gate"], sv["up"], norm2_g[l][None], sv["G"], n_out, n_ff, wparts, TMM, in_flight)
        if got:
            chip_recv[l + 1] = got
        d_wd = _dw_tn(sv["act"], dxb, FF // 2, TM)
        d_wgT = _dw_tn(dgate, sv["h2"], FF // 2, TM)
        d_wuT = _dw_tn(dup, sv["h2"], FF // 2, TM)
        d_wo = _dw_tn(sv["mixed"], dxmb, D // 2, TM)
        dab, dcw, dcb, dlg, dlb = _b_conv(dmix, sv["c"], sv["proj"], sv["cw"], conv_ln_g[l][None],
                                          conv_ln_b[l][None], TM)
        dproj, dgn = _b_ret(dmix, dab, sv["proj"], sv["rraw"], sv["states"], cosT, sinT, Mt, qd, kd, gs,
                            ret_gn_g[l][None])
        d_winT = _dw_tn(dproj, sv["h"], IN // 2, TM)
        parts = [d.reshape(N_DEV, -1, D) for d in (d_winT, d_wo, d_wgT, d_wuT, d_wd)]
        (dx, dxb, dg1), pair_recv = _b_in(dproj, sv["G"], n_in, sv["x"], norm1_g[l][None], dxm, TMI,
                                          _rs_pair_comm(parts))
        chip_sums[l], sums_b = _add_pair(parts, pair_recv)
        in_flight = _rs_chip_comm(sums_b)
        small.append(jnp.concatenate([dcw, dcb, dlg, dlb, dgn, dg1.reshape(2, CW), dg2.reshape(2, CW)], axis=0))
    chip_recv[0] = _comm_only(in_flight, "rs_last")
    small = small[::-1]
    grad_x = dx.reshape(1, S, D)

    rows_l = HALO + 8
    loss_row = jnp.zeros((1, CW), F32).at[0, 0].set(loss_p[0, 0])
    sm = jnp.concatenate(small + [dfg.reshape(2, CW), loss_row], axis=0)
    sm_rows = -(-sm.shape[0] // 8) * 8
    sm = _gather_small(_pad_rows(sm, sm_rows), reduce=True)
    loss = sm[L * rows_l + 2, 0]
    g_final = sm[L * rows_l:L * rows_l + 2].reshape(D)
    per = sm[:L * rows_l].reshape(L, rows_l, CW)
    me = 4 * lax.axis_index("x") + 2 * lax.axis_index("y") + lax.axis_index("c")
    g_conv_w = lax.dynamic_slice_in_dim(per[:, :CONV_K, :], me * ncw, ncw, axis=2)
    g_conv_b, g_ln_g, g_ln_b, g_gn = per[:, HALO], per[:, HALO + 1], per[:, HALO + 2], per[:, HALO + 3]
    g_n1 = per[:, HALO + 4:HALO + 6].reshape(L, D)
    g_n2 = per[:, HALO + 6:HALO + 8].reshape(L, D)

    gl = [_sum_chips(chip_sums[l], chip_recv[l]) for l in range(L)]
    g_w_in = jnp.stack([gl[l][0].T for l in range(L)])
    g_w_out = jnp.stack([gl[l][1] for l in range(L)])
    g_w_gate = jnp.stack([gl[l][2].T for l in range(L)])
    g_w_up = jnp.stack([gl[l][3].T for l in range(L)])
    g_w_down = jnp.stack([gl[l][4] for l in range(L)])

    def big(w, g, m, v):
        sh = w.shape
        two = lambda a: a.reshape(-1, sh[-1])
        rows = two(w).shape[0]
        d, mn, vn = _adamw(two(w), two(g), two(m), two(v), rows // 8)
        return d.reshape(sh), mn.reshape(sh), vn.reshape(sh)

    names = ["norm1_g", "conv_w", "conv_b", "conv_ln_g", "conv_ln_b", "ret_gn_g", "norm2_g", "final_g"]
    sw = dict(norm1_g=(norm1_g, g_n1, m_norm1_g, v_norm1_g), conv_w=(conv_w, g_conv_w, m_conv_w, v_conv_w),
              conv_b=(conv_b, g_conv_b, m_conv_b, v_conv_b), conv_ln_g=(conv_ln_g, g_ln_g, m_conv_ln_g, v_conv_ln_g),
              conv_ln_b=(conv_ln_b, g_ln_b, m_conv_ln_b, v_conv_ln_b), ret_gn_g=(ret_gn_g, g_gn, m_ret_gn_g, v_ret_gn_g),
              norm2_g=(norm2_g, g_n2, m_norm2_g, v_norm2_g), final_g=(final_g, g_final, m_final_g, v_final_g))
    lens = [int(math.prod(sw[n][0].shape)) for n in names]
    tot = sum(lens)
    prow = -(-tot // (8 * CW)) * 8

    def packs(j):
        flat = jnp.concatenate([sw[n][j].reshape(-1) for n in names])
        return jnp.pad(flat, (0, prow * CW - tot)).reshape(prow, CW)

    sd, smn, svn = _adamw(packs(0), packs(1), packs(2), packs(3), prow)

    def unpack(a):
        flat = a.reshape(-1)
        out, o = {}, 0
        for n, ln in zip(names, lens):
            out[n] = flat[o:o + ln].reshape(sw[n][0].shape)
            o += ln
        return out

    sd, smn, svn = unpack(sd), unpack(smn), unpack(svn)
    res = {n: (sw[n][1], sd[n], smn[n], svn[n]) for n in names}
    res["w_in"] = (g_w_in,) + big(w_in, g_w_in, m_w_in, v_w_in)
    res["w_out"] = (g_w_out,) + big(w_out, g_w_out, m_w_out, v_w_out)
    res["w_gate"] = (g_w_gate,) + big(w_gate, g_w_gate, m_w_gate, v_w_gate)
    res["w_up"] = (g_w_up,) + big(w_up, g_w_up, m_w_up, v_w_up)
    res["w_down"] = (g_w_down,) + big(w_down, g_w_down, m_w_down, v_w_down)

    order = ["norm1_g", "w_in", "conv_w", "conv_b", "conv_ln_g", "conv_ln_b", "ret_gn_g", "w_out", "norm2_g",
             "w_gate", "w_up", "w_down", "final_g"]
    return (loss, grad_x, *[res[n][0] for n in order], *[res[n][1] for n in order],
            *[res[n][2] for n in order], *[res[n][3] for n in order])
```

```python
import functools
import math

import jax
import jax.numpy as jnp
from jax import lax
from jax.experimental import pallas as pl
from jax.experimental.pallas import tpu as pltpu

F32 = jnp.float32
CDT = jnp.bfloat16
EPS = 1e-6
CHUNK = 64
SC = 256
HEADS = 4
HD = 128
CONV_K = 31
HALO = 32
ROPE_BASE = 10000.0
ADAM_LR = 0.001
ADAM_B1 = 0.9
ADAM_B2 = 0.999
ADAM_EPS = 1e-08
ADAM_WD = 0.01
ADAM_STEP = 10
N_DEV = 8
MESH = pl.DeviceIdType.MESH
VMEM_LIMIT = 60 * 1024 * 1024


def _call(body, **kw):
    return pl.pallas_call(body, **kw)


def _cp(sem=None, vmem=VMEM_LIMIT):
    return pltpu.CompilerParams(dimension_semantics=sem, vmem_limit_bytes=vmem)


def _resident(shape):
    nd = len(shape)
    return pl.BlockSpec(shape, lambda *_: (0,) * nd, pipeline_mode=pl.Buffered(1))


def _dot(a, b):
    return jnp.dot(a, b, preferred_element_type=F32)


def _dot_nt(a, b):
    return lax.dot_general(a, b, (((1,), (1,)), ((), ())), preferred_element_type=F32)


def _dot_tn(a, b):
    return lax.dot_general(a, b, (((0,), (0,)), ((), ())), preferred_element_type=F32)


def _sigmoid(x):
    return 1.0 / (1.0 + jnp.exp(-x))


def _rms_bwd(x, r, g, dy):
    xh = x * r
    dyg = dy * g
    dx = r * (dyg - xh * jnp.mean(dyg * xh, axis=-1, keepdims=True))
    return dx, dy * xh


class _Comm:
    def __init__(self, ins, out_shapes, sems, start, mid, finish):
        self.ins, self.out_shapes, self.sems = list(ins), list(out_shapes), list(sems)
        self.start, self.mid, self.finish = start, mid, finish


_ANY = pl.BlockSpec(memory_space=pl.ANY)


def _launch(name, compute, grid, in_specs, out_specs, out_shape, scratch, operands, sem, comm=None):
    n_in, n_out, n_sc = len(in_specs), len(out_specs), len(scratch)
    if comm is None:
        res = _call(compute, name=name, grid=grid, in_specs=in_specs, out_specs=out_specs, out_shape=out_shape,
                    scratch_shapes=scratch, compiler_params=_cp(sem))(*operands)
        return list(res), []
    c_in, c_out = len(comm.ins), len(comm.out_shapes)
    inner = grid[1] if len(grid) > 1 else 1
    steps = grid[0] * inner
    mid_step = (3 * steps) // 4

    def body(*refs):
        ins = refs[:n_in]
        cins = refs[n_in:n_in + c_in]
        o = n_in + c_in
        outs = refs[o:o + n_out]
        couts = refs[o + n_out:o + n_out + c_out]
        o += n_out + c_out
        sc = refs[o:o + n_sc]
        csem = refs[o + n_sc:]
        i = pl.program_id(0)
        if len(grid) > 1:
            i = i * inner + pl.program_id(1)

        @pl.when(i == 0)
        def _():
            comm.start(cins, couts, csem)

        if comm.mid is not None:
            @pl.when(i == mid_step)
            def _():
                comm.mid(cins, couts, csem)

        compute(*ins, *outs, *sc)

        @pl.when(i == steps - 1)
        def _():
            comm.finish(cins, couts, csem)

    res = _call(body, name=name, grid=grid, in_specs=list(in_specs) + [_ANY] * c_in,
                out_specs=list(out_specs) + [_ANY] * c_out, out_shape=list(out_shape) + comm.out_shapes,
                scratch_shapes=list(scratch) + comm.sems,
                compiler_params=_cp(("arbitrary",) * len(grid)))(*operands, *comm.ins)
    return list(res[:n_out]), list(res[n_out:])


def _comm_only(comm, name):
    c_in, c_out = len(comm.ins), len(comm.out_shapes)

    def body(*refs):
        cins, couts, csem = refs[:c_in], refs[c_in:c_in + c_out], refs[c_in + c_out:]
        comm.start(cins, couts, csem)
        if comm.mid is not None:
            comm.mid(cins, couts, csem)
        comm.finish(cins, couts, csem)

    res = _call(body, name=name, in_specs=[_ANY] * c_in, out_specs=[_ANY] * c_out, out_shape=comm.out_shapes,
                scratch_shapes=comm.sems)(*comm.ins)
    return list(res)


def _wres(n, D, part):
    return pl.BlockSpec((N_DEV, n, D), lambda i: (0, part, 0), pipeline_mode=pl.Buffered(1))


def _f_in(x, g, G, n_in, tm):
    S, D = x.shape
    N = N_DEV * n_in

    def body(x_ref, g_ref, w_ref, h_ref, p_ref):
        xv = x_ref[...]
        r = lax.rsqrt(jnp.mean(xv * xv, axis=-1, keepdims=True) + EPS)
        h = ((xv * r) * g_ref[...]).astype(CDT)
        h_ref[...] = h
        p_ref[...] = _dot_nt(h, w_ref[...].reshape(N, D))

    return _call(
        body, name="f_in", grid=(S // tm,),
        in_specs=[pl.BlockSpec((tm, D), lambda i: (i, 0)), _resident((1, D)), _wres(n_in, D, 0)],
        out_specs=[pl.BlockSpec((tm, D), lambda i: (i, 0)), pl.BlockSpec((tm, N), lambda i: (i, 0))],
        out_shape=[jax.ShapeDtypeStruct((S, D), CDT), jax.ShapeDtypeStruct((S, N), F32)],
        compiler_params=_cp(("parallel",)),
    )(x, g, G)


def _phases(ext, ph, rows):
    for p in range(8):
        ph[p, :, :] = ext[p:p + rows, :]


def _f_conv(proj, cw, cb, lg, lb, tm):
    S = proj.shape[0]
    CW = cw.shape[1]
    RB = 64
    hb = tm // HALO

    def body(ab_ref, halo_ref, cw_ref, cb_ref, lg_ref, lb_ref, c_ref, u_ref, ext, ph):
        i = pl.program_id(0)
        hv = halo_ref[...]
        u0h = hv[:, :CW] * _sigmoid(hv[:, CW:])
        ext[0:HALO, :] = jnp.where(i > 0, u0h, 0.0)
        av = ab_ref[...]
        ext[HALO:HALO + tm, :] = av[:, :CW] * _sigmoid(av[:, CW:])
        ext[HALO + tm:, :] = jnp.zeros((8, CW), F32)
        _phases(ext, ph, tm + HALO)
        for rb in range(tm // RB):
            acc = jnp.zeros((RB, CW), F32) + cb_ref[...]
            for k in range(CONV_K):
                o = rb * RB + HALO - (CONV_K - 1) + k
                acc = acc + cw_ref[k:k + 1, :] * ph[o % 8, o - o % 8:o - o % 8 + RB, :]
            c_ref[rb * RB:(rb + 1) * RB, :] = acc
            mu = jnp.mean(acc, axis=-1, keepdims=True)
            var = jnp.mean(jnp.square(acc - mu), axis=-1, keepdims=True)
            z = ((acc - mu) * lax.rsqrt(var + EPS)) * lg_ref[...] + lb_ref[...]
            u_ref[rb * RB:(rb + 1) * RB, :] = (z * _sigmoid(z)).astype(CDT)

    return _call(
        body, name="f_conv", grid=(S // tm,),
        in_specs=[pl.BlockSpec((tm, 2 * CW), lambda i: (i, 0)),
                  pl.BlockSpec((HALO, 2 * CW), lambda i: (jnp.maximum(i * hb - 1, 0), 0)),
                  _resident((HALO, CW)), _resident((1, CW)), _resident((1, CW)), _resident((1, CW))],
        out_specs=[pl.BlockSpec((tm, CW), lambda i: (i, 0)), pl.BlockSpec((tm, CW), lambda i: (i, 0))],
        out_shape=[jax.ShapeDtypeStruct((S, CW), F32), jax.ShapeDtypeStruct((S, CW), CDT)],
        scratch_shapes=[pltpu.VMEM((tm + HALO + 8, CW), F32), pltpu.VMEM((8, tm + HALO, CW), F32)],
        compiler_params=_cp(("parallel",)),
    )(proj, proj, cw, cb, lg, lb)


def _rot(t, c, s):
    return t * c + pltpu.roll(t, HD // 2, 1) * s


def _f_ret(proj, u, cosT, sinT, Mt, qd, kd, gs, gn):
    S = proj.shape[0]
    RW = HEADS * HD
    NB = S // SC
    scale = HD ** -0.5

    def body(q_ref, k_ref, v_ref, g_ref, u_ref, c_ref, s_ref, m_ref, qd_ref, kd_ref, gs_ref, gn_ref,
             rraw_ref, st_ref, mix_ref, state):
        @pl.when(pl.program_id(0) == 0)
        def _():
            state[...] = jnp.zeros_like(state)

        mix_ref[:, 0:RW] = u_ref[...]
        cv = c_ref[...]
        sv = s_ref[...]
        for h in range(HEADS):
            cs = slice(h * HD, (h + 1) * HD)
            q = _rot(q_ref[:, cs], cv, sv)
            k = _rot(k_ref[:, cs], cv, sv) * scale
            vb = v_ref[:, cs].astype(CDT)
            qb = q.astype(CDT)
            kb = k.astype(CDT)
            a = _dot_nt(qb, kb) * m_ref[h]
            sp = state[h]
            spb = sp.astype(CDT)
            st_ref[0, h] = spb
            r = _dot(a.astype(CDT), vb) + _dot((q * qd_ref[h]).astype(CDT), spb)
            kv = _dot_tn((k * kd_ref[h]).astype(CDT), vb)
            state[h] = gs_ref[h, 0:1, :] * sp + kv
            rraw_ref[:, cs] = r
            mu = jnp.mean(r, axis=-1, keepdims=True)
            var = jnp.mean(jnp.square(r - mu), axis=-1, keepdims=True)
            n = (r - mu) * lax.rsqrt(var + EPS)
            gv = g_ref[:, cs]
            mix_ref[:, RW + h * HD:RW + (h + 1) * HD] = ((n * gn_ref[:, cs]) * (gv * _sigmoid(gv))).astype(CDT)

    col = lambda j: pl.BlockSpec((SC, RW), lambda i: (i, j))
    return _call(
        body, name="f_ret", grid=(NB,),
        in_specs=[col(2), col(3), col(4), col(5),
                  pl.BlockSpec((SC, RW), lambda i: (i, 0)),
                  pl.BlockSpec((SC, HD), lambda i: (i, 0)), pl.BlockSpec((SC, HD), lambda i: (i, 0)),
                  _resident((HEADS, SC, SC)), _resident((HEADS, SC, HD)), _resident((HEADS, SC, HD)),
                  _resident((HEADS, 8, HD)), _resident((1, RW))],
        out_specs=[pl.BlockSpec((SC, RW), lambda i: (i, 0)),
                   pl.BlockSpec((1, HEADS, HD, HD), lambda i: (i, 0, 0, 0)),
                   pl.BlockSpec((SC, 2 * RW), lambda i: (i, 0))],
        out_shape=[jax.ShapeDtypeStruct((S, RW), F32),
                   jax.ShapeDtypeStruct((NB, HEADS, HD, HD), CDT),
                   jax.ShapeDtypeStruct((S, 2 * RW), CDT)],
        scratch_shapes=[pltpu.VMEM((HEADS, HD, HD), F32)],
        compiler_params=_cp(("arbitrary",)),
    )(proj, proj, proj, proj, u, cosT, sinT, Mt, qd, kd, gs, gn)


def _f_mlp(x, mixed, g2, G, n_out, n_ff, parts, tm, comm=None):
    S, D = x.shape
    FF = N_DEV * n_ff
    p_wo, p_wg, p_wu, p_wd = parts

    def body(x_ref, m_ref, wo_ref, g_ref, wg_ref, wu_ref, wd_ref,
             xm_ref, h2_ref, gate_ref, up_ref, act_ref, xo_ref):
        xm = x_ref[...] + _dot(m_ref[...], wo_ref[...].reshape(N_DEV * n_out, D))
        xm_ref[...] = xm
        r = lax.rsqrt(jnp.mean(xm * xm, axis=-1, keepdims=True) + EPS)
        h2 = ((xm * r) * g_ref[...]).astype(CDT)
        h2_ref[...] = h2
        gate = _dot_nt(h2, wg_ref[...].reshape(FF, D))
        up = _dot_nt(h2, wu_ref[...].reshape(FF, D))
        gate_ref[...] = gate.astype(CDT)
        up_ref[...] = up.astype(CDT)
        act = ((gate * _sigmoid(gate)) * up).astype(CDT)
        act_ref[...] = act
        xo_ref[...] = xm + _dot(act, wd_ref[...].reshape(FF, D))

    row = lambda n: pl.BlockSpec((tm, n), lambda i: (i, 0))
    return _launch(
        "f_mlp" if comm is None else "f_mlp_ag", body, (S // tm,),
        [row(D), row(D), _wres(n_out, D, p_wo), _resident((1, D)),
         _wres(n_ff, D, p_wg), _wres(n_ff, D, p_wu), _wres(n_ff, D, p_wd)],
        [row(D), row(D), row(FF), row(FF), row(FF), row(D)],
        [jax.ShapeDtypeStruct((S, D), F32), jax.ShapeDtypeStruct((S, D), CDT),
         jax.ShapeDtypeStruct((S, FF), CDT), jax.ShapeDtypeStruct((S, FF), CDT),
         jax.ShapeDtypeStruct((S, FF), CDT), jax.ShapeDtypeStruct((S, D), F32)],
        [], (x, mixed, G, g2, G, G, G), ("parallel",), comm)


def _f_loss(x, fg, tgt, tm):
    S, D = x.shape

    def body(x_ref, g_ref, t_ref, dx_ref, dxb_ref, loss_ref, dg_ref):
        @pl.when(pl.program_id(0) == 0)
        def _():
            loss_ref[...] = jnp.zeros_like(loss_ref)
            dg_ref[...] = jnp.zeros_like(dg_ref)

        xv = x_ref[...]
        r = lax.rsqrt(jnp.mean(xv * xv, axis=-1, keepdims=True) + EPS)
        y = (xv * r) * g_ref[...]
        e = y - t_ref[...]
        loss_ref[...] += 0.5 * jnp.sum(jnp.mean(e * e, axis=-1, keepdims=True))
        dy = e * (1.0 / D)
        dx, dgx = _rms_bwd(xv, r, g_ref[...], dy)
        dg_ref[...] += jnp.sum(dgx, axis=0, keepdims=True)
        dx_ref[...] = dx
        dxb_ref[...] = dx.astype(CDT)

    row = pl.BlockSpec((tm, D), lambda i: (i, 0))
    return _call(
        body, name="f_loss", grid=(S // tm,),
        in_specs=[row, _resident((1, D)), row],
        out_specs=[row, row, pl.BlockSpec((1, 128), lambda i: (0, 0)), pl.BlockSpec((1, D), lambda i: (0, 0))],
        out_shape=[jax.ShapeDtypeStruct((S, D), F32), jax.ShapeDtypeStruct((S, D), CDT),
                   jax.ShapeDtypeStruct((1, 128), F32), jax.ShapeDtypeStruct((1, D), F32)],
        compiler_params=_cp(("arbitrary",)),
    )(x, fg, tgt)


def _b_mlp(dxb, dx, xm, gate, up, g2, G, n_out, n_ff, parts, tm, comm=None):
    S, D = dx.shape
    FF = N_DEV * n_ff
    p_wo, p_wg, p_wu, p_wd = parts

    def body(dxb_ref, dx_ref, xm_ref, gate_ref, up_ref, g_ref, wd_ref, wg_ref, wu_ref, wo_ref,
             dgate_ref, dup_ref, dxm_ref, dxmb_ref, dmix_ref, dg_ref):
        @pl.when(pl.program_id(0) == 0)
        def _():
            dg_ref[...] = jnp.zeros_like(dg_ref)

        dact = _dot_nt(dxb_ref[...], wd_ref[...].reshape(FF, D))
        gate = gate_ref[...].astype(F32)
        up = up_ref[...].astype(F32)
        sg = _sigmoid(gate)
        sil = gate * sg
        dgate = ((dact * up) * (sg * (1.0 + gate * (1.0 - sg)))).astype(CDT)
        dup = (dact * sil).astype(CDT)
        dgate_ref[...] = dgate
        dup_ref[...] = dup
        dh2 = _dot(dgate, wg_ref[...].reshape(FF, D)) + _dot(dup, wu_ref[...].reshape(FF, D))
        xm = xm_ref[...]
        r = lax.rsqrt(jnp.mean(xm * xm, axis=-1, keepdims=True) + EPS)
        dxn, dgx = _rms_bwd(xm, r, g_ref[...], dh2)
        dg_ref[...] += jnp.sum(dgx, axis=0, keepdims=True)
        dxm = dx_ref[...] + dxn
        dxm_ref[...] = dxm
        dxmb = dxm.astype(CDT)
        dxmb_ref[...] = dxmb
        dmix_ref[...] = _dot_nt(dxmb, wo_ref[...].reshape(N_DEV * n_out, D))

    row = lambda n: pl.BlockSpec((tm, n), lambda i: (i, 0))
    return _launch(
        "b_mlp" if comm is None else "b_mlp_rs", body, (S // tm,),
        [row(D), row(D), row(D), row(FF), row(FF), _resident((1, D)),
         _wres(n_ff, D, p_wd), _wres(n_ff, D, p_wg), _wres(n_ff, D, p_wu), _wres(n_out, D, p_wo)],
        [row(FF), row(FF), row(D), row(D), row(D), pl.BlockSpec((1, D), lambda i: (0, 0))],
        [jax.ShapeDtypeStruct((S, FF), CDT), jax.ShapeDtypeStruct((S, FF), CDT),
         jax.ShapeDtypeStruct((S, D), F32), jax.ShapeDtypeStruct((S, D), CDT),
         jax.ShapeDtypeStruct((S, D), F32), jax.ShapeDtypeStruct((1, D), F32)],
        [], (dxb, dx, xm, gate, up, g2, G, G, G, G), ("arbitrary",), comm)


def _b_conv(dmix, c, proj, cw, lg, lb, tm):
    S = proj.shape[0]
    CW = cw.shape[1]
    RB = 64
    hb = tm // HALO
    nt = S // tm
    last_h = S // HALO - 1

    def body(du_ref, duh_ref, c_ref, ch_ref, ab_ref, abh_ref, cw_ref, lg_ref, lb_ref,
             dab_ref, dcw_ref, dcb_ref, dlg_ref, dlb_ref, ext_u, ext_dc, ph_u, ph_dc, wacc):
        i = pl.program_id(0)

        @pl.when(i == 0)
        def _():
            wacc[...] = jnp.zeros_like(wacc)
            dcb_ref[...] = jnp.zeros_like(dcb_ref)
            dlg_ref[...] = jnp.zeros_like(dlg_ref)
            dlb_ref[...] = jnp.zeros_like(dlb_ref)

        def ln_bwd(cv, du):
            mu = jnp.mean(cv, axis=-1, keepdims=True)
            var = jnp.mean(jnp.square(cv - mu), axis=-1, keepdims=True)
            rstd = lax.rsqrt(var + EPS)
            n = (cv - mu) * rstd
            z = n * lg_ref[...] + lb_ref[...]
            sz = _sigmoid(z)
            dz = du * (sz * (1.0 + z * (1.0 - sz)))
            dn = dz * lg_ref[...]
            dc = rstd * (dn - jnp.mean(dn, axis=-1, keepdims=True)
                         - n * jnp.mean(dn * n, axis=-1, keepdims=True))
            return dc, dz, n

        hv = abh_ref[...]
        ext_u[0:HALO, :] = jnp.where(i > 0, hv[:, :CW] * _sigmoid(hv[:, CW:]), 0.0)
        av = ab_ref[...]
        sb = _sigmoid(av[:, CW:])
        ext_u[HALO:HALO + tm, :] = av[:, :CW] * sb
        ext_u[HALO + tm:, :] = jnp.zeros((8, CW), F32)
        dc, dz, n = ln_bwd(c_ref[...], du_ref[...])
        ext_dc[0:tm, :] = dc
        dch, _, _ = ln_bwd(ch_ref[...], duh_ref[...])
        ext_dc[tm:tm + HALO, :] = jnp.where(i < nt - 1, dch, 0.0)
        ext_dc[tm + HALO:, :] = jnp.zeros((8, CW), F32)
        dlg_ref[...] += jnp.sum(dz * n, axis=0, keepdims=True)
        dlb_ref[...] += jnp.sum(dz, axis=0, keepdims=True)
        dcb_ref[...] += jnp.sum(dc, axis=0, keepdims=True)
        _phases(ext_u, ph_u, tm + HALO)
        _phases(ext_dc, ph_dc, tm + HALO)

        for rb in range(tm // RB):
            rs = slice(rb * RB, (rb + 1) * RB)
            dcb = ext_dc[rs, :]
            for k in range(CONV_K):
                o = rb * RB + HALO - (CONV_K - 1) + k
                prod = dcb * ph_u[o % 8, o - o % 8:o - o % 8 + RB, :]
                part = prod[0:8]
                for j in range(1, RB // 8):
                    part = part + prod[8 * j:8 * j + 8]
                wacc[k] += part
            acc = jnp.zeros((RB, CW), F32)
            for k in range(CONV_K):
                o = rb * RB + (CONV_K - 1) - k
                acc = acc + cw_ref[k:k + 1, :] * ph_dc[o % 8, o - o % 8:o - o % 8 + RB, :]
            a_r = ab_ref[rs, 0:CW]
            s_r = _sigmoid(ab_ref[rs, CW:2 * CW])
            dab_ref[rs, 0:CW] = (acc * s_r).astype(CDT)
            dab_ref[rs, CW:2 * CW] = (acc * a_r * (s_r * (1.0 - s_r))).astype(CDT)

        @pl.when(i == nt - 1)
        def _():
            for k in range(CONV_K):
                dcw_ref[k:k + 1, :] = jnp.sum(wacc[k], axis=0, keepdims=True)
            dcw_ref[CONV_K:, :] = jnp.zeros((HALO - CONV_K, CW), F32)

    tile = lambda n, j: pl.BlockSpec((tm, n), lambda i: (i, j))
    nxt = lambda n, j: pl.BlockSpec((HALO, n), lambda i: (jnp.minimum((i + 1) * hb, last_h), j))
    return _call(
        body, name="b_conv", grid=(nt,),
        in_specs=[tile(CW, 0), nxt(CW, 0), tile(CW, 0), nxt(CW, 0),
                  tile(2 * CW, 0),
                  pl.BlockSpec((HALO, 2 * CW), lambda i: (jnp.maximum(i * hb - 1, 0), 0)),
                  _resident((HALO, CW)), _resident((1, CW)), _resident((1, CW))],
        out_specs=[tile(2 * CW, 0),
                   pl.BlockSpec((HALO, CW), lambda i: (0, 0)), pl.BlockSpec((1, CW), lambda i: (0, 0)),
                   pl.BlockSpec((1, CW), lambda i: (0, 0)), pl.BlockSpec((1, CW), lambda i: (0, 0))],
        out_shape=[jax.ShapeDtypeStruct((S, 2 * CW), CDT),
                   jax.ShapeDtypeStruct((HALO, CW), F32), jax.ShapeDtypeStruct((1, CW), F32),
                   jax.ShapeDtypeStruct((1, CW), F32), jax.ShapeDtypeStruct((1, CW), F32)],
        scratch_shapes=[pltpu.VMEM((tm + HALO + 8, CW), F32), pltpu.VMEM((tm + HALO + 8, CW), F32),
                        pltpu.VMEM((8, tm + HALO, CW), F32), pltpu.VMEM((8, tm + HALO, CW), F32),
                        pltpu.VMEM((HALO, 8, CW), F32)],
        compiler_params=_cp(("arbitrary",)),
    )(dmix, dmix, c, c, proj, proj, cw, lg, lb)


def _b_ret(dmix, dab, proj, rraw, states, cosT, sinT, Mt, qd, kd, gs, gn):
    S = proj.shape[0]
    RW = HEADS * HD
    NB = S // SC
    scale = HD ** -0.5

    def body(dro_ref, dab_ref, q_ref, k_ref, v_ref, g_ref, rraw_ref, st_ref, c_ref, s_ref,
             m_ref, qd_ref, kd_ref, gs_ref, gn_ref, dp_ref, dgn_ref, G):
        @pl.when(pl.program_id(0) == 0)
        def _():
            G[...] = jnp.zeros_like(G)
            dgn_ref[...] = jnp.zeros_like(dgn_ref)

        dp_ref[:, 0:2 * RW] = dab_ref[...]
        cv = c_ref[...]
        sv = s_ref[...]
        for h in range(HEADS):
            cs = slice(h * HD, (h + 1) * HD)
            q = _rot(q_ref[:, cs], cv, sv)
            k = _rot(k_ref[:, cs], cv, sv) * scale
            qb = q.astype(CDT)
            kb = k.astype(CDT)
            vb = v_ref[:, cs].astype(CDT)
            spb = st_ref[0, h]
            r = rraw_ref[:, cs]
            mu = jnp.mean(r, axis=-1, keepdims=True)
            var = jnp.mean(jnp.square(r - mu), axis=-1, keepdims=True)
            rstd = lax.rsqrt(var + EPS)
            n = (r - mu) * rstd
            gv = g_ref[:, cs]
            sg = _sigmoid(gv)
            sil = gv * sg
            dro = dro_ref[:, cs]
            gnv = gn_ref[:, cs]
            dgn_ref[:, cs] += jnp.sum(dro * n * sil, axis=0, keepdims=True)
            dgate = dro * (n * gnv) * (sg * (1.0 + gv * (1.0 - sg)))
            dn = dro * gnv * sil
            dr = rstd * (dn - jnp.mean(dn, axis=-1, keepdims=True)
                         - n * jnp.mean(dn * n, axis=-1, keepdims=True))
            drb = dr.astype(CDT)
            mh = m_ref[h]
            ab = (_dot_nt(qb, kb) * mh).astype(CDT)
            dab_ = (_dot_nt(drb, vb) * mh).astype(CDT)
            qdb = (q * qd_ref[h]).astype(CDT)
            kdb = (k * kd_ref[h]).astype(CDT)
            gc = G[h]
            gb = gc.astype(CDT)
            dq = _dot(dab_, kb) + _dot_nt(drb, spb) * qd_ref[h]
            dk = _dot_tn(dab_, qb) + _dot_nt(vb, gb) * kd_ref[h]
            dv = _dot_tn(ab, drb) + _dot(kdb, gb)
            G[h] = gs_ref[h, 0:1, :] * gc + _dot_tn(qdb, drb)
            dk = dk * scale
            dqp = dq * cv + pltpu.roll(dq * sv, HD // 2, 1)
            dkp = dk * cv + pltpu.roll(dk * sv, HD // 2, 1)
            base = 2 * RW
            dp_ref[:, base + h * HD:base + (h + 1) * HD] = dqp.astype(CDT)
            dp_ref[:, base + RW + h * HD:base + RW + (h + 1) * HD] = dkp.astype(CDT)
            dp_ref[:, base + 2 * RW + h * HD:base + 2 * RW + (h + 1) * HD] = dv.astype(CDT)
            dp_ref[:, base + 3 * RW + h * HD:base + 3 * RW + (h + 1) * HD] = dgate.astype(CDT)

    rev = lambda n, j: pl.BlockSpec((SC, n), lambda i: (NB - 1 - i, j))
    return _call(
        body, name="b_ret", grid=(NB,),
        in_specs=[rev(RW, 1), rev(2 * RW, 0), rev(RW, 2), rev(RW, 3), rev(RW, 4), rev(RW, 5), rev(RW, 0),
                  pl.BlockSpec((1, HEADS, HD, HD), lambda i: (NB - 1 - i, 0, 0, 0)),
                  rev(HD, 0), rev(HD, 0),
                  _resident((HEADS, SC, SC)), _resident((HEADS, SC, HD)), _resident((HEADS, SC, HD)),
                  _resident((HEADS, 8, HD)), _resident((1, RW))],
        out_specs=[rev(6 * RW, 0), pl.BlockSpec((1, RW), lambda i: (0, 0))],
        out_shape=[jax.ShapeDtypeStruct((S, 6 * RW), CDT), jax.ShapeDtypeStruct((1, RW), F32)],
        scratch_shapes=[pltpu.VMEM((HEADS, HD, HD), F32)],
        compiler_params=_cp(("arbitrary",)),
    )(dmix, dab, proj, proj, proj, proj, rraw, states, cosT, sinT, Mt, qd, kd, gs, gn)


def _b_in(dproj, G, n_in, x, g1, dxm, tm, comm=None):
    S, D = x.shape
    N = N_DEV * n_in

    def body(dp_ref, w_ref, x_ref, g_ref, dxm_ref, dx_ref, dxb_ref, dg_ref):
        @pl.when(pl.program_id(0) == 0)
        def _():
            dg_ref[...] = jnp.zeros_like(dg_ref)

        dh = _dot(dp_ref[...], w_ref[...].reshape(N, D))
        xv = x_ref[...]
        r = lax.rsqrt(jnp.mean(xv * xv, axis=-1, keepdims=True) + EPS)
        dxn, dgx = _rms_bwd(xv, r, g_ref[...], dh)
        dg_ref[...] += jnp.sum(dgx, axis=0, keepdims=True)
        dx = dxm_ref[...] + dxn
        dx_ref[...] = dx
        dxb_ref[...] = dx.astype(CDT)

    row = lambda n: pl.BlockSpec((tm, n), lambda i: (i, 0))
    return _launch(
        "b_in" if comm is None else "b_in_rs", body, (S // tm,),
        [row(N), _wres(n_in, D, 0), row(D), _resident((1, D)), row(D)],
        [row(D), row(D), pl.BlockSpec((1, D), lambda i: (0, 0))],
        [jax.ShapeDtypeStruct((S, D), F32), jax.ShapeDtypeStruct((S, D), CDT),
         jax.ShapeDtypeStruct((1, D), F32)],
        [], (dproj, G, x, g1, dxm), ("arbitrary",), comm)


def _dw_tn(a, b, tm, tk):
    S, M = a.shape
    N = b.shape[1]

    def body(a_ref, b_ref, o_ref):
        @pl.when(pl.program_id(1) == 0)
        def _():
            o_ref[...] = jnp.zeros_like(o_ref)

        o_ref[...] += _dot_tn(a_ref[...], b_ref[...])

    return _call(
        body, name="dw_tn", grid=(M // tm, S // tk),
        in_specs=[pl.BlockSpec((tk, tm), lambda m, k: (k, m)), pl.BlockSpec((tk, N), lambda m, k: (k, 0))],
        out_specs=pl.BlockSpec((tm, N), lambda m, k: (m, 0)),
        out_shape=jax.ShapeDtypeStruct((M, N), F32),
        compiler_params=_cp(("parallel", "arbitrary")),
    )(a, b)


def _add_pair(parts, recv):
    K = len(parts)
    C = parts[0].shape[2]
    halves = 2

    def body(*refs):
        cc = lax.axis_index("c")
        for k in range(K):
            s = refs[k][cc] + refs[K + k][...]
            refs[2 * K + k][...] = s
            refs[3 * K + k][...] = s.astype(CDT)

    ns = [p.shape[1] for p in parts]
    in_specs = [pl.BlockSpec((None, 2, n // halves, C), lambda q, r: (q, 0, r, 0)) for n in ns]
    in_specs += [pl.BlockSpec((None, n // halves, C), lambda q, r: (q, r, 0)) for n in ns]
    outb = [pl.BlockSpec((None, n // halves, C), lambda q, r: (q, r, 0)) for n in ns]
    res = _call(
        body, name="add_pair", grid=(N_DEV // 2, halves),
        in_specs=in_specs, out_specs=outb + outb,
        out_shape=[jax.ShapeDtypeStruct((N_DEV // 2, n, C), F32) for n in ns]
        + [jax.ShapeDtypeStruct((N_DEV // 2, n, C), CDT) for n in ns],
        compiler_params=_cp(("parallel", "parallel")),
    )(*[p.reshape(N_DEV // 2, 2, p.shape[1], C) for p in parts], *recv)
    return list(res[:K]), list(res[K:])


def _sum_chips(sums, recv):
    K = len(sums)

    def body(*refs):
        chip = 2 * lax.axis_index("x") + lax.axis_index("y")
        for k in range(K):
            r = refs[K + k]
            refs[2 * K + k][...] = ((refs[k][chip] + r[0].astype(F32)) + r[1].astype(F32)) + r[2].astype(F32)

    vm = pl.BlockSpec(memory_space=pltpu.VMEM)
    res = _call(
        body, name="sum_chips", in_specs=[vm] * (2 * K), out_specs=[vm] * K,
        out_shape=[jax.ShapeDtypeStruct(s.shape[1:], F32) for s in sums],
        compiler_params=_cp(),
    )(*sums, *recv)
    return list(res)


def _adamw(w, g, m, v, tr):
    R, C = w.shape
    c1 = 1.0 - ADAM_B1 ** ADAM_STEP
    c2 = 1.0 - ADAM_B2 ** ADAM_STEP

    def body(w_ref, g_ref, m_ref, v_ref, d_ref, mo_ref, vo_ref):
        gv = g_ref[...]
        mn = ADAM_B1 * m_ref[...] + (1.0 - ADAM_B1) * gv
        vn = ADAM_B2 * v_ref[...] + (1.0 - ADAM_B2) * jnp.square(gv)
        mo_ref[...] = mn
        vo_ref[...] = vn
        d_ref[...] = -ADAM_LR * ((mn / c1) / (jnp.sqrt(vn / c2) + ADAM_EPS) + ADAM_WD * w_ref[...])

    blk = pl.BlockSpec((tr, C), lambda i: (i, 0))
    sh = jax.ShapeDtypeStruct((R, C), F32)
    return _call(
        body, name="adamw", grid=(R // tr,),
        in_specs=[blk, blk, blk, blk], out_specs=[blk, blk, blk], out_shape=[sh, sh, sh],
        compiler_params=_cp(("parallel",)),
    )(w, g, m, v)


def _coords():
    return lax.axis_index("x"), lax.axis_index("y"), lax.axis_index("c")


def _peer(x, y, c, d):
    return (x ^ (d >> 2), y ^ ((d >> 1) & 1), c ^ (d & 1))


def _ag_comm(p):
    R, C = p.shape

    def plan(cins, couts, sems):
        x_ref, out_ref = cins[0], couts[0]
        send_sems, recv_sems, local_sem = sems
        x, y, c = _coords()
        me, sibling = (x, y, c), (x, y, 1 - c)
        chips = [(1 - x, y), (x, 1 - y), (1 - x, 1 - y)]

        def rows(px, py, pc):
            return out_ref.at[pl.ds((4 * px + 2 * py + pc) * R, R), :]

        def copy(k, block, to, src=None):
            return pltpu.make_async_remote_copy(
                src_ref=rows(*block) if src is None else src, dst_ref=rows(*block),
                send_sem=send_sems.at[k], recv_sem=recv_sems.at[k], device_id=to, device_id_type=MESH)

        mine = pltpu.make_async_copy(x_ref, rows(*me), local_sem)
        first = [copy(0, me, sibling, src=x_ref)]
        first += [copy(1 + j, me, (*chip, c), src=x_ref) for j, chip in enumerate(chips)]
        passed = [copy(4 + j, (*chip, c), sibling) for j, chip in enumerate(chips)]
        got_ici = [copy(1 + j, (*chip, c), me) for j, chip in enumerate(chips)]
        got_d2d = [copy(0, sibling, me)] + [copy(4 + j, (*chip, 1 - c), me) for j, chip in enumerate(chips)]
        return mine, first, passed, got_ici, got_d2d

    def start(*a):
        mine, first, _, _, _ = plan(*a)
        mine.start()
        for cp in first:
            cp.start()

    def mid(*a):
        _, _, passed, got_ici, _ = plan(*a)
        for got, fwd in zip(got_ici, passed):
            got.wait_recv()
            fwd.start()

    def finish(*a):
        mine, first, passed, _, got_d2d = plan(*a)
        for got in got_d2d:
            got.wait_recv()
        for cp in first + passed:
            cp.wait_send()
        mine.wait()

    return _Comm([p], [jax.ShapeDtypeStruct((N_DEV * R, C), p.dtype)],
                 [pltpu.SemaphoreType.DMA((7,)), pltpu.SemaphoreType.DMA((7,)), pltpu.SemaphoreType.DMA],
                 start, mid, finish)


def _rs_pair_comm(parts):
    K = len(parts)

    def plan(cins, couts, sems):
        send_sems, recv_sems = sems
        x, y, c = _coords()
        sibling = (x, y, 1 - c)
        cps = []
        for k in range(K):
            for q in range(N_DEV // 2):
                cps.append(pltpu.make_async_remote_copy(
                    src_ref=cins[k].at[2 * q + (1 - c)], dst_ref=couts[k].at[q],
                    send_sem=send_sems.at[k], recv_sem=recv_sems.at[k], device_id=sibling, device_id_type=MESH))
        whole = [pltpu.make_async_remote_copy(
            src_ref=couts[k], dst_ref=couts[k], send_sem=send_sems.at[k], recv_sem=recv_sems.at[k],
            device_id=sibling, device_id_type=MESH) for k in range(K)]
        return cps, whole

    def start(*a):
        for cp in plan(*a)[0]:
            cp.start()

    def finish(*a):
        for w in plan(*a)[1]:
            w.wait_recv()
            w.wait_send()

    return _Comm(parts, [jax.ShapeDtypeStruct((N_DEV // 2,) + p.shape[1:], p.dtype) for p in parts],
                 [pltpu.SemaphoreType.DMA((K,)), pltpu.SemaphoreType.DMA((K,))], start, None, finish)


def _rs_chip_comm(sums):
    K = len(sums)

    def plan(cins, couts, sems):
        send_sems, recv_sems = sems
        x, y, c = _coords()
        cps = []
        for d in range(1, N_DEV // 2):
            px, py = x ^ (d >> 1), y ^ (d & 1)
            for k in range(K):
                s = (d - 1) * K + k
                cps.append(pltpu.make_async_remote_copy(
                    src_ref=cins[k].at[2 * px + py], dst_ref=couts[k].at[d - 1],
                    send_sem=send_sems.at[s], recv_sem=recv_sems.at[s], device_id=(px, py, c), device_id_type=MESH))
        return cps

    def start(*a):
        for cp in plan(*a):
            cp.start()

    def finish(*a):
        cps = plan(*a)
        for cp in cps:
            cp.wait_recv()
        for cp in cps:
            cp.wait_send()

    n_sem = (N_DEV // 2 - 1) * K
    return _Comm(sums, [jax.ShapeDtypeStruct((N_DEV // 2 - 1,) + s.shape[1:], s.dtype) for s in sums],
                 [pltpu.SemaphoreType.DMA((n_sem,)), pltpu.SemaphoreType.DMA((n_sem,))], start, None, finish)


def _gather_small(v, reduce):
    R, C = v.shape

    def exchange(v_ref, buf, send_sems, recv_sems):
        x, y, c = _coords()
        me = 4 * x + 2 * y + c
        buf[me] = v_ref[...]
        cps = []
        for d in range(1, N_DEV):
            cp = pltpu.make_async_remote_copy(
                src_ref=v_ref, dst_ref=buf.at[me], send_sem=send_sems.at[d - 1], recv_sem=recv_sems.at[d - 1],
                device_id=_peer(x, y, c, d), device_id_type=MESH)
            cp.start()
            cps.append(cp)
        for cp in cps:
            cp.wait_recv()
        for cp in cps:
            cp.wait_send()

    sems = [pltpu.SemaphoreType.DMA((7,)), pltpu.SemaphoreType.DMA((7,))]
    vm = pl.BlockSpec(memory_space=pltpu.VMEM)
    if reduce:
        def body(v_ref, o_ref, buf, send_sems, recv_sems):
            exchange(v_ref, buf, send_sems, recv_sems)
            acc = buf[0]
            for s in range(1, N_DEV):
                acc = acc + buf[s]
            o_ref[...] = acc

        return _call(body, name="allreduce_small", in_specs=[vm], out_specs=vm,
                     out_shape=jax.ShapeDtypeStruct((R, C), F32),
                     scratch_shapes=[pltpu.VMEM((N_DEV, R, C), F32)] + sems)(v)

    def body(v_ref, o_ref, send_sems, recv_sems):
        exchange(v_ref, o_ref, send_sems, recv_sems)

    return _call(body, name="allgather_small", in_specs=[vm], out_specs=vm,
                 out_shape=jax.ShapeDtypeStruct((N_DEV, R, C), F32), scratch_shapes=sems)(v)


def _tables(S):
    half = HD // 2
    pos = jnp.arange(S, dtype=F32)
    freqs = ROPE_BASE ** (-jnp.arange(half, dtype=F32) / half)
    ang = pos[:, None] * freqs[None, :]
    cos, sin = jnp.cos(ang), jnp.sin(ang)
    cosT = jnp.concatenate([cos, cos], axis=-1)
    sinT = jnp.concatenate([-sin, sin], axis=-1)
    log_g = jnp.log(1.0 - 2.0 ** (-5.0 - jnp.arange(HEADS, dtype=F32)))
    idx = jnp.arange(SC, dtype=F32)
    ci = jnp.arange(SC) // CHUNK
    diff = idx[:, None] - idx[None, :]
    same = ci[:, None] == ci[None, :]
    earlier = ci[None, :] < ci[:, None]
    expo = jnp.where(same, jnp.abs(diff), diff)
    Mt = jnp.where((same | earlier)[None], jnp.exp(log_g[:, None, None] * expo[None]), 0.0)
    ones = jnp.ones((1, 1, HD), F32)
    qd = jnp.exp(log_g[:, None] * (idx + 1.0)[None, :])[:, :, None] * ones
    kd = jnp.exp(log_g[:, None] * (SC - 1.0 - idx)[None, :])[:, :, None] * ones
    gs = jnp.exp(log_g * SC)[:, None, None] * jnp.ones((1, 8, HD), F32)
    return cosT, sinT, Mt, qd, kd, gs


def _pad_rows(a, rows):
    return jnp.pad(a, ((0, rows - a.shape[0]), (0, 0)))


def kernel(x, norm1_g, w_in, conv_w, conv_b, conv_ln_g, conv_ln_b, ret_gn_g, w_out, norm2_g, w_gate, w_up, w_down, final_g, loss_target, m_norm1_g, m_w_in, m_conv_w, m_conv_b, m_conv_ln_g, m_conv_ln_b, m_ret_gn_g, m_w_out, m_norm2_g, m_w_gate, m_w_up, m_w_down, m_final_g, v_norm1_g, v_w_in, v_conv_w, v_conv_b, v_conv_ln_g, v_conv_ln_b, v_ret_gn_g, v_w_out, v_norm2_g, v_w_gate, v_w_up, v_w_down, v_final_g):
    L, D, n_in = w_in.shape
    n_out = w_out.shape[1]
    n_ff = w_gate.shape[2]
    S = x.shape[1]
    CW = conv_b.shape[1]
    IN, FF = N_DEV * n_in, N_DEV * n_ff
    ncw = conv_w.shape[2]
    x0 = x.reshape(S, D)
    tgt = loss_target.reshape(S, D)
    TM = min(512, S)
    TKW = min(1024, S)
    TMI = min(512, S)
    TMM = min(256, S)

    assert n_in % n_out == 0
    o_ff = -(-(n_in + n_out) // n_ff) * n_ff
    RL = o_ff + 3 * n_ff
    wparts = (n_in // n_out, o_ff // n_ff, o_ff // n_ff + 1, o_ff // n_ff + 2)
    pack = jnp.concatenate([jnp.swapaxes(w_in, 1, 2), w_out, jnp.zeros((L, o_ff - n_in - n_out, D), F32),
                            jnp.swapaxes(w_gate, 1, 2), jnp.swapaxes(w_up, 1, 2), w_down],
                           axis=1).astype(CDT)
    Gl = _comm_only(_ag_comm(pack[0]), "ag_first")[0].reshape(N_DEV, RL, D)

    cwp = conv_w.reshape(L * CONV_K * ncw // 128, 128)
    cw_rows = -(-cwp.shape[0] // 8) * 8
    cwg = _gather_small(_pad_rows(cwp, cw_rows), reduce=False)[:, :cwp.shape[0], :]
    conv_w_full = jnp.moveaxis(cwg.reshape(N_DEV, L, CONV_K, ncw), 0, 2).reshape(L, CONV_K, CW)

    cosT, sinT, Mt, qd, kd, gs = _tables(S)

    saved = []
    xl = x0
    for l in range(L):
        cw = _pad_rows(conv_w_full[l], HALO)
        h, proj = _f_in(xl, norm1_g[l][None], Gl, n_in, TMI)
        c, u = _f_conv(proj, cw, conv_b[l][None], conv_ln_g[l][None], conv_ln_b[l][None], TM)
        rraw, states, mixed = _f_ret(proj, u, cosT, sinT, Mt, qd, kd, gs, ret_gn_g[l][None])
        (xm, h2, gate, up, act, xo), nxt = _f_mlp(xl, mixed, norm2_g[l][None], Gl, n_out, n_ff, wparts, TMM,
                                                 _ag_comm(pack[l + 1]) if l + 1 < L else None)
        saved.append(dict(x=xl, h=h, proj=proj, c=c, rraw=rraw, states=states, mixed=mixed, xm=xm, h2=h2,
                          gate=gate, up=up, act=act, cw=cw, G=Gl))
        if nxt:
            Gl = nxt[0].reshape(N_DEV, RL, D)
        xl = xo

    dx, dxb, loss_p, dfg = _f_loss(xl, final_g[None], tgt, TM)

    small = []
    chip_sums = [None] * L
    chip_recv = [None] * L
    in_flight = None
    for l in reversed(range(L)):
        sv = saved[l]
        (dgate, dup, dxm, dxmb, dmix, dg2), got = _b_mlp(
            dxb, dx, sv["xm"], sv["gate"], sv["up"], norm2_g[l][None], sv["G"], n_out, n_ff, wparts, TMM, in_flight)
        if got:
            chip_recv[l + 1] = got
        d_wd = _dw_tn(sv["act"], dxb, FF // 2, TKW)
        d_wgT = _dw_tn(dgate, sv["h2"], FF // 2, TKW)
        d_wuT = _dw_tn(dup, sv["h2"], FF // 2, TKW)
        d_wo = _dw_tn(sv["mixed"], dxmb, D, TKW)
        dab, dcw, dcb, dlg, dlb = _b_conv(dmix, sv["c"], sv["proj"], sv["cw"], conv_ln_g[l][None],
                                          conv_ln_b[l][None], TM)
        dproj, dgn = _b_ret(dmix, dab, sv["proj"], sv["rraw"], sv["states"], cosT, sinT, Mt, qd, kd, gs,
                            ret_gn_g[l][None])
        d_winT = _dw_tn(dproj, sv["h"], IN // 2, TKW)
        parts = [d.reshape(N_DEV, -1, D) for d in (d_winT, d_wo, d_wgT, d_wuT, d_wd)]
        (dx, dxb, dg1), pair_recv = _b_in(dproj, sv["G"], n_in, sv["x"], norm1_g[l][None], dxm, TMI,
                                          _rs_pair_comm(parts))
        chip_sums[l], sums_b = _add_pair(parts, pair_recv)
        in_flight = _rs_chip_comm(sums_b)
        small.append(jnp.concatenate([dcw, dcb, dlg, dlb, dgn, dg1.reshape(2, CW), dg2.reshape(2, CW)], axis=0))
    chip_recv[0] = _comm_only(in_flight, "rs_last")
    small = small[::-1]
    grad_x = dx.reshape(1, S, D)

    rows_l = HALO + 8
    loss_row = jnp.zeros((1, CW), F32).at[0, 0].set(loss_p[0, 0])
    sm = jnp.concatenate(small + [dfg.reshape(2, CW), loss_row], axis=0)
    sm_rows = -(-sm.shape[0] // 8) * 8
    sm = _gather_small(_pad_rows(sm, sm_rows), reduce=True)
    loss = sm[L * rows_l + 2, 0]
    g_final = sm[L * rows_l:L * rows_l + 2].reshape(D)
    per = sm[:L * rows_l].reshape(L, rows_l, CW)
    me = 4 * lax.axis_index("x") + 2 * lax.axis_index("y") + lax.axis_index("c")
    g_conv_w = lax.dynamic_slice_in_dim(per[:, :CONV_K, :], me * ncw, ncw, axis=2)
    g_conv_b, g_ln_g, g_ln_b, g_gn = per[:, HALO], per[:, HALO + 1], per[:, HALO + 2], per[:, HALO + 3]
    g_n1 = per[:, HALO + 4:HALO + 6].reshape(L, D)
    g_n2 = per[:, HALO + 6:HALO + 8].reshape(L, D)

    gl = [_sum_chips(chip_sums[l], chip_recv[l]) for l in range(L)]
    g_w_in = jnp.stack([gl[l][0].T for l in range(L)])
    g_w_out = jnp.stack([gl[l][1] for l in range(L)])
    g_w_gate = jnp.stack([gl[l][2].T for l in range(L)])
    g_w_up = jnp.stack([gl[l][3].T for l in range(L)])
    g_w_down = jnp.stack([gl[l][4] for l in range(L)])

    def big(w, g, m, v):
        sh = w.shape
        two = lambda a: a.reshape(-1, sh[-1])
        rows = two(w).shape[0]
        d, mn, vn = _adamw(two(w), two(g), two(m), two(v), rows // 8)
        return d.reshape(sh), mn.reshape(sh), vn.reshape(sh)

    names = ["norm1_g", "conv_w", "conv_b", "conv_ln_g", "conv_ln_b", "ret_gn_g", "norm2_g", "final_g"]
    sw = dict(norm1_g=(norm1_g, g_n1, m_norm1_g, v_norm1_g), conv_w=(conv_w, g_conv_w, m_conv_w, v_conv_w),
              conv_b=(conv_b, g_conv_b, m_conv_b, v_conv_b), conv_ln_g=(conv_ln_g, g_ln_g, m_conv_ln_g, v_conv_ln_g),
              conv_ln_b=(conv_ln_b, g_ln_b, m_conv_ln_b, v_conv_ln_b), ret_gn_g=(ret_gn_g, g_gn, m_ret_gn_g, v_ret_gn_g),
              norm2_g=(norm2_g, g_n2, m_norm2_g, v_norm2_g), final_g=(final_g, g_final, m_final_g, v_final_g))
    lens = [int(math.prod(sw[n][0].shape)) for n in names]
    tot = sum(lens)
    prow = -(-tot // (8 * CW)) * 8

    def packs(j):
        flat = jnp.concatenate([sw[n][j].reshape(-1) for n in names])
        return jnp.pad(flat, (0, prow * CW - tot)).reshape(prow, CW)

    sd, smn, svn = _adamw(packs(0), packs(1), packs(2), packs(3), prow)

    def unpack(a):
        flat = a.reshape(-1)
        out, o = {}, 0
        for n, ln in zip(names, lens):
            out[n] = flat[o:o + ln].reshape(sw[n][0].shape)
            o += ln
        return out

    sd, smn, svn = unpack(sd), unpack(smn), unpack(svn)
    res = {n: (sw[n][1], sd[n], smn[n], svn[n]) for n in names}
    res["w_in"] = (g_w_in,) + big(w_in, g_w_in, m_w_in, v_w_in)
    res["w_out"] = (g_w_out,) + big(w_out, g_w_out, m_w_out, v_w_out)
    res["w_gate"] = (g_w_gate,) + big(w_gate, g_w_gate, m_w_gate, v_w_gate)
    res["w_up"] = (g_w_up,) + big(w_up, g_w_up, m_w_up, v_w_up)
    res["w_down"] = (g_w_down,) + big(w_down, g_w_down, m_w_down, v_w_down)

    order = ["norm1_g", "w_in", "conv_w", "conv_b", "conv_ln_g", "conv_ln_b", "ret_gn_g", "w_out", "norm2_g",
             "w_gate", "w_up", "w_down", "final_g"]
    return (loss, grad_x, *[res[n][0] for n in order], *[res[n][1] for n in order],
            *[res[n][2] for n in order], *[res[n][3] for n in order])
```

```python
import functools
import math

import jax
import jax.numpy as jnp
from jax import lax
from jax.experimental import pallas as pl
from jax.experimental.pallas import tpu as pltpu

F32 = jnp.float32
CDT = jnp.bfloat16
EPS = 1e-6
CHUNK = 64
SC = 256
HEADS = 4
HD = 128
CONV_K = 31
HALO = 32
ROPE_BASE = 10000.0
ADAM_LR = 0.001
ADAM_B1 = 0.9
ADAM_B2 = 0.999
ADAM_EPS = 1e-08
ADAM_WD = 0.01
ADAM_STEP = 10
N_DEV = 8
MESH = pl.DeviceIdType.MESH
VMEM_LIMIT = 60 * 1024 * 1024


def _call(body, **kw):
    return pl.pallas_call(body, **kw)


def _cp(sem=None, vmem=VMEM_LIMIT):
    return pltpu.CompilerParams(dimension_semantics=sem, vmem_limit_bytes=vmem)


def _resident(shape):
    nd = len(shape)
    return pl.BlockSpec(shape, lambda *_: (0,) * nd, pipeline_mode=pl.Buffered(1))


def _dot(a, b):
    return jnp.dot(a, b, preferred_element_type=F32)


def _dot_nt(a, b):
    return lax.dot_general(a, b, (((1,), (1,)), ((), ())), preferred_element_type=F32)


def _dot_tn(a, b):
    return lax.dot_general(a, b, (((0,), (0,)), ((), ())), preferred_element_type=F32)


def _sigmoid(x):
    return 1.0 / (1.0 + jnp.exp(-x))


def _rms_bwd(x, r, g, dy):
    xh = x * r
    dyg = dy * g
    dx = r * (dyg - xh * jnp.mean(dyg * xh, axis=-1, keepdims=True))
    return dx, dy * xh


class _Comm:
    def __init__(self, ins, out_shapes, sems, start, mid, finish):
        self.ins, self.out_shapes, self.sems = list(ins), list(out_shapes), list(sems)
        self.start, self.mid, self.finish = start, mid, finish


_ANY = pl.BlockSpec(memory_space=pl.ANY)


def _launch(name, compute, grid, in_specs, out_specs, out_shape, scratch, operands, sem, comm=None):
    n_in, n_out, n_sc = len(in_specs), len(out_specs), len(scratch)
    if comm is None:
        res = _call(compute, name=name, grid=grid, in_specs=in_specs, out_specs=out_specs, out_shape=out_shape,
                    scratch_shapes=scratch, compiler_params=_cp(sem))(*operands)
        return list(res), []
    c_in, c_out = len(comm.ins), len(comm.out_shapes)
    inner = grid[1] if len(grid) > 1 else 1
    steps = grid[0] * inner
    mid_step = (3 * steps) // 4

    def body(*refs):
        ins = refs[:n_in]
        cins = refs[n_in:n_in + c_in]
        o = n_in + c_in
        outs = refs[o:o + n_out]
        couts = refs[o + n_out:o + n_out + c_out]
        o += n_out + c_out
        sc = refs[o:o + n_sc]
        csem = refs[o + n_sc:]
        i = pl.program_id(0)
        if len(grid) > 1:
            i = i * inner + pl.program_id(1)

        @pl.when(i == 0)
        def _():
            comm.start(cins, couts, csem)

        if comm.mid is not None:
            @pl.when(i == mid_step)
            def _():
                comm.mid(cins, couts, csem)

        compute(*ins, *outs, *sc)

        @pl.when(i == steps - 1)
        def _():
            comm.finish(cins, couts, csem)

    res = _call(body, name=name, grid=grid, in_specs=list(in_specs) + [_ANY] * c_in,
                out_specs=list(out_specs) + [_ANY] * c_out, out_shape=list(out_shape) + comm.out_shapes,
                scratch_shapes=list(scratch) + comm.sems,
                compiler_params=_cp(("arbitrary",) * len(grid)))(*operands, *comm.ins)
    return list(res[:n_out]), list(res[n_out:])


def _comm_only(comm, name):
    c_in, c_out = len(comm.ins), len(comm.out_shapes)

    def body(*refs):
        cins, couts, csem = refs[:c_in], refs[c_in:c_in + c_out], refs[c_in + c_out:]
        comm.start(cins, couts, csem)
        if comm.mid is not None:
            comm.mid(cins, couts, csem)
        comm.finish(cins, couts, csem)

    res = _call(body, name=name, in_specs=[_ANY] * c_in, out_specs=[_ANY] * c_out, out_shape=comm.out_shapes,
                scratch_shapes=comm.sems)(*comm.ins)
    return list(res)


def _wres(n, D, part):
    return pl.BlockSpec((N_DEV, n, D), lambda i: (0, part, 0), pipeline_mode=pl.Buffered(1))


def _phases(ext, ph, rows):
    for p in range(8):
        ph[p, :, :] = ext[p:p + rows, :]


def _f_in_conv(x, g, G, n_in, cw, cb, lg, lb, tm):
    S, D = x.shape
    N = N_DEV * n_in
    CW = cw.shape[1]
    RB = 64

    def body(x_ref, g_ref, w_ref, cw_ref, cb_ref, lg_ref, lb_ref, h_ref, p_ref, c_ref, u_ref, ext, ph):
        i = pl.program_id(0)

        @pl.when(i == 0)
        def _():
            ext[0:HALO, :] = jnp.zeros((HALO, CW), F32)
            ext[HALO + tm:, :] = jnp.zeros((8, CW), F32)

        @pl.when(i > 0)
        def _():
            ext[0:HALO, :] = ext[tm:tm + HALO, :]

        xv = x_ref[...]
        r = lax.rsqrt(jnp.mean(xv * xv, axis=-1, keepdims=True) + EPS)
        h = ((xv * r) * g_ref[...]).astype(CDT)
        h_ref[...] = h
        w = w_ref[...].reshape(N, D)
        pab = _dot_nt(h, w[0:2 * CW])
        p_ref[:, 0:2 * CW] = pab
        p_ref[:, 2 * CW:] = _dot_nt(h, w[2 * CW:])
        ext[HALO:HALO + tm, :] = pab[:, :CW] * _sigmoid(pab[:, CW:])
        _phases(ext, ph, tm + HALO)
        for rb in range(tm // RB):
            acc = jnp.zeros((RB, CW), F32) + cb_ref[...]
            for k in range(CONV_K):
                o = rb * RB + HALO - (CONV_K - 1) + k
                acc = acc + cw_ref[k:k + 1, :] * ph[o % 8, o - o % 8:o - o % 8 + RB, :]
            c_ref[rb * RB:(rb + 1) * RB, :] = acc
            mu = jnp.mean(acc, axis=-1, keepdims=True)
            var = jnp.mean(jnp.square(acc - mu), axis=-1, keepdims=True)
            z = ((acc - mu) * lax.rsqrt(var + EPS)) * lg_ref[...] + lb_ref[...]
            u_ref[rb * RB:(rb + 1) * RB, :] = (z * _sigmoid(z)).astype(CDT)

    row = lambda n: pl.BlockSpec((tm, n), lambda i: (i, 0))
    return _call(
        body, name="f_in_conv", grid=(S // tm,),
        in_specs=[row(D), _resident((1, D)), _wres(n_in, D, 0),
                  _resident((HALO, CW)), _resident((1, CW)), _resident((1, CW)), _resident((1, CW))],
        out_specs=[row(D), row(N), row(CW), row(CW)],
        out_shape=[jax.ShapeDtypeStruct((S, D), CDT), jax.ShapeDtypeStruct((S, N), F32),
                   jax.ShapeDtypeStruct((S, CW), F32), jax.ShapeDtypeStruct((S, CW), CDT)],
        scratch_shapes=[pltpu.VMEM((tm + HALO + 8, CW), F32), pltpu.VMEM((8, tm + HALO, CW), F32)],
        compiler_params=_cp(("arbitrary",)),
    )(x, g, G, cw, cb, lg, lb)


def _rot(t, c, s):
    return t * c + pltpu.roll(t, HD // 2, 1) * s


def _f_ret(proj, u, cosT, sinT, Mt, qd, kd, gs, gn):
    S = proj.shape[0]
    RW = HEADS * HD
    NB = S // SC
    scale = HD ** -0.5

    def body(q_ref, k_ref, v_ref, g_ref, u_ref, c_ref, s_ref, m_ref, qd_ref, kd_ref, gs_ref, gn_ref,
             rraw_ref, st_ref, mix_ref, state):
        @pl.when(pl.program_id(0) == 0)
        def _():
            state[...] = jnp.zeros_like(state)

        mix_ref[:, 0:RW] = u_ref[...]
        cv = c_ref[...]
        sv = s_ref[...]
        for h in range(HEADS):
            cs = slice(h * HD, (h + 1) * HD)
            q = _rot(q_ref[:, cs], cv, sv)
            k = _rot(k_ref[:, cs], cv, sv) * scale
            vb = v_ref[:, cs].astype(CDT)
            qb = q.astype(CDT)
            kb = k.astype(CDT)
            a = _dot_nt(qb, kb) * m_ref[h]
            sp = state[h]
            spb = sp.astype(CDT)
            st_ref[0, h] = spb
            r = _dot(a.astype(CDT), vb) + _dot((q * qd_ref[h]).astype(CDT), spb)
            kv = _dot_tn((k * kd_ref[h]).astype(CDT), vb)
            state[h] = gs_ref[h, 0:1, :] * sp + kv
            rraw_ref[:, cs] = r
            mu = jnp.mean(r, axis=-1, keepdims=True)
            var = jnp.mean(jnp.square(r - mu), axis=-1, keepdims=True)
            n = (r - mu) * lax.rsqrt(var + EPS)
            gv = g_ref[:, cs]
            mix_ref[:, RW + h * HD:RW + (h + 1) * HD] = ((n * gn_ref[:, cs]) * (gv * _sigmoid(gv))).astype(CDT)

    col = lambda j: pl.BlockSpec((SC, RW), lambda i: (i, j))
    return _call(
        body, name="f_ret", grid=(NB,),
        in_specs=[col(2), col(3), col(4), col(5),
                  pl.BlockSpec((SC, RW), lambda i: (i, 0)),
                  pl.BlockSpec((SC, HD), lambda i: (i, 0)), pl.BlockSpec((SC, HD), lambda i: (i, 0)),
                  _resident((HEADS, SC, SC)), _resident((HEADS, SC, HD)), _resident((HEADS, SC, HD)),
                  _resident((HEADS, 8, HD)), _resident((1, RW))],
        out_specs=[pl.BlockSpec((SC, RW), lambda i: (i, 0)),
                   pl.BlockSpec((1, HEADS, HD, HD), lambda i: (i, 0, 0, 0)),
                   pl.BlockSpec((SC, 2 * RW), lambda i: (i, 0))],
        out_shape=[jax.ShapeDtypeStruct((S, RW), F32),
                   jax.ShapeDtypeStruct((NB, HEADS, HD, HD), CDT),
                   jax.ShapeDtypeStruct((S, 2 * RW), CDT)],
        scratch_shapes=[pltpu.VMEM((HEADS, HD, HD), F32)],
        compiler_params=_cp(("arbitrary",)),
    )(proj, proj, proj, proj, u, cosT, sinT, Mt, qd, kd, gs, gn)


def _f_mlp(x, mixed, g2, G, n_out, n_ff, parts, tm, comm=None):
    S, D = x.shape
    FF = N_DEV * n_ff
    p_wo, p_wg, p_wu, p_wd = parts

    def body(x_ref, m_ref, wo_ref, g_ref, wg_ref, wu_ref, wd_ref,
             xm_ref, h2_ref, gate_ref, up_ref, act_ref, xo_ref):
        xm = x_ref[...] + _dot(m_ref[...], wo_ref[...].reshape(N_DEV * n_out, D))
        xm_ref[...] = xm
        r = lax.rsqrt(jnp.mean(xm * xm, axis=-1, keepdims=True) + EPS)
        h2 = ((xm * r) * g_ref[...]).astype(CDT)
        h2_ref[...] = h2
        gate = _dot_nt(h2, wg_ref[...].reshape(FF, D))
        up = _dot_nt(h2, wu_ref[...].reshape(FF, D))
        gate_ref[...] = gate.astype(CDT)
        up_ref[...] = up.astype(CDT)
        act = ((gate * _sigmoid(gate)) * up).astype(CDT)
        act_ref[...] = act
        xo_ref[...] = xm + _dot(act, wd_ref[...].reshape(FF, D))

    row = lambda n: pl.BlockSpec((tm, n), lambda i: (i, 0))
    return _launch(
        "f_mlp" if comm is None else "f_mlp_ag", body, (S // tm,),
        [row(D), row(D), _wres(n_out, D, p_wo), _resident((1, D)),
         _wres(n_ff, D, p_wg), _wres(n_ff, D, p_wu), _wres(n_ff, D, p_wd)],
        [row(D), row(D), row(FF), row(FF), row(FF), row(D)],
        [jax.ShapeDtypeStruct((S, D), F32), jax.ShapeDtypeStruct((S, D), CDT),
         jax.ShapeDtypeStruct((S, FF), CDT), jax.ShapeDtypeStruct((S, FF), CDT),
         jax.ShapeDtypeStruct((S, FF), CDT), jax.ShapeDtypeStruct((S, D), F32)],
        [], (x, mixed, G, g2, G, G, G), ("parallel",), comm)


def _f_loss(x, fg, tgt, tm):
    S, D = x.shape

    def body(x_ref, g_ref, t_ref, dx_ref, dxb_ref, loss_ref, dg_ref):
        @pl.when(pl.program_id(0) == 0)
        def _():
            loss_ref[...] = jnp.zeros_like(loss_ref)
            dg_ref[...] = jnp.zeros_like(dg_ref)

        xv = x_ref[...]
        r = lax.rsqrt(jnp.mean(xv * xv, axis=-1, keepdims=True) + EPS)
        y = (xv * r) * g_ref[...]
        e = y - t_ref[...]
        loss_ref[...] += 0.5 * jnp.sum(jnp.mean(e * e, axis=-1, keepdims=True))
        dy = e * (1.0 / D)
        dx, dgx = _rms_bwd(xv, r, g_ref[...], dy)
        dg_ref[...] += jnp.sum(dgx, axis=0, keepdims=True)
        dx_ref[...] = dx
        dxb_ref[...] = dx.astype(CDT)

    row = pl.BlockSpec((tm, D), lambda i: (i, 0))
    return _call(
        body, name="f_loss", grid=(S // tm,),
        in_specs=[row, _resident((1, D)), row],
        out_specs=[row, row, pl.BlockSpec((1, 128), lambda i: (0, 0)), pl.BlockSpec((1, D), lambda i: (0, 0))],
        out_shape=[jax.ShapeDtypeStruct((S, D), F32), jax.ShapeDtypeStruct((S, D), CDT),
                   jax.ShapeDtypeStruct((1, 128), F32), jax.ShapeDtypeStruct((1, D), F32)],
        compiler_params=_cp(("arbitrary",)),
    )(x, fg, tgt)


def _b_mlp(dxb, dx, xm, gate, up, g2, G, n_out, n_ff, parts, tm, comm=None):
    S, D = dx.shape
    FF = N_DEV * n_ff
    p_wo, p_wg, p_wu, p_wd = parts

    def body(dxb_ref, dx_ref, xm_ref, gate_ref, up_ref, g_ref, wd_ref, wg_ref, wu_ref, wo_ref,
             dgate_ref, dup_ref, dxm_ref, dxmb_ref, dmix_ref, dg_ref):
        @pl.when(pl.program_id(0) == 0)
        def _():
            dg_ref[...] = jnp.zeros_like(dg_ref)

        dact = _dot_nt(dxb_ref[...], wd_ref[...].reshape(FF, D))
        gate = gate_ref[...].astype(F32)
        up = up_ref[...].astype(F32)
        sg = _sigmoid(gate)
        sil = gate * sg
        dgate = ((dact * up) * (sg * (1.0 + gate * (1.0 - sg)))).astype(CDT)
        dup = (dact * sil).astype(CDT)
        dgate_ref[...] = dgate
        dup_ref[...] = dup
        dh2 = _dot(dgate, wg_ref[...].reshape(FF, D)) + _dot(dup, wu_ref[...].reshape(FF, D))
        xm = xm_ref[...]
        r = lax.rsqrt(jnp.mean(xm * xm, axis=-1, keepdims=True) + EPS)
        dxn, dgx = _rms_bwd(xm, r, g_ref[...], dh2)
        dg_ref[...] += jnp.sum(dgx, axis=0, keepdims=True)
        dxm = dx_ref[...] + dxn
        dxm_ref[...] = dxm
        dxmb = dxm.astype(CDT)
        dxmb_ref[...] = dxmb
        dmix_ref[...] = _dot_nt(dxmb, wo_ref[...].reshape(N_DEV * n_out, D))

    row = lambda n: pl.BlockSpec((tm, n), lambda i: (i, 0))
    return _launch(
        "b_mlp" if comm is None else "b_mlp_rs", body, (S // tm,),
        [row(D), row(D), row(D), row(FF), row(FF), _resident((1, D)),
         _wres(n_ff, D, p_wd), _wres(n_ff, D, p_wg), _wres(n_ff, D, p_wu), _wres(n_out, D, p_wo)],
        [row(FF), row(FF), row(D), row(D), row(D), pl.BlockSpec((1, D), lambda i: (0, 0))],
        [jax.ShapeDtypeStruct((S, FF), CDT), jax.ShapeDtypeStruct((S, FF), CDT),
         jax.ShapeDtypeStruct((S, D), F32), jax.ShapeDtypeStruct((S, D), CDT),
         jax.ShapeDtypeStruct((S, D), F32), jax.ShapeDtypeStruct((1, D), F32)],
        [], (dxb, dx, xm, gate, up, g2, G, G, G, G), ("arbitrary",), comm)


def _b_conv(dmix, c, proj, cw, lg, lb, tm, comm=None):
    S = proj.shape[0]
    CW = cw.shape[1]
    RB = 64
    hb = tm // HALO
    nt = S // tm
    last_h = S // HALO - 1

    def body(du_ref, duh_ref, c_ref, ch_ref, ab_ref, abh_ref, cw_ref, lg_ref, lb_ref,
             dab_ref, dcw_ref, dcb_ref, dlg_ref, dlb_ref, ext_u, ext_dc, ph_u, ph_dc, wacc):
        i = pl.program_id(0)

        @pl.when(i == 0)
        def _():
            wacc[...] = jnp.zeros_like(wacc)
            dcb_ref[...] = jnp.zeros_like(dcb_ref)
            dlg_ref[...] = jnp.zeros_like(dlg_ref)
            dlb_ref[...] = jnp.zeros_like(dlb_ref)

        def ln_bwd(cv, du):
            mu = jnp.mean(cv, axis=-1, keepdims=True)
            var = jnp.mean(jnp.square(cv - mu), axis=-1, keepdims=True)
            rstd = lax.rsqrt(var + EPS)
            n = (cv - mu) * rstd
            z = n * lg_ref[...] + lb_ref[...]
            sz = _sigmoid(z)
            dz = du * (sz * (1.0 + z * (1.0 - sz)))
            dn = dz * lg_ref[...]
            dc = rstd * (dn - jnp.mean(dn, axis=-1, keepdims=True)
                         - n * jnp.mean(dn * n, axis=-1, keepdims=True))
            return dc, dz, n

        hv = abh_ref[...]
        ext_u[0:HALO, :] = jnp.where(i > 0, hv[:, :CW] * _sigmoid(hv[:, CW:]), 0.0)
        av = ab_ref[...]
        sb = _sigmoid(av[:, CW:])
        ext_u[HALO:HALO + tm, :] = av[:, :CW] * sb
        ext_u[HALO + tm:, :] = jnp.zeros((8, CW), F32)
        dc, dz, n = ln_bwd(c_ref[...], du_ref[...])
        ext_dc[0:tm, :] = dc
        dch, _, _ = ln_bwd(ch_ref[...], duh_ref[...])
        ext_dc[tm:tm + HALO, :] = jnp.where(i < nt - 1, dch, 0.0)
        ext_dc[tm + HALO:, :] = jnp.zeros((8, CW), F32)
        dlg_ref[...] += jnp.sum(dz * n, axis=0, keepdims=True)
        dlb_ref[...] += jnp.sum(dz, axis=0, keepdims=True)
        dcb_ref[...] += jnp.sum(dc, axis=0, keepdims=True)
        _phases(ext_u, ph_u, tm + HALO)
        _phases(ext_dc, ph_dc, tm + HALO)

        for rb in range(tm // RB):
            rs = slice(rb * RB, (rb + 1) * RB)
            dcb = ext_dc[rs, :]
            for k in range(CONV_K):
                o = rb * RB + HALO - (CONV_K - 1) + k
                prod = dcb * ph_u[o % 8, o - o % 8:o - o % 8 + RB, :]
                part = prod[0:8]
                for j in range(1, RB // 8):
                    part = part + prod[8 * j:8 * j + 8]
                wacc[k] += part
            acc = jnp.zeros((RB, CW), F32)
            for k in range(CONV_K):
                o = rb * RB + (CONV_K - 1) - k
                acc = acc + cw_ref[k:k + 1, :] * ph_dc[o % 8, o - o % 8:o - o % 8 + RB, :]
            a_r = ab_ref[rs, 0:CW]
            s_r = _sigmoid(ab_ref[rs, CW:2 * CW])
            dab_ref[rs, 0:CW] = (acc * s_r).astype(CDT)
            dab_ref[rs, CW:2 * CW] = (acc * a_r * (s_r * (1.0 - s_r))).astype(CDT)

        @pl.when(i == nt - 1)
        def _():
            for k in range(CONV_K):
                dcw_ref[k:k + 1, :] = jnp.sum(wacc[k], axis=0, keepdims=True)
            dcw_ref[CONV_K:, :] = jnp.zeros((HALO - CONV_K, CW), F32)

    tile = lambda n, j: pl.BlockSpec((tm, n), lambda i: (i, j))
    nxt = lambda n, j: pl.BlockSpec((HALO, n), lambda i: (jnp.minimum((i + 1) * hb, last_h), j))
    return _launch(
        "b_conv" if comm is None else "b_conv_rs", body, (nt,),
        [tile(CW, 0), nxt(CW, 0), tile(CW, 0), nxt(CW, 0),
         tile(2 * CW, 0),
         pl.BlockSpec((HALO, 2 * CW), lambda i: (jnp.maximum(i * hb - 1, 0), 0)),
         _resident((HALO, CW)), _resident((1, CW)), _resident((1, CW))],
        [tile(2 * CW, 0),
         pl.BlockSpec((HALO, CW), lambda i: (0, 0)), pl.BlockSpec((1, CW), lambda i: (0, 0)),
         pl.BlockSpec((1, CW), lambda i: (0, 0)), pl.BlockSpec((1, CW), lambda i: (0, 0))],
        [jax.ShapeDtypeStruct((S, 2 * CW), CDT),
         jax.ShapeDtypeStruct((HALO, CW), F32), jax.ShapeDtypeStruct((1, CW), F32),
         jax.ShapeDtypeStruct((1, CW), F32), jax.ShapeDtypeStruct((1, CW), F32)],
        [pltpu.VMEM((tm + HALO + 8, CW), F32), pltpu.VMEM((tm + HALO + 8, CW), F32),
         pltpu.VMEM((8, tm + HALO, CW), F32), pltpu.VMEM((8, tm + HALO, CW), F32),
         pltpu.VMEM((HALO, 8, CW), F32)],
        (dmix, dmix, c, c, proj, proj, cw, lg, lb), ("arbitrary",), comm)


def _b_ret(dmix, dab, proj, rraw, states, cosT, sinT, Mt, qd, kd, gs, gn, comm=None):
    S = proj.shape[0]
    RW = HEADS * HD
    NB = S // SC
    scale = HD ** -0.5

    def body(dro_ref, dab_ref, q_ref, k_ref, v_ref, g_ref, rraw_ref, st_ref, c_ref, s_ref,
             m_ref, qd_ref, kd_ref, gs_ref, gn_ref, dp_ref, dgn_ref, G):
        @pl.when(pl.program_id(0) == 0)
        def _():
            G[...] = jnp.zeros_like(G)
            dgn_ref[...] = jnp.zeros_like(dgn_ref)

        dp_ref[:, 0:2 * RW] = dab_ref[...]
        cv = c_ref[...]
        sv = s_ref[...]
        for h in range(HEADS):
            cs = slice(h * HD, (h + 1) * HD)
            q = _rot(q_ref[:, cs], cv, sv)
            k = _rot(k_ref[:, cs], cv, sv) * scale
            qb = q.astype(CDT)
            kb = k.astype(CDT)
            vb = v_ref[:, cs].astype(CDT)
            spb = st_ref[0, h]
            r = rraw_ref[:, cs]
            mu = jnp.mean(r, axis=-1, keepdims=True)
            var = jnp.mean(jnp.square(r - mu), axis=-1, keepdims=True)
            rstd = lax.rsqrt(var + EPS)
            n = (r - mu) * rstd
            gv = g_ref[:, cs]
            sg = _sigmoid(gv)
            sil = gv * sg
            dro = dro_ref[:, cs]
            gnv = gn_ref[:, cs]
            dgn_ref[:, cs] += jnp.sum(dro * n * sil, axis=0, keepdims=True)
            dgate = dro * (n * gnv) * (sg * (1.0 + gv * (1.0 - sg)))
            dn = dro * gnv * sil
            dr = rstd * (dn - jnp.mean(dn, axis=-1, keepdims=True)
                         - n * jnp.mean(dn * n, axis=-1, keepdims=True))
            drb = dr.astype(CDT)
            mh = m_ref[h]
            ab = (_dot_nt(qb, kb) * mh).astype(CDT)
            dab_ = (_dot_nt(drb, vb) * mh).astype(CDT)
            qdb = (q * qd_ref[h]).astype(CDT)
            kdb = (k * kd_ref[h]).astype(CDT)
            gc = G[h]
            gb = gc.astype(CDT)
            dq = _dot(dab_, kb) + _dot_nt(drb, spb) * qd_ref[h]
            dk = _dot_tn(dab_, qb) + _dot_nt(vb, gb) * kd_ref[h]
            dv = _dot_tn(ab, drb) + _dot(kdb, gb)
            G[h] = gs_ref[h, 0:1, :] * gc + _dot_tn(qdb, drb)
            dk = dk * scale
            dqp = dq * cv + pltpu.roll(dq * sv, HD // 2, 1)
            dkp = dk * cv + pltpu.roll(dk * sv, HD // 2, 1)
            base = 2 * RW
            dp_ref[:, base + h * HD:base + (h + 1) * HD] = dqp.astype(CDT)
            dp_ref[:, base + RW + h * HD:base + RW + (h + 1) * HD] = dkp.astype(CDT)
            dp_ref[:, base + 2 * RW + h * HD:base + 2 * RW + (h + 1) * HD] = dv.astype(CDT)
            dp_ref[:, base + 3 * RW + h * HD:base + 3 * RW + (h + 1) * HD] = dgate.astype(CDT)

    rev = lambda n, j: pl.BlockSpec((SC, n), lambda i: (NB - 1 - i, j))
    return _launch(
        "b_ret" if comm is None else "b_ret_rs", body, (NB,),
        [rev(RW, 1), rev(2 * RW, 0), rev(RW, 2), rev(RW, 3), rev(RW, 4), rev(RW, 5), rev(RW, 0),
         pl.BlockSpec((1, HEADS, HD, HD), lambda i: (NB - 1 - i, 0, 0, 0)),
         rev(HD, 0), rev(HD, 0),
         _resident((HEADS, SC, SC)), _resident((HEADS, SC, HD)), _resident((HEADS, SC, HD)),
         _resident((HEADS, 8, HD)), _resident((1, RW))],
        [rev(6 * RW, 0), pl.BlockSpec((1, RW), lambda i: (0, 0))],
        [jax.ShapeDtypeStruct((S, 6 * RW), CDT), jax.ShapeDtypeStruct((1, RW), F32)],
        [pltpu.VMEM((HEADS, HD, HD), F32)],
        (dmix, dab, proj, proj, proj, proj, rraw, states, cosT, sinT, Mt, qd, kd, gs, gn), ("arbitrary",), comm)


def _b_in(dproj, G, n_in, x, g1, dxm, tm, comm=None):
    S, D = x.shape
    N = N_DEV * n_in

    def body(dp_ref, w_ref, x_ref, g_ref, dxm_ref, dx_ref, dxb_ref, dg_ref):
        @pl.when(pl.program_id(0) == 0)
        def _():
            dg_ref[...] = jnp.zeros_like(dg_ref)

        dh = _dot(dp_ref[...], w_ref[...].reshape(N, D))
        xv = x_ref[...]
        r = lax.rsqrt(jnp.mean(xv * xv, axis=-1, keepdims=True) + EPS)
        dxn, dgx = _rms_bwd(xv, r, g_ref[...], dh)
        dg_ref[...] += jnp.sum(dgx, axis=0, keepdims=True)
        dx = dxm_ref[...] + dxn
        dx_ref[...] = dx
        dxb_ref[...] = dx.astype(CDT)

    row = lambda n: pl.BlockSpec((tm, n), lambda i: (i, 0))
    return _launch(
        "b_in" if comm is None else "b_in_rs", body, (S // tm,),
        [row(N), _wres(n_in, D, 0), row(D), _resident((1, D)), row(D)],
        [row(D), row(D), pl.BlockSpec((1, D), lambda i: (0, 0))],
        [jax.ShapeDtypeStruct((S, D), F32), jax.ShapeDtypeStruct((S, D), CDT),
         jax.ShapeDtypeStruct((1, D), F32)],
        [], (dproj, G, x, g1, dxm), ("arbitrary",), comm)


def _dw_tn(a, b, tm, tk):
    S, M = a.shape
    N = b.shape[1]

    def body(a_ref, b_ref, o_ref):
        @pl.when(pl.program_id(1) == 0)
        def _():
            o_ref[...] = jnp.zeros_like(o_ref)

        o_ref[...] += _dot_tn(a_ref[...], b_ref[...])

    return _call(
        body, name="dw_tn", grid=(M // tm, S // tk),
        in_specs=[pl.BlockSpec((tk, tm), lambda m, k: (k, m)), pl.BlockSpec((tk, N), lambda m, k: (k, 0))],
        out_specs=pl.BlockSpec((tm, N), lambda m, k: (m, 0)),
        out_shape=jax.ShapeDtypeStruct((M, N), F32),
        compiler_params=_cp(("parallel", "arbitrary")),
    )(a, b)


def _add_pair(parts, recv):
    K = len(parts)
    C = parts[0].shape[2]
    halves = 2

    def body(*refs):
        cc = lax.axis_index("c")
        for k in range(K):
            s = refs[k][cc] + refs[K + k][...]
            refs[2 * K + k][...] = s
            refs[3 * K + k][...] = s.astype(CDT)

    ns = [p.shape[1] for p in parts]
    in_specs = [pl.BlockSpec((None, 2, n // halves, C), lambda q, r: (q, 0, r, 0)) for n in ns]
    in_specs += [pl.BlockSpec((None, n // halves, C), lambda q, r: (q, r, 0)) for n in ns]
    outb = [pl.BlockSpec((None, n // halves, C), lambda q, r: (q, r, 0)) for n in ns]
    res = _call(
        body, name="add_pair", grid=(N_DEV // 2, halves),
        in_specs=in_specs, out_specs=outb + outb,
        out_shape=[jax.ShapeDtypeStruct((N_DEV // 2, n, C), F32) for n in ns]
        + [jax.ShapeDtypeStruct((N_DEV // 2, n, C), CDT) for n in ns],
        compiler_params=_cp(("parallel", "parallel")),
    )(*[p.reshape(N_DEV // 2, 2, p.shape[1], C) for p in parts], *recv)
    return list(res[:K]), list(res[K:])


def _sum_chips(sums, recv):
    K = len(sums)

    def body(*refs):
        chip = 2 * lax.axis_index("x") + lax.axis_index("y")
        for k in range(K):
            r = refs[K + k]
            refs[2 * K + k][...] = ((refs[k][chip] + r[0].astype(F32)) + r[1].astype(F32)) + r[2].astype(F32)

    vm = pl.BlockSpec(memory_space=pltpu.VMEM)
    res = _call(
        body, name="sum_chips", in_specs=[vm] * (2 * K), out_specs=[vm] * K,
        out_shape=[jax.ShapeDtypeStruct(s.shape[1:], F32) for s in sums],
        compiler_params=_cp(),
    )(*sums, *recv)
    return list(res)


def _adamw(w, g, m, v, tr):
    R, C = w.shape
    c1 = 1.0 - ADAM_B1 ** ADAM_STEP
    c2 = 1.0 - ADAM_B2 ** ADAM_STEP

    def body(w_ref, g_ref, m_ref, v_ref, d_ref, mo_ref, vo_ref):
        gv = g_ref[...]
        mn = ADAM_B1 * m_ref[...] + (1.0 - ADAM_B1) * gv
        vn = ADAM_B2 * v_ref[...] + (1.0 - ADAM_B2) * jnp.square(gv)
        mo_ref[...] = mn
        vo_ref[...] = vn
        d_ref[...] = -ADAM_LR * ((mn / c1) / (jnp.sqrt(vn / c2) + ADAM_EPS) + ADAM_WD * w_ref[...])

    blk = pl.BlockSpec((tr, C), lambda i: (i, 0))
    sh = jax.ShapeDtypeStruct((R, C), F32)
    return _call(
        body, name="adamw", grid=(R // tr,),
        in_specs=[blk, blk, blk, blk], out_specs=[blk, blk, blk], out_shape=[sh, sh, sh],
        compiler_params=_cp(("parallel",)),
    )(w, g, m, v)


def _coords():
    return lax.axis_index("x"), lax.axis_index("y"), lax.axis_index("c")


def _peer(x, y, c, d):
    return (x ^ (d >> 2), y ^ ((d >> 1) & 1), c ^ (d & 1))


def _ag_comm(p):
    R, C = p.shape

    def plan(cins, couts, sems):
        x_ref, out_ref = cins[0], couts[0]
        send_sems, recv_sems, local_sem = sems
        x, y, c = _coords()
        me, sibling = (x, y, c), (x, y, 1 - c)
        chips = [(1 - x, y), (x, 1 - y), (1 - x, 1 - y)]

        def rows(px, py, pc):
            return out_ref.at[pl.ds((4 * px + 2 * py + pc) * R, R), :]

        def copy(k, block, to, src=None):
            return pltpu.make_async_remote_copy(
                src_ref=rows(*block) if src is None else src, dst_ref=rows(*block),
                send_sem=send_sems.at[k], recv_sem=recv_sems.at[k], device_id=to, device_id_type=MESH)

        mine = pltpu.make_async_copy(x_ref, rows(*me), local_sem)
        first = [copy(0, me, sibling, src=x_ref)]
        first += [copy(1 + j, me, (*chip, c), src=x_ref) for j, chip in enumerate(chips)]
        passed = [copy(4 + j, (*chip, c), sibling) for j, chip in enumerate(chips)]
        got_ici = [copy(1 + j, (*chip, c), me) for j, chip in enumerate(chips)]
        got_d2d = [copy(0, sibling, me)] + [copy(4 + j, (*chip, 1 - c), me) for j, chip in enumerate(chips)]
        return mine, first, passed, got_ici, got_d2d

    def start(*a):
        mine, first, _, _, _ = plan(*a)
        mine.start()
        for cp in first:
            cp.start()

    def mid(*a):
        _, _, passed, got_ici, _ = plan(*a)
        for got, fwd in zip(got_ici, passed):
            got.wait_recv()
            fwd.start()

    def finish(*a):
        mine, first, passed, _, got_d2d = plan(*a)
        for got in got_d2d:
            got.wait_recv()
        for cp in first + passed:
            cp.wait_send()
        mine.wait()

    return _Comm([p], [jax.ShapeDtypeStruct((N_DEV * R, C), p.dtype)],
                 [pltpu.SemaphoreType.DMA((7,)), pltpu.SemaphoreType.DMA((7,)), pltpu.SemaphoreType.DMA],
                 start, mid, finish)


def _rs_pair_comm(parts):
    K = len(parts)

    def plan(cins, couts, sems):
        send_sems, recv_sems = sems
        x, y, c = _coords()
        sibling = (x, y, 1 - c)
        cps = []
        for k in range(K):
            for q in range(N_DEV // 2):
                cps.append(pltpu.make_async_remote_copy(
                    src_ref=cins[k].at[2 * q + (1 - c)], dst_ref=couts[k].at[q],
                    send_sem=send_sems.at[k], recv_sem=recv_sems.at[k], device_id=sibling, device_id_type=MESH))
        whole = [pltpu.make_async_remote_copy(
            src_ref=couts[k], dst_ref=couts[k], send_sem=send_sems.at[k], recv_sem=recv_sems.at[k],
            device_id=sibling, device_id_type=MESH) for k in range(K)]
        return cps, whole

    def start(*a):
        for cp in plan(*a)[0]:
            cp.start()

    def finish(*a):
        for w in plan(*a)[1]:
            w.wait_recv()
            w.wait_send()

    return _Comm(parts, [jax.ShapeDtypeStruct((N_DEV // 2,) + p.shape[1:], p.dtype) for p in parts],
                 [pltpu.SemaphoreType.DMA((K,)), pltpu.SemaphoreType.DMA((K,))], start, None, finish)


def _rs_chip_comm(sums):
    K = len(sums)

    def plan(cins, couts, sems):
        send_sems, recv_sems = sems
        x, y, c = _coords()
        cps = []
        for d in range(1, N_DEV // 2):
            px, py = x ^ (d >> 1), y ^ (d & 1)
            for k in range(K):
                s = (d - 1) * K + k
                cps.append(pltpu.make_async_remote_copy(
                    src_ref=cins[k].at[2 * px + py], dst_ref=couts[k].at[d - 1],
                    send_sem=send_sems.at[s], recv_sem=recv_sems.at[s], device_id=(px, py, c), device_id_type=MESH))
        return cps

    def start(*a):
        for cp in plan(*a):
            cp.start()

    def finish(*a):
        cps = plan(*a)
        for cp in cps:
            cp.wait_recv()
        for cp in cps:
            cp.wait_send()

    n_sem = (N_DEV // 2 - 1) * K
    return _Comm(sums, [jax.ShapeDtypeStruct((N_DEV // 2 - 1,) + s.shape[1:], s.dtype) for s in sums],
                 [pltpu.SemaphoreType.DMA((n_sem,)), pltpu.SemaphoreType.DMA((n_sem,))], start, None, finish)


def _gather_small(v, reduce):
    R, C = v.shape

    def exchange(v_ref, buf, send_sems, recv_sems):
        x, y, c = _coords()
        me = 4 * x + 2 * y + c
        buf[me] = v_ref[...]
        cps = []
        for d in range(1, N_DEV):
            cp = pltpu.make_async_remote_copy(
                src_ref=v_ref, dst_ref=buf.at[me], send_sem=send_sems.at[d - 1], recv_sem=recv_sems.at[d - 1],
                device_id=_peer(x, y, c, d), device_id_type=MESH)
            cp.start()
            cps.append(cp)
        for cp in cps:
            cp.wait_recv()
        for cp in cps:
            cp.wait_send()

    sems = [pltpu.SemaphoreType.DMA((7,)), pltpu.SemaphoreType.DMA((7,))]
    vm = pl.BlockSpec(memory_space=pltpu.VMEM)
    if reduce:
        def body(v_ref, o_ref, buf, send_sems, recv_sems):
            exchange(v_ref, buf, send_sems, recv_sems)
            acc = buf[0]
            for s in range(1, N_DEV):
                acc = acc + buf[s]
            o_ref[...] = acc

        return _call(body, name="allreduce_small", in_specs=[vm], out_specs=vm,
                     out_shape=jax.ShapeDtypeStruct((R, C), F32),
                     scratch_shapes=[pltpu.VMEM((N_DEV, R, C), F32)] + sems)(v)

    def body(v_ref, o_ref, send_sems, recv_sems):
        exchange(v_ref, o_ref, send_sems, recv_sems)

    return _call(body, name="allgather_small", in_specs=[vm], out_specs=vm,
                 out_shape=jax.ShapeDtypeStruct((N_DEV, R, C), F32), scratch_shapes=sems)(v)


def _tables(S):
    half = HD // 2
    pos = jnp.arange(S, dtype=F32)
    freqs = ROPE_BASE ** (-jnp.arange(half, dtype=F32) / half)
    ang = pos[:, None] * freqs[None, :]
    cos, sin = jnp.cos(ang), jnp.sin(ang)
    cosT = jnp.concatenate([cos, cos], axis=-1)
    sinT = jnp.concatenate([-sin, sin], axis=-1)
    log_g = jnp.log(1.0 - 2.0 ** (-5.0 - jnp.arange(HEADS, dtype=F32)))
    idx = jnp.arange(SC, dtype=F32)
    ci = jnp.arange(SC) // CHUNK
    diff = idx[:, None] - idx[None, :]
    same = ci[:, None] == ci[None, :]
    earlier = ci[None, :] < ci[:, None]
    expo = jnp.where(same, jnp.abs(diff), diff)
    Mt = jnp.where((same | earlier)[None], jnp.exp(log_g[:, None, None] * expo[None]), 0.0)
    ones = jnp.ones((1, 1, HD), F32)
    qd = jnp.exp(log_g[:, None] * (idx + 1.0)[None, :])[:, :, None] * ones
    kd = jnp.exp(log_g[:, None] * (SC - 1.0 - idx)[None, :])[:, :, None] * ones
    gs = jnp.exp(log_g * SC)[:, None, None] * jnp.ones((1, 8, HD), F32)
    return cosT, sinT, Mt, qd, kd, gs


def _pad_rows(a, rows):
    return jnp.pad(a, ((0, rows - a.shape[0]), (0, 0)))


def kernel(x, norm1_g, w_in, conv_w, conv_b, conv_ln_g, conv_ln_b, ret_gn_g, w_out, norm2_g, w_gate, w_up, w_down, final_g, loss_target, m_norm1_g, m_w_in, m_conv_w, m_conv_b, m_conv_ln_g, m_conv_ln_b, m_ret_gn_g, m_w_out, m_norm2_g, m_w_gate, m_w_up, m_w_down, m_final_g, v_norm1_g, v_w_in, v_conv_w, v_conv_b, v_conv_ln_g, v_conv_ln_b, v_ret_gn_g, v_w_out, v_norm2_g, v_w_gate, v_w_up, v_w_down, v_final_g):
    L, D, n_in = w_in.shape
    n_out = w_out.shape[1]
    n_ff = w_gate.shape[2]
    S = x.shape[1]
    CW = conv_b.shape[1]
    IN, FF = N_DEV * n_in, N_DEV * n_ff
    ncw = conv_w.shape[2]
    x0 = x.reshape(S, D)
    tgt = loss_target.reshape(S, D)
    TM = min(512, S)
    TKW = min(1024, S)
    TMI = min(512, S)
    TMM = min(256, S)

    assert n_in % n_out == 0
    o_ff = -(-(n_in + n_out) // n_ff) * n_ff
    RL = o_ff + 3 * n_ff
    wparts = (n_in // n_out, o_ff // n_ff, o_ff // n_ff + 1, o_ff // n_ff + 2)
    pack = jnp.concatenate([jnp.swapaxes(w_in, 1, 2), w_out, jnp.zeros((L, o_ff - n_in - n_out, D), F32),
                            jnp.swapaxes(w_gate, 1, 2), jnp.swapaxes(w_up, 1, 2), w_down],
                           axis=1).astype(CDT)
    Gl = _comm_only(_ag_comm(pack[0]), "ag_first")[0].reshape(N_DEV, RL, D)

    cwp = conv_w.reshape(L * CONV_K * ncw // 128, 128)
    cw_rows = -(-cwp.shape[0] // 8) * 8
    cwg = _gather_small(_pad_rows(cwp, cw_rows), reduce=False)[:, :cwp.shape[0], :]
    conv_w_full = jnp.moveaxis(cwg.reshape(N_DEV, L, CONV_K, ncw), 0, 2).reshape(L, CONV_K, CW)

    cosT, sinT, Mt, qd, kd, gs = _tables(S)

    saved = []
    xl = x0
    for l in range(L):
        cw = _pad_rows(conv_w_full[l], HALO)
        h, proj, c, u = _f_in_conv(xl, norm1_g[l][None], Gl, n_in, cw, conv_b[l][None], conv_ln_g[l][None],
                                   conv_ln_b[l][None], TMI)
        rraw, states, mixed = _f_ret(proj, u, cosT, sinT, Mt, qd, kd, gs, ret_gn_g[l][None])
        (xm, h2, gate, up, act, xo), nxt = _f_mlp(xl, mixed, norm2_g[l][None], Gl, n_out, n_ff, wparts, TMM,
                                                 _ag_comm(pack[l + 1]) if l + 1 < L else None)
        saved.append(dict(x=xl, h=h, proj=proj, c=c, rraw=rraw, states=states, mixed=mixed, xm=xm, h2=h2,
                          gate=gate, up=up, act=act, cw=cw, G=Gl))
        if nxt:
            Gl = nxt[0].reshape(N_DEV, RL, D)
        xl = xo

    dx, dxb, loss_p, dfg = _f_loss(xl, final_g[None], tgt, TM)

    small = []
    chip_sums = [None] * L
    chip_recv = [None] * L
    in_flight = None
    for l in reversed(range(L)):
        sv = saved[l]
        (dgate, dup, dxm, dxmb, dmix, dg2), got = _b_mlp(
            dxb, dx, sv["xm"], sv["gate"], sv["up"], norm2_g[l][None], sv["G"], n_out, n_ff, wparts, TMM, in_flight)
        if got:
            chip_recv[l + 1] = got + chip_recv[l + 1]
        d_wd = _dw_tn(sv["act"], dxb, FF // 2, TKW)
        d_wgT = _dw_tn(dgate, sv["h2"], FF // 2, TKW)
        d_wuT = _dw_tn(dup, sv["h2"], FF // 2, TKW)
        d_wo = _dw_tn(sv["mixed"], dxmb, D, TKW)
        parts_m = [d.reshape(N_DEV, -1, D) for d in (d_wo, d_wgT, d_wuT, d_wd)]
        (dab, dcw, dcb, dlg, dlb), pair_m = _b_conv(dmix, sv["c"], sv["proj"], sv["cw"], conv_ln_g[l][None],
                                                    conv_ln_b[l][None], TM, _rs_pair_comm(parts_m))
        sums_m, sums_mb = _add_pair(parts_m, pair_m)
        (dproj, dgn), recv_m = _b_ret(dmix, dab, sv["proj"], sv["rraw"], sv["states"], cosT, sinT, Mt, qd, kd, gs,
                                      ret_gn_g[l][None], _rs_chip_comm(sums_mb))
        d_winT = _dw_tn(dproj, sv["h"], IN // 2, TKW)
        parts_i = [d_winT.reshape(N_DEV, -1, D)]
        (dx, dxb, dg1), pair_i = _b_in(dproj, sv["G"], n_in, sv["x"], norm1_g[l][None], dxm, TMI,
                                       _rs_pair_comm(parts_i))
        sums_i, sums_ib = _add_pair(parts_i, pair_i)
        chip_sums[l] = sums_i + sums_m
        chip_recv[l] = recv_m
        in_flight = _rs_chip_comm(sums_ib)
        small.append(jnp.concatenate([dcw, dcb, dlg, dlb, dgn, dg1.reshape(2, CW), dg2.reshape(2, CW)], axis=0))
    chip_recv[0] = _comm_only(in_flight, "rs_last") + chip_recv[0]
    small = small[::-1]
    grad_x = dx.reshape(1, S, D)

    rows_l = HALO + 8
    loss_row = jnp.zeros((1, CW), F32).at[0, 0].set(loss_p[0, 0])
    sm = jnp.concatenate(small + [dfg.reshape(2, CW), loss_row], axis=0)
    sm_rows = -(-sm.shape[0] // 8) * 8
    sm = _gather_small(_pad_rows(sm, sm_rows), reduce=True)
    loss = sm[L * rows_l + 2, 0]
    g_final = sm[L * rows_l:L * rows_l + 2].reshape(D)
    per = sm[:L * rows_l].reshape(L, rows_l, CW)
    me = 4 * lax.axis_index("x") + 2 * lax.axis_index("y") + lax.axis_index("c")
    g_conv_w = lax.dynamic_slice_in_dim(per[:, :CONV_K, :], me * ncw, ncw, axis=2)
    g_conv_b, g_ln_g, g_ln_b, g_gn = per[:, HALO], per[:, HALO + 1], per[:, HALO + 2], per[:, HALO + 3]
    g_n1 = per[:, HALO + 4:HALO + 6].reshape(L, D)
    g_n2 = per[:, HALO + 6:HALO + 8].reshape(L, D)

    gl = [_sum_chips(chip_sums[l], chip_recv[l]) for l in range(L)]
    g_w_in = jnp.stack([gl[l][0].T for l in range(L)])
    g_w_out = jnp.stack([gl[l][1] for l in range(L)])
    g_w_gate = jnp.stack([gl[l][2].T for l in range(L)])
    g_w_up = jnp.stack([gl[l][3].T for l in range(L)])
    g_w_down = jnp.stack([gl[l][4] for l in range(L)])

    def big(w, g, m, v):
        sh = w.shape
        two = lambda a: a.reshape(-1, sh[-1])
        rows = two(w).shape[0]
        d, mn, vn = _adamw(two(w), two(g), two(m), two(v), rows // 8)
        return d.reshape(sh), mn.reshape(sh), vn.reshape(sh)

    names = ["norm1_g", "conv_w", "conv_b", "conv_ln_g", "conv_ln_b", "ret_gn_g", "norm2_g", "final_g"]
    sw = dict(norm1_g=(norm1_g, g_n1, m_norm1_g, v_norm1_g), conv_w=(conv_w, g_conv_w, m_conv_w, v_conv_w),
              conv_b=(conv_b, g_conv_b, m_conv_b, v_conv_b), conv_ln_g=(conv_ln_g, g_ln_g, m_conv_ln_g, v_conv_ln_g),
              conv_ln_b=(conv_ln_b, g_ln_b, m_conv_ln_b, v_conv_ln_b), ret_gn_g=(ret_gn_g, g_gn, m_ret_gn_g, v_ret_gn_g),
              norm2_g=(norm2_g, g_n2, m_norm2_g, v_norm2_g), final_g=(final_g, g_final, m_final_g, v_final_g))
    lens = [int(math.prod(sw[n][0].shape)) for n in names]
    tot = sum(lens)
    prow = -(-tot // (8 * CW)) * 8

    def packs(j):
        flat = jnp.concatenate([sw[n][j].reshape(-1) for n in names])
        return jnp.pad(flat, (0, prow * CW - tot)).reshape(prow, CW)

    sd, smn, svn = _adamw(packs(0), packs(1), packs(2), packs(3), prow)

    def unpack(a):
        flat = a.reshape(-1)
        out, o = {}, 0
        for n, ln in zip(names, lens):
            out[n] = flat[o:o + ln].reshape(sw[n][0].shape)
            o += ln
        return out

    sd, smn, svn = unpack(sd), unpack(smn), unpack(svn)
    res = {n: (sw[n][1], sd[n], smn[n], svn[n]) for n in names}
    res["w_in"] = (g_w_in,) + big(w_in, g_w_in, m_w_in, v_w_in)
    res["w_out"] = (g_w_out,) + big(w_out, g_w_out, m_w_out, v_w_out)
    res["w_gate"] = (g_w_gate,) + big(w_gate, g_w_gate, m_w_gate, v_w_gate)
    res["w_up"] = (g_w_up,) + big(w_up, g_w_up, m_w_up, v_w_up)
    res["w_down"] = (g_w_down,) + big(w_down, g_w_down, m_w_down, v_w_down)

    order = ["norm1_g", "w_in", "conv_w", "conv_b", "conv_ln_g", "conv_ln_b", "ret_gn_g", "w_out", "norm2_g",
             "w_gate", "w_up", "w_down", "final_g"]
    return (loss, grad_x, *[res[n][0] for n in order], *[res[n][1] for n in order],
            *[res[n][2] for n in order], *[res[n][3] for n in order])
```

```python
import functools
import math

import jax
import jax.numpy as jnp
from jax import lax
from jax.experimental import pallas as pl
from jax.experimental.pallas import tpu as pltpu

F32 = jnp.float32
CDT = jnp.bfloat16
EPS = 1e-6
CHUNK = 64
SC = 256
HEADS = 4
HD = 128
CONV_K = 31
HALO = 32
ROPE_BASE = 10000.0
ADAM_LR = 0.001
ADAM_B1 = 0.9
ADAM_B2 = 0.999
ADAM_EPS = 1e-08
ADAM_WD = 0.01
ADAM_STEP = 10
N_DEV = 8
MESH = pl.DeviceIdType.MESH
VMEM_LIMIT = 60 * 1024 * 1024


def _call(body, **kw):
    return pl.pallas_call(body, **kw)


def _cp(sem=None, vmem=VMEM_LIMIT):
    return pltpu.CompilerParams(dimension_semantics=sem, vmem_limit_bytes=vmem)


def _resident(shape):
    nd = len(shape)
    return pl.BlockSpec(shape, lambda *_: (0,) * nd, pipeline_mode=pl.Buffered(1))


def _dot(a, b):
    return jnp.dot(a, b, preferred_element_type=F32)


def _dot_nt(a, b):
    return lax.dot_general(a, b, (((1,), (1,)), ((), ())), preferred_element_type=F32)


def _dot_tn(a, b):
    return lax.dot_general(a, b, (((0,), (0,)), ((), ())), preferred_element_type=F32)


def _sigmoid(x):
    return 1.0 / (1.0 + jnp.exp(-x))


def _rms_bwd(x, r, g, dy):
    xh = x * r
    dyg = dy * g
    dx = r * (dyg - xh * jnp.mean(dyg * xh, axis=-1, keepdims=True))
    return dx, dy * xh


class _Comm:
    def __init__(self, ins, out_shapes, sems, start, mid, finish):
        self.ins, self.out_shapes, self.sems = list(ins), list(out_shapes), list(sems)
        self.start, self.mid, self.finish = start, mid, finish


_ANY = pl.BlockSpec(memory_space=pl.ANY)


def _launch(name, compute, grid, in_specs, out_specs, out_shape, scratch, operands, sem, comm=None):
    n_in, n_out, n_sc = len(in_specs), len(out_specs), len(scratch)
    if comm is None:
        res = _call(compute, name=name, grid=grid, in_specs=in_specs, out_specs=out_specs, out_shape=out_shape,
                    scratch_shapes=scratch, compiler_params=_cp(sem))(*operands)
        return list(res), []
    c_in, c_out = len(comm.ins), len(comm.out_shapes)
    inner = grid[1] if len(grid) > 1 else 1
    steps = grid[0] * inner
    mid_step = (3 * steps) // 4

    def body(*refs):
        ins = refs[:n_in]
        cins = refs[n_in:n_in + c_in]
        o = n_in + c_in
        outs = refs[o:o + n_out]
        couts = refs[o + n_out:o + n_out + c_out]
        o += n_out + c_out
        sc = refs[o:o + n_sc]
        csem = refs[o + n_sc:]
        i = pl.program_id(0)
        if len(grid) > 1:
            i = i * inner + pl.program_id(1)

        @pl.when(i == 0)
        def _():
            comm.start(cins, couts, csem)

        if comm.mid is not None:
            @pl.when(i == mid_step)
            def _():
                comm.mid(cins, couts, csem)

        compute(*ins, *outs, *sc)

        @pl.when(i == steps - 1)
        def _():
            comm.finish(cins, couts, csem)

    res = _call(body, name=name, grid=grid, in_specs=list(in_specs) + [_ANY] * c_in,
                out_specs=list(out_specs) + [_ANY] * c_out, out_shape=list(out_shape) + comm.out_shapes,
                scratch_shapes=list(scratch) + comm.sems,
                compiler_params=_cp(("arbitrary",) * len(grid)))(*operands, *comm.ins)
    return list(res[:n_out]), list(res[n_out:])


def _comm_only(comm, name):
    c_in, c_out = len(comm.ins), len(comm.out_shapes)

    def body(*refs):
        cins, couts, csem = refs[:c_in], refs[c_in:c_in + c_out], refs[c_in + c_out:]
        comm.start(cins, couts, csem)
        if comm.mid is not None:
            comm.mid(cins, couts, csem)
        comm.finish(cins, couts, csem)

    res = _call(body, name=name, in_specs=[_ANY] * c_in, out_specs=[_ANY] * c_out, out_shape=comm.out_shapes,
                scratch_shapes=comm.sems)(*comm.ins)
    return list(res)


def _wres(n, D, part):
    return pl.BlockSpec((N_DEV, n, D), lambda i: (0, part, 0), pipeline_mode=pl.Buffered(1))


def _phases(ext, ph, rows):
    for p in range(8):
        ph[p, :, :] = ext[p:p + rows, :]


def _f_in_conv(x, g, G, n_in, cw, cb, lg, lb, tm, comm=None):
    S, D = x.shape
    N = N_DEV * n_in
    CW = cw.shape[1]
    RB = 64

    def body(x_ref, g_ref, w_ref, cw_ref, cb_ref, lg_ref, lb_ref, h_ref, p_ref, c_ref, u_ref, ext, ph):
        i = pl.program_id(0)

        @pl.when(i == 0)
        def _():
            ext[0:HALO, :] = jnp.zeros((HALO, CW), F32)
            ext[HALO + tm:, :] = jnp.zeros((8, CW), F32)

        @pl.when(i > 0)
        def _():
            ext[0:HALO, :] = ext[tm:tm + HALO, :]

        xv = x_ref[...]
        r = lax.rsqrt(jnp.mean(xv * xv, axis=-1, keepdims=True) + EPS)
        h = ((xv * r) * g_ref[...]).astype(CDT)
        h_ref[...] = h
        w = w_ref[...].reshape(N, D)
        pab = _dot_nt(h, w[0:2 * CW])
        p_ref[:, 0:2 * CW] = pab
        p_ref[:, 2 * CW:] = _dot_nt(h, w[2 * CW:])
        ext[HALO:HALO + tm, :] = pab[:, :CW] * _sigmoid(pab[:, CW:])
        _phases(ext, ph, tm + HALO)
        for rb in range(tm // RB):
            acc = jnp.zeros((RB, CW), F32) + cb_ref[...]
            for k in range(CONV_K):
                o = rb * RB + HALO - (CONV_K - 1) + k
                acc = acc + cw_ref[k:k + 1, :] * ph[o % 8, o - o % 8:o - o % 8 + RB, :]
            c_ref[rb * RB:(rb + 1) * RB, :] = acc
            mu = jnp.mean(acc, axis=-1, keepdims=True)
            var = jnp.mean(jnp.square(acc - mu), axis=-1, keepdims=True)
            z = ((acc - mu) * lax.rsqrt(var + EPS)) * lg_ref[...] + lb_ref[...]
            u_ref[rb * RB:(rb + 1) * RB, :] = (z * _sigmoid(z)).astype(CDT)

    row = lambda n: pl.BlockSpec((tm, n), lambda i: (i, 0))
    return _launch(
        "f_in_conv" if comm is None else "f_in_conv_ag", body, (S // tm,),
        [row(D), _resident((1, D)), _wres(n_in, D, 0),
         _resident((HALO, CW)), _resident((1, CW)), _resident((1, CW)), _resident((1, CW))],
        [row(D), row(N), row(CW), row(CW)],
        [jax.ShapeDtypeStruct((S, D), CDT), jax.ShapeDtypeStruct((S, N), F32),
         jax.ShapeDtypeStruct((S, CW), F32), jax.ShapeDtypeStruct((S, CW), CDT)],
        [pltpu.VMEM((tm + HALO + 8, CW), F32), pltpu.VMEM((8, tm + HALO, CW), F32)],
        (x, g, G, cw, cb, lg, lb), ("arbitrary",), comm)


def _rot(t, c, s):
    return t * c + pltpu.roll(t, HD // 2, 1) * s


def _f_ret(proj, u, cosT, sinT, Mt, qd, kd, gs, gn):
    S = proj.shape[0]
    RW = HEADS * HD
    NB = S // SC
    scale = HD ** -0.5

    def body(q_ref, k_ref, v_ref, g_ref, u_ref, c_ref, s_ref, m_ref, qd_ref, kd_ref, gs_ref, gn_ref,
             rraw_ref, st_ref, mix_ref, state):
        @pl.when(pl.program_id(0) == 0)
        def _():
            state[...] = jnp.zeros_like(state)

        mix_ref[:, 0:RW] = u_ref[...]
        cv = c_ref[...]
        sv = s_ref[...]
        for h in range(HEADS):
            cs = slice(h * HD, (h + 1) * HD)
            q = _rot(q_ref[:, cs], cv, sv)
            k = _rot(k_ref[:, cs], cv, sv) * scale
            vb = v_ref[:, cs].astype(CDT)
            qb = q.astype(CDT)
            kb = k.astype(CDT)
            a = _dot_nt(qb, kb) * m_ref[h]
            sp = state[h]
            spb = sp.astype(CDT)
            st_ref[0, h] = spb
            r = _dot(a.astype(CDT), vb) + _dot((q * qd_ref[h]).astype(CDT), spb)
            kv = _dot_tn((k * kd_ref[h]).astype(CDT), vb)
            state[h] = gs_ref[h, 0:1, :] * sp + kv
            rraw_ref[:, cs] = r
            mu = jnp.mean(r, axis=-1, keepdims=True)
            var = jnp.mean(jnp.square(r - mu), axis=-1, keepdims=True)
            n = (r - mu) * lax.rsqrt(var + EPS)
            gv = g_ref[:, cs]
            mix_ref[:, RW + h * HD:RW + (h + 1) * HD] = ((n * gn_ref[:, cs]) * (gv * _sigmoid(gv))).astype(CDT)

    col = lambda j: pl.BlockSpec((SC, RW), lambda i: (i, j))
    return _call(
        body, name="f_ret", grid=(NB,),
        in_specs=[col(2), col(3), col(4), col(5),
                  pl.BlockSpec((SC, RW), lambda i: (i, 0)),
                  pl.BlockSpec((SC, HD), lambda i: (i, 0)), pl.BlockSpec((SC, HD), lambda i: (i, 0)),
                  _resident((HEADS, SC, SC)), _resident((HEADS, SC, HD)), _resident((HEADS, SC, HD)),
                  _resident((HEADS, 8, HD)), _resident((1, RW))],
        out_specs=[pl.BlockSpec((SC, RW), lambda i: (i, 0)),
                   pl.BlockSpec((1, HEADS, HD, HD), lambda i: (i, 0, 0, 0)),
                   pl.BlockSpec((SC, 2 * RW), lambda i: (i, 0))],
        out_shape=[jax.ShapeDtypeStruct((S, RW), F32),
                   jax.ShapeDtypeStruct((NB, HEADS, HD, HD), CDT),
                   jax.ShapeDtypeStruct((S, 2 * RW), CDT)],
        scratch_shapes=[pltpu.VMEM((HEADS, HD, HD), F32)],
        compiler_params=_cp(("arbitrary",)),
    )(proj, proj, proj, proj, u, cosT, sinT, Mt, qd, kd, gs, gn)


def _f_mlp(x, mixed, g2, G, n_out, n_ff, parts, tm, comm=None):
    S, D = x.shape
    FF = N_DEV * n_ff
    p_wo, p_wg, p_wu, p_wd = parts

    def body(x_ref, m_ref, wo_ref, g_ref, wg_ref, wu_ref, wd_ref,
             xm_ref, h2_ref, gate_ref, up_ref, act_ref, xo_ref):
        xm = x_ref[...] + _dot(m_ref[...], wo_ref[...].reshape(N_DEV * n_out, D))
        xm_ref[...] = xm
        r = lax.rsqrt(jnp.mean(xm * xm, axis=-1, keepdims=True) + EPS)
        h2 = ((xm * r) * g_ref[...]).astype(CDT)
        h2_ref[...] = h2
        gate = _dot_nt(h2, wg_ref[...].reshape(FF, D))
        up = _dot_nt(h2, wu_ref[...].reshape(FF, D))
        gate_ref[...] = gate.astype(CDT)
        up_ref[...] = up.astype(CDT)
        act = ((gate * _sigmoid(gate)) * up).astype(CDT)
        act_ref[...] = act
        xo_ref[...] = xm + _dot(act, wd_ref[...].reshape(FF, D))

    row = lambda n: pl.BlockSpec((tm, n), lambda i: (i, 0))
    return _launch(
        "f_mlp" if comm is None else "f_mlp_ag", body, (S // tm,),
        [row(D), row(D), _wres(n_out, D, p_wo), _resident((1, D)),
         _wres(n_ff, D, p_wg), _wres(n_ff, D, p_wu), _wres(n_ff, D, p_wd)],
        [row(D), row(D), row(FF), row(FF), row(FF), row(D)],
        [jax.ShapeDtypeStruct((S, D), F32), jax.ShapeDtypeStruct((S, D), CDT),
         jax.ShapeDtypeStruct((S, FF), CDT), jax.ShapeDtypeStruct((S, FF), CDT),
         jax.ShapeDtypeStruct((S, FF), CDT), jax.ShapeDtypeStruct((S, D), F32)],
        [], (x, mixed, G, g2, G, G, G), ("parallel",), comm)


def _f_loss(x, fg, tgt, tm):
    S, D = x.shape

    def body(x_ref, g_ref, t_ref, dx_ref, dxb_ref, loss_ref, dg_ref):
        @pl.when(pl.program_id(0) == 0)
        def _():
            loss_ref[...] = jnp.zeros_like(loss_ref)
            dg_ref[...] = jnp.zeros_like(dg_ref)

        xv = x_ref[...]
        r = lax.rsqrt(jnp.mean(xv * xv, axis=-1, keepdims=True) + EPS)
        y = (xv * r) * g_ref[...]
        e = y - t_ref[...]
        loss_ref[...] += 0.5 * jnp.sum(jnp.mean(e * e, axis=-1, keepdims=True))
        dy = e * (1.0 / D)
        dx, dgx = _rms_bwd(xv, r, g_ref[...], dy)
        dg_ref[...] += jnp.sum(dgx, axis=0, keepdims=True)
        dx_ref[...] = dx
        dxb_ref[...] = dx.astype(CDT)

    row = pl.BlockSpec((tm, D), lambda i: (i, 0))
    return _call(
        body, name="f_loss", grid=(S // tm,),
        in_specs=[row, _resident((1, D)), row],
        out_specs=[row, row, pl.BlockSpec((1, 128), lambda i: (0, 0)), pl.BlockSpec((1, D), lambda i: (0, 0))],
        out_shape=[jax.ShapeDtypeStruct((S, D), F32), jax.ShapeDtypeStruct((S, D), CDT),
                   jax.ShapeDtypeStruct((1, 128), F32), jax.ShapeDtypeStruct((1, D), F32)],
        compiler_params=_cp(("arbitrary",)),
    )(x, fg, tgt)


def _b_mlp(dxb, dx, xm, gate, up, g2, G, n_out, n_ff, parts, tm, comm=None):
    S, D = dx.shape
    FF = N_DEV * n_ff
    p_wo, p_wg, p_wu, p_wd = parts

    def body(dxb_ref, dx_ref, xm_ref, gate_ref, up_ref, g_ref, wd_ref, wg_ref, wu_ref, wo_ref,
             dgate_ref, dup_ref, dxm_ref, dxmb_ref, dmix_ref, dg_ref):
        @pl.when(pl.program_id(0) == 0)
        def _():
            dg_ref[...] = jnp.zeros_like(dg_ref)

        dact = _dot_nt(dxb_ref[...], wd_ref[...].reshape(FF, D))
        gate = gate_ref[...].astype(F32)
        up = up_ref[...].astype(F32)
        sg = _sigmoid(gate)
        sil = gate * sg
        dgate = ((dact * up) * (sg * (1.0 + gate * (1.0 - sg)))).astype(CDT)
        dup = (dact * sil).astype(CDT)
        dgate_ref[...] = dgate
        dup_ref[...] = dup
        dh2 = _dot(dgate, wg_ref[...].reshape(FF, D)) + _dot(dup, wu_ref[...].reshape(FF, D))
        xm = xm_ref[...]
        r = lax.rsqrt(jnp.mean(xm * xm, axis=-1, keepdims=True) + EPS)
        dxn, dgx = _rms_bwd(xm, r, g_ref[...], dh2)
        dg_ref[...] += jnp.sum(dgx, axis=0, keepdims=True)
        dxm = dx_ref[...] + dxn
        dxm_ref[...] = dxm
        dxmb = dxm.astype(CDT)
        dxmb_ref[...] = dxmb
        dmix_ref[...] = _dot_nt(dxmb, wo_ref[...].reshape(N_DEV * n_out, D))

    row = lambda n: pl.BlockSpec((tm, n), lambda i: (i, 0))
    return _launch(
        "b_mlp" if comm is None else "b_mlp_rs", body, (S // tm,),
        [row(D), row(D), row(D), row(FF), row(FF), _resident((1, D)),
         _wres(n_ff, D, p_wd), _wres(n_ff, D, p_wg), _wres(n_ff, D, p_wu), _wres(n_out, D, p_wo)],
        [row(FF), row(FF), row(D), row(D), row(D), pl.BlockSpec((1, D), lambda i: (0, 0))],
        [jax.ShapeDtypeStruct((S, FF), CDT), jax.ShapeDtypeStruct((S, FF), CDT),
         jax.ShapeDtypeStruct((S, D), F32), jax.ShapeDtypeStruct((S, D), CDT),
         jax.ShapeDtypeStruct((S, D), F32), jax.ShapeDtypeStruct((1, D), F32)],
        [], (dxb, dx, xm, gate, up, g2, G, G, G, G), ("arbitrary",), comm)


def _b_conv(dmix, c, proj, cw, lg, lb, tm, comm=None):
    S = proj.shape[0]
    CW = cw.shape[1]
    RB = 64
    hb = tm // HALO
    nt = S // tm
    last_h = S // HALO - 1

    def body(du_ref, duh_ref, c_ref, ch_ref, ab_ref, abh_ref, cw_ref, lg_ref, lb_ref,
             dab_ref, dcw_ref, dcb_ref, dlg_ref, dlb_ref, ext_u, ext_dc, ph_u, ph_dc, wacc):
        i = pl.program_id(0)

        @pl.when(i == 0)
        def _():
            wacc[...] = jnp.zeros_like(wacc)
            dcb_ref[...] = jnp.zeros_like(dcb_ref)
            dlg_ref[...] = jnp.zeros_like(dlg_ref)
            dlb_ref[...] = jnp.zeros_like(dlb_ref)

        def ln_bwd(cv, du):
            mu = jnp.mean(cv, axis=-1, keepdims=True)
            var = jnp.mean(jnp.square(cv - mu), axis=-1, keepdims=True)
            rstd = lax.rsqrt(var + EPS)
            n = (cv - mu) * rstd
            z = n * lg_ref[...] + lb_ref[...]
            sz = _sigmoid(z)
            dz = du * (sz * (1.0 + z * (1.0 - sz)))
            dn = dz * lg_ref[...]
            dc = rstd * (dn - jnp.mean(dn, axis=-1, keepdims=True)
                         - n * jnp.mean(dn * n, axis=-1, keepdims=True))
            return dc, dz, n

        hv = abh_ref[...]
        ext_u[0:HALO, :] = jnp.where(i > 0, hv[:, :CW] * _sigmoid(hv[:, CW:]), 0.0)
        av = ab_ref[...]
        sb = _sigmoid(av[:, CW:])
        ext_u[HALO:HALO + tm, :] = av[:, :CW] * sb
        ext_u[HALO + tm:, :] = jnp.zeros((8, CW), F32)
        dc, dz, n = ln_bwd(c_ref[...], du_ref[...])
        ext_dc[0:tm, :] = dc
        dch, _, _ = ln_bwd(ch_ref[...], duh_ref[...])
        ext_dc[tm:tm + HALO, :] = jnp.where(i < nt - 1, dch, 0.0)
        ext_dc[tm + HALO:, :] = jnp.zeros((8, CW), F32)
        dlg_ref[...] += jnp.sum(dz * n, axis=0, keepdims=True)
        dlb_ref[...] += jnp.sum(dz, axis=0, keepdims=True)
        dcb_ref[...] += jnp.sum(dc, axis=0, keepdims=True)
        _phases(ext_u, ph_u, tm + HALO)
        _phases(ext_dc, ph_dc, tm + HALO)

        for rb in range(tm // RB):
            rs = slice(rb * RB, (rb + 1) * RB)
            dcb = ext_dc[rs, :]
            for k in range(CONV_K):
                o = rb * RB + HALO - (CONV_K - 1) + k
                prod = dcb * ph_u[o % 8, o - o % 8:o - o % 8 + RB, :]
                part = prod[0:8]
                for j in range(1, RB // 8):
                    part = part + prod[8 * j:8 * j + 8]
                wacc[k] += part
            acc = jnp.zeros((RB, CW), F32)
            for k in range(CONV_K):
                o = rb * RB + (CONV_K - 1) - k
                acc = acc + cw_ref[k:k + 1, :] * ph_dc[o % 8, o - o % 8:o - o % 8 + RB, :]
            a_r = ab_ref[rs, 0:CW]
            s_r = _sigmoid(ab_ref[rs, CW:2 * CW])
            dab_ref[rs, 0:CW] = (acc * s_r).astype(CDT)
            dab_ref[rs, CW:2 * CW] = (acc * a_r * (s_r * (1.0 - s_r))).astype(CDT)

        @pl.when(i == nt - 1)
        def _():
            for k in range(CONV_K):
                dcw_ref[k:k + 1, :] = jnp.sum(wacc[k], axis=0, keepdims=True)
            dcw_ref[CONV_K:, :] = jnp.zeros((HALO - CONV_K, CW), F32)

    tile = lambda n, j: pl.BlockSpec((tm, n), lambda i: (i, j))
    nxt = lambda n, j: pl.BlockSpec((HALO, n), lambda i: (jnp.minimum((i + 1) * hb, last_h), j))
    return _launch(
        "b_conv" if comm is None else "b_conv_rs", body, (nt,),
        [tile(CW, 0), nxt(CW, 0), tile(CW, 0), nxt(CW, 0),
         tile(2 * CW, 0),
         pl.BlockSpec((HALO, 2 * CW), lambda i: (jnp.maximum(i * hb - 1, 0), 0)),
         _resident((HALO, CW)), _resident((1, CW)), _resident((1, CW))],
        [tile(2 * CW, 0),
         pl.BlockSpec((HALO, CW), lambda i: (0, 0)), pl.BlockSpec((1, CW), lambda i: (0, 0)),
         pl.BlockSpec((1, CW), lambda i: (0, 0)), pl.BlockSpec((1, CW), lambda i: (0, 0))],
        [jax.ShapeDtypeStruct((S, 2 * CW), CDT),
         jax.ShapeDtypeStruct((HALO, CW), F32), jax.ShapeDtypeStruct((1, CW), F32),
         jax.ShapeDtypeStruct((1, CW), F32), jax.ShapeDtypeStruct((1, CW), F32)],
        [pltpu.VMEM((tm + HALO + 8, CW), F32), pltpu.VMEM((tm + HALO + 8, CW), F32),
         pltpu.VMEM((8, tm + HALO, CW), F32), pltpu.VMEM((8, tm + HALO, CW), F32),
         pltpu.VMEM((HALO, 8, CW), F32)],
        (dmix, dmix, c, c, proj, proj, cw, lg, lb), ("arbitrary",), comm)


def _b_ret(dmix, dab, proj, rraw, states, cosT, sinT, Mt, qd, kd, gs, gn, comm=None):
    S = proj.shape[0]
    RW = HEADS * HD
    NB = S // SC
    scale = HD ** -0.5

    def body(dro_ref, dab_ref, q_ref, k_ref, v_ref, g_ref, rraw_ref, st_ref, c_ref, s_ref,
             m_ref, qd_ref, kd_ref, gs_ref, gn_ref, dp_ref, dgn_ref, G):
        @pl.when(pl.program_id(0) == 0)
        def _():
            G[...] = jnp.zeros_like(G)
            dgn_ref[...] = jnp.zeros_like(dgn_ref)

        dp_ref[:, 0:2 * RW] = dab_ref[...]
        cv = c_ref[...]
        sv = s_ref[...]
        for h in range(HEADS):
            cs = slice(h * HD, (h + 1) * HD)
            q = _rot(q_ref[:, cs], cv, sv)
            k = _rot(k_ref[:, cs], cv, sv) * scale
            qb = q.astype(CDT)
            kb = k.astype(CDT)
            vb = v_ref[:, cs].astype(CDT)
            spb = st_ref[0, h]
            r = rraw_ref[:, cs]
            mu = jnp.mean(r, axis=-1, keepdims=True)
            var = jnp.mean(jnp.square(r - mu), axis=-1, keepdims=True)
            rstd = lax.rsqrt(var + EPS)
            n = (r - mu) * rstd
            gv = g_ref[:, cs]
            sg = _sigmoid(gv)
            sil = gv * sg
            dro = dro_ref[:, cs]
            gnv = gn_ref[:, cs]
            dgn_ref[:, cs] += jnp.sum(dro * n * sil, axis=0, keepdims=True)
            dgate = dro * (n * gnv) * (sg * (1.0 + gv * (1.0 - sg)))
            dn = dro * gnv * sil
            dr = rstd * (dn - jnp.mean(dn, axis=-1, keepdims=True)
                         - n * jnp.mean(dn * n, axis=-1, keepdims=True))
            drb = dr.astype(CDT)
            mh = m_ref[h]
            ab = (_dot_nt(qb, kb) * mh).astype(CDT)
            dab_ = (_dot_nt(drb, vb) * mh).astype(CDT)
            qdb = (q * qd_ref[h]).astype(CDT)
            kdb = (k * kd_ref[h]).astype(CDT)
            gc = G[h]
            gb = gc.astype(CDT)
            dq = _dot(dab_, kb) + _dot_nt(drb, spb) * qd_ref[h]
            dk = _dot_tn(dab_, qb) + _dot_nt(vb, gb) * kd_ref[h]
            dv = _dot_tn(ab, drb) + _dot(kdb, gb)
            G[h] = gs_ref[h, 0:1, :] * gc + _dot_tn(qdb, drb)
            dk = dk * scale
            dqp = dq * cv + pltpu.roll(dq * sv, HD // 2, 1)
            dkp = dk * cv + pltpu.roll(dk * sv, HD // 2, 1)
            base = 2 * RW
            dp_ref[:, base + h * HD:base + (h + 1) * HD] = dqp.astype(CDT)
            dp_ref[:, base + RW + h * HD:base + RW + (h + 1) * HD] = dkp.astype(CDT)
            dp_ref[:, base + 2 * RW + h * HD:base + 2 * RW + (h + 1) * HD] = dv.astype(CDT)
            dp_ref[:, base + 3 * RW + h * HD:base + 3 * RW + (h + 1) * HD] = dgate.astype(CDT)

    rev = lambda n, j: pl.BlockSpec((SC, n), lambda i: (NB - 1 - i, j))
    return _launch(
        "b_ret" if comm is None else "b_ret_rs", body, (NB,),
        [rev(RW, 1), rev(2 * RW, 0), rev(RW, 2), rev(RW, 3), rev(RW, 4), rev(RW, 5), rev(RW, 0),
         pl.BlockSpec((1, HEADS, HD, HD), lambda i: (NB - 1 - i, 0, 0, 0)),
         rev(HD, 0), rev(HD, 0),
         _resident((HEADS, SC, SC)), _resident((HEADS, SC, HD)), _resident((HEADS, SC, HD)),
         _resident((HEADS, 8, HD)), _resident((1, RW))],
        [rev(6 * RW, 0), pl.BlockSpec((1, RW), lambda i: (0, 0))],
        [jax.ShapeDtypeStruct((S, 6 * RW), CDT), jax.ShapeDtypeStruct((1, RW), F32)],
        [pltpu.VMEM((HEADS, HD, HD), F32)],
        (dmix, dab, proj, proj, proj, proj, rraw, states, cosT, sinT, Mt, qd, kd, gs, gn), ("arbitrary",), comm)


def _b_in(dproj, G, n_in, x, g1, dxm, tm, comm=None):
    S, D = x.shape
    N = N_DEV * n_in

    def body(dp_ref, w_ref, x_ref, g_ref, dxm_ref, dx_ref, dxb_ref, dg_ref):
        @pl.when(pl.program_id(0) == 0)
        def _():
            dg_ref[...] = jnp.zeros_like(dg_ref)

        dh = _dot(dp_ref[...], w_ref[...].reshape(N, D))
        xv = x_ref[...]
        r = lax.rsqrt(jnp.mean(xv * xv, axis=-1, keepdims=True) + EPS)
        dxn, dgx = _rms_bwd(xv, r, g_ref[...], dh)
        dg_ref[...] += jnp.sum(dgx, axis=0, keepdims=True)
        dx = dxm_ref[...] + dxn
        dx_ref[...] = dx
        dxb_ref[...] = dx.astype(CDT)

    row = lambda n: pl.BlockSpec((tm, n), lambda i: (i, 0))
    return _launch(
        "b_in" if comm is None else "b_in_rs", body, (S // tm,),
        [row(N), _wres(n_in, D, 0), row(D), _resident((1, D)), row(D)],
        [row(D), row(D), pl.BlockSpec((1, D), lambda i: (0, 0))],
        [jax.ShapeDtypeStruct((S, D), F32), jax.ShapeDtypeStruct((S, D), CDT),
         jax.ShapeDtypeStruct((1, D), F32)],
        [], (dproj, G, x, g1, dxm), ("arbitrary",), comm)


def _dw_tn(a, b, tm, tk):
    S, M = a.shape
    N = b.shape[1]

    def body(a_ref, b_ref, o_ref):
        @pl.when(pl.program_id(1) == 0)
        def _():
            o_ref[...] = jnp.zeros_like(o_ref)

        o_ref[...] += _dot_tn(a_ref[...], b_ref[...])

    return _call(
        body, name="dw_tn", grid=(M // tm, S // tk),
        in_specs=[pl.BlockSpec((tk, tm), lambda m, k: (k, m)), pl.BlockSpec((tk, N), lambda m, k: (k, 0))],
        out_specs=pl.BlockSpec((tm, N), lambda m, k: (m, 0)),
        out_shape=jax.ShapeDtypeStruct((M, N), F32),
        compiler_params=_cp(("parallel", "arbitrary")),
    )(a, b)


def _add_pair(parts, recv):
    K = len(parts)
    C = parts[0].shape[2]
    halves = 2

    def body(*refs):
        cc = lax.axis_index("c")
        chip = 2 * lax.axis_index("x") + lax.axis_index("y")
        q = pl.program_id(1)
        for k in range(K):
            s = refs[k][cc] + refs[K + k][...]
            refs[3 * K + k][...] = s.astype(CDT)

            @pl.when(q == chip)
            def _():
                refs[2 * K + k][...] = s

    ns = [p.shape[1] for p in parts]
    in_specs = [pl.BlockSpec((None, 2, n // halves, C), lambda r, q: (q, 0, r, 0)) for n in ns]
    in_specs += [pl.BlockSpec((None, n // halves, C), lambda r, q: (q, r, 0)) for n in ns]
    out_f = [pl.BlockSpec((n // halves, C), lambda r, q: (r, 0)) for n in ns]
    out_b = [pl.BlockSpec((None, n // halves, C), lambda r, q: (q, r, 0)) for n in ns]
    res = _call(
        body, name="add_pair", grid=(halves, N_DEV // 2),
        in_specs=in_specs, out_specs=out_f + out_b,
        out_shape=[jax.ShapeDtypeStruct((n, C), F32) for n in ns]
        + [jax.ShapeDtypeStruct((N_DEV // 2, n, C), CDT) for n in ns],
        compiler_params=_cp(("parallel", "arbitrary")),
    )(*[p.reshape(N_DEV // 2, 2, p.shape[1], C) for p in parts], *recv)
    return list(res[:K]), list(res[K:])


def _sum_chips(sums, recv):
    K = len(sums)

    def body(*refs):
        for k in range(K):
            r = refs[K + k]
            refs[2 * K + k][...] = ((refs[k][...] + r[0].astype(F32)) + r[1].astype(F32)) + r[2].astype(F32)

    vm = pl.BlockSpec(memory_space=pltpu.VMEM)
    res = _call(
        body, name="sum_chips", in_specs=[vm] * (2 * K), out_specs=[vm] * K,
        out_shape=[jax.ShapeDtypeStruct(s.shape, F32) for s in sums],
        compiler_params=_cp(),
    )(*sums, *recv)
    return list(res)


def _adamw(w, g, m, v, tr):
    R, C = w.shape
    c1 = 1.0 - ADAM_B1 ** ADAM_STEP
    c2 = 1.0 - ADAM_B2 ** ADAM_STEP

    def body(w_ref, g_ref, m_ref, v_ref, d_ref, mo_ref, vo_ref):
        gv = g_ref[...]
        mn = ADAM_B1 * m_ref[...] + (1.0 - ADAM_B1) * gv
        vn = ADAM_B2 * v_ref[...] + (1.0 - ADAM_B2) * jnp.square(gv)
        mo_ref[...] = mn
        vo_ref[...] = vn
        d_ref[...] = -ADAM_LR * ((mn / c1) / (jnp.sqrt(vn / c2) + ADAM_EPS) + ADAM_WD * w_ref[...])

    blk = pl.BlockSpec((tr, C), lambda i: (i, 0))
    sh = jax.ShapeDtypeStruct((R, C), F32)
    return _call(
        body, name="adamw", grid=(R // tr,),
        in_specs=[blk, blk, blk, blk], out_specs=[blk, blk, blk], out_shape=[sh, sh, sh],
        compiler_params=_cp(("parallel",)),
    )(w, g, m, v)


def _coords():
    return lax.axis_index("x"), lax.axis_index("y"), lax.axis_index("c")


def _peer(x, y, c, d):
    return (x ^ (d >> 2), y ^ ((d >> 1) & 1), c ^ (d & 1))


def _ag_comm(ps):
    K = len(ps)

    def plan(cins, couts, sems):
        send_sems, recv_sems, local_sems = sems
        x, y, c = _coords()
        me, sibling = (x, y, c), (x, y, 1 - c)
        chips = [(1 - x, y), (x, 1 - y), (1 - x, 1 - y)]
        mine, first, passed, got_ici, got_d2d = [], [], [], [], []
        for a in range(K):
            x_ref, out_ref = cins[a], couts[a]
            R = x_ref.shape[0]

            def rows(px, py, pc, out_ref=out_ref, R=R):
                return out_ref.at[pl.ds((4 * px + 2 * py + pc) * R, R), :]

            def copy(k, block, to, src=None, rows=rows, a=a):
                return pltpu.make_async_remote_copy(
                    src_ref=rows(*block) if src is None else src, dst_ref=rows(*block),
                    send_sem=send_sems.at[7 * a + k], recv_sem=recv_sems.at[7 * a + k],
                    device_id=to, device_id_type=MESH)

            mine.append(pltpu.make_async_copy(x_ref, rows(*me), local_sems.at[a]))
            first.append(copy(0, me, sibling, src=x_ref))
            first += [copy(1 + j, me, (*chip, c), src=x_ref) for j, chip in enumerate(chips)]
            passed += [copy(4 + j, (*chip, c), sibling) for j, chip in enumerate(chips)]
            got_ici += [copy(1 + j, (*chip, c), me) for j, chip in enumerate(chips)]
            got_d2d.append(copy(0, sibling, me))
            got_d2d += [copy(4 + j, (*chip, 1 - c), me) for j, chip in enumerate(chips)]
        return mine, first, passed, got_ici, got_d2d

    def start(*a):
        mine, first, _, _, _ = plan(*a)
        for cp in mine + first:
            cp.start()

    def mid(*a):
        _, _, passed, got_ici, _ = plan(*a)
        for got, fwd in zip(got_ici, passed):
            got.wait_recv()
            fwd.start()

    def finish(*a):
        mine, first, passed, _, got_d2d = plan(*a)
        for got in got_d2d:
            got.wait_recv()
        for cp in first + passed:
            cp.wait_send()
        for cp in mine:
            cp.wait()

    return _Comm(ps, [jax.ShapeDtypeStruct((N_DEV * p.shape[0], p.shape[1]), p.dtype) for p in ps],
                 [pltpu.SemaphoreType.DMA((7 * K,)), pltpu.SemaphoreType.DMA((7 * K,)),
                  pltpu.SemaphoreType.DMA((K,))], start, mid, finish)


def _rs_pair_comm(parts):
    K = len(parts)

    def plan(cins, couts, sems):
        send_sems, recv_sems = sems
        x, y, c = _coords()
        sibling = (x, y, 1 - c)
        cps = []
        for k in range(K):
            for q in range(N_DEV // 2):
                cps.append(pltpu.make_async_remote_copy(
                    src_ref=cins[k].at[2 * q + (1 - c)], dst_ref=couts[k].at[q],
                    send_sem=send_sems.at[k], recv_sem=recv_sems.at[k], device_id=sibling, device_id_type=MESH))
        whole = [pltpu.make_async_remote_copy(
            src_ref=couts[k], dst_ref=couts[k], send_sem=send_sems.at[k], recv_sem=recv_sems.at[k],
            device_id=sibling, device_id_type=MESH) for k in range(K)]
        return cps, whole

    def start(*a):
        for cp in plan(*a)[0]:
            cp.start()

    def finish(*a):
        for w in plan(*a)[1]:
            w.wait_recv()
            w.wait_send()

    return _Comm(parts, [jax.ShapeDtypeStruct((N_DEV // 2,) + p.shape[1:], p.dtype) for p in parts],
                 [pltpu.SemaphoreType.DMA((K,)), pltpu.SemaphoreType.DMA((K,))], start, None, finish)


def _rs_chip_comm(sums):
    K = len(sums)

    def plan(cins, couts, sems):
        send_sems, recv_sems = sems
        x, y, c = _coords()
        cps = []
        for d in range(1, N_DEV // 2):
            px, py = x ^ (d >> 1), y ^ (d & 1)
            for k in range(K):
                s = (d - 1) * K + k
                cps.append(pltpu.make_async_remote_copy(
                    src_ref=cins[k].at[2 * px + py], dst_ref=couts[k].at[d - 1],
                    send_sem=send_sems.at[s], recv_sem=recv_sems.at[s], device_id=(px, py, c), device_id_type=MESH))
        return cps

    def start(*a):
        for cp in plan(*a):
            cp.start()

    def finish(*a):
        cps = plan(*a)
        for cp in cps:
            cp.wait_recv()
        for cp in cps:
            cp.wait_send()

    n_sem = (N_DEV // 2 - 1) * K
    return _Comm(sums, [jax.ShapeDtypeStruct((N_DEV // 2 - 1,) + s.shape[1:], s.dtype) for s in sums],
                 [pltpu.SemaphoreType.DMA((n_sem,)), pltpu.SemaphoreType.DMA((n_sem,))], start, None, finish)


def _gather_small(v, reduce):
    R, C = v.shape

    def exchange(v_ref, buf, send_sems, recv_sems):
        x, y, c = _coords()
        me = 4 * x + 2 * y + c
        buf[me] = v_ref[...]
        cps = []
        for d in range(1, N_DEV):
            cp = pltpu.make_async_remote_copy(
                src_ref=v_ref, dst_ref=buf.at[me], send_sem=send_sems.at[d - 1], recv_sem=recv_sems.at[d - 1],
                device_id=_peer(x, y, c, d), device_id_type=MESH)
            cp.start()
            cps.append(cp)
        for cp in cps:
            cp.wait_recv()
        for cp in cps:
            cp.wait_send()

    sems = [pltpu.SemaphoreType.DMA((7,)), pltpu.SemaphoreType.DMA((7,))]
    vm = pl.BlockSpec(memory_space=pltpu.VMEM)
    if reduce:
        def body(v_ref, o_ref, buf, send_sems, recv_sems):
            exchange(v_ref, buf, send_sems, recv_sems)
            acc = buf[0]
            for s in range(1, N_DEV):
                acc = acc + buf[s]
            o_ref[...] = acc

        return _call(body, name="allreduce_small", in_specs=[vm], out_specs=vm,
                     out_shape=jax.ShapeDtypeStruct((R, C), F32),
                     scratch_shapes=[pltpu.VMEM((N_DEV, R, C), F32)] + sems)(v)

    def body(v_ref, o_ref, send_sems, recv_sems):
        exchange(v_ref, o_ref, send_sems, recv_sems)

    return _call(body, name="allgather_small", in_specs=[vm], out_specs=vm,
                 out_shape=jax.ShapeDtypeStruct((N_DEV, R, C), F32), scratch_shapes=sems)(v)


def _tables(S):
    half = HD // 2
    pos = jnp.arange(S, dtype=F32)
    freqs = ROPE_BASE ** (-jnp.arange(half, dtype=F32) / half)
    ang = pos[:, None] * freqs[None, :]
    cos, sin = jnp.cos(ang), jnp.sin(ang)
    cosT = jnp.concatenate([cos, cos], axis=-1)
    sinT = jnp.concatenate([-sin, sin], axis=-1)
    log_g = jnp.log(1.0 - 2.0 ** (-5.0 - jnp.arange(HEADS, dtype=F32)))
    idx = jnp.arange(SC, dtype=F32)
    ci = jnp.arange(SC) // CHUNK
    diff = idx[:, None] - idx[None, :]
    same = ci[:, None] == ci[None, :]
    earlier = ci[None, :] < ci[:, None]
    expo = jnp.where(same, jnp.abs(diff), diff)
    Mt = jnp.where((same | earlier)[None], jnp.exp(log_g[:, None, None] * expo[None]), 0.0)
    ones = jnp.ones((1, 1, HD), F32)
    qd = jnp.exp(log_g[:, None] * (idx + 1.0)[None, :])[:, :, None] * ones
    kd = jnp.exp(log_g[:, None] * (SC - 1.0 - idx)[None, :])[:, :, None] * ones
    gs = jnp.exp(log_g * SC)[:, None, None] * jnp.ones((1, 8, HD), F32)
    return cosT, sinT, Mt, qd, kd, gs


def _pad_rows(a, rows):
    return jnp.pad(a, ((0, rows - a.shape[0]), (0, 0)))


def kernel(x, norm1_g, w_in, conv_w, conv_b, conv_ln_g, conv_ln_b, ret_gn_g, w_out, norm2_g, w_gate, w_up, w_down, final_g, loss_target, m_norm1_g, m_w_in, m_conv_w, m_conv_b, m_conv_ln_g, m_conv_ln_b, m_ret_gn_g, m_w_out, m_norm2_g, m_w_gate, m_w_up, m_w_down, m_final_g, v_norm1_g, v_w_in, v_conv_w, v_conv_b, v_conv_ln_g, v_conv_ln_b, v_ret_gn_g, v_w_out, v_norm2_g, v_w_gate, v_w_up, v_w_down, v_final_g):
    L, D, n_in = w_in.shape
    n_out = w_out.shape[1]
    n_ff = w_gate.shape[2]
    S = x.shape[1]
    CW = conv_b.shape[1]
    IN, FF = N_DEV * n_in, N_DEV * n_ff
    ncw = conv_w.shape[2]
    x0 = x.reshape(S, D)
    tgt = loss_target.reshape(S, D)
    TM = min(512, S)
    TKW = min(1024, S)
    TMI = min(512, S)
    TMM = min(256, S)

    assert n_out <= n_ff
    RB_ = 4 * n_ff
    wparts = (0, 1, 2, 3)
    pack_a = jnp.swapaxes(w_in, 1, 2).astype(CDT)
    pack_b = jnp.concatenate([w_out, jnp.zeros((L, n_ff - n_out, D), F32), jnp.swapaxes(w_gate, 1, 2),
                              jnp.swapaxes(w_up, 1, 2), w_down], axis=1).astype(CDT)
    Ga = _comm_only(_ag_comm([pack_a[0]]), "ag_first")[0].reshape(N_DEV, n_in, D)
    Gb = None

    cwp = conv_w.reshape(L * CONV_K * ncw // 128, 128)
    cw_rows = -(-cwp.shape[0] // 8) * 8
    cwg = _gather_small(_pad_rows(cwp, cw_rows), reduce=False)[:, :cwp.shape[0], :]
    conv_w_full = jnp.moveaxis(cwg.reshape(N_DEV, L, CONV_K, ncw), 0, 2).reshape(L, CONV_K, CW)

    cosT, sinT, Mt, qd, kd, gs = _tables(S)

    saved = []
    xl = x0
    for l in range(L):
        cw = _pad_rows(conv_w_full[l], HALO)
        (h, proj, c, u), got = _f_in_conv(xl, norm1_g[l][None], Ga, n_in, cw, conv_b[l][None], conv_ln_g[l][None],
                                          conv_ln_b[l][None], TMI, _ag_comm([pack_b[0]]) if l == 0 else None)
        if got:
            Gb = got[0].reshape(N_DEV, RB_, D)
        rraw, states, mixed = _f_ret(proj, u, cosT, sinT, Mt, qd, kd, gs, ret_gn_g[l][None])
        (xm, h2, gate, up, act, xo), nxt = _f_mlp(
            xl, mixed, norm2_g[l][None], Gb, n_out, n_ff, wparts, TMM,
            _ag_comm([pack_a[l + 1], pack_b[l + 1]]) if l + 1 < L else None)
        saved.append(dict(x=xl, h=h, proj=proj, c=c, rraw=rraw, states=states, mixed=mixed, xm=xm, h2=h2,
                          gate=gate, up=up, act=act, cw=cw, Ga=Ga, Gb=Gb))
        if nxt:
            Ga, Gb = nxt[0].reshape(N_DEV, n_in, D), nxt[1].reshape(N_DEV, RB_, D)
        xl = xo

    dx, dxb, loss_p, dfg = _f_loss(xl, final_g[None], tgt, TM)

    small = []
    chip_sums = [None] * L
    chip_recv = [None] * L
    in_flight = None
    for l in reversed(range(L)):
        sv = saved[l]
        (dgate, dup, dxm, dxmb, dmix, dg2), got = _b_mlp(
            dxb, dx, sv["xm"], sv["gate"], sv["up"], norm2_g[l][None], sv["Gb"], n_out, n_ff, wparts, TMM, in_flight)
        if got:
            chip_recv[l + 1] = got + chip_recv[l + 1]
        d_wd = _dw_tn(sv["act"], dxb, FF // 2, TKW)
        d_wgT = _dw_tn(dgate, sv["h2"], FF // 2, TKW)
        d_wuT = _dw_tn(dup, sv["h2"], FF // 2, TKW)
        d_wo = _dw_tn(sv["mixed"], dxmb, D, TKW)
        parts_m = [d.reshape(N_DEV, -1, D) for d in (d_wo, d_wgT, d_wuT, d_wd)]
        (dab, dcw, dcb, dlg, dlb), pair_m = _b_conv(dmix, sv["c"], sv["proj"], sv["cw"], conv_ln_g[l][None],
                                                    conv_ln_b[l][None], TM, _rs_pair_comm(parts_m))
        sums_m, sums_mb = _add_pair(parts_m, pair_m)
        (dproj, dgn), recv_m = _b_ret(dmix, dab, sv["proj"], sv["rraw"], sv["states"], cosT, sinT, Mt, qd, kd, gs,
                                      ret_gn_g[l][None], _rs_chip_comm(sums_mb))
        d_winT = _dw_tn(dproj, sv["h"], IN // 2, TKW)
        parts_i = [d_winT.reshape(N_DEV, -1, D)]
        (dx, dxb, dg1), pair_i = _b_in(dproj, sv["Ga"], n_in, sv["x"], norm1_g[l][None], dxm, TMI,
                                       _rs_pair_comm(parts_i))
        sums_i, sums_ib = _add_pair(parts_i, pair_i)
        chip_sums[l] = sums_i + sums_m
        chip_recv[l] = recv_m
        in_flight = _rs_chip_comm(sums_ib)
        small.append(jnp.concatenate([dcw, dcb, dlg, dlb, dgn, dg1.reshape(2, CW), dg2.reshape(2, CW)], axis=0))
    chip_recv[0] = _comm_only(in_flight, "rs_last") + chip_recv[0]
    small = small[::-1]
    grad_x = dx.reshape(1, S, D)

    rows_l = HALO + 8
    loss_row = jnp.zeros((1, CW), F32).at[0, 0].set(loss_p[0, 0])
    sm = jnp.concatenate(small + [dfg.reshape(2, CW), loss_row], axis=0)
    sm_rows = -(-sm.shape[0] // 8) * 8
    sm = _gather_small(_pad_rows(sm, sm_rows), reduce=True)
    loss = sm[L * rows_l + 2, 0]
    g_final = sm[L * rows_l:L * rows_l + 2].reshape(D)
    per = sm[:L * rows_l].reshape(L, rows_l, CW)
    me = 4 * lax.axis_index("x") + 2 * lax.axis_index("y") + lax.axis_index("c")
    g_conv_w = lax.dynamic_slice_in_dim(per[:, :CONV_K, :], me * ncw, ncw, axis=2)
    g_conv_b, g_ln_g, g_ln_b, g_gn = per[:, HALO], per[:, HALO + 1], per[:, HALO + 2], per[:, HALO + 3]
    g_n1 = per[:, HALO + 4:HALO + 6].reshape(L, D)
    g_n2 = per[:, HALO + 6:HALO + 8].reshape(L, D)

    gl = [_sum_chips(chip_sums[l], chip_recv[l]) for l in range(L)]
    g_w_in = jnp.stack([gl[l][0].T for l in range(L)])
    g_w_out = jnp.stack([gl[l][1] for l in range(L)])
    g_w_gate = jnp.stack([gl[l][2].T for l in range(L)])
    g_w_up = jnp.stack([gl[l][3].T for l in range(L)])
    g_w_down = jnp.stack([gl[l][4] for l in range(L)])

    def big(w, g, m, v):
        sh = w.shape
        two = lambda a: a.reshape(-1, sh[-1])
        rows = two(w).shape[0]
        d, mn, vn = _adamw(two(w), two(g), two(m), two(v), rows // 8)
        return d.reshape(sh), mn.reshape(sh), vn.reshape(sh)

    names = ["norm1_g", "conv_w", "conv_b", "conv_ln_g", "conv_ln_b", "ret_gn_g", "norm2_g", "final_g"]
    sw = dict(norm1_g=(norm1_g, g_n1, m_norm1_g, v_norm1_g), conv_w=(conv_w, g_conv_w, m_conv_w, v_conv_w),
              conv_b=(conv_b, g_conv_b, m_conv_b, v_conv_b), conv_ln_g=(conv_ln_g, g_ln_g, m_conv_ln_g, v_conv_ln_g),
              conv_ln_b=(conv_ln_b, g_ln_b, m_conv_ln_b, v_conv_ln_b), ret_gn_g=(ret_gn_g, g_gn, m_ret_gn_g, v_ret_gn_g),
              norm2_g=(norm2_g, g_n2, m_norm2_g, v_norm2_g), final_g=(final_g, g_final, m_final_g, v_final_g))
    lens = [int(math.prod(sw[n][0].shape)) for n in names]
    tot = sum(lens)
    prow = -(-tot // (8 * CW)) * 8

    def packs(j):
        flat = jnp.concatenate([sw[n][j].reshape(-1) for n in names])
        return jnp.pad(flat, (0, prow * CW - tot)).reshape(prow, CW)

    sd, smn, svn = _adamw(packs(0), packs(1), packs(2), packs(3), prow)

    def unpack(a):
        flat = a.reshape(-1)
        out, o = {}, 0
        for n, ln in zip(names, lens):
            out[n] = flat[o:o + ln].reshape(sw[n][0].shape)
            o += ln
        return out

    sd, smn, svn = unpack(sd), unpack(smn), unpack(svn)
    res = {n: (sw[n][1], sd[n], smn[n], svn[n]) for n in names}
    res["w_in"] = (g_w_in,) + big(w_in, g_w_in, m_w_in, v_w_in)
    res["w_out"] = (g_w_out,) + big(w_out, g_w_out, m_w_out, v_w_out)
    res["w_gate"] = (g_w_gate,) + big(w_gate, g_w_gate, m_w_gate, v_w_gate)
    res["w_up"] = (g_w_up,) + big(w_up, g_w_up, m_w_up, v_w_up)
    res["w_down"] = (g_w_down,) + big(w_down, g_w_down, m_w_down, v_w_down)

    order = ["norm1_g", "w_in", "conv_w", "conv_b", "conv_ln_g", "conv_ln_b", "ret_gn_g", "w_out", "norm2_g",
             "w_gate", "w_up", "w_down", "final_g"]
    return (loss, grad_x, *[res[n][0] for n in order], *[res[n][1] for n in order],
            *[res[n][2] for n in order], *[res[n][3] for n in order])
```

```python
import functools
import math

import jax
import jax.numpy as jnp
from jax import lax
from jax.experimental import pallas as pl
from jax.experimental.pallas import tpu as pltpu

F32 = jnp.float32
CDT = jnp.bfloat16
EPS = 1e-6
CHUNK = 64
SC = 256
HEADS = 4
HD = 128
CONV_K = 31
HALO = 32
ROPE_BASE = 10000.0
ADAM_LR = 0.001
ADAM_B1 = 0.9
ADAM_B2 = 0.999
ADAM_EPS = 1e-08
ADAM_WD = 0.01
ADAM_STEP = 10
N_DEV = 8
MESH = pl.DeviceIdType.MESH
VMEM_LIMIT = 60 * 1024 * 1024


def _call(body, **kw):
    return pl.pallas_call(body, **kw)


def _cp(sem=None, vmem=VMEM_LIMIT):
    return pltpu.CompilerParams(dimension_semantics=sem, vmem_limit_bytes=vmem)


def _resident(shape):
    nd = len(shape)
    return pl.BlockSpec(shape, lambda *_: (0,) * nd, pipeline_mode=pl.Buffered(1))


def _dot(a, b):
    return jnp.dot(a, b, preferred_element_type=F32)


def _dot_nt(a, b):
    return lax.dot_general(a, b, (((1,), (1,)), ((), ())), preferred_element_type=F32)


def _dot_tn(a, b):
    return lax.dot_general(a, b, (((0,), (0,)), ((), ())), preferred_element_type=F32)


def _sigmoid(x):
    return 1.0 / (1.0 + jnp.exp(-x))


def _rms_bwd(x, r, g, dy):
    xh = x * r
    dyg = dy * g
    dx = r * (dyg - xh * jnp.mean(dyg * xh, axis=-1, keepdims=True))
    return dx, dy * xh


class _Comm:
    def __init__(self, ins, out_shapes, sems, start, mid, finish):
        self.ins, self.out_shapes, self.sems = list(ins), list(out_shapes), list(sems)
        self.start, self.mid, self.finish = start, mid, finish


_ANY = pl.BlockSpec(memory_space=pl.ANY)


def _launch(name, compute, grid, in_specs, out_specs, out_shape, scratch, operands, sem, comm=None):
    n_in, n_out, n_sc = len(in_specs), len(out_specs), len(scratch)
    if comm is None:
        res = _call(compute, name=name, grid=grid, in_specs=in_specs, out_specs=out_specs, out_shape=out_shape,
                    scratch_shapes=scratch, compiler_params=_cp(sem))(*operands)
        return list(res), []
    c_in, c_out = len(comm.ins), len(comm.out_shapes)
    inner = grid[1] if len(grid) > 1 else 1
    steps = grid[0] * inner
    mid_step = (3 * steps) // 4

    def body(*refs):
        ins = refs[:n_in]
        cins = refs[n_in:n_in + c_in]
        o = n_in + c_in
        outs = refs[o:o + n_out]
        couts = refs[o + n_out:o + n_out + c_out]
        o += n_out + c_out
        sc = refs[o:o + n_sc]
        csem = refs[o + n_sc:]
        i = pl.program_id(0)
        if len(grid) > 1:
            i = i * inner + pl.program_id(1)

        @pl.when(i == 0)
        def _():
            comm.start(cins, couts, csem)

        if comm.mid is not None:
            @pl.when(i == mid_step)
            def _():
                comm.mid(cins, couts, csem)

        compute(*ins, *outs, *sc)

        @pl.when(i == steps - 1)
        def _():
            comm.finish(cins, couts, csem)

    res = _call(body, name=name, grid=grid, in_specs=list(in_specs) + [_ANY] * c_in,
                out_specs=list(out_specs) + [_ANY] * c_out, out_shape=list(out_shape) + comm.out_shapes,
                scratch_shapes=list(scratch) + comm.sems,
                compiler_params=_cp(("arbitrary",) * len(grid)))(*operands, *comm.ins)
    return list(res[:n_out]), list(res[n_out:])


def _comm_only(comm, name):
    c_in, c_out = len(comm.ins), len(comm.out_shapes)

    def body(*refs):
        cins, couts, csem = refs[:c_in], refs[c_in:c_in + c_out], refs[c_in + c_out:]
        comm.start(cins, couts, csem)
        if comm.mid is not None:
            comm.mid(cins, couts, csem)
        comm.finish(cins, couts, csem)

    res = _call(body, name=name, in_specs=[_ANY] * c_in, out_specs=[_ANY] * c_out, out_shape=comm.out_shapes,
                scratch_shapes=comm.sems)(*comm.ins)
    return list(res)


def _wres(n, D, part):
    return pl.BlockSpec((N_DEV, n, D), lambda i: (0, part, 0), pipeline_mode=pl.Buffered(1))


def _phases(ext, ph, rows):
    for p in range(8):
        ph[p, :, :] = ext[p:p + rows, :]


def _f_in_conv(x, g, G, n_in, cw, cb, lg, lb, tm, comm=None):
    S, D = x.shape
    N = N_DEV * n_in
    CW = cw.shape[1]
    RB = 64

    def body(x_ref, g_ref, w_ref, cw_ref, cb_ref, lg_ref, lb_ref, h_ref, p_ref, c_ref, u_ref, ext, ph):
        i = pl.program_id(0)

        @pl.when(i == 0)
        def _():
            ext[0:HALO, :] = jnp.zeros((HALO, CW), F32)
            ext[HALO + tm:, :] = jnp.zeros((8, CW), F32)

        @pl.when(i > 0)
        def _():
            ext[0:HALO, :] = ext[tm:tm + HALO, :]

        xv = x_ref[...]
        r = lax.rsqrt(jnp.mean(xv * xv, axis=-1, keepdims=True) + EPS)
        h = ((xv * r) * g_ref[...]).astype(CDT)
        h_ref[...] = h
        w = w_ref[...].reshape(N, D)
        pab = _dot_nt(h, w[0:2 * CW])
        p_ref[:, 0:2 * CW] = pab
        p_ref[:, 2 * CW:] = _dot_nt(h, w[2 * CW:])
        ext[HALO:HALO + tm, :] = pab[:, :CW] * _sigmoid(pab[:, CW:])
        _phases(ext, ph, tm + HALO)
        for rb in range(tm // RB):
            acc = jnp.zeros((RB, CW), F32) + cb_ref[...]
            for k in range(CONV_K):
                o = rb * RB + HALO - (CONV_K - 1) + k
                acc = acc + cw_ref[k:k + 1, :] * ph[o % 8, o - o % 8:o - o % 8 + RB, :]
            c_ref[rb * RB:(rb + 1) * RB, :] = acc
            mu = jnp.mean(acc, axis=-1, keepdims=True)
            var = jnp.mean(jnp.square(acc - mu), axis=-1, keepdims=True)
            z = ((acc - mu) * lax.rsqrt(var + EPS)) * lg_ref[...] + lb_ref[...]
            u_ref[rb * RB:(rb + 1) * RB, :] = (z * _sigmoid(z)).astype(CDT)

    row = lambda n: pl.BlockSpec((tm, n), lambda i: (i, 0))
    return _launch(
        "f_in_conv" if comm is None else "f_in_conv_ag", body, (S // tm,),
        [row(D), _resident((1, D)), _wres(n_in, D, 0),
         _resident((HALO, CW)), _resident((1, CW)), _resident((1, CW)), _resident((1, CW))],
        [row(D), row(N), row(CW), row(CW)],
        [jax.ShapeDtypeStruct((S, D), CDT), jax.ShapeDtypeStruct((S, N), F32),
         jax.ShapeDtypeStruct((S, CW), F32), jax.ShapeDtypeStruct((S, CW), CDT)],
        [pltpu.VMEM((tm + HALO + 8, CW), F32), pltpu.VMEM((8, tm + HALO, CW), F32)],
        (x, g, G, cw, cb, lg, lb), ("arbitrary",), comm)


def _rot(t, c, s):
    return t * c + pltpu.roll(t, HD // 2, 1) * s


def _f_ret(proj, u, cosT, sinT, Mt, qd, kd, gs, gn):
    S = proj.shape[0]
    RW = HEADS * HD
    NB = S // SC
    scale = HD ** -0.5

    def body(q_ref, k_ref, v_ref, g_ref, u_ref, c_ref, s_ref, m_ref, qd_ref, kd_ref, gs_ref, gn_ref,
             rraw_ref, st_ref, mix_ref, state):
        @pl.when(pl.program_id(0) == 0)
        def _():
            state[...] = jnp.zeros_like(state)

        mix_ref[:, 0:RW] = u_ref[...]
        cv = c_ref[...]
        sv = s_ref[...]
        for h in range(HEADS):
            cs = slice(h * HD, (h + 1) * HD)
            q = _rot(q_ref[:, cs], cv, sv)
            k = _rot(k_ref[:, cs], cv, sv) * scale
            vb = v_ref[:, cs].astype(CDT)
            qb = q.astype(CDT)
            kb = k.astype(CDT)
            a = _dot_nt(qb, kb) * m_ref[h]
            sp = state[h]
            spb = sp.astype(CDT)
            st_ref[0, h] = spb
            r = _dot(a.astype(CDT), vb) + _dot((q * qd_ref[h]).astype(CDT), spb)
            kv = _dot_tn((k * kd_ref[h]).astype(CDT), vb)
            state[h] = gs_ref[h, 0:1, :] * sp + kv
            rraw_ref[:, cs] = r
            mu = jnp.mean(r, axis=-1, keepdims=True)
            var = jnp.mean(jnp.square(r - mu), axis=-1, keepdims=True)
            n = (r - mu) * lax.rsqrt(var + EPS)
            gv = g_ref[:, cs]
            mix_ref[:, RW + h * HD:RW + (h + 1) * HD] = ((n * gn_ref[:, cs]) * (gv * _sigmoid(gv))).astype(CDT)

    col = lambda j: pl.BlockSpec((SC, RW), lambda i: (i, j))
    return _call(
        body, name="f_ret", grid=(NB,),
        in_specs=[col(2), col(3), col(4), col(5),
                  pl.BlockSpec((SC, RW), lambda i: (i, 0)),
                  pl.BlockSpec((SC, HD), lambda i: (i, 0)), pl.BlockSpec((SC, HD), lambda i: (i, 0)),
                  _resident((HEADS, SC, SC)), _resident((HEADS, SC, HD)), _resident((HEADS, SC, HD)),
                  _resident((HEADS, 8, HD)), _resident((1, RW))],
        out_specs=[pl.BlockSpec((SC, RW), lambda i: (i, 0)),
                   pl.BlockSpec((1, HEADS, HD, HD), lambda i: (i, 0, 0, 0)),
                   pl.BlockSpec((SC, 2 * RW), lambda i: (i, 0))],
        out_shape=[jax.ShapeDtypeStruct((S, RW), F32),
                   jax.ShapeDtypeStruct((NB, HEADS, HD, HD), CDT),
                   jax.ShapeDtypeStruct((S, 2 * RW), CDT)],
        scratch_shapes=[pltpu.VMEM((HEADS, HD, HD), F32)],
        compiler_params=_cp(("arbitrary",)),
    )(proj, proj, proj, proj, u, cosT, sinT, Mt, qd, kd, gs, gn)


def _f_mlp(x, mixed, g2, G, n_out, n_ff, parts, tm, comm=None):
    S, D = x.shape
    FF = N_DEV * n_ff
    p_wo, p_wg, p_wu, p_wd = parts

    def body(x_ref, m_ref, wo_ref, g_ref, wg_ref, wu_ref, wd_ref,
             xm_ref, h2_ref, gate_ref, up_ref, act_ref, xo_ref):
        xm = x_ref[...] + _dot(m_ref[...], wo_ref[...].reshape(N_DEV * n_out, D))
        xm_ref[...] = xm
        r = lax.rsqrt(jnp.mean(xm * xm, axis=-1, keepdims=True) + EPS)
        h2 = ((xm * r) * g_ref[...]).astype(CDT)
        h2_ref[...] = h2
        gate = _dot_nt(h2, wg_ref[...].reshape(FF, D))
        up = _dot_nt(h2, wu_ref[...].reshape(FF, D))
        gate_ref[...] = gate.astype(CDT)
        up_ref[...] = up.astype(CDT)
        act = ((gate * _sigmoid(gate)) * up).astype(CDT)
        act_ref[...] = act
        xo_ref[...] = xm + _dot(act, wd_ref[...].reshape(FF, D))

    row = lambda n: pl.BlockSpec((tm, n), lambda i: (i, 0))
    return _launch(
        "f_mlp" if comm is None else "f_mlp_ag", body, (S // tm,),
        [row(D), row(D), _wres(n_out, D, p_wo), _resident((1, D)),
         _wres(n_ff, D, p_wg), _wres(n_ff, D, p_wu), _wres(n_ff, D, p_wd)],
        [row(D), row(D), row(FF), row(FF), row(FF), row(D)],
        [jax.ShapeDtypeStruct((S, D), F32), jax.ShapeDtypeStruct((S, D), CDT),
         jax.ShapeDtypeStruct((S, FF), CDT), jax.ShapeDtypeStruct((S, FF), CDT),
         jax.ShapeDtypeStruct((S, FF), CDT), jax.ShapeDtypeStruct((S, D), F32)],
        [], (x, mixed, G, g2, G, G, G), ("parallel",), comm)


def _f_loss(x, fg, tgt, tm):
    S, D = x.shape

    def body(x_ref, g_ref, t_ref, dx_ref, dxb_ref, loss_ref, dg_ref):
        @pl.when(pl.program_id(0) == 0)
        def _():
            loss_ref[...] = jnp.zeros_like(loss_ref)
            dg_ref[...] = jnp.zeros_like(dg_ref)

        xv = x_ref[...]
        r = lax.rsqrt(jnp.mean(xv * xv, axis=-1, keepdims=True) + EPS)
        y = (xv * r) * g_ref[...]
        e = y - t_ref[...]
        loss_ref[...] += 0.5 * jnp.sum(jnp.mean(e * e, axis=-1, keepdims=True))
        dy = e * (1.0 / D)
        dx, dgx = _rms_bwd(xv, r, g_ref[...], dy)
        dg_ref[...] += jnp.sum(dgx, axis=0, keepdims=True)
        dx_ref[...] = dx
        dxb_ref[...] = dx.astype(CDT)

    row = pl.BlockSpec((tm, D), lambda i: (i, 0))
    return _call(
        body, name="f_loss", grid=(S // tm,),
        in_specs=[row, _resident((1, D)), row],
        out_specs=[row, row, pl.BlockSpec((1, 128), lambda i: (0, 0)), pl.BlockSpec((1, D), lambda i: (0, 0))],
        out_shape=[jax.ShapeDtypeStruct((S, D), F32), jax.ShapeDtypeStruct((S, D), CDT),
                   jax.ShapeDtypeStruct((1, 128), F32), jax.ShapeDtypeStruct((1, D), F32)],
        compiler_params=_cp(("arbitrary",)),
    )(x, fg, tgt)


def _b_mlp(dxb, dx, xm, gate, up, g2, G, n_out, n_ff, parts, tm, comm=None):
    S, D = dx.shape
    FF = N_DEV * n_ff
    p_wo, p_wg, p_wu, p_wd = parts

    def body(dxb_ref, dx_ref, xm_ref, gate_ref, up_ref, g_ref, wd_ref, wg_ref, wu_ref, wo_ref,
             dgate_ref, dup_ref, dxm_ref, dxmb_ref, dmix_ref, dg_ref):
        @pl.when(pl.program_id(0) == 0)
        def _():
            dg_ref[...] = jnp.zeros_like(dg_ref)

        dact = _dot_nt(dxb_ref[...], wd_ref[...].reshape(FF, D))
        gate = gate_ref[...].astype(F32)
        up = up_ref[...].astype(F32)
        sg = _sigmoid(gate)
        sil = gate * sg
        dgate = ((dact * up) * (sg * (1.0 + gate * (1.0 - sg)))).astype(CDT)
        dup = (dact * sil).astype(CDT)
        dgate_ref[...] = dgate
        dup_ref[...] = dup
        dh2 = _dot(dgate, wg_ref[...].reshape(FF, D)) + _dot(dup, wu_ref[...].reshape(FF, D))
        xm = xm_ref[...]
        r = lax.rsqrt(jnp.mean(xm * xm, axis=-1, keepdims=True) + EPS)
        dxn, dgx = _rms_bwd(xm, r, g_ref[...], dh2)
        dg_ref[...] += jnp.sum(dgx, axis=0, keepdims=True)
        dxm = dx_ref[...] + dxn
        dxm_ref[...] = dxm
        dxmb = dxm.astype(CDT)
        dxmb_ref[...] = dxmb
        dmix_ref[...] = _dot_nt(dxmb, wo_ref[...].reshape(N_DEV * n_out, D))

    row = lambda n: pl.BlockSpec((tm, n), lambda i: (i, 0))
    return _launch(
        "b_mlp" if comm is None else "b_mlp_rs", body, (S // tm,),
        [row(D), row(D), row(D), row(FF), row(FF), _resident((1, D)),
         _wres(n_ff, D, p_wd), _wres(n_ff, D, p_wg), _wres(n_ff, D, p_wu), _wres(n_out, D, p_wo)],
        [row(FF), row(FF), row(D), row(D), row(D), pl.BlockSpec((1, D), lambda i: (0, 0))],
        [jax.ShapeDtypeStruct((S, FF), CDT), jax.ShapeDtypeStruct((S, FF), CDT),
         jax.ShapeDtypeStruct((S, D), F32), jax.ShapeDtypeStruct((S, D), CDT),
         jax.ShapeDtypeStruct((S, D), F32), jax.ShapeDtypeStruct((1, D), F32)],
        [], (dxb, dx, xm, gate, up, g2, G, G, G, G), ("arbitrary",), comm)


def _b_conv(dmix, c, proj, cw, lg, lb, tm, comm=None):
    S = proj.shape[0]
    CW = cw.shape[1]
    RB = 64
    hb = tm // HALO
    nt = S // tm
    last_h = S // HALO - 1

    def body(du_ref, duh_ref, c_ref, ch_ref, ab_ref, abh_ref, cw_ref, lg_ref, lb_ref,
             dab_ref, dcw_ref, dcb_ref, dlg_ref, dlb_ref, ext_u, ext_dc, ph_u, ph_dc, wacc):
        i = pl.program_id(0)

        @pl.when(i == 0)
        def _():
            wacc[...] = jnp.zeros_like(wacc)
            dcb_ref[...] = jnp.zeros_like(dcb_ref)
            dlg_ref[...] = jnp.zeros_like(dlg_ref)
            dlb_ref[...] = jnp.zeros_like(dlb_ref)

        def ln_bwd(cv, du):
            mu = jnp.mean(cv, axis=-1, keepdims=True)
            var = jnp.mean(jnp.square(cv - mu), axis=-1, keepdims=True)
            rstd = lax.rsqrt(var + EPS)
            n = (cv - mu) * rstd
            z = n * lg_ref[...] + lb_ref[...]
            sz = _sigmoid(z)
            dz = du * (sz * (1.0 + z * (1.0 - sz)))
            dn = dz * lg_ref[...]
            dc = rstd * (dn - jnp.mean(dn, axis=-1, keepdims=True)
                         - n * jnp.mean(dn * n, axis=-1, keepdims=True))
            return dc, dz, n

        hv = abh_ref[...]
        ext_u[0:HALO, :] = jnp.where(i > 0, hv[:, :CW] * _sigmoid(hv[:, CW:]), 0.0)
        av = ab_ref[...]
        sb = _sigmoid(av[:, CW:])
        ext_u[HALO:HALO + tm, :] = av[:, :CW] * sb
        ext_u[HALO + tm:, :] = jnp.zeros((8, CW), F32)
        dc, dz, n = ln_bwd(c_ref[...], du_ref[...])
        ext_dc[0:tm, :] = dc
        dch, _, _ = ln_bwd(ch_ref[...], duh_ref[...])
        ext_dc[tm:tm + HALO, :] = jnp.where(i < nt - 1, dch, 0.0)
        ext_dc[tm + HALO:, :] = jnp.zeros((8, CW), F32)
        dlg_ref[...] += jnp.sum(dz * n, axis=0, keepdims=True)
        dlb_ref[...] += jnp.sum(dz, axis=0, keepdims=True)
        dcb_ref[...] += jnp.sum(dc, axis=0, keepdims=True)
        _phases(ext_u, ph_u, tm + HALO)
        _phases(ext_dc, ph_dc, tm + HALO)

        for rb in range(tm // RB):
            rs = slice(rb * RB, (rb + 1) * RB)
            dcb = ext_dc[rs, :]
            for k in range(CONV_K):
                o = rb * RB + HALO - (CONV_K - 1) + k
                prod = dcb * ph_u[o % 8, o - o % 8:o - o % 8 + RB, :]
                part = prod[0:8]
                for j in range(1, RB // 8):
                    part = part + prod[8 * j:8 * j + 8]
                wacc[k] += part
            acc = jnp.zeros((RB, CW), F32)
            for k in range(CONV_K):
                o = rb * RB + (CONV_K - 1) - k
                acc = acc + cw_ref[k:k + 1, :] * ph_dc[o % 8, o - o % 8:o - o % 8 + RB, :]
            a_r = ab_ref[rs, 0:CW]
            s_r = _sigmoid(ab_ref[rs, CW:2 * CW])
            dab_ref[rs, 0:CW] = (acc * s_r).astype(CDT)
            dab_ref[rs, CW:2 * CW] = (acc * a_r * (s_r * (1.0 - s_r))).astype(CDT)

        @pl.when(i == nt - 1)
        def _():
            for k in range(CONV_K):
                dcw_ref[k:k + 1, :] = jnp.sum(wacc[k], axis=0, keepdims=True)
            dcw_ref[CONV_K:, :] = jnp.zeros((HALO - CONV_K, CW), F32)

    tile = lambda n, j: pl.BlockSpec((tm, n), lambda i: (i, j))
    nxt = lambda n, j: pl.BlockSpec((HALO, n), lambda i: (jnp.minimum((i + 1) * hb, last_h), j))
    return _launch(
        "b_conv" if comm is None else "b_conv_rs", body, (nt,),
        [tile(CW, 0), nxt(CW, 0), tile(CW, 0), nxt(CW, 0),
         tile(2 * CW, 0),
         pl.BlockSpec((HALO, 2 * CW), lambda i: (jnp.maximum(i * hb - 1, 0), 0)),
         _resident((HALO, CW)), _resident((1, CW)), _resident((1, CW))],
        [tile(2 * CW, 0),
         pl.BlockSpec((HALO, CW), lambda i: (0, 0)), pl.BlockSpec((1, CW), lambda i: (0, 0)),
         pl.BlockSpec((1, CW), lambda i: (0, 0)), pl.BlockSpec((1, CW), lambda i: (0, 0))],
        [jax.ShapeDtypeStruct((S, 2 * CW), CDT),
         jax.ShapeDtypeStruct((HALO, CW), F32), jax.ShapeDtypeStruct((1, CW), F32),
         jax.ShapeDtypeStruct((1, CW), F32), jax.ShapeDtypeStruct((1, CW), F32)],
        [pltpu.VMEM((tm + HALO + 8, CW), F32), pltpu.VMEM((tm + HALO + 8, CW), F32),
         pltpu.VMEM((8, tm + HALO, CW), F32), pltpu.VMEM((8, tm + HALO, CW), F32),
         pltpu.VMEM((HALO, 8, CW), F32)],
        (dmix, dmix, c, c, proj, proj, cw, lg, lb), ("arbitrary",), comm)


def _b_ret(dmix, dab, proj, rraw, states, cosT, sinT, Mt, qd, kd, gs, gn, comm=None):
    S = proj.shape[0]
    RW = HEADS * HD
    NB = S // SC
    scale = HD ** -0.5

    def body(dro_ref, dab_ref, q_ref, k_ref, v_ref, g_ref, rraw_ref, st_ref, c_ref, s_ref,
             m_ref, qd_ref, kd_ref, gs_ref, gn_ref, dp_ref, dgn_ref, G):
        @pl.when(pl.program_id(0) == 0)
        def _():
            G[...] = jnp.zeros_like(G)
            dgn_ref[...] = jnp.zeros_like(dgn_ref)

        dp_ref[:, 0:2 * RW] = dab_ref[...]
        cv = c_ref[...]
        sv = s_ref[...]
        for h in range(HEADS):
            cs = slice(h * HD, (h + 1) * HD)
            q = _rot(q_ref[:, cs], cv, sv)
            k = _rot(k_ref[:, cs], cv, sv) * scale
            qb = q.astype(CDT)
            kb = k.astype(CDT)
            vb = v_ref[:, cs].astype(CDT)
            spb = st_ref[0, h]
            r = rraw_ref[:, cs]
            mu = jnp.mean(r, axis=-1, keepdims=True)
            var = jnp.mean(jnp.square(r - mu), axis=-1, keepdims=True)
            rstd = lax.rsqrt(var + EPS)
            n = (r - mu) * rstd
            gv = g_ref[:, cs]
            sg = _sigmoid(gv)
            sil = gv * sg
            dro = dro_ref[:, cs]
            gnv = gn_ref[:, cs]
            dgn_ref[:, cs] += jnp.sum(dro * n * sil, axis=0, keepdims=True)
            dgate = dro * (n * gnv) * (sg * (1.0 + gv * (1.0 - sg)))
            dn = dro * gnv * sil
            dr = rstd * (dn - jnp.mean(dn, axis=-1, keepdims=True)
                         - n * jnp.mean(dn * n, axis=-1, keepdims=True))
            drb = dr.astype(CDT)
            mh = m_ref[h]
            ab = (_dot_nt(qb, kb) * mh).astype(CDT)
            dab_ = (_dot_nt(drb, vb) * mh).astype(CDT)
            qdb = (q * qd_ref[h]).astype(CDT)
            kdb = (k * kd_ref[h]).astype(CDT)
            gc = G[h]
            gb = gc.astype(CDT)
            dq = _dot(dab_, kb) + _dot_nt(drb, spb) * qd_ref[h]
            dk = _dot_tn(dab_, qb) + _dot_nt(vb, gb) * kd_ref[h]
            dv = _dot_tn(ab, drb) + _dot(kdb, gb)
            G[h] = gs_ref[h, 0:1, :] * gc + _dot_tn(qdb, drb)
            dk = dk * scale
            dqp = dq * cv + pltpu.roll(dq * sv, HD // 2, 1)
            dkp = dk * cv + pltpu.roll(dk * sv, HD // 2, 1)
            base = 2 * RW
            dp_ref[:, base + h * HD:base + (h + 1) * HD] = dqp.astype(CDT)
            dp_ref[:, base + RW + h * HD:base + RW + (h + 1) * HD] = dkp.astype(CDT)
            dp_ref[:, base + 2 * RW + h * HD:base + 2 * RW + (h + 1) * HD] = dv.astype(CDT)
            dp_ref[:, base + 3 * RW + h * HD:base + 3 * RW + (h + 1) * HD] = dgate.astype(CDT)

    rev = lambda n, j: pl.BlockSpec((SC, n), lambda i: (NB - 1 - i, j))
    return _launch(
        "b_ret" if comm is None else "b_ret_rs", body, (NB,),
        [rev(RW, 1), rev(2 * RW, 0), rev(RW, 2), rev(RW, 3), rev(RW, 4), rev(RW, 5), rev(RW, 0),
         pl.BlockSpec((1, HEADS, HD, HD), lambda i: (NB - 1 - i, 0, 0, 0)),
         rev(HD, 0), rev(HD, 0),
         _resident((HEADS, SC, SC)), _resident((HEADS, SC, HD)), _resident((HEADS, SC, HD)),
         _resident((HEADS, 8, HD)), _resident((1, RW))],
        [rev(6 * RW, 0), pl.BlockSpec((1, RW), lambda i: (0, 0))],
        [jax.ShapeDtypeStruct((S, 6 * RW), CDT), jax.ShapeDtypeStruct((1, RW), F32)],
        [pltpu.VMEM((HEADS, HD, HD), F32)],
        (dmix, dab, proj, proj, proj, proj, rraw, states, cosT, sinT, Mt, qd, kd, gs, gn), ("arbitrary",), comm)


def _b_in(dproj, G, n_in, x, g1, dxm, tm, comm=None):
    S, D = x.shape
    N = N_DEV * n_in

    def body(dp_ref, w_ref, x_ref, g_ref, dxm_ref, dx_ref, dxb_ref, dg_ref):
        @pl.when(pl.program_id(0) == 0)
        def _():
            dg_ref[...] = jnp.zeros_like(dg_ref)

        dh = _dot(dp_ref[...], w_ref[...].reshape(N, D))
        xv = x_ref[...]
        r = lax.rsqrt(jnp.mean(xv * xv, axis=-1, keepdims=True) + EPS)
        dxn, dgx = _rms_bwd(xv, r, g_ref[...], dh)
        dg_ref[...] += jnp.sum(dgx, axis=0, keepdims=True)
        dx = dxm_ref[...] + dxn
        dx_ref[...] = dx
        dxb_ref[...] = dx.astype(CDT)

    row = lambda n: pl.BlockSpec((tm, n), lambda i: (i, 0))
    return _launch(
        "b_in" if comm is None else "b_in_rs", body, (S // tm,),
        [row(N), _wres(n_in, D, 0), row(D), _resident((1, D)), row(D)],
        [row(D), row(D), pl.BlockSpec((1, D), lambda i: (0, 0))],
        [jax.ShapeDtypeStruct((S, D), F32), jax.ShapeDtypeStruct((S, D), CDT),
         jax.ShapeDtypeStruct((1, D), F32)],
        [], (dproj, G, x, g1, dxm), ("arbitrary",), comm)


def _dw_tn(a, b, tm, tk):
    S, M = a.shape
    N = b.shape[1]
    nk = S // tk

    def body(a_ref, b_ref, o_ref, ob_ref):
        k = pl.program_id(1)

        @pl.when(k == 0)
        def _():
            o_ref[...] = jnp.zeros_like(o_ref)

        o_ref[...] += _dot_tn(a_ref[...], b_ref[...])

        @pl.when(k == nk - 1)
        def _():
            ob_ref[...] = o_ref[...].astype(CDT)

    out = pl.BlockSpec((tm, N), lambda m, k: (m, 0))
    return _call(
        body, name="dw_tn", grid=(M // tm, nk),
        in_specs=[pl.BlockSpec((tk, tm), lambda m, k: (k, m)), pl.BlockSpec((tk, N), lambda m, k: (k, 0))],
        out_specs=[out, out],
        out_shape=[jax.ShapeDtypeStruct((M, N), F32), jax.ShapeDtypeStruct((M, N), CDT)],
        compiler_params=_cp(("parallel", "arbitrary")),
    )(a, b)


def _add_pair(parts, recv):
    K = len(parts)
    C = parts[0].shape[2]
    halves = 2

    def body(*refs):
        cc = lax.axis_index("c")
        chip = 2 * lax.axis_index("x") + lax.axis_index("y")
        q = pl.program_id(1)
        for k in range(K):
            s = refs[k][cc] + refs[K + k][...]
            refs[3 * K + k][...] = s.astype(CDT)

            @pl.when(q == chip)
            def _():
                refs[2 * K + k][...] = s

    ns = [p.shape[1] for p in parts]
    in_specs = [pl.BlockSpec((None, 2, n // halves, C), lambda r, q: (q, 0, r, 0)) for n in ns]
    in_specs += [pl.BlockSpec((None, n // halves, C), lambda r, q: (q, r, 0)) for n in ns]
    out_f = [pl.BlockSpec((n // halves, C), lambda r, q: (r, 0)) for n in ns]
    out_b = [pl.BlockSpec((None, n // halves, C), lambda r, q: (q, r, 0)) for n in ns]
    res = _call(
        body, name="add_pair", grid=(halves, N_DEV // 2),
        in_specs=in_specs, out_specs=out_f + out_b,
        out_shape=[jax.ShapeDtypeStruct((n, C), F32) for n in ns]
        + [jax.ShapeDtypeStruct((N_DEV // 2, n, C), CDT) for n in ns],
        compiler_params=_cp(("parallel", "arbitrary")),
    )(*[p.reshape(N_DEV // 2, 2, p.shape[1], C) for p in parts], *recv)
    return list(res[:K]), list(res[K:])


def _sum_chips(sums, recv):
    K = len(sums)

    def body(*refs):
        for k in range(K):
            r = refs[K + k]
            refs[2 * K + k][...] = ((refs[k][...] + r[0].astype(F32)) + r[1].astype(F32)) + r[2].astype(F32)

    vm = pl.BlockSpec(memory_space=pltpu.VMEM)
    res = _call(
        body, name="sum_chips", in_specs=[vm] * (2 * K), out_specs=[vm] * K,
        out_shape=[jax.ShapeDtypeStruct(s.shape, F32) for s in sums],
        compiler_params=_cp(),
    )(*sums, *recv)
    return list(res)


def _adamw(w, g, m, v, tr):
    R, C = w.shape
    c1 = 1.0 - ADAM_B1 ** ADAM_STEP
    c2 = 1.0 - ADAM_B2 ** ADAM_STEP

    def body(w_ref, g_ref, m_ref, v_ref, d_ref, mo_ref, vo_ref):
        gv = g_ref[...]
        mn = ADAM_B1 * m_ref[...] + (1.0 - ADAM_B1) * gv
        vn = ADAM_B2 * v_ref[...] + (1.0 - ADAM_B2) * jnp.square(gv)
        mo_ref[...] = mn
        vo_ref[...] = vn
        d_ref[...] = -ADAM_LR * ((mn / c1) / (jnp.sqrt(vn / c2) + ADAM_EPS) + ADAM_WD * w_ref[...])

    blk = pl.BlockSpec((tr, C), lambda i: (i, 0))
    sh = jax.ShapeDtypeStruct((R, C), F32)
    return _call(
        body, name="adamw", grid=(R // tr,),
        in_specs=[blk, blk, blk, blk], out_specs=[blk, blk, blk], out_shape=[sh, sh, sh],
        compiler_params=_cp(("parallel",)),
    )(w, g, m, v)


def _coords():
    return lax.axis_index("x"), lax.axis_index("y"), lax.axis_index("c")


def _peer(x, y, c, d):
    return (x ^ (d >> 2), y ^ ((d >> 1) & 1), c ^ (d & 1))


def _ag_comm(ps):
    K = len(ps)

    def plan(cins, couts, sems):
        send_sems, recv_sems, local_sems = sems
        x, y, c = _coords()
        me, sibling = (x, y, c), (x, y, 1 - c)
        chips = [(1 - x, y), (x, 1 - y), (1 - x, 1 - y)]
        mine, first, passed, got_ici, got_d2d = [], [], [], [], []
        for a in range(K):
            x_ref, out_ref = cins[a], couts[a]
            R = x_ref.shape[0]

            def rows(px, py, pc, out_ref=out_ref, R=R):
                return out_ref.at[pl.ds((4 * px + 2 * py + pc) * R, R), :]

            def copy(k, block, to, src=None, rows=rows, a=a):
                return pltpu.make_async_remote_copy(
                    src_ref=rows(*block) if src is None else src, dst_ref=rows(*block),
                    send_sem=send_sems.at[7 * a + k], recv_sem=recv_sems.at[7 * a + k],
                    device_id=to, device_id_type=MESH)

            mine.append(pltpu.make_async_copy(x_ref, rows(*me), local_sems.at[a]))
            first.append(copy(0, me, sibling, src=x_ref))
            first += [copy(1 + j, me, (*chip, c), src=x_ref) for j, chip in enumerate(chips)]
            passed += [copy(4 + j, (*chip, c), sibling) for j, chip in enumerate(chips)]
            got_ici += [copy(1 + j, (*chip, c), me) for j, chip in enumerate(chips)]
            got_d2d.append(copy(0, sibling, me))
            got_d2d += [copy(4 + j, (*chip, 1 - c), me) for j, chip in enumerate(chips)]
        return mine, first, passed, got_ici, got_d2d

    def start(*a):
        mine, first, _, _, _ = plan(*a)
        for cp in mine + first:
            cp.start()

    def mid(*a):
        _, _, passed, got_ici, _ = plan(*a)
        for got, fwd in zip(got_ici, passed):
            got.wait_recv()
            fwd.start()

    def finish(*a):
        mine, first, passed, _, got_d2d = plan(*a)
        for got in got_d2d:
            got.wait_recv()
        for cp in first + passed:
            cp.wait_send()
        for cp in mine:
            cp.wait()

    return _Comm(ps, [jax.ShapeDtypeStruct((N_DEV * p.shape[0], p.shape[1]), p.dtype) for p in ps],
                 [pltpu.SemaphoreType.DMA((7 * K,)), pltpu.SemaphoreType.DMA((7 * K,)),
                  pltpu.SemaphoreType.DMA((K,))], start, mid, finish)


def _rs_pair_comm(parts):
    K = len(parts)

    def plan(cins, couts, sems):
        send_sems, recv_sems = sems
        x, y, c = _coords()
        sibling = (x, y, 1 - c)
        cps = []
        for k in range(K):
            for q in range(N_DEV // 2):
                cps.append(pltpu.make_async_remote_copy(
                    src_ref=cins[k].at[2 * q + (1 - c)], dst_ref=couts[k].at[q],
                    send_sem=send_sems.at[k], recv_sem=recv_sems.at[k], device_id=sibling, device_id_type=MESH))
        whole = [pltpu.make_async_remote_copy(
            src_ref=couts[k], dst_ref=couts[k], send_sem=send_sems.at[k], recv_sem=recv_sems.at[k],
            device_id=sibling, device_id_type=MESH) for k in range(K)]
        return cps, whole

    def start(*a):
        for cp in plan(*a)[0]:
            cp.start()

    def finish(*a):
        for w in plan(*a)[1]:
            w.wait_recv()
            w.wait_send()

    return _Comm(parts, [jax.ShapeDtypeStruct((N_DEV // 2,) + p.shape[1:], p.dtype) for p in parts],
                 [pltpu.SemaphoreType.DMA((K,)), pltpu.SemaphoreType.DMA((K,))], start, None, finish)


def _rs_chip_comm(sums):
    K = len(sums)

    def plan(cins, couts, sems):
        send_sems, recv_sems = sems
        x, y, c = _coords()
        cps = []
        for d in range(1, N_DEV // 2):
            px, py = x ^ (d >> 1), y ^ (d & 1)
            for k in range(K):
                s = (d - 1) * K + k
                cps.append(pltpu.make_async_remote_copy(
                    src_ref=cins[k].at[2 * px + py], dst_ref=couts[k].at[d - 1],
                    send_sem=send_sems.at[s], recv_sem=recv_sems.at[s], device_id=(px, py, c), device_id_type=MESH))
        return cps

    def start(*a):
        for cp in plan(*a):
            cp.start()

    def finish(*a):
        cps = plan(*a)
        for cp in cps:
            cp.wait_recv()
        for cp in cps:
            cp.wait_send()

    n_sem = (N_DEV // 2 - 1) * K
    return _Comm(sums, [jax.ShapeDtypeStruct((N_DEV // 2 - 1,) + s.shape[1:], s.dtype) for s in sums],
                 [pltpu.SemaphoreType.DMA((n_sem,)), pltpu.SemaphoreType.DMA((n_sem,))], start, None, finish)


def _rs_direct_comm(parts):
    K = len(parts)

    def plan(cins, couts, sems):
        send_sems, recv_sems = sems
        x, y, c = _coords()
        cps = []
        for d in range(1, N_DEV):
            px, py, pc = _peer(x, y, c, d)
            for k in range(K):
                s = (d - 1) * K + k
                cps.append(pltpu.make_async_remote_copy(
                    src_ref=cins[k].at[4 * px + 2 * py + pc], dst_ref=couts[k].at[d - 1],
                    send_sem=send_sems.at[s], recv_sem=recv_sems.at[s], device_id=(px, py, pc),
                    device_id_type=MESH))
        return cps

    def start(*a):
        for cp in plan(*a):
            cp.start()

    def finish(*a):
        cps = plan(*a)
        for cp in cps:
            cp.wait_recv()
        for cp in cps:
            cp.wait_send()

    n_sem = (N_DEV - 1) * K
    return _Comm(parts, [jax.ShapeDtypeStruct((N_DEV - 1,) + p.shape[1:], p.dtype) for p in parts],
                 [pltpu.SemaphoreType.DMA((n_sem,)), pltpu.SemaphoreType.DMA((n_sem,))], start, None, finish)


def _sum_all(parts, recv):
    K = len(parts)

    def body(*refs):
        ins, rcv, outs = refs[:K], refs[K:2 * K], refs[2 * K:3 * K]
        bufs, sem = refs[3 * K:4 * K], refs[4 * K]
        x, y, c = _coords()
        me = 4 * x + 2 * y + c
        cps = [pltpu.make_async_copy(ins[k].at[me], bufs[k], sem.at[k]) for k in range(K)]
        for cp in cps:
            cp.start()
        for k in range(K):
            cps[k].wait()
            acc = bufs[k][...]
            for d in range(N_DEV - 1):
                acc = acc + rcv[k][d].astype(F32)
            outs[k][...] = acc

    vm = pl.BlockSpec(memory_space=pltpu.VMEM)
    res = _call(
        body, name="sum_all", in_specs=[_ANY] * K + [vm] * K, out_specs=[vm] * K,
        out_shape=[jax.ShapeDtypeStruct(p.shape[1:], F32) for p in parts],
        scratch_shapes=[pltpu.VMEM(p.shape[1:], F32) for p in parts] + [pltpu.SemaphoreType.DMA((K,))],
        compiler_params=_cp(),
    )(*parts, *recv)
    return list(res)


def _gather_small(v, reduce):
    R, C = v.shape

    def exchange(v_ref, buf, send_sems, recv_sems):
        x, y, c = _coords()
        me = 4 * x + 2 * y + c
        buf[me] = v_ref[...]
        cps = []
        for d in range(1, N_DEV):
            cp = pltpu.make_async_remote_copy(
                src_ref=v_ref, dst_ref=buf.at[me], send_sem=send_sems.at[d - 1], recv_sem=recv_sems.at[d - 1],
                device_id=_peer(x, y, c, d), device_id_type=MESH)
            cp.start()
            cps.append(cp)
        for cp in cps:
            cp.wait_recv()
        for cp in cps:
            cp.wait_send()

    sems = [pltpu.SemaphoreType.DMA((7,)), pltpu.SemaphoreType.DMA((7,))]
    vm = pl.BlockSpec(memory_space=pltpu.VMEM)
    if reduce:
        def body(v_ref, o_ref, buf, send_sems, recv_sems):
            exchange(v_ref, buf, send_sems, recv_sems)
            acc = buf[0]
            for s in range(1, N_DEV):
                acc = acc + buf[s]
            o_ref[...] = acc

        return _call(body, name="allreduce_small", in_specs=[vm], out_specs=vm,
                     out_shape=jax.ShapeDtypeStruct((R, C), F32),
                     scratch_shapes=[pltpu.VMEM((N_DEV, R, C), F32)] + sems)(v)

    def body(v_ref, o_ref, send_sems, recv_sems):
        exchange(v_ref, o_ref, send_sems, recv_sems)

    return _call(body, name="allgather_small", in_specs=[vm], out_specs=vm,
                 out_shape=jax.ShapeDtypeStruct((N_DEV, R, C), F32), scratch_shapes=sems)(v)


def _tables(S):
    half = HD // 2
    pos = jnp.arange(S, dtype=F32)
    freqs = ROPE_BASE ** (-jnp.arange(half, dtype=F32) / half)
    ang = pos[:, None] * freqs[None, :]
    cos, sin = jnp.cos(ang), jnp.sin(ang)
    cosT = jnp.concatenate([cos, cos], axis=-1)
    sinT = jnp.concatenate([-sin, sin], axis=-1)
    log_g = jnp.log(1.0 - 2.0 ** (-5.0 - jnp.arange(HEADS, dtype=F32)))
    idx = jnp.arange(SC, dtype=F32)
    ci = jnp.arange(SC) // CHUNK
    diff = idx[:, None] - idx[None, :]
    same = ci[:, None] == ci[None, :]
    earlier = ci[None, :] < ci[:, None]
    expo = jnp.where(same, jnp.abs(diff), diff)
    Mt = jnp.where((same | earlier)[None], jnp.exp(log_g[:, None, None] * expo[None]), 0.0)
    ones = jnp.ones((1, 1, HD), F32)
    qd = jnp.exp(log_g[:, None] * (idx + 1.0)[None, :])[:, :, None] * ones
    kd = jnp.exp(log_g[:, None] * (SC - 1.0 - idx)[None, :])[:, :, None] * ones
    gs = jnp.exp(log_g * SC)[:, None, None] * jnp.ones((1, 8, HD), F32)
    return cosT, sinT, Mt, qd, kd, gs


def _pad_rows(a, rows):
    return jnp.pad(a, ((0, rows - a.shape[0]), (0, 0)))


def kernel(x, norm1_g, w_in, conv_w, conv_b, conv_ln_g, conv_ln_b, ret_gn_g, w_out, norm2_g, w_gate, w_up, w_down, final_g, loss_target, m_norm1_g, m_w_in, m_conv_w, m_conv_b, m_conv_ln_g, m_conv_ln_b, m_ret_gn_g, m_w_out, m_norm2_g, m_w_gate, m_w_up, m_w_down, m_final_g, v_norm1_g, v_w_in, v_conv_w, v_conv_b, v_conv_ln_g, v_conv_ln_b, v_ret_gn_g, v_w_out, v_norm2_g, v_w_gate, v_w_up, v_w_down, v_final_g):
    L, D, n_in = w_in.shape
    n_out = w_out.shape[1]
    n_ff = w_gate.shape[2]
    S = x.shape[1]
    CW = conv_b.shape[1]
    IN, FF = N_DEV * n_in, N_DEV * n_ff
    ncw = conv_w.shape[2]
    x0 = x.reshape(S, D)
    tgt = loss_target.reshape(S, D)
    TM = min(512, S)
    TKW = min(1024, S)
    TMI = min(512, S)
    TMM = min(256, S)

    assert n_out <= n_ff
    RB_ = 4 * n_ff
    wparts = (0, 1, 2, 3)
    pack_a = jnp.swapaxes(w_in, 1, 2).astype(CDT)
    pack_b = jnp.concatenate([w_out, jnp.zeros((L, n_ff - n_out, D), F32), jnp.swapaxes(w_gate, 1, 2),
                              jnp.swapaxes(w_up, 1, 2), w_down], axis=1).astype(CDT)
    Ga = _comm_only(_ag_comm([pack_a[0]]), "ag_first")[0].reshape(N_DEV, n_in, D)
    Gb = None

    cwp = conv_w.reshape(L * CONV_K * ncw // 128, 128)
    cw_rows = -(-cwp.shape[0] // 8) * 8
    cwg = _gather_small(_pad_rows(cwp, cw_rows), reduce=False)[:, :cwp.shape[0], :]
    conv_w_full = jnp.moveaxis(cwg.reshape(N_DEV, L, CONV_K, ncw), 0, 2).reshape(L, CONV_K, CW)

    cosT, sinT, Mt, qd, kd, gs = _tables(S)

    saved = []
    xl = x0
    for l in range(L):
        cw = _pad_rows(conv_w_full[l], HALO)
        (h, proj, c, u), got = _f_in_conv(xl, norm1_g[l][None], Ga, n_in, cw, conv_b[l][None], conv_ln_g[l][None],
                                          conv_ln_b[l][None], TMI, _ag_comm([pack_b[0]]) if l == 0 else None)
        if got:
            Gb = got[0].reshape(N_DEV, RB_, D)
        rraw, states, mixed = _f_ret(proj, u, cosT, sinT, Mt, qd, kd, gs, ret_gn_g[l][None])
        (xm, h2, gate, up, act, xo), nxt = _f_mlp(
            xl, mixed, norm2_g[l][None], Gb, n_out, n_ff, wparts, TMM,
            _ag_comm([pack_a[l + 1], pack_b[l + 1]]) if l + 1 < L else None)
        saved.append(dict(x=xl, h=h, proj=proj, c=c, rraw=rraw, states=states, mixed=mixed, xm=xm, h2=h2,
                          gate=gate, up=up, act=act, cw=cw, Ga=Ga, Gb=Gb))
        if nxt:
            Ga, Gb = nxt[0].reshape(N_DEV, n_in, D), nxt[1].reshape(N_DEV, RB_, D)
        xl = xo

    dx, dxb, loss_p, dfg = _f_loss(xl, final_g[None], tgt, TM)

    small = []
    own = [None] * L
    recv = [None] * L
    blocks = lambda d: d.reshape(N_DEV, -1, D)
    for l in reversed(range(L)):
        sv = saved[l]
        (dgate, dup, dxm, dxmb, dmix, dg2), _ = _b_mlp(
            dxb, dx, sv["xm"], sv["gate"], sv["up"], norm2_g[l][None], sv["Gb"], n_out, n_ff, wparts, TMM)
        d_wd, d_wd_b = _dw_tn(sv["act"], dxb, FF // 2, TKW)
        d_wgT, d_wgT_b = _dw_tn(dgate, sv["h2"], FF // 2, TKW)
        d_wuT, d_wuT_b = _dw_tn(dup, sv["h2"], FF // 2, TKW)
        d_wo, d_wo_b = _dw_tn(sv["mixed"], dxmb, D, TKW)
        (dab, dcw, dcb, dlg, dlb), r_go = _b_conv(dmix, sv["c"], sv["proj"], sv["cw"], conv_ln_g[l][None],
                                                  conv_ln_b[l][None], TM,
                                                  _rs_direct_comm([blocks(d_wgT_b), blocks(d_wo_b)]))
        (dproj, dgn), r_ud = _b_ret(dmix, dab, sv["proj"], sv["rraw"], sv["states"], cosT, sinT, Mt, qd, kd, gs,
                                    ret_gn_g[l][None], _rs_direct_comm([blocks(d_wuT_b), blocks(d_wd_b)]))
        d_winT, d_winT_b = _dw_tn(dproj, sv["h"], IN // 2, TKW)
        (dx, dxb, dg1), r_i = _b_in(dproj, sv["Ga"], n_in, sv["x"], norm1_g[l][None], dxm, TMI,
                                    _rs_direct_comm([blocks(d_winT_b)]))
        own[l] = [blocks(d) for d in (d_winT, d_wo, d_wgT, d_wuT, d_wd)]
        recv[l] = [r_i[0], r_go[1], r_go[0], r_ud[0], r_ud[1]]
        small.append(jnp.concatenate([dcw, dcb, dlg, dlb, dgn, dg1.reshape(2, CW), dg2.reshape(2, CW)], axis=0))
    small = small[::-1]
    grad_x = dx.reshape(1, S, D)

    rows_l = HALO + 8
    loss_row = jnp.zeros((1, CW), F32).at[0, 0].set(loss_p[0, 0])
    sm = jnp.concatenate(small + [dfg.reshape(2, CW), loss_row], axis=0)
    sm_rows = -(-sm.shape[0] // 8) * 8
    sm = _gather_small(_pad_rows(sm, sm_rows), reduce=True)
    loss = sm[L * rows_l + 2, 0]
    g_final = sm[L * rows_l:L * rows_l + 2].reshape(D)
    per = sm[:L * rows_l].reshape(L, rows_l, CW)
    me = 4 * lax.axis_index("x") + 2 * lax.axis_index("y") + lax.axis_index("c")
    g_conv_w = lax.dynamic_slice_in_dim(per[:, :CONV_K, :], me * ncw, ncw, axis=2)
    g_conv_b, g_ln_g, g_ln_b, g_gn = per[:, HALO], per[:, HALO + 1], per[:, HALO + 2], per[:, HALO + 3]
    g_n1 = per[:, HALO + 4:HALO + 6].reshape(L, D)
    g_n2 = per[:, HALO + 6:HALO + 8].reshape(L, D)

    gl = [_sum_all(own[l], recv[l]) for l in range(L)]
    g_w_in = jnp.stack([gl[l][0].T for l in range(L)])
    g_w_out = jnp.stack([gl[l][1] for l in range(L)])
    g_w_gate = jnp.stack([gl[l][2].T for l in range(L)])
    g_w_up = jnp.stack([gl[l][3].T for l in range(L)])
    g_w_down = jnp.stack([gl[l][4] for l in range(L)])

    def big(w, g, m, v):
        sh = w.shape
        two = lambda a: a.reshape(-1, sh[-1])
        rows = two(w).shape[0]
        d, mn, vn = _adamw(two(w), two(g), two(m), two(v), rows // 8)
        return d.reshape(sh), mn.reshape(sh), vn.reshape(sh)

    names = ["norm1_g", "conv_w", "conv_b", "conv_ln_g", "conv_ln_b", "ret_gn_g", "norm2_g", "final_g"]
    sw = dict(norm1_g=(norm1_g, g_n1, m_norm1_g, v_norm1_g), conv_w=(conv_w, g_conv_w, m_conv_w, v_conv_w),
              conv_b=(conv_b, g_conv_b, m_conv_b, v_conv_b), conv_ln_g=(conv_ln_g, g_ln_g, m_conv_ln_g, v_conv_ln_g),
              conv_ln_b=(conv_ln_b, g_ln_b, m_conv_ln_b, v_conv_ln_b), ret_gn_g=(ret_gn_g, g_gn, m_ret_gn_g, v_ret_gn_g),
              norm2_g=(norm2_g, g_n2, m_norm2_g, v_norm2_g), final_g=(final_g, g_final, m_final_g, v_final_g))
    lens = [int(math.prod(sw[n][0].shape)) for n in names]
    tot = sum(lens)
    prow = -(-tot // (8 * CW)) * 8

    def packs(j):
        flat = jnp.concatenate([sw[n][j].reshape(-1) for n in names])
        return jnp.pad(flat, (0, prow * CW - tot)).reshape(prow, CW)

    sd, smn, svn = _adamw(packs(0), packs(1), packs(2), packs(3), prow)

    def unpack(a):
        flat = a.reshape(-1)
        out, o = {}, 0
        for n, ln in zip(names, lens):
            out[n] = flat[o:o + ln].reshape(sw[n][0].shape)
            o += ln
        return out

    sd, smn, svn = unpack(sd), unpack(smn), unpack(svn)
    res = {n: (sw[n][1], sd[n], smn[n], svn[n]) for n in names}
    res["w_in"] = (g_w_in,) + big(w_in, g_w_in, m_w_in, v_w_in)
    res["w_out"] = (g_w_out,) + big(w_out, g_w_out, m_w_out, v_w_out)
    res["w_gate"] = (g_w_gate,) + big(w_gate, g_w_gate, m_w_gate, v_w_gate)
    res["w_up"] = (g_w_up,) + big(w_up, g_w_up, m_w_up, v_w_up)
    res["w_down"] = (g_w_down,) + big(w_down, g_w_down, m_w_down, v_w_down)

    order = ["norm1_g", "w_in", "conv_w", "conv_b", "conv_ln_g", "conv_ln_b", "ret_gn_g", "w_out", "norm2_g",
             "w_gate", "w_up", "w_down", "final_g"]
    return (loss, grad_x, *[res[n][0] for n in order], *[res[n][1] for n in order],
            *[res[n][2] for n in order], *[res[n][3] for n in order])
```

```python
import math

import jax
import jax.numpy as jnp
from jax import lax
from jax.experimental import pallas as pl
from jax.experimental.pallas import tpu as pltpu

F32 = jnp.float32
CDT = jnp.bfloat16
EPS = 1e-6
CHUNK = 64
SC = 256
HEADS = 4
HD = 128
CONV_K = 31
HALO = 32
ROPE_BASE = 10000.0
ADAM_LR = 0.001
ADAM_B1 = 0.9
ADAM_B2 = 0.999
ADAM_EPS = 1e-08
ADAM_WD = 0.01
ADAM_STEP = 10
N_DEV = 8
MESH = pl.DeviceIdType.MESH
VMEM_LIMIT = 60 * 1024 * 1024


def _call(body, **kw):
    return pl.pallas_call(body, **kw)


def _cp(sem=None, vmem=VMEM_LIMIT):
    return pltpu.CompilerParams(dimension_semantics=sem, vmem_limit_bytes=vmem)


def _resident(shape):
    nd = len(shape)
    return pl.BlockSpec(shape, lambda *_: (0,) * nd, pipeline_mode=pl.Buffered(1))


def _dot(a, b):
    return jnp.dot(a, b, preferred_element_type=F32)


def _dot_nt(a, b):
    return lax.dot_general(a, b, (((1,), (1,)), ((), ())), preferred_element_type=F32)


def _dot_tn(a, b):
    return lax.dot_general(a, b, (((0,), (0,)), ((), ())), preferred_element_type=F32)


def _sigmoid(x):
    return 1.0 / (1.0 + jnp.exp(-x))


def _rms_bwd(x, r, g, dy):
    xh = x * r
    dyg = dy * g
    dx = r * (dyg - xh * jnp.mean(dyg * xh, axis=-1, keepdims=True))
    return dx, dy * xh


class _Comm:
    def __init__(self, ins, out_shapes, sems, start, mid, finish):
        self.ins, self.out_shapes, self.sems = list(ins), list(out_shapes), list(sems)
        self.start, self.mid, self.finish = start, mid, finish


_ANY = pl.BlockSpec(memory_space=pl.ANY)


def _launch(name, compute, grid, in_specs, out_specs, out_shape, scratch, operands, sem, comm=None):
    n_in, n_out, n_sc = len(in_specs), len(out_specs), len(scratch)
    if comm is None:
        res = _call(compute, name=name, grid=grid, in_specs=in_specs, out_specs=out_specs, out_shape=out_shape,
                    scratch_shapes=scratch, compiler_params=_cp(sem))(*operands)
        return list(res), []
    c_in, c_out = len(comm.ins), len(comm.out_shapes)
    inner = grid[1] if len(grid) > 1 else 1
    steps = grid[0] * inner
    mid_step = (3 * steps) // 4

    def body(*refs):
        ins = refs[:n_in]
        cins = refs[n_in:n_in + c_in]
        o = n_in + c_in
        outs = refs[o:o + n_out]
        couts = refs[o + n_out:o + n_out + c_out]
        o += n_out + c_out
        sc = refs[o:o + n_sc]
        csem = refs[o + n_sc:]
        i = pl.program_id(0)
        if len(grid) > 1:
            i = i * inner + pl.program_id(1)

        @pl.when(i == 0)
        def _():
            comm.start(cins, couts, csem)

        if comm.mid is not None:
            @pl.when(i == mid_step)
            def _():
                comm.mid(cins, couts, csem)

        compute(*ins, *outs, *sc)

        @pl.when(i == steps - 1)
        def _():
            comm.finish(cins, couts, csem)

    res = _call(body, name=name, grid=grid, in_specs=list(in_specs) + [_ANY] * c_in,
                out_specs=list(out_specs) + [_ANY] * c_out, out_shape=list(out_shape) + comm.out_shapes,
                scratch_shapes=list(scratch) + comm.sems,
                compiler_params=_cp(("arbitrary",) * len(grid)))(*operands, *comm.ins)
    return list(res[:n_out]), list(res[n_out:])


def _comm_only(comm, name):
    c_in, c_out = len(comm.ins), len(comm.out_shapes)

    def body(*refs):
        cins, couts, csem = refs[:c_in], refs[c_in:c_in + c_out], refs[c_in + c_out:]
        comm.start(cins, couts, csem)
        if comm.mid is not None:
            comm.mid(cins, couts, csem)
        comm.finish(cins, couts, csem)

    res = _call(body, name=name, in_specs=[_ANY] * c_in, out_specs=[_ANY] * c_out, out_shape=comm.out_shapes,
                scratch_shapes=comm.sems)(*comm.ins)
    return list(res)


def _wres(n, D, part):
    return pl.BlockSpec((N_DEV, n, D), lambda i: (0, part, 0), pipeline_mode=pl.Buffered(1))


def _phases(ext, ph, rows):
    for p in range(8):
        ph[p, :, :] = ext[p:p + rows, :]


def _f_in_conv(x, g, G, n_in, cw, cb, lg, lb, tm, comm=None):
    S, D = x.shape
    N = N_DEV * n_in
    CW = cw.shape[1]
    RB = 64

    def body(x_ref, g_ref, w_ref, cw_ref, cb_ref, lg_ref, lb_ref, h_ref, p_ref, c_ref, u_ref, ext, ph):
        i = pl.program_id(0)

        @pl.when(i == 0)
        def _():
            ext[0:HALO, :] = jnp.zeros((HALO, CW), F32)
            ext[HALO + tm:, :] = jnp.zeros((8, CW), F32)

        @pl.when(i > 0)
        def _():
            ext[0:HALO, :] = ext[tm:tm + HALO, :]

        xv = x_ref[...]
        r = lax.rsqrt(jnp.mean(xv * xv, axis=-1, keepdims=True) + EPS)
        h = ((xv * r) * g_ref[...]).astype(CDT)
        h_ref[...] = h
        w = w_ref[...].reshape(N, D)
        pab = _dot_nt(h, w[0:2 * CW])
        p_ref[:, 0:2 * CW] = pab
        p_ref[:, 2 * CW:] = _dot_nt(h, w[2 * CW:])
        ext[HALO:HALO + tm, :] = pab[:, :CW] * _sigmoid(pab[:, CW:])
        _phases(ext, ph, tm + HALO)
        for rb in range(tm // RB):
            acc = jnp.zeros((RB, CW), F32) + cb_ref[...]
            for k in range(CONV_K):
                o = rb * RB + HALO - (CONV_K - 1) + k
                acc = acc + cw_ref[k:k + 1, :] * ph[o % 8, o - o % 8:o - o % 8 + RB, :]
            c_ref[rb * RB:(rb + 1) * RB, :] = acc
            mu = jnp.mean(acc, axis=-1, keepdims=True)
            var = jnp.mean(jnp.square(acc - mu), axis=-1, keepdims=True)
            z = ((acc - mu) * lax.rsqrt(var + EPS)) * lg_ref[...] + lb_ref[...]
            u_ref[rb * RB:(rb + 1) * RB, :] = (z * _sigmoid(z)).astype(CDT)

    row = lambda n: pl.BlockSpec((tm, n), lambda i: (i, 0))
    return _launch(
        "f_in_conv" if comm is None else "f_in_conv_ag", body, (S // tm,),
        [row(D), _resident((1, D)), _wres(n_in, D, 0),
         _resident((HALO, CW)), _resident((1, CW)), _resident((1, CW)), _resident((1, CW))],
        [row(D), row(N), row(CW), row(CW)],
        [jax.ShapeDtypeStruct((S, D), CDT), jax.ShapeDtypeStruct((S, N), F32),
         jax.ShapeDtypeStruct((S, CW), F32), jax.ShapeDtypeStruct((S, CW), CDT)],
        [pltpu.VMEM((tm + HALO + 8, CW), F32), pltpu.VMEM((8, tm + HALO, CW), F32)],
        (x, g, G, cw, cb, lg, lb), ("arbitrary",), comm)


def _rot(t, c, s):
    return t * c + pltpu.roll(t, HD // 2, 1) * s


def _f_ret(proj, u, cosT, sinT, Mt, qd, kd, gs, gn, comm=None):
    S = proj.shape[0]
    RW = HEADS * HD
    NB = S // SC
    scale = HD ** -0.5

    def body(q_ref, k_ref, v_ref, g_ref, u_ref, c_ref, s_ref, m_ref, qd_ref, kd_ref, gs_ref, gn_ref,
             rraw_ref, st_ref, mix_ref, state):
        @pl.when(pl.program_id(0) == 0)
        def _():
            state[...] = jnp.zeros_like(state)

        mix_ref[:, 0:RW] = u_ref[...]
        cv = c_ref[...]
        sv = s_ref[...]
        for h in range(HEADS):
            cs = slice(h * HD, (h + 1) * HD)
            q = _rot(q_ref[:, cs], cv, sv)
            k = _rot(k_ref[:, cs], cv, sv) * scale
            vb = v_ref[:, cs].astype(CDT)
            qb = q.astype(CDT)
            kb = k.astype(CDT)
            a = _dot_nt(qb, kb) * m_ref[h]
            sp = state[h]
            spb = sp.astype(CDT)
            st_ref[0, h] = spb
            r = _dot(a.astype(CDT), vb) + _dot((q * qd_ref[h]).astype(CDT), spb)
            kv = _dot_tn((k * kd_ref[h]).astype(CDT), vb)
            state[h] = gs_ref[h, 0:1, :] * sp + kv
            rraw_ref[:, cs] = r
            mu = jnp.mean(r, axis=-1, keepdims=True)
            var = jnp.mean(jnp.square(r - mu), axis=-1, keepdims=True)
            n = (r - mu) * lax.rsqrt(var + EPS)
            gv = g_ref[:, cs]
            mix_ref[:, RW + h * HD:RW + (h + 1) * HD] = ((n * gn_ref[:, cs]) * (gv * _sigmoid(gv))).astype(CDT)

    col = lambda j: pl.BlockSpec((SC, RW), lambda i: (i, j))
    return _launch(
        "f_ret" if comm is None else "f_ret_ag", body, (NB,),
        [col(2), col(3), col(4), col(5),
         pl.BlockSpec((SC, RW), lambda i: (i, 0)),
         pl.BlockSpec((SC, HD), lambda i: (i, 0)), pl.BlockSpec((SC, HD), lambda i: (i, 0)),
         _resident((HEADS, SC, SC)), _resident((HEADS, SC, HD)), _resident((HEADS, SC, HD)),
         _resident((HEADS, 8, HD)), _resident((1, RW))],
        [pl.BlockSpec((SC, RW), lambda i: (i, 0)),
         pl.BlockSpec((1, HEADS, HD, HD), lambda i: (i, 0, 0, 0)),
         pl.BlockSpec((SC, 2 * RW), lambda i: (i, 0))],
        [jax.ShapeDtypeStruct((S, RW), F32),
         jax.ShapeDtypeStruct((NB, HEADS, HD, HD), CDT),
         jax.ShapeDtypeStruct((S, 2 * RW), CDT)],
        [pltpu.VMEM((HEADS, HD, HD), F32)],
        (proj, proj, proj, proj, u, cosT, sinT, Mt, qd, kd, gs, gn), ("arbitrary",), comm)


def _f_mlp(x, mixed, g2, G, n_out, n_ff, parts, tm, comm=None):
    S, D = x.shape
    FF = N_DEV * n_ff
    p_wo, p_wg, p_wu, p_wd = parts

    def body(x_ref, m_ref, wo_ref, g_ref, wg_ref, wu_ref, wd_ref,
             xm_ref, h2_ref, gate_ref, up_ref, act_ref, xo_ref):
        xm = x_ref[...] + _dot(m_ref[...], wo_ref[...].reshape(N_DEV * n_out, D))
        xm_ref[...] = xm
        r = lax.rsqrt(jnp.mean(xm * xm, axis=-1, keepdims=True) + EPS)
        h2 = ((xm * r) * g_ref[...]).astype(CDT)
        h2_ref[...] = h2
        gate = _dot_nt(h2, wg_ref[...].reshape(FF, D))
        up = _dot_nt(h2, wu_ref[...].reshape(FF, D))
        gate_ref[...] = gate.astype(CDT)
        up_ref[...] = up.astype(CDT)
        act = ((gate * _sigmoid(gate)) * up).astype(CDT)
        act_ref[...] = act
        xo_ref[...] = xm + _dot(act, wd_ref[...].reshape(FF, D))

    row = lambda n: pl.BlockSpec((tm, n), lambda i: (i, 0))
    return _launch(
        "f_mlp" if comm is None else "f_mlp_ag", body, (S // tm,),
        [row(D), row(D), _wres(n_out, D, p_wo), _resident((1, D)),
         _wres(n_ff, D, p_wg), _wres(n_ff, D, p_wu), _wres(n_ff, D, p_wd)],
        [row(D), row(D), row(FF), row(FF), row(FF), row(D)],
        [jax.ShapeDtypeStruct((S, D), F32), jax.ShapeDtypeStruct((S, D), CDT),
         jax.ShapeDtypeStruct((S, FF), CDT), jax.ShapeDtypeStruct((S, FF), CDT),
         jax.ShapeDtypeStruct((S, FF), CDT), jax.ShapeDtypeStruct((S, D), F32)],
        [], (x, mixed, G[0], g2, G[0], G[0], G[1]), ("parallel",), comm)


def _f_loss(x, fg, tgt, tm):
    S, D = x.shape

    def body(x_ref, g_ref, t_ref, dx_ref, dxb_ref, loss_ref, dg_ref):
        @pl.when(pl.program_id(0) == 0)
        def _():
            loss_ref[...] = jnp.zeros_like(loss_ref)
            dg_ref[...] = jnp.zeros_like(dg_ref)

        xv = x_ref[...]
        r = lax.rsqrt(jnp.mean(xv * xv, axis=-1, keepdims=True) + EPS)
        y = (xv * r) * g_ref[...]
        e = y - t_ref[...]
        loss_ref[...] += 0.5 * jnp.sum(jnp.mean(e * e, axis=-1, keepdims=True))
        dy = e * (1.0 / D)
        dx, dgx = _rms_bwd(xv, r, g_ref[...], dy)
        dg_ref[...] += jnp.sum(dgx, axis=0, keepdims=True)
        dx_ref[...] = dx
        dxb_ref[...] = dx.astype(CDT)

    row = pl.BlockSpec((tm, D), lambda i: (i, 0))
    return _call(
        body, name="f_loss", grid=(S // tm,),
        in_specs=[row, _resident((1, D)), row],
        out_specs=[row, row, pl.BlockSpec((1, 128), lambda i: (0, 0)), pl.BlockSpec((1, D), lambda i: (0, 0))],
        out_shape=[jax.ShapeDtypeStruct((S, D), F32), jax.ShapeDtypeStruct((S, D), CDT),
                   jax.ShapeDtypeStruct((1, 128), F32), jax.ShapeDtypeStruct((1, D), F32)],
        compiler_params=_cp(("arbitrary",)),
    )(x, fg, tgt)


def _b_mlp(dxb, dx, xm, gate, up, g2, G, n_out, n_ff, parts, tm, comm=None):
    S, D = dx.shape
    FF = N_DEV * n_ff
    p_wo, p_wg, p_wu, p_wd = parts

    def body(dxb_ref, dx_ref, xm_ref, gate_ref, up_ref, g_ref, wd_ref, wg_ref, wu_ref, wo_ref,
             dgate_ref, dup_ref, dxm_ref, dxmb_ref, dmix_ref, dg_ref):
        @pl.when(pl.program_id(0) == 0)
        def _():
            dg_ref[...] = jnp.zeros_like(dg_ref)

        dact = _dot_nt(dxb_ref[...], wd_ref[...].reshape(FF, D))
        gate = gate_ref[...].astype(F32)
        up = up_ref[...].astype(F32)
        sg = _sigmoid(gate)
        sil = gate * sg
        dgate = ((dact * up) * (sg * (1.0 + gate * (1.0 - sg)))).astype(CDT)
        dup = (dact * sil).astype(CDT)
        dgate_ref[...] = dgate
        dup_ref[...] = dup
        dh2 = _dot(dgate, wg_ref[...].reshape(FF, D)) + _dot(dup, wu_ref[...].reshape(FF, D))
        xm = xm_ref[...]
        r = lax.rsqrt(jnp.mean(xm * xm, axis=-1, keepdims=True) + EPS)
        dxn, dgx = _rms_bwd(xm, r, g_ref[...], dh2)
        dg_ref[...] += jnp.sum(dgx, axis=0, keepdims=True)
        dxm = dx_ref[...] + dxn
        dxm_ref[...] = dxm
        dxmb = dxm.astype(CDT)
        dxmb_ref[...] = dxmb
        dmix_ref[...] = _dot_nt(dxmb, wo_ref[...].reshape(N_DEV * n_out, D))

    row = lambda n: pl.BlockSpec((tm, n), lambda i: (i, 0))
    return _launch(
        "b_mlp" if comm is None else "b_mlp_rs", body, (S // tm,),
        [row(D), row(D), row(D), row(FF), row(FF), _resident((1, D)),
         _wres(n_ff, D, p_wd), _wres(n_ff, D, p_wg), _wres(n_ff, D, p_wu), _wres(n_out, D, p_wo)],
        [row(FF), row(FF), row(D), row(D), row(D), pl.BlockSpec((1, D), lambda i: (0, 0))],
        [jax.ShapeDtypeStruct((S, FF), CDT), jax.ShapeDtypeStruct((S, FF), CDT),
         jax.ShapeDtypeStruct((S, D), F32), jax.ShapeDtypeStruct((S, D), CDT),
         jax.ShapeDtypeStruct((S, D), F32), jax.ShapeDtypeStruct((1, D), F32)],
        [], (dxb, dx, xm, gate, up, g2, G[1], G[0], G[0], G[0]), ("arbitrary",), comm)


def _b_conv(dmix, c, proj, cw, lg, lb, tm, comm=None):
    S = proj.shape[0]
    CW = cw.shape[1]
    RB = 64
    hb = tm // HALO
    nt = S // tm
    last_h = S // HALO - 1

    def body(du_ref, duh_ref, c_ref, ch_ref, ab_ref, abh_ref, cw_ref, lg_ref, lb_ref,
             dab_ref, dcw_ref, dcb_ref, dlg_ref, dlb_ref, ext_u, ext_dc, ph_u, ph_dc, wacc):
        i = pl.program_id(0)

        @pl.when(i == 0)
        def _():
            wacc[...] = jnp.zeros_like(wacc)
            dcb_ref[...] = jnp.zeros_like(dcb_ref)
            dlg_ref[...] = jnp.zeros_like(dlg_ref)
            dlb_ref[...] = jnp.zeros_like(dlb_ref)

        def ln_bwd(cv, du):
            mu = jnp.mean(cv, axis=-1, keepdims=True)
            var = jnp.mean(jnp.square(cv - mu), axis=-1, keepdims=True)
            rstd = lax.rsqrt(var + EPS)
            n = (cv - mu) * rstd
            z = n * lg_ref[...] + lb_ref[...]
            sz = _sigmoid(z)
            dz = du * (sz * (1.0 + z * (1.0 - sz)))
            dn = dz * lg_ref[...]
            dc = rstd * (dn - jnp.mean(dn, axis=-1, keepdims=True)
                         - n * jnp.mean(dn * n, axis=-1, keepdims=True))
            return dc, dz, n

        hv = abh_ref[...]
        ext_u[0:HALO, :] = jnp.where(i > 0, hv[:, :CW] * _sigmoid(hv[:, CW:]), 0.0)
        av = ab_ref[...]
        sb = _sigmoid(av[:, CW:])
        ext_u[HALO:HALO + tm, :] = av[:, :CW] * sb
        ext_u[HALO + tm:, :] = jnp.zeros((8, CW), F32)
        dc, dz, n = ln_bwd(c_ref[...], du_ref[...])
        ext_dc[0:tm, :] = dc
        dch, _, _ = ln_bwd(ch_ref[...], duh_ref[...])
        ext_dc[tm:tm + HALO, :] = jnp.where(i < nt - 1, dch, 0.0)
        ext_dc[tm + HALO:, :] = jnp.zeros((8, CW), F32)
        dlg_ref[...] += jnp.sum(dz * n, axis=0, keepdims=True)
        dlb_ref[...] += jnp.sum(dz, axis=0, keepdims=True)
        dcb_ref[...] += jnp.sum(dc, axis=0, keepdims=True)
        _phases(ext_u, ph_u, tm + HALO)
        _phases(ext_dc, ph_dc, tm + HALO)

        for rb in range(tm // RB):
            rs = slice(rb * RB, (rb + 1) * RB)
            dcb = ext_dc[rs, :]
            for k in range(CONV_K):
                o = rb * RB + HALO - (CONV_K - 1) + k
                prod = dcb * ph_u[o % 8, o - o % 8:o - o % 8 + RB, :]
                part = prod[0:8]
                for j in range(1, RB // 8):
                    part = part + prod[8 * j:8 * j + 8]
                wacc[k] += part
            acc = jnp.zeros((RB, CW), F32)
            for k in range(CONV_K):
                o = rb * RB + (CONV_K - 1) - k
                acc = acc + cw_ref[k:k + 1, :] * ph_dc[o % 8, o - o % 8:o - o % 8 + RB, :]
            a_r = ab_ref[rs, 0:CW]
            s_r = _sigmoid(ab_ref[rs, CW:2 * CW])
            dab_ref[rs, 0:CW] = (acc * s_r).astype(CDT)
            dab_ref[rs, CW:2 * CW] = (acc * a_r * (s_r * (1.0 - s_r))).astype(CDT)

        @pl.when(i == nt - 1)
        def _():
            for k in range(CONV_K):
                dcw_ref[k:k + 1, :] = jnp.sum(wacc[k], axis=0, keepdims=True)
            dcw_ref[CONV_K:, :] = jnp.zeros((HALO - CONV_K, CW), F32)

    tile = lambda n, j: pl.BlockSpec((tm, n), lambda i: (i, j))
    nxt = lambda n, j: pl.BlockSpec((HALO, n), lambda i: (jnp.minimum((i + 1) * hb, last_h), j))
    return _launch(
        "b_conv" if comm is None else "b_conv_rs", body, (nt,),
        [tile(CW, 0), nxt(CW, 0), tile(CW, 0), nxt(CW, 0),
         tile(2 * CW, 0),
         pl.BlockSpec((HALO, 2 * CW), lambda i: (jnp.maximum(i * hb - 1, 0), 0)),
         _resident((HALO, CW)), _resident((1, CW)), _resident((1, CW))],
        [tile(2 * CW, 0),
         pl.BlockSpec((HALO, CW), lambda i: (0, 0)), pl.BlockSpec((1, CW), lambda i: (0, 0)),
         pl.BlockSpec((1, CW), lambda i: (0, 0)), pl.BlockSpec((1, CW), lambda i: (0, 0))],
        [jax.ShapeDtypeStruct((S, 2 * CW), CDT),
         jax.ShapeDtypeStruct((HALO, CW), F32), jax.ShapeDtypeStruct((1, CW), F32),
         jax.ShapeDtypeStruct((1, CW), F32), jax.ShapeDtypeStruct((1, CW), F32)],
        [pltpu.VMEM((tm + HALO + 8, CW), F32), pltpu.VMEM((tm + HALO + 8, CW), F32),
         pltpu.VMEM((8, tm + HALO, CW), F32), pltpu.VMEM((8, tm + HALO, CW), F32),
         pltpu.VMEM((HALO, 8, CW), F32)],
        (dmix, dmix, c, c, proj, proj, cw, lg, lb), ("arbitrary",), comm)


def _b_ret(dmix, dab, proj, rraw, states, cosT, sinT, Mt, qd, kd, gs, gn, comm=None):
    S = proj.shape[0]
    RW = HEADS * HD
    NB = S // SC
    scale = HD ** -0.5

    def body(dro_ref, dab_ref, q_ref, k_ref, v_ref, g_ref, rraw_ref, st_ref, c_ref, s_ref,
             m_ref, qd_ref, kd_ref, gs_ref, gn_ref, dp_ref, dgn_ref, G):
        @pl.when(pl.program_id(0) == 0)
        def _():
            G[...] = jnp.zeros_like(G)
            dgn_ref[...] = jnp.zeros_like(dgn_ref)

        dp_ref[:, 0:2 * RW] = dab_ref[...]
        cv = c_ref[...]
        sv = s_ref[...]
        for h in range(HEADS):
            cs = slice(h * HD, (h + 1) * HD)
            q = _rot(q_ref[:, cs], cv, sv)
            k = _rot(k_ref[:, cs], cv, sv) * scale
            qb = q.astype(CDT)
            kb = k.astype(CDT)
            vb = v_ref[:, cs].astype(CDT)
            spb = st_ref[0, h]
            r = rraw_ref[:, cs]
            mu = jnp.mean(r, axis=-1, keepdims=True)
            var = jnp.mean(jnp.square(r - mu), axis=-1, keepdims=True)
            rstd = lax.rsqrt(var + EPS)
            n = (r - mu) * rstd
            gv = g_ref[:, cs]
            sg = _sigmoid(gv)
            sil = gv * sg
            dro = dro_ref[:, cs]
            gnv = gn_ref[:, cs]
            dgn_ref[:, cs] += jnp.sum(dro * n * sil, axis=0, keepdims=True)
            dgate = dro * (n * gnv) * (sg * (1.0 + gv * (1.0 - sg)))
            dn = dro * gnv * sil
            dr = rstd * (dn - jnp.mean(dn, axis=-1, keepdims=True)
                         - n * jnp.mean(dn * n, axis=-1, keepdims=True))
            drb = dr.astype(CDT)
            mh = m_ref[h]
            ab = (_dot_nt(qb, kb) * mh).astype(CDT)
            dab_ = (_dot_nt(drb, vb) * mh).astype(CDT)
            qdb = (q * qd_ref[h]).astype(CDT)
            kdb = (k * kd_ref[h]).astype(CDT)
            gc = G[h]
            gb = gc.astype(CDT)
            dq = _dot(dab_, kb) + _dot_nt(drb, spb) * qd_ref[h]
            dk = _dot_tn(dab_, qb) + _dot_nt(vb, gb) * kd_ref[h]
            dv = _dot_tn(ab, drb) + _dot(kdb, gb)
            G[h] = gs_ref[h, 0:1, :] * gc + _dot_tn(qdb, drb)
            dk = dk * scale
            dqp = dq * cv + pltpu.roll(dq * sv, HD // 2, 1)
            dkp = dk * cv + pltpu.roll(dk * sv, HD // 2, 1)
            base = 2 * RW
            dp_ref[:, base + h * HD:base + (h + 1) * HD] = dqp.astype(CDT)
            dp_ref[:, base + RW + h * HD:base + RW + (h + 1) * HD] = dkp.astype(CDT)
            dp_ref[:, base + 2 * RW + h * HD:base + 2 * RW + (h + 1) * HD] = dv.astype(CDT)
            dp_ref[:, base + 3 * RW + h * HD:base + 3 * RW + (h + 1) * HD] = dgate.astype(CDT)

    rev = lambda n, j: pl.BlockSpec((SC, n), lambda i: (NB - 1 - i, j))
    return _launch(
        "b_ret" if comm is None else "b_ret_rs", body, (NB,),
        [rev(RW, 1), rev(2 * RW, 0), rev(RW, 2), rev(RW, 3), rev(RW, 4), rev(RW, 5), rev(RW, 0),
         pl.BlockSpec((1, HEADS, HD, HD), lambda i: (NB - 1 - i, 0, 0, 0)),
         rev(HD, 0), rev(HD, 0),
         _resident((HEADS, SC, SC)), _resident((HEADS, SC, HD)), _resident((HEADS, SC, HD)),
         _resident((HEADS, 8, HD)), _resident((1, RW))],
        [rev(6 * RW, 0), pl.BlockSpec((1, RW), lambda i: (0, 0))],
        [jax.ShapeDtypeStruct((S, 6 * RW), CDT), jax.ShapeDtypeStruct((1, RW), F32)],
        [pltpu.VMEM((HEADS, HD, HD), F32)],
        (dmix, dab, proj, proj, proj, proj, rraw, states, cosT, sinT, Mt, qd, kd, gs, gn), ("arbitrary",), comm)


def _b_in(dproj, G, n_in, x, g1, dxm, tm, comm=None):
    S, D = x.shape
    N = N_DEV * n_in

    def body(dp_ref, w_ref, x_ref, g_ref, dxm_ref, dx_ref, dxb_ref, dg_ref):
        @pl.when(pl.program_id(0) == 0)
        def _():
            dg_ref[...] = jnp.zeros_like(dg_ref)

        dh = _dot(dp_ref[...], w_ref[...].reshape(N, D))
        xv = x_ref[...]
        r = lax.rsqrt(jnp.mean(xv * xv, axis=-1, keepdims=True) + EPS)
        dxn, dgx = _rms_bwd(xv, r, g_ref[...], dh)
        dg_ref[...] += jnp.sum(dgx, axis=0, keepdims=True)
        dx = dxm_ref[...] + dxn
        dx_ref[...] = dx
        dxb_ref[...] = dx.astype(CDT)

    row = lambda n: pl.BlockSpec((tm, n), lambda i: (i, 0))
    return _launch(
        "b_in" if comm is None else "b_in_rs", body, (S // tm,),
        [row(N), _wres(n_in, D, 0), row(D), _resident((1, D)), row(D)],
        [row(D), row(D), pl.BlockSpec((1, D), lambda i: (0, 0))],
        [jax.ShapeDtypeStruct((S, D), F32), jax.ShapeDtypeStruct((S, D), CDT),
         jax.ShapeDtypeStruct((1, D), F32)],
        [], (dproj, G, x, g1, dxm), ("arbitrary",), comm)


def _dw_tn(a, b, tm, tk):
    S, M = a.shape
    N = b.shape[1]
    nk = S // tk

    def body(a_ref, b_ref, o_ref, ob_ref):
        k = pl.program_id(1)

        @pl.when(k == 0)
        def _():
            o_ref[...] = jnp.zeros_like(o_ref)

        o_ref[...] += _dot_tn(a_ref[...], b_ref[...])

        @pl.when(k == nk - 1)
        def _():
            ob_ref[...] = o_ref[...].astype(CDT)

    out = pl.BlockSpec((tm, N), lambda m, k: (m, 0))
    return _call(
        body, name="dw_tn", grid=(M // tm, nk),
        in_specs=[pl.BlockSpec((tk, tm), lambda m, k: (k, m)), pl.BlockSpec((tk, N), lambda m, k: (k, 0))],
        out_specs=[out, out],
        out_shape=[jax.ShapeDtypeStruct((M, N), F32), jax.ShapeDtypeStruct((M, N), CDT)],
        compiler_params=_cp(("parallel", "arbitrary")),
    )(a, b)


def _adamw(w, g, m, v, tr):
    R, C = w.shape[-2:]
    c1 = 1.0 - ADAM_B1 ** ADAM_STEP
    c2 = 1.0 - ADAM_B2 ** ADAM_STEP

    def body(w_ref, g_ref, m_ref, v_ref, d_ref, mo_ref, vo_ref):
        gv = g_ref[...]
        mn = ADAM_B1 * m_ref[...] + (1.0 - ADAM_B1) * gv
        vn = ADAM_B2 * v_ref[...] + (1.0 - ADAM_B2) * jnp.square(gv)
        mo_ref[...] = mn
        vo_ref[...] = vn
        d_ref[...] = -ADAM_LR * ((mn / c1) / (jnp.sqrt(vn / c2) + ADAM_EPS) + ADAM_WD * w_ref[...])

    sh = jax.ShapeDtypeStruct(w.shape, F32)
    if w.ndim == 2:
        grid, sem = (R // tr,), ("parallel",)
        blk = pl.BlockSpec((tr, C), lambda i: (i, 0))
    else:
        grid, sem = (w.shape[0], R // tr), ("parallel", "parallel")
        blk = pl.BlockSpec((None, tr, C), lambda l, i: (l, i, 0))
    return _call(
        body, name="adamw", grid=grid,
        in_specs=[blk, blk, blk, blk], out_specs=[blk, blk, blk], out_shape=[sh, sh, sh],
        compiler_params=_cp(sem),
    )(w, g, m, v)


def _coords():
    return lax.axis_index("x"), lax.axis_index("y"), lax.axis_index("c")


def _peer(x, y, c, d):
    return (x ^ (d >> 2), y ^ ((d >> 1) & 1), c ^ (d & 1))


def _ag_comm(ps):
    K = len(ps)

    def plan(cins, couts, sems):
        send_sems, recv_sems, local_sems = sems
        x, y, c = _coords()
        me, sibling = (x, y, c), (x, y, 1 - c)
        chips = [(1 - x, y), (x, 1 - y), (1 - x, 1 - y)]
        mine, first, passed, got_ici, got_d2d = [], [], [], [], []
        for a in range(K):
            x_ref, out_ref = cins[a], couts[a]
            R = x_ref.shape[0]

            def rows(px, py, pc, out_ref=out_ref, R=R):
                return out_ref.at[pl.ds((4 * px + 2 * py + pc) * R, R), :]

            def copy(k, block, to, src=None, rows=rows, a=a):
                return pltpu.make_async_remote_copy(
                    src_ref=rows(*block) if src is None else src, dst_ref=rows(*block),
                    send_sem=send_sems.at[7 * a + k], recv_sem=recv_sems.at[7 * a + k],
                    device_id=to, device_id_type=MESH)

            mine.append(pltpu.make_async_copy(x_ref, rows(*me), local_sems.at[a]))
            first.append(copy(0, me, sibling, src=x_ref))
            first += [copy(1 + j, me, (*chip, c), src=x_ref) for j, chip in enumerate(chips)]
            passed += [copy(4 + j, (*chip, c), sibling) for j, chip in enumerate(chips)]
            got_ici += [copy(1 + j, (*chip, c), me) for j, chip in enumerate(chips)]
            got_d2d.append(copy(0, sibling, me))
            got_d2d += [copy(4 + j, (*chip, 1 - c), me) for j, chip in enumerate(chips)]
        return mine, first, passed, got_ici, got_d2d

    def start(*a):
        mine, first, _, _, _ = plan(*a)
        for cp in mine + first:
            cp.start()

    def mid(*a):
        _, _, passed, got_ici, _ = plan(*a)
        for got, fwd in zip(got_ici, passed):
            got.wait_recv()
            fwd.start()

    def finish(*a):
        mine, first, passed, _, got_d2d = plan(*a)
        for got in got_d2d:
            got.wait_recv()
        for cp in first + passed:
            cp.wait_send()
        for cp in mine:
            cp.wait()

    return _Comm(ps, [jax.ShapeDtypeStruct((N_DEV * p.shape[0], p.shape[1]), p.dtype) for p in ps],
                 [pltpu.SemaphoreType.DMA((7 * K,)), pltpu.SemaphoreType.DMA((7 * K,)),
                  pltpu.SemaphoreType.DMA((K,))], start, mid, finish)


def _rs_direct_comm(parts):
    K = len(parts)

    def plan(cins, couts, sems):
        send_sems, recv_sems = sems
        x, y, c = _coords()
        cps = []
        for d in range(1, N_DEV):
            px, py, pc = _peer(x, y, c, d)
            for k in range(K):
                s = (d - 1) * K + k
                cps.append(pltpu.make_async_remote_copy(
                    src_ref=cins[k].at[4 * px + 2 * py + pc], dst_ref=couts[k].at[d - 1],
                    send_sem=send_sems.at[s], recv_sem=recv_sems.at[s], device_id=(px, py, pc),
                    device_id_type=MESH))
        return cps

    def start(*a):
        for cp in plan(*a):
            cp.start()

    def finish(*a):
        cps = plan(*a)
        for cp in cps:
            cp.wait_recv()
        for cp in cps:
            cp.wait_send()

    n_sem = (N_DEV - 1) * K
    return _Comm(parts, [jax.ShapeDtypeStruct((N_DEV - 1,) + p.shape[1:], p.dtype) for p in parts],
                 [pltpu.SemaphoreType.DMA((n_sem,)), pltpu.SemaphoreType.DMA((n_sem,))], start, None, finish)


def _sum_all(parts, recv):
    K = len(parts)

    def body(*refs):
        ins, rcv, outs = refs[:K], refs[K:2 * K], refs[2 * K:3 * K]
        bufs, sem = refs[3 * K:4 * K], refs[4 * K]
        x, y, c = _coords()
        me = 4 * x + 2 * y + c
        cps = [pltpu.make_async_copy(ins[k].at[me], bufs[k], sem.at[k]) for k in range(K)]
        for cp in cps:
            cp.start()
        for k in range(K):
            cps[k].wait()
            acc = bufs[k][...]
            for d in range(N_DEV - 1):
                acc = acc + rcv[k][d].astype(F32)
            outs[k][...] = acc

    vm = pl.BlockSpec(memory_space=pltpu.VMEM)
    res = _call(
        body, name="sum_all", in_specs=[_ANY] * K + [vm] * K, out_specs=[vm] * K,
        out_shape=[jax.ShapeDtypeStruct(p.shape[1:], F32) for p in parts],
        scratch_shapes=[pltpu.VMEM(p.shape[1:], F32) for p in parts] + [pltpu.SemaphoreType.DMA((K,))],
        compiler_params=_cp(),
    )(*parts, *recv)
    return list(res)


def _gather_small(v, reduce):
    R, C = v.shape

    def exchange(v_ref, buf, send_sems, recv_sems):
        x, y, c = _coords()
        me = 4 * x + 2 * y + c
        buf[me] = v_ref[...]
        cps = []
        for d in range(1, N_DEV):
            cp = pltpu.make_async_remote_copy(
                src_ref=v_ref, dst_ref=buf.at[me], send_sem=send_sems.at[d - 1], recv_sem=recv_sems.at[d - 1],
                device_id=_peer(x, y, c, d), device_id_type=MESH)
            cp.start()
            cps.append(cp)
        for cp in cps:
            cp.wait_recv()
        for cp in cps:
            cp.wait_send()

    sems = [pltpu.SemaphoreType.DMA((7,)), pltpu.SemaphoreType.DMA((7,))]
    vm = pl.BlockSpec(memory_space=pltpu.VMEM)
    if reduce:
        def body(v_ref, o_ref, buf, send_sems, recv_sems):
            exchange(v_ref, buf, send_sems, recv_sems)
            acc = buf[0]
            for s in range(1, N_DEV):
                acc = acc + buf[s]
            o_ref[...] = acc

        return _call(body, name="allreduce_small", in_specs=[vm], out_specs=vm,
                     out_shape=jax.ShapeDtypeStruct((R, C), F32),
                     scratch_shapes=[pltpu.VMEM((N_DEV, R, C), F32)] + sems)(v)

    def body(v_ref, o_ref, send_sems, recv_sems):
        exchange(v_ref, o_ref, send_sems, recv_sems)

    return _call(body, name="allgather_small", in_specs=[vm], out_specs=vm,
                 out_shape=jax.ShapeDtypeStruct((N_DEV, R, C), F32), scratch_shapes=sems)(v)


def _tables(S):
    half = HD // 2
    pos = jnp.arange(S, dtype=F32)
    freqs = ROPE_BASE ** (-jnp.arange(half, dtype=F32) / half)
    ang = pos[:, None] * freqs[None, :]
    cos, sin = jnp.cos(ang), jnp.sin(ang)
    cosT = jnp.concatenate([cos, cos], axis=-1)
    sinT = jnp.concatenate([-sin, sin], axis=-1)
    log_g = jnp.log(1.0 - 2.0 ** (-5.0 - jnp.arange(HEADS, dtype=F32)))
    idx = jnp.arange(SC, dtype=F32)
    ci = jnp.arange(SC) // CHUNK
    diff = idx[:, None] - idx[None, :]
    same = ci[:, None] == ci[None, :]
    earlier = ci[None, :] < ci[:, None]
    expo = jnp.where(same, jnp.abs(diff), diff)
    Mt = jnp.where((same | earlier)[None], jnp.exp(log_g[:, None, None] * expo[None]), 0.0)
    ones = jnp.ones((1, 1, HD), F32)
    qd = jnp.exp(log_g[:, None] * (idx + 1.0)[None, :])[:, :, None] * ones
    kd = jnp.exp(log_g[:, None] * (SC - 1.0 - idx)[None, :])[:, :, None] * ones
    gs = jnp.exp(log_g * SC)[:, None, None] * jnp.ones((1, 8, HD), F32)
    return cosT, sinT, Mt, qd, kd, gs


def _pad_rows(a, rows):
    return jnp.pad(a, ((0, rows - a.shape[0]), (0, 0)))


def kernel(x, norm1_g, w_in, conv_w, conv_b, conv_ln_g, conv_ln_b, ret_gn_g, w_out, norm2_g, w_gate, w_up, w_down, final_g, loss_target, m_norm1_g, m_w_in, m_conv_w, m_conv_b, m_conv_ln_g, m_conv_ln_b, m_ret_gn_g, m_w_out, m_norm2_g, m_w_gate, m_w_up, m_w_down, m_final_g, v_norm1_g, v_w_in, v_conv_w, v_conv_b, v_conv_ln_g, v_conv_ln_b, v_ret_gn_g, v_w_out, v_norm2_g, v_w_gate, v_w_up, v_w_down, v_final_g):
    L, D, n_in = w_in.shape
    n_out = w_out.shape[1]
    n_ff = w_gate.shape[2]
    S = x.shape[1]
    CW = conv_b.shape[1]
    IN, FF = N_DEV * n_in, N_DEV * n_ff
    ncw = conv_w.shape[2]
    x0 = x.reshape(S, D)
    tgt = loss_target.reshape(S, D)
    TM = min(512, S)
    TKW = min(2048, S)
    TMI = min(512, S)
    TMM = min(256, S)

    assert n_out <= n_ff
    wparts = (0, 1, 2, 0)
    pack_a = jnp.swapaxes(w_in, 1, 2).astype(CDT)
    pack_b = jnp.concatenate([w_out, jnp.zeros((L, n_ff - n_out, D), F32), jnp.swapaxes(w_gate, 1, 2),
                              jnp.swapaxes(w_up, 1, 2)], axis=1).astype(CDT)
    pack_c = w_down.astype(CDT)
    per_dev = lambda g: g.reshape(N_DEV, -1, D)
    Ga = per_dev(_comm_only(_ag_comm([pack_a[0]]), "ag_first")[0])
    Gb = Gc = None

    cwp = conv_w.reshape(L * CONV_K * ncw // 128, 128)
    cw_rows = -(-cwp.shape[0] // 8) * 8
    cwg = _gather_small(_pad_rows(cwp, cw_rows), reduce=False)[:, :cwp.shape[0], :]
    conv_w_full = jnp.moveaxis(cwg.reshape(N_DEV, L, CONV_K, ncw), 0, 2).reshape(L, CONV_K, CW)

    cosT, sinT, Mt, qd, kd, gs = _tables(S)

    saved = []
    xl = x0
    for l in range(L):
        cw = _pad_rows(conv_w_full[l], HALO)
        (h, proj, c, u), got = _f_in_conv(xl, norm1_g[l][None], Ga, n_in, cw, conv_b[l][None], conv_ln_g[l][None],
                                          conv_ln_b[l][None], TMI, _ag_comm([pack_b[0]]) if l == 0 else None)
        if got:
            Gb = per_dev(got[0])
        (rraw, states, mixed), got = _f_ret(proj, u, cosT, sinT, Mt, qd, kd, gs, ret_gn_g[l][None],
                                            _ag_comm([pack_c[0]]) if l == 0 else None)
        if got:
            Gc = per_dev(got[0])
        (xm, h2, gate, up, act, xo), nxt = _f_mlp(
            xl, mixed, norm2_g[l][None], (Gb, Gc), n_out, n_ff, wparts, TMM,
            _ag_comm([pack_a[l + 1], pack_b[l + 1], pack_c[l + 1]]) if l + 1 < L else None)
        saved.append(dict(x=xl, h=h, proj=proj, c=c, rraw=rraw, states=states, mixed=mixed, xm=xm, h2=h2,
                          gate=gate, up=up, act=act, cw=cw, Ga=Ga, Gbc=(Gb, Gc)))
        if nxt:
            Ga, Gb, Gc = (per_dev(g) for g in nxt)
        xl = xo

    dx, dxb, loss_p, dfg = _f_loss(xl, final_g[None], tgt, TM)

    small = []
    own = [None] * L
    recv = [None] * L
    blocks = lambda d: d.reshape(N_DEV, -1, D)
    for l in reversed(range(L)):
        sv = saved[l]
        (dgate, dup, dxm, dxmb, dmix, dg2), _ = _b_mlp(
            dxb, dx, sv["xm"], sv["gate"], sv["up"], norm2_g[l][None], sv["Gbc"], n_out, n_ff, wparts, TMM)
        d_wd, d_wd_b = _dw_tn(sv["act"], dxb, FF // 2, TKW)
        d_wgT, d_wgT_b = _dw_tn(dgate, sv["h2"], FF // 2, TKW)
        d_wuT, d_wuT_b = _dw_tn(dup, sv["h2"], FF // 2, TKW)
        d_wo, d_wo_b = _dw_tn(sv["mixed"], dxmb, D, TKW)
        (dab, dcw, dcb, dlg, dlb), r_go = _b_conv(dmix, sv["c"], sv["proj"], sv["cw"], conv_ln_g[l][None],
                                                  conv_ln_b[l][None], TM,
                                                  _rs_direct_comm([blocks(d_wgT_b), blocks(d_wo_b)]))
        (dproj, dgn), r_ud = _b_ret(dmix, dab, sv["proj"], sv["rraw"], sv["states"], cosT, sinT, Mt, qd, kd, gs,
                                    ret_gn_g[l][None], _rs_direct_comm([blocks(d_wuT_b), blocks(d_wd_b)]))
        d_winT, d_winT_b = _dw_tn(dproj, sv["h"], IN // 2, TKW)
        (dx, dxb, dg1), r_i = _b_in(dproj, sv["Ga"], n_in, sv["x"], norm1_g[l][None], dxm, TMI,
                                    _rs_direct_comm([blocks(d_winT_b)]))
        own[l] = [blocks(d) for d in (d_winT, d_wo, d_wgT, d_wuT, d_wd)]
        recv[l] = [r_i[0], r_go[1], r_go[0], r_ud[0], r_ud[1]]
        small.append(jnp.concatenate([dcw, dcb, dlg, dlb, dgn, dg1.reshape(2, CW), dg2.reshape(2, CW)], axis=0))
    small = small[::-1]
    grad_x = dx.reshape(1, S, D)

    rows_l = HALO + 8
    loss_row = jnp.zeros((1, CW), F32).at[0, 0].set(loss_p[0, 0])
    sm = jnp.concatenate(small + [dfg.reshape(2, CW), loss_row], axis=0)
    sm_rows = -(-sm.shape[0] // 8) * 8
    sm = _gather_small(_pad_rows(sm, sm_rows), reduce=True)
    loss = sm[L * rows_l + 2, 0]
    g_final = sm[L * rows_l:L * rows_l + 2].reshape(D)
    per = sm[:L * rows_l].reshape(L, rows_l, CW)
    me = 4 * lax.axis_index("x") + 2 * lax.axis_index("y") + lax.axis_index("c")
    g_conv_w = lax.dynamic_slice_in_dim(per[:, :CONV_K, :], me * ncw, ncw, axis=2)
    g_conv_b, g_ln_g, g_ln_b, g_gn = per[:, HALO], per[:, HALO + 1], per[:, HALO + 2], per[:, HALO + 3]
    g_n1 = per[:, HALO + 4:HALO + 6].reshape(L, D)
    g_n2 = per[:, HALO + 6:HALO + 8].reshape(L, D)

    gl = [_sum_all(own[l], recv[l]) for l in range(L)]
    g_w_in = jnp.stack([gl[l][0].T for l in range(L)])
    g_w_out = jnp.stack([gl[l][1] for l in range(L)])
    g_w_gate = jnp.stack([gl[l][2].T for l in range(L)])
    g_w_up = jnp.stack([gl[l][3].T for l in range(L)])
    g_w_down = jnp.stack([gl[l][4] for l in range(L)])

    def big(w, g, m, v):
        return tuple(_adamw(w, g, m, v, w.shape[1] // 2))

    names = ["norm1_g", "conv_w", "conv_b", "conv_ln_g", "conv_ln_b", "ret_gn_g", "norm2_g", "final_g"]
    sw = dict(norm1_g=(norm1_g, g_n1, m_norm1_g, v_norm1_g), conv_w=(conv_w, g_conv_w, m_conv_w, v_conv_w),
              conv_b=(conv_b, g_conv_b, m_conv_b, v_conv_b), conv_ln_g=(conv_ln_g, g_ln_g, m_conv_ln_g, v_conv_ln_g),
              conv_ln_b=(conv_ln_b, g_ln_b, m_conv_ln_b, v_conv_ln_b), ret_gn_g=(ret_gn_g, g_gn, m_ret_gn_g, v_ret_gn_g),
              norm2_g=(norm2_g, g_n2, m_norm2_g, v_norm2_g), final_g=(final_g, g_final, m_final_g, v_final_g))
    lens = [int(math.prod(sw[n][0].shape)) for n in names]
    tot = sum(lens)
    prow = -(-tot // (8 * CW)) * 8

    def packs(j):
        flat = jnp.concatenate([sw[n][j].reshape(-1) for n in names])
        return jnp.pad(flat, (0, prow * CW - tot)).reshape(prow, CW)

    sd, smn, svn = _adamw(packs(0), packs(1), packs(2), packs(3), prow)

    def unpack(a):
        flat = a.reshape(-1)
        out, o = {}, 0
        for n, ln in zip(names, lens):
            out[n] = flat[o:o + ln].reshape(sw[n][0].shape)
            o += ln
        return out

    sd, smn, svn = unpack(sd), unpack(smn), unpack(svn)
    res = {n: (sw[n][1], sd[n], smn[n], svn[n]) for n in names}
    res["w_in"] = (g_w_in,) + big(w_in, g_w_in, m_w_in, v_w_in)
    res["w_out"] = (g_w_out,) + big(w_out, g_w_out, m_w_out, v_w_out)
    res["w_gate"] = (g_w_gate,) + big(w_gate, g_w_gate, m_w_gate, v_w_gate)
    res["w_up"] = (g_w_up,) + big(w_up, g_w_up, m_w_up, v_w_up)
    res["w_down"] = (g_w_down,) + big(w_down, g_w_down, m_w_down, v_w_down)

    order = ["norm1_g", "w_in", "conv_w", "conv_b", "conv_ln_g", "conv_ln_b", "ret_gn_g", "w_out", "norm2_g",
             "w_gate", "w_up", "w_down", "final_g"]
    return (loss, grad_x, *[res[n][0] for n in order], *[res[n][1] for n in order],
            *[res[n][2] for n in order], *[res[n][3] for n in order])
```

```python
import math

import jax
import jax.numpy as jnp
from jax import lax
from jax.experimental import pallas as pl
from jax.experimental.pallas import tpu as pltpu

F32 = jnp.float32
CDT = jnp.bfloat16
EPS = 1e-6
CHUNK = 64
SC = 256
HEADS = 4
HD = 128
CONV_K = 31
HALO = 32
ROPE_BASE = 10000.0
ADAM_LR = 0.001
ADAM_B1 = 0.9
ADAM_B2 = 0.999
ADAM_EPS = 1e-08
ADAM_WD = 0.01
ADAM_STEP = 10
N_DEV = 8
MESH = pl.DeviceIdType.MESH
VMEM_LIMIT = 60 * 1024 * 1024


def _call(body, **kw):
    return pl.pallas_call(body, **kw)


def _cp(sem=None, vmem=VMEM_LIMIT):
    return pltpu.CompilerParams(dimension_semantics=sem, vmem_limit_bytes=vmem)


def _resident(shape):
    nd = len(shape)
    return pl.BlockSpec(shape, lambda *_: (0,) * nd, pipeline_mode=pl.Buffered(1))


def _dot(a, b):
    return jnp.dot(a, b, preferred_element_type=F32)


def _dot_nt(a, b):
    return lax.dot_general(a, b, (((1,), (1,)), ((), ())), preferred_element_type=F32)


def _dot_tn(a, b):
    return lax.dot_general(a, b, (((0,), (0,)), ((), ())), preferred_element_type=F32)


def _sigmoid(x):
    return 1.0 / (1.0 + jnp.exp(-x))


def _rms_bwd(x, r, g, dy):
    xh = x * r
    dyg = dy * g
    dx = r * (dyg - xh * jnp.mean(dyg * xh, axis=-1, keepdims=True))
    return dx, dy * xh


class _Comm:
    def __init__(self, ins, out_shapes, sems, start, mid, finish):
        self.ins, self.out_shapes, self.sems = list(ins), list(out_shapes), list(sems)
        self.start, self.mid, self.finish = start, mid, finish


_ANY = pl.BlockSpec(memory_space=pl.ANY)


def _launch(name, compute, grid, in_specs, out_specs, out_shape, scratch, operands, sem, comm=None):
    n_in, n_out, n_sc = len(in_specs), len(out_specs), len(scratch)
    if comm is None:
        res = _call(compute, name=name, grid=grid, in_specs=in_specs, out_specs=out_specs, out_shape=out_shape,
                    scratch_shapes=scratch, compiler_params=_cp(sem))(*operands)
        return list(res), []
    c_in, c_out = len(comm.ins), len(comm.out_shapes)
    inner = grid[1] if len(grid) > 1 else 1
    steps = grid[0] * inner
    mid_step = (3 * steps) // 4

    def body(*refs):
        ins = refs[:n_in]
        cins = refs[n_in:n_in + c_in]
        o = n_in + c_in
        outs = refs[o:o + n_out]
        couts = refs[o + n_out:o + n_out + c_out]
        o += n_out + c_out
        sc = refs[o:o + n_sc]
        csem = refs[o + n_sc:]
        i = pl.program_id(0)
        if len(grid) > 1:
            i = i * inner + pl.program_id(1)

        @pl.when(i == 0)
        def _():
            comm.start(cins, couts, csem)

        if comm.mid is not None:
            @pl.when(i == mid_step)
            def _():
                comm.mid(cins, couts, csem)

        compute(*ins, *outs, *sc)

        @pl.when(i == steps - 1)
        def _():
            comm.finish(cins, couts, csem)

    res = _call(body, name=name, grid=grid, in_specs=list(in_specs) + [_ANY] * c_in,
                out_specs=list(out_specs) + [_ANY] * c_out, out_shape=list(out_shape) + comm.out_shapes,
                scratch_shapes=list(scratch) + comm.sems,
                compiler_params=_cp(("arbitrary",) * len(grid)))(*operands, *comm.ins)
    return list(res[:n_out]), list(res[n_out:])


def _comm_only(comm, name):
    c_in, c_out = len(comm.ins), len(comm.out_shapes)

    def body(*refs):
        cins, couts, csem = refs[:c_in], refs[c_in:c_in + c_out], refs[c_in + c_out:]
        comm.start(cins, couts, csem)
        if comm.mid is not None:
            comm.mid(cins, couts, csem)
        comm.finish(cins, couts, csem)

    res = _call(body, name=name, in_specs=[_ANY] * c_in, out_specs=[_ANY] * c_out, out_shape=comm.out_shapes,
                scratch_shapes=comm.sems)(*comm.ins)
    return list(res)


def _wres(n, D, part):
    return pl.BlockSpec((N_DEV, n, D), lambda i: (0, part, 0), pipeline_mode=pl.Buffered(1))


def _taps_by_phase(offsets):
    groups = []
    for p in range(8):
        taps = [(k, o - p) for k, o in enumerate(offsets) if o % 8 == p]
        if taps:
            rows = [a for _, a in taps]
            groups.append((p, min(rows), max(rows), taps))
    return groups


def _phases(ext, ph, rows):
    for p in range(8):
        ph[p, :, :] = ext[p:p + rows, :]


def _f_in_conv(x, g, G, n_in, cw, cb, lg, lb, tm, comm=None):
    S, D = x.shape
    N = N_DEV * n_in
    CW = cw.shape[1]
    RB = 64

    def body(x_ref, g_ref, w_ref, cw_ref, cb_ref, lg_ref, lb_ref, h_ref, p_ref, c_ref, u_ref, ext, ph):
        i = pl.program_id(0)

        @pl.when(i == 0)
        def _():
            ext[0:HALO, :] = jnp.zeros((HALO, CW), F32)
            ext[HALO + tm:, :] = jnp.zeros((8, CW), F32)

        @pl.when(i > 0)
        def _():
            ext[0:HALO, :] = ext[tm:tm + HALO, :]

        xv = x_ref[...]
        r = lax.rsqrt(jnp.mean(xv * xv, axis=-1, keepdims=True) + EPS)
        h = ((xv * r) * g_ref[...]).astype(CDT)
        h_ref[...] = h
        w = w_ref[...].reshape(N, D)
        pab = _dot_nt(h, w[0:2 * CW])
        p_ref[:, 0:2 * CW] = pab
        p_ref[:, 2 * CW:] = _dot_nt(h, w[2 * CW:])
        ext[HALO:HALO + tm, :] = pab[:, :CW] * _sigmoid(pab[:, CW:])
        _phases(ext, ph, tm + HALO)
        for rb in range(tm // RB):
            acc = jnp.zeros((RB, CW), F32) + cb_ref[...]
            for k in range(CONV_K):
                o = rb * RB + HALO - (CONV_K - 1) + k
                acc = acc + cw_ref[k:k + 1, :] * ph[o % 8, o - o % 8:o - o % 8 + RB, :]
            c_ref[rb * RB:(rb + 1) * RB, :] = acc
            mu = jnp.mean(acc, axis=-1, keepdims=True)
            var = jnp.mean(jnp.square(acc - mu), axis=-1, keepdims=True)
            z = ((acc - mu) * lax.rsqrt(var + EPS)) * lg_ref[...] + lb_ref[...]
            u_ref[rb * RB:(rb + 1) * RB, :] = (z * _sigmoid(z)).astype(CDT)

    row = lambda n: pl.BlockSpec((tm, n), lambda i: (i, 0))
    return _launch(
        "f_in_conv" if comm is None else "f_in_conv_ag", body, (S // tm,),
        [row(D), _resident((1, D)), _wres(n_in, D, 0),
         _resident((HALO, CW)), _resident((1, CW)), _resident((1, CW)), _resident((1, CW))],
        [row(D), row(N), row(CW), row(CW)],
        [jax.ShapeDtypeStruct((S, D), CDT), jax.ShapeDtypeStruct((S, N), F32),
         jax.ShapeDtypeStruct((S, CW), F32), jax.ShapeDtypeStruct((S, CW), CDT)],
        [pltpu.VMEM((tm + HALO + 8, CW), F32), pltpu.VMEM((8, tm + HALO, CW), F32)],
        (x, g, G, cw, cb, lg, lb), ("arbitrary",), comm)


def _rot(t, c, s):
    return t * c + pltpu.roll(t, HD // 2, 1) * s


def _f_ret(proj, u, cosT, sinT, Mt, qd, kd, gs, gn, comm=None):
    S = proj.shape[0]
    RW = HEADS * HD
    NB = S // SC
    scale = HD ** -0.5

    def body(q_ref, k_ref, v_ref, g_ref, u_ref, c_ref, s_ref, m_ref, qd_ref, kd_ref, gs_ref, gn_ref,
             rraw_ref, st_ref, mix_ref, state):
        @pl.when(pl.program_id(0) == 0)
        def _():
            state[...] = jnp.zeros_like(state)

        mix_ref[:, 0:RW] = u_ref[...]
        cv = c_ref[...]
        sv = s_ref[...]
        for h in range(HEADS):
            cs = slice(h * HD, (h + 1) * HD)
            q = _rot(q_ref[:, cs], cv, sv)
            k = _rot(k_ref[:, cs], cv, sv) * scale
            vb = v_ref[:, cs].astype(CDT)
            qb = q.astype(CDT)
            kb = k.astype(CDT)
            a = _dot_nt(qb, kb) * m_ref[h]
            sp = state[h]
            spb = sp.astype(CDT)
            st_ref[0, h] = spb
            r = _dot(a.astype(CDT), vb) + _dot((q * qd_ref[h]).astype(CDT), spb)
            kv = _dot_tn((k * kd_ref[h]).astype(CDT), vb)
            state[h] = gs_ref[h, 0:1, :] * sp + kv
            rraw_ref[:, cs] = r
            mu = jnp.mean(r, axis=-1, keepdims=True)
            var = jnp.mean(jnp.square(r - mu), axis=-1, keepdims=True)
            n = (r - mu) * lax.rsqrt(var + EPS)
            gv = g_ref[:, cs]
            mix_ref[:, RW + h * HD:RW + (h + 1) * HD] = ((n * gn_ref[:, cs]) * (gv * _sigmoid(gv))).astype(CDT)

    col = lambda j: pl.BlockSpec((SC, RW), lambda i: (i, j))
    return _launch(
        "f_ret" if comm is None else "f_ret_ag", body, (NB,),
        [col(2), col(3), col(4), col(5),
         pl.BlockSpec((SC, RW), lambda i: (i, 0)),
         pl.BlockSpec((SC, HD), lambda i: (i, 0)), pl.BlockSpec((SC, HD), lambda i: (i, 0)),
         _resident((HEADS, SC, SC)), _resident((HEADS, SC, HD)), _resident((HEADS, SC, HD)),
         _resident((HEADS, 8, HD)), _resident((1, RW))],
        [pl.BlockSpec((SC, RW), lambda i: (i, 0)),
         pl.BlockSpec((1, HEADS, HD, HD), lambda i: (i, 0, 0, 0)),
         pl.BlockSpec((SC, 2 * RW), lambda i: (i, 0))],
        [jax.ShapeDtypeStruct((S, RW), F32),
         jax.ShapeDtypeStruct((NB, HEADS, HD, HD), CDT),
         jax.ShapeDtypeStruct((S, 2 * RW), CDT)],
        [pltpu.VMEM((HEADS, HD, HD), F32)],
        (proj, proj, proj, proj, u, cosT, sinT, Mt, qd, kd, gs, gn), ("arbitrary",), comm)


def _f_mlp(x, mixed, g2, G, n_out, n_ff, parts, tm, comm=None):
    S, D = x.shape
    FF = N_DEV * n_ff
    p_wo, p_wg, p_wu, p_wd = parts

    def body(x_ref, m_ref, wo_ref, g_ref, wg_ref, wu_ref, wd_ref,
             xm_ref, h2_ref, gate_ref, up_ref, act_ref, xo_ref):
        xm = x_ref[...] + _dot(m_ref[...], wo_ref[...].reshape(N_DEV * n_out, D))
        xm_ref[...] = xm
        r = lax.rsqrt(jnp.mean(xm * xm, axis=-1, keepdims=True) + EPS)
        h2 = ((xm * r) * g_ref[...]).astype(CDT)
        h2_ref[...] = h2
        gate = _dot_nt(h2, wg_ref[...].reshape(FF, D))
        up = _dot_nt(h2, wu_ref[...].reshape(FF, D))
        gate_ref[...] = gate.astype(CDT)
        up_ref[...] = up.astype(CDT)
        act = ((gate * _sigmoid(gate)) * up).astype(CDT)
        act_ref[...] = act
        xo_ref[...] = xm + _dot(act, wd_ref[...].reshape(FF, D))

    row = lambda n: pl.BlockSpec((tm, n), lambda i: (i, 0))
    return _launch(
        "f_mlp" if comm is None else "f_mlp_ag", body, (S // tm,),
        [row(D), row(D), _wres(n_out, D, p_wo), _resident((1, D)),
         _wres(n_ff, D, p_wg), _wres(n_ff, D, p_wu), _wres(n_ff, D, p_wd)],
        [row(D), row(D), row(FF), row(FF), row(FF), row(D)],
        [jax.ShapeDtypeStruct((S, D), F32), jax.ShapeDtypeStruct((S, D), CDT),
         jax.ShapeDtypeStruct((S, FF), CDT), jax.ShapeDtypeStruct((S, FF), CDT),
         jax.ShapeDtypeStruct((S, FF), CDT), jax.ShapeDtypeStruct((S, D), F32)],
        [], (x, mixed, G[0], g2, G[0], G[0], G[1]), ("parallel",), comm)


def _f_loss(x, fg, tgt, tm):
    S, D = x.shape

    def body(x_ref, g_ref, t_ref, dx_ref, dxb_ref, loss_ref, dg_ref):
        @pl.when(pl.program_id(0) == 0)
        def _():
            loss_ref[...] = jnp.zeros_like(loss_ref)
            dg_ref[...] = jnp.zeros_like(dg_ref)

        xv = x_ref[...]
        r = lax.rsqrt(jnp.mean(xv * xv, axis=-1, keepdims=True) + EPS)
        y = (xv * r) * g_ref[...]
        e = y - t_ref[...]
        loss_ref[...] += 0.5 * jnp.sum(jnp.mean(e * e, axis=-1, keepdims=True))
        dy = e * (1.0 / D)
        dx, dgx = _rms_bwd(xv, r, g_ref[...], dy)
        dg_ref[...] += jnp.sum(dgx, axis=0, keepdims=True)
        dx_ref[...] = dx
        dxb_ref[...] = dx.astype(CDT)

    row = pl.BlockSpec((tm, D), lambda i: (i, 0))
    return _call(
        body, name="f_loss", grid=(S // tm,),
        in_specs=[row, _resident((1, D)), row],
        out_specs=[row, row, pl.BlockSpec((1, 128), lambda i: (0, 0)), pl.BlockSpec((1, D), lambda i: (0, 0))],
        out_shape=[jax.ShapeDtypeStruct((S, D), F32), jax.ShapeDtypeStruct((S, D), CDT),
                   jax.ShapeDtypeStruct((1, 128), F32), jax.ShapeDtypeStruct((1, D), F32)],
        compiler_params=_cp(("arbitrary",)),
    )(x, fg, tgt)


def _b_mlp(dxb, dx, xm, gate, up, g2, G, n_out, n_ff, parts, tm, comm=None):
    S, D = dx.shape
    FF = N_DEV * n_ff
    p_wo, p_wg, p_wu, p_wd = parts

    def body(dxb_ref, dx_ref, xm_ref, gate_ref, up_ref, g_ref, wd_ref, wg_ref, wu_ref, wo_ref,
             dgate_ref, dup_ref, dxm_ref, dxmb_ref, dmix_ref, dg_ref):
        @pl.when(pl.program_id(0) == 0)
        def _():
            dg_ref[...] = jnp.zeros_like(dg_ref)

        dact = _dot_nt(dxb_ref[...], wd_ref[...].reshape(FF, D))
        gate = gate_ref[...].astype(F32)
        up = up_ref[...].astype(F32)
        sg = _sigmoid(gate)
        sil = gate * sg
        dgate = ((dact * up) * (sg * (1.0 + gate * (1.0 - sg)))).astype(CDT)
        dup = (dact * sil).astype(CDT)
        dgate_ref[...] = dgate
        dup_ref[...] = dup
        dh2 = _dot(dgate, wg_ref[...].reshape(FF, D)) + _dot(dup, wu_ref[...].reshape(FF, D))
        xm = xm_ref[...]
        r = lax.rsqrt(jnp.mean(xm * xm, axis=-1, keepdims=True) + EPS)
        dxn, dgx = _rms_bwd(xm, r, g_ref[...], dh2)
        dg_ref[...] += jnp.sum(dgx, axis=0, keepdims=True)
        dxm = dx_ref[...] + dxn
        dxm_ref[...] = dxm
        dxmb = dxm.astype(CDT)
        dxmb_ref[...] = dxmb
        dmix_ref[...] = _dot_nt(dxmb, wo_ref[...].reshape(N_DEV * n_out, D))

    row = lambda n: pl.BlockSpec((tm, n), lambda i: (i, 0))
    return _launch(
        "b_mlp" if comm is None else "b_mlp_rs", body, (S // tm,),
        [row(D), row(D), row(D), row(FF), row(FF), _resident((1, D)),
         _wres(n_ff, D, p_wd), _wres(n_ff, D, p_wg), _wres(n_ff, D, p_wu), _wres(n_out, D, p_wo)],
        [row(FF), row(FF), row(D), row(D), row(D), pl.BlockSpec((1, D), lambda i: (0, 0))],
        [jax.ShapeDtypeStruct((S, FF), CDT), jax.ShapeDtypeStruct((S, FF), CDT),
         jax.ShapeDtypeStruct((S, D), F32), jax.ShapeDtypeStruct((S, D), CDT),
         jax.ShapeDtypeStruct((S, D), F32), jax.ShapeDtypeStruct((1, D), F32)],
        [], (dxb, dx, xm, gate, up, g2, G[1], G[0], G[0], G[0]), ("arbitrary",), comm)


def _b_conv(dmix, c, proj, cw, lg, lb, tm, comm=None):
    S = proj.shape[0]
    CW = cw.shape[1]
    RB = 32
    hb = tm // HALO
    nt = S // tm
    last_h = S // HALO - 1

    def body(du_ref, duh_ref, c_ref, ch_ref, ab_ref, abh_ref, cw_ref, lg_ref, lb_ref,
             dab_ref, dcw_ref, dcb_ref, dlg_ref, dlb_ref, ext_u, ext_dc, ph_u, ph_dc, wacc):
        i = pl.program_id(0)

        @pl.when(i == 0)
        def _():
            wacc[...] = jnp.zeros_like(wacc)
            dcb_ref[...] = jnp.zeros_like(dcb_ref)
            dlg_ref[...] = jnp.zeros_like(dlg_ref)
            dlb_ref[...] = jnp.zeros_like(dlb_ref)

        def ln_bwd(cv, du):
            mu = jnp.mean(cv, axis=-1, keepdims=True)
            var = jnp.mean(jnp.square(cv - mu), axis=-1, keepdims=True)
            rstd = lax.rsqrt(var + EPS)
            n = (cv - mu) * rstd
            z = n * lg_ref[...] + lb_ref[...]
            sz = _sigmoid(z)
            dz = du * (sz * (1.0 + z * (1.0 - sz)))
            dn = dz * lg_ref[...]
            dc = rstd * (dn - jnp.mean(dn, axis=-1, keepdims=True)
                         - n * jnp.mean(dn * n, axis=-1, keepdims=True))
            return dc, dz, n

        hv = abh_ref[...]
        ext_u[0:HALO, :] = jnp.where(i > 0, hv[:, :CW] * _sigmoid(hv[:, CW:]), 0.0)
        av = ab_ref[...]
        sb = _sigmoid(av[:, CW:])
        ext_u[HALO:HALO + tm, :] = av[:, :CW] * sb
        ext_u[HALO + tm:, :] = jnp.zeros((8, CW), F32)
        s_lg = s_lb = s_cb = jnp.zeros((1, CW), F32)
        for b in range(tm // RB):
            bs = slice(b * RB, (b + 1) * RB)
            dc, dz, n = ln_bwd(c_ref[bs, :], du_ref[bs, :])
            ext_dc[bs, :] = dc
            s_lg = s_lg + jnp.sum(dz * n, axis=0, keepdims=True)
            s_lb = s_lb + jnp.sum(dz, axis=0, keepdims=True)
            s_cb = s_cb + jnp.sum(dc, axis=0, keepdims=True)
        dch, _, _ = ln_bwd(ch_ref[...], duh_ref[...])
        ext_dc[tm:tm + HALO, :] = jnp.where(i < nt - 1, dch, 0.0)
        ext_dc[tm + HALO:, :] = jnp.zeros((8, CW), F32)
        dlg_ref[...] += s_lg
        dlb_ref[...] += s_lb
        dcb_ref[...] += s_cb
        _phases(ext_u, ph_u, tm + HALO)
        _phases(ext_dc, ph_dc, tm + HALO)

        for rb in range(tm // RB):
            rs = slice(rb * RB, (rb + 1) * RB)
            dcb = ext_dc[rs, :]
            for p, a0, a1, taps in _taps_by_phase([rb * RB + HALO - (CONV_K - 1) + k for k in range(CONV_K)]):
                win = ph_u[p, a0:a1 + RB, :]
                for k, a in taps:
                    prod = dcb * win[a - a0:a - a0 + RB]
                    part = prod[0:8]
                    for j in range(1, RB // 8):
                        part = part + prod[8 * j:8 * j + 8]
                    wacc[k] += part
            acc = jnp.zeros((RB, CW), F32)
            for p, a0, a1, taps in _taps_by_phase([rb * RB + (CONV_K - 1) - k for k in range(CONV_K)]):
                win = ph_dc[p, a0:a1 + RB, :]
                for k, a in taps:
                    acc = acc + cw_ref[k:k + 1, :] * win[a - a0:a - a0 + RB]
            a_r = ab_ref[rs, 0:CW]
            s_r = _sigmoid(ab_ref[rs, CW:2 * CW])
            dab_ref[rs, 0:CW] = (acc * s_r).astype(CDT)
            dab_ref[rs, CW:2 * CW] = (acc * a_r * (s_r * (1.0 - s_r))).astype(CDT)

        @pl.when(i == nt - 1)
        def _():
            for k in range(CONV_K):
                dcw_ref[k:k + 1, :] = jnp.sum(wacc[k], axis=0, keepdims=True)
            dcw_ref[CONV_K:, :] = jnp.zeros((HALO - CONV_K, CW), F32)

    tile = lambda n, j: pl.BlockSpec((tm, n), lambda i: (i, j))
    nxt = lambda n, j: pl.BlockSpec((HALO, n), lambda i: (jnp.minimum((i + 1) * hb, last_h), j))
    return _launch(
        "b_conv" if comm is None else "b_conv_rs", body, (nt,),
        [tile(CW, 0), nxt(CW, 0), tile(CW, 0), nxt(CW, 0),
         tile(2 * CW, 0),
         pl.BlockSpec((HALO, 2 * CW), lambda i: (jnp.maximum(i * hb - 1, 0), 0)),
         _resident((HALO, CW)), _resident((1, CW)), _resident((1, CW))],
        [tile(2 * CW, 0),
         pl.BlockSpec((HALO, CW), lambda i: (0, 0)), pl.BlockSpec((1, CW), lambda i: (0, 0)),
         pl.BlockSpec((1, CW), lambda i: (0, 0)), pl.BlockSpec((1, CW), lambda i: (0, 0))],
        [jax.ShapeDtypeStruct((S, 2 * CW), CDT),
         jax.ShapeDtypeStruct((HALO, CW), F32), jax.ShapeDtypeStruct((1, CW), F32),
         jax.ShapeDtypeStruct((1, CW), F32), jax.ShapeDtypeStruct((1, CW), F32)],
        [pltpu.VMEM((tm + HALO + 8, CW), F32), pltpu.VMEM((tm + HALO + 8, CW), F32),
         pltpu.VMEM((8, tm + HALO, CW), F32), pltpu.VMEM((8, tm + HALO, CW), F32),
         pltpu.VMEM((HALO, 8, CW), F32)],
        (dmix, dmix, c, c, proj, proj, cw, lg, lb), ("arbitrary",), comm)


def _b_ret(dmix, dab, proj, rraw, states, cosT, sinT, Mt, qd, kd, gs, gn, comm=None):
    S = proj.shape[0]
    RW = HEADS * HD
    NB = S // SC
    scale = HD ** -0.5

    def body(dro_ref, dab_ref, q_ref, k_ref, v_ref, g_ref, rraw_ref, st_ref, c_ref, s_ref,
             m_ref, qd_ref, kd_ref, gs_ref, gn_ref, dp_ref, dgn_ref, G):
        @pl.when(pl.program_id(0) == 0)
        def _():
            G[...] = jnp.zeros_like(G)
            dgn_ref[...] = jnp.zeros_like(dgn_ref)

        dp_ref[:, 0:2 * RW] = dab_ref[...]
        cv = c_ref[...]
        sv = s_ref[...]
        for h in range(HEADS):
            cs = slice(h * HD, (h + 1) * HD)
            q = _rot(q_ref[:, cs], cv, sv)
            k = _rot(k_ref[:, cs], cv, sv) * scale
            qb = q.astype(CDT)
            kb = k.astype(CDT)
            vb = v_ref[:, cs].astype(CDT)
            spb = st_ref[0, h]
            r = rraw_ref[:, cs]
            mu = jnp.mean(r, axis=-1, keepdims=True)
            var = jnp.mean(jnp.square(r - mu), axis=-1, keepdims=True)
            rstd = lax.rsqrt(var + EPS)
            n = (r - mu) * rstd
            gv = g_ref[:, cs]
            sg = _sigmoid(gv)
            sil = gv * sg
            dro = dro_ref[:, cs]
            gnv = gn_ref[:, cs]
            dgn_ref[:, cs] += jnp.sum(dro * n * sil, axis=0, keepdims=True)
            dgate = dro * (n * gnv) * (sg * (1.0 + gv * (1.0 - sg)))
            dn = dro * gnv * sil
            dr = rstd * (dn - jnp.mean(dn, axis=-1, keepdims=True)
                         - n * jnp.mean(dn * n, axis=-1, keepdims=True))
            drb = dr.astype(CDT)
            mh = m_ref[h]
            ab = (_dot_nt(qb, kb) * mh).astype(CDT)
            dab_ = (_dot_nt(drb, vb) * mh).astype(CDT)
            qdb = (q * qd_ref[h]).astype(CDT)
            kdb = (k * kd_ref[h]).astype(CDT)
            gc = G[h]
            gb = gc.astype(CDT)
            dq = _dot(dab_, kb) + _dot_nt(drb, spb) * qd_ref[h]
            dk = _dot_tn(dab_, qb) + _dot_nt(vb, gb) * kd_ref[h]
            dv = _dot_tn(ab, drb) + _dot(kdb, gb)
            G[h] = gs_ref[h, 0:1, :] * gc + _dot_tn(qdb, drb)
            dk = dk * scale
            dqp = dq * cv + pltpu.roll(dq * sv, HD // 2, 1)
            dkp = dk * cv + pltpu.roll(dk * sv, HD // 2, 1)
            base = 2 * RW
            dp_ref[:, base + h * HD:base + (h + 1) * HD] = dqp.astype(CDT)
            dp_ref[:, base + RW + h * HD:base + RW + (h + 1) * HD] = dkp.astype(CDT)
            dp_ref[:, base + 2 * RW + h * HD:base + 2 * RW + (h + 1) * HD] = dv.astype(CDT)
            dp_ref[:, base + 3 * RW + h * HD:base + 3 * RW + (h + 1) * HD] = dgate.astype(CDT)

    rev = lambda n, j: pl.BlockSpec((SC, n), lambda i: (NB - 1 - i, j))
    return _launch(
        "b_ret" if comm is None else "b_ret_rs", body, (NB,),
        [rev(RW, 1), rev(2 * RW, 0), rev(RW, 2), rev(RW, 3), rev(RW, 4), rev(RW, 5), rev(RW, 0),
         pl.BlockSpec((1, HEADS, HD, HD), lambda i: (NB - 1 - i, 0, 0, 0)),
         rev(HD, 0), rev(HD, 0),
         _resident((HEADS, SC, SC)), _resident((HEADS, SC, HD)), _resident((HEADS, SC, HD)),
         _resident((HEADS, 8, HD)), _resident((1, RW))],
        [rev(6 * RW, 0), pl.BlockSpec((1, RW), lambda i: (0, 0))],
        [jax.ShapeDtypeStruct((S, 6 * RW), CDT), jax.ShapeDtypeStruct((1, RW), F32)],
        [pltpu.VMEM((HEADS, HD, HD), F32)],
        (dmix, dab, proj, proj, proj, proj, rraw, states, cosT, sinT, Mt, qd, kd, gs, gn), ("arbitrary",), comm)


def _b_in(dproj, G, n_in, x, g1, dxm, tm, comm=None):
    S, D = x.shape
    N = N_DEV * n_in

    def body(dp_ref, w_ref, x_ref, g_ref, dxm_ref, dx_ref, dxb_ref, dg_ref):
        @pl.when(pl.program_id(0) == 0)
        def _():
            dg_ref[...] = jnp.zeros_like(dg_ref)

        dh = _dot(dp_ref[...], w_ref[...].reshape(N, D))
        xv = x_ref[...]
        r = lax.rsqrt(jnp.mean(xv * xv, axis=-1, keepdims=True) + EPS)
        dxn, dgx = _rms_bwd(xv, r, g_ref[...], dh)
        dg_ref[...] += jnp.sum(dgx, axis=0, keepdims=True)
        dx = dxm_ref[...] + dxn
        dx_ref[...] = dx
        dxb_ref[...] = dx.astype(CDT)

    row = lambda n: pl.BlockSpec((tm, n), lambda i: (i, 0))
    return _launch(
        "b_in" if comm is None else "b_in_rs", body, (S // tm,),
        [row(N), _wres(n_in, D, 0), row(D), _resident((1, D)), row(D)],
        [row(D), row(D), pl.BlockSpec((1, D), lambda i: (0, 0))],
        [jax.ShapeDtypeStruct((S, D), F32), jax.ShapeDtypeStruct((S, D), CDT),
         jax.ShapeDtypeStruct((1, D), F32)],
        [], (dproj, G, x, g1, dxm), ("arbitrary",), comm)


def _dw_tn(a, b, tm, tk):
    S, M = a.shape
    N = b.shape[1]
    nk = S // tk

    def body(a_ref, b_ref, o_ref, ob_ref):
        k = pl.program_id(1)

        @pl.when(k == 0)
        def _():
            o_ref[...] = jnp.zeros_like(o_ref)

        o_ref[...] += _dot_tn(a_ref[...], b_ref[...])

        @pl.when(k == nk - 1)
        def _():
            ob_ref[...] = o_ref[...].astype(CDT)

    out = pl.BlockSpec((tm, N), lambda m, k: (m, 0))
    return _call(
        body, name="dw_tn", grid=(M // tm, nk),
        in_specs=[pl.BlockSpec((tk, tm), lambda m, k: (k, m)), pl.BlockSpec((tk, N), lambda m, k: (k, 0))],
        out_specs=[out, out],
        out_shape=[jax.ShapeDtypeStruct((M, N), F32), jax.ShapeDtypeStruct((M, N), CDT)],
        compiler_params=_cp(("parallel", "arbitrary")),
    )(a, b)


def _adamw(w, g, m, v, tr):
    R, C = w.shape
    c1 = 1.0 - ADAM_B1 ** ADAM_STEP
    c2 = 1.0 - ADAM_B2 ** ADAM_STEP

    def body(w_ref, g_ref, m_ref, v_ref, d_ref, mo_ref, vo_ref):
        gv = g_ref[...]
        mn = ADAM_B1 * m_ref[...] + (1.0 - ADAM_B1) * gv
        vn = ADAM_B2 * v_ref[...] + (1.0 - ADAM_B2) * jnp.square(gv)
        mo_ref[...] = mn
        vo_ref[...] = vn
        d_ref[...] = -ADAM_LR * ((mn / c1) / (jnp.sqrt(vn / c2) + ADAM_EPS) + ADAM_WD * w_ref[...])

    blk = pl.BlockSpec((tr, C), lambda i: (i, 0))
    sh = jax.ShapeDtypeStruct((R, C), F32)
    return _call(
        body, name="adamw", grid=(R // tr,),
        in_specs=[blk, blk, blk, blk], out_specs=[blk, blk, blk], out_shape=[sh, sh, sh],
        compiler_params=_cp(("parallel",)),
    )(w, g, m, v)


def _coords():
    return lax.axis_index("x"), lax.axis_index("y"), lax.axis_index("c")


def _peer(x, y, c, d):
    return (x ^ (d >> 2), y ^ ((d >> 1) & 1), c ^ (d & 1))


def _ag_comm(ps):
    K = len(ps)

    def plan(cins, couts, sems):
        send_sems, recv_sems, local_sems = sems
        x, y, c = _coords()
        me, sibling = (x, y, c), (x, y, 1 - c)
        chips = [(1 - x, y), (x, 1 - y), (1 - x, 1 - y)]
        mine, first, passed, got_ici, got_d2d = [], [], [], [], []
        for a in range(K):
            x_ref, out_ref = cins[a], couts[a]
            R = x_ref.shape[0]

            def rows(px, py, pc, out_ref=out_ref, R=R):
                return out_ref.at[pl.ds((4 * px + 2 * py + pc) * R, R), :]

            def copy(k, block, to, src=None, rows=rows, a=a):
                return pltpu.make_async_remote_copy(
                    src_ref=rows(*block) if src is None else src, dst_ref=rows(*block),
                    send_sem=send_sems.at[7 * a + k], recv_sem=recv_sems.at[7 * a + k],
                    device_id=to, device_id_type=MESH)

            mine.append(pltpu.make_async_copy(x_ref, rows(*me), local_sems.at[a]))
            first.append(copy(0, me, sibling, src=x_ref))
            first += [copy(1 + j, me, (*chip, c), src=x_ref) for j, chip in enumerate(chips)]
            passed += [copy(4 + j, (*chip, c), sibling) for j, chip in enumerate(chips)]
            got_ici += [copy(1 + j, (*chip, c), me) for j, chip in enumerate(chips)]
            got_d2d.append(copy(0, sibling, me))
            got_d2d += [copy(4 + j, (*chip, 1 - c), me) for j, chip in enumerate(chips)]
        return mine, first, passed, got_ici, got_d2d

    def start(*a):
        mine, first, _, _, _ = plan(*a)
        for cp in mine + first:
            cp.start()

    def mid(*a):
        _, _, passed, got_ici, _ = plan(*a)
        for got, fwd in zip(got_ici, passed):
            got.wait_recv()
            fwd.start()

    def finish(*a):
        mine, first, passed, _, got_d2d = plan(*a)
        for got in got_d2d:
            got.wait_recv()
        for cp in first + passed:
            cp.wait_send()
        for cp in mine:
            cp.wait()

    return _Comm(ps, [jax.ShapeDtypeStruct((N_DEV * p.shape[0], p.shape[1]), p.dtype) for p in ps],
                 [pltpu.SemaphoreType.DMA((7 * K,)), pltpu.SemaphoreType.DMA((7 * K,)),
                  pltpu.SemaphoreType.DMA((K,))], start, mid, finish)


def _rs_direct_comm(parts):
    K = len(parts)

    def plan(cins, couts, sems):
        send_sems, recv_sems = sems
        x, y, c = _coords()
        cps = []
        for d in range(1, N_DEV):
            px, py, pc = _peer(x, y, c, d)
            for k in range(K):
                s = (d - 1) * K + k
                cps.append(pltpu.make_async_remote_copy(
                    src_ref=cins[k].at[4 * px + 2 * py + pc], dst_ref=couts[k].at[d - 1],
                    send_sem=send_sems.at[s], recv_sem=recv_sems.at[s], device_id=(px, py, pc),
                    device_id_type=MESH))
        return cps

    def start(*a):
        for cp in plan(*a):
            cp.start()

    def finish(*a):
        cps = plan(*a)
        for cp in cps:
            cp.wait_recv()
        for cp in cps:
            cp.wait_send()

    n_sem = (N_DEV - 1) * K
    return _Comm(parts, [jax.ShapeDtypeStruct((N_DEV - 1,) + p.shape[1:], p.dtype) for p in parts],
                 [pltpu.SemaphoreType.DMA((n_sem,)), pltpu.SemaphoreType.DMA((n_sem,))], start, None, finish)


def _sum_all(parts, recv):
    K = len(parts)

    def body(*refs):
        ins, rcv, outs = refs[:K], refs[K:2 * K], refs[2 * K:3 * K]
        bufs, sem = refs[3 * K:4 * K], refs[4 * K]
        x, y, c = _coords()
        me = 4 * x + 2 * y + c
        cps = [pltpu.make_async_copy(ins[k].at[me], bufs[k], sem.at[k]) for k in range(K)]
        for cp in cps:
            cp.start()
        for k in range(K):
            cps[k].wait()
            acc = bufs[k][...]
            for d in range(N_DEV - 1):
                acc = acc + rcv[k][d].astype(F32)
            outs[k][...] = acc

    vm = pl.BlockSpec(memory_space=pltpu.VMEM)
    res = _call(
        body, name="sum_all", in_specs=[_ANY] * K + [vm] * K, out_specs=[vm] * K,
        out_shape=[jax.ShapeDtypeStruct(p.shape[1:], F32) for p in parts],
        scratch_shapes=[pltpu.VMEM(p.shape[1:], F32) for p in parts] + [pltpu.SemaphoreType.DMA((K,))],
        compiler_params=_cp(),
    )(*parts, *recv)
    return list(res)


def _gather_small(v, reduce):
    R, C = v.shape

    def exchange(v_ref, buf, send_sems, recv_sems):
        x, y, c = _coords()
        me = 4 * x + 2 * y + c
        buf[me] = v_ref[...]
        cps = []
        for d in range(1, N_DEV):
            cp = pltpu.make_async_remote_copy(
                src_ref=v_ref, dst_ref=buf.at[me], send_sem=send_sems.at[d - 1], recv_sem=recv_sems.at[d - 1],
                device_id=_peer(x, y, c, d), device_id_type=MESH)
            cp.start()
            cps.append(cp)
        for cp in cps:
            cp.wait_recv()
        for cp in cps:
            cp.wait_send()

    sems = [pltpu.SemaphoreType.DMA((7,)), pltpu.SemaphoreType.DMA((7,))]
    vm = pl.BlockSpec(memory_space=pltpu.VMEM)
    if reduce:
        def body(v_ref, o_ref, buf, send_sems, recv_sems):
            exchange(v_ref, buf, send_sems, recv_sems)
            acc = buf[0]
            for s in range(1, N_DEV):
                acc = acc + buf[s]
            o_ref[...] = acc

        return _call(body, name="allreduce_small", in_specs=[vm], out_specs=vm,
                     out_shape=jax.ShapeDtypeStruct((R, C), F32),
                     scratch_shapes=[pltpu.VMEM((N_DEV, R, C), F32)] + sems)(v)

    def body(v_ref, o_ref, send_sems, recv_sems):
        exchange(v_ref, o_ref, send_sems, recv_sems)

    return _call(body, name="allgather_small", in_specs=[vm], out_specs=vm,
                 out_shape=jax.ShapeDtypeStruct((N_DEV, R, C), F32), scratch_shapes=sems)(v)


def _tables(S):
    half = HD // 2
    pos = jnp.arange(S, dtype=F32)
    freqs = ROPE_BASE ** (-jnp.arange(half, dtype=F32) / half)
    ang = pos[:, None] * freqs[None, :]
    cos, sin = jnp.cos(ang), jnp.sin(ang)
    cosT = jnp.concatenate([cos, cos], axis=-1)
    sinT = jnp.concatenate([-sin, sin], axis=-1)
    log_g = jnp.log(1.0 - 2.0 ** (-5.0 - jnp.arange(HEADS, dtype=F32)))
    idx = jnp.arange(SC, dtype=F32)
    ci = jnp.arange(SC) // CHUNK
    diff = idx[:, None] - idx[None, :]
    same = ci[:, None] == ci[None, :]
    earlier = ci[None, :] < ci[:, None]
    expo = jnp.where(same, jnp.abs(diff), diff)
    Mt = jnp.where((same | earlier)[None], jnp.exp(log_g[:, None, None] * expo[None]), 0.0)
    ones = jnp.ones((1, 1, HD), F32)
    qd = jnp.exp(log_g[:, None] * (idx + 1.0)[None, :])[:, :, None] * ones
    kd = jnp.exp(log_g[:, None] * (SC - 1.0 - idx)[None, :])[:, :, None] * ones
    gs = jnp.exp(log_g * SC)[:, None, None] * jnp.ones((1, 8, HD), F32)
    return cosT, sinT, Mt, qd, kd, gs


def _pad_rows(a, rows):
    return jnp.pad(a, ((0, rows - a.shape[0]), (0, 0)))


def kernel(x, norm1_g, w_in, conv_w, conv_b, conv_ln_g, conv_ln_b, ret_gn_g, w_out, norm2_g, w_gate, w_up, w_down, final_g, loss_target, m_norm1_g, m_w_in, m_conv_w, m_conv_b, m_conv_ln_g, m_conv_ln_b, m_ret_gn_g, m_w_out, m_norm2_g, m_w_gate, m_w_up, m_w_down, m_final_g, v_norm1_g, v_w_in, v_conv_w, v_conv_b, v_conv_ln_g, v_conv_ln_b, v_ret_gn_g, v_w_out, v_norm2_g, v_w_gate, v_w_up, v_w_down, v_final_g):
    L, D, n_in = w_in.shape
    n_out = w_out.shape[1]
    n_ff = w_gate.shape[2]
    S = x.shape[1]
    CW = conv_b.shape[1]
    IN, FF = N_DEV * n_in, N_DEV * n_ff
    ncw = conv_w.shape[2]
    x0 = x.reshape(S, D)
    tgt = loss_target.reshape(S, D)
    TM = min(512, S)
    TKW = min(2048, S)
    TMI = min(512, S)
    TMM = min(256, S)

    assert n_out <= n_ff
    wparts = (0, 1, 2, 0)
    pack_a = jnp.swapaxes(w_in, 1, 2).astype(CDT)
    pack_b = jnp.concatenate([w_out, jnp.zeros((L, n_ff - n_out, D), F32), jnp.swapaxes(w_gate, 1, 2),
                              jnp.swapaxes(w_up, 1, 2)], axis=1).astype(CDT)
    pack_c = w_down.astype(CDT)
    per_dev = lambda g: g.reshape(N_DEV, -1, D)
    Ga = per_dev(_comm_only(_ag_comm([pack_a[0]]), "ag_first")[0])
    Gb = Gc = None

    cwp = conv_w.reshape(L * CONV_K * ncw // 128, 128)
    cw_rows = -(-cwp.shape[0] // 8) * 8
    cwg = _gather_small(_pad_rows(cwp, cw_rows), reduce=False)[:, :cwp.shape[0], :]
    conv_w_full = jnp.moveaxis(cwg.reshape(N_DEV, L, CONV_K, ncw), 0, 2).reshape(L, CONV_K, CW)

    cosT, sinT, Mt, qd, kd, gs = _tables(S)

    saved = []
    xl = x0
    for l in range(L):
        cw = _pad_rows(conv_w_full[l], HALO)
        (h, proj, c, u), got = _f_in_conv(xl, norm1_g[l][None], Ga, n_in, cw, conv_b[l][None], conv_ln_g[l][None],
                                          conv_ln_b[l][None], TMI, _ag_comm([pack_b[0]]) if l == 0 else None)
        if got:
            Gb = per_dev(got[0])
        (rraw, states, mixed), got = _f_ret(proj, u, cosT, sinT, Mt, qd, kd, gs, ret_gn_g[l][None],
                                            _ag_comm([pack_c[0]]) if l == 0 else None)
        if got:
            Gc = per_dev(got[0])
        (xm, h2, gate, up, act, xo), nxt = _f_mlp(
            xl, mixed, norm2_g[l][None], (Gb, Gc), n_out, n_ff, wparts, TMM,
            _ag_comm([pack_a[l + 1], pack_b[l + 1], pack_c[l + 1]]) if l + 1 < L else None)
        saved.append(dict(x=xl, h=h, proj=proj, c=c, rraw=rraw, states=states, mixed=mixed, xm=xm, h2=h2,
                          gate=gate, up=up, act=act, cw=cw, Ga=Ga, Gbc=(Gb, Gc)))
        if nxt:
            Ga, Gb, Gc = (per_dev(g) for g in nxt)
        xl = xo

    dx, dxb, loss_p, dfg = _f_loss(xl, final_g[None], tgt, TM)

    small = []
    own = [None] * L
    recv = [None] * L
    blocks = lambda d: d.reshape(N_DEV, -1, D)
    for l in reversed(range(L)):
        sv = saved[l]
        (dgate, dup, dxm, dxmb, dmix, dg2), _ = _b_mlp(
            dxb, dx, sv["xm"], sv["gate"], sv["up"], norm2_g[l][None], sv["Gbc"], n_out, n_ff, wparts, TMM)
        d_wd, d_wd_b = _dw_tn(sv["act"], dxb, FF // 2, TKW)
        d_wgT, d_wgT_b = _dw_tn(dgate, sv["h2"], FF // 2, TKW)
        d_wuT, d_wuT_b = _dw_tn(dup, sv["h2"], FF // 2, TKW)
        d_wo, d_wo_b = _dw_tn(sv["mixed"], dxmb, D, TKW)
        (dab, dcw, dcb, dlg, dlb), r_go = _b_conv(dmix, sv["c"], sv["proj"], sv["cw"], conv_ln_g[l][None],
                                                  conv_ln_b[l][None], TM,
                                                  _rs_direct_comm([blocks(d_wgT_b), blocks(d_wo_b)]))
        (dproj, dgn), r_ud = _b_ret(dmix, dab, sv["proj"], sv["rraw"], sv["states"], cosT, sinT, Mt, qd, kd, gs,
                                    ret_gn_g[l][None], _rs_direct_comm([blocks(d_wuT_b), blocks(d_wd_b)]))
        d_winT, d_winT_b = _dw_tn(dproj, sv["h"], IN // 2, TKW)
        (dx, dxb, dg1), r_i = _b_in(dproj, sv["Ga"], n_in, sv["x"], norm1_g[l][None], dxm, TMI,
                                    _rs_direct_comm([blocks(d_winT_b)]))
        own[l] = [blocks(d) for d in (d_winT, d_wo, d_wgT, d_wuT, d_wd)]
        recv[l] = [r_i[0], r_go[1], r_go[0], r_ud[0], r_ud[1]]
        small.append(jnp.concatenate([dcw, dcb, dlg, dlb, dgn, dg1.reshape(2, CW), dg2.reshape(2, CW)], axis=0))
    small = small[::-1]
    grad_x = dx.reshape(1, S, D)

    rows_l = HALO + 8
    loss_row = jnp.zeros((1, CW), F32).at[0, 0].set(loss_p[0, 0])
    sm = jnp.concatenate(small + [dfg.reshape(2, CW), loss_row], axis=0)
    sm_rows = -(-sm.shape[0] // 8) * 8
    sm = _gather_small(_pad_rows(sm, sm_rows), reduce=True)
    loss = sm[L * rows_l + 2, 0]
    g_final = sm[L * rows_l:L * rows_l + 2].reshape(D)
    per = sm[:L * rows_l].reshape(L, rows_l, CW)
    me = 4 * lax.axis_index("x") + 2 * lax.axis_index("y") + lax.axis_index("c")
    g_conv_w = lax.dynamic_slice_in_dim(per[:, :CONV_K, :], me * ncw, ncw, axis=2)
    g_conv_b, g_ln_g, g_ln_b, g_gn = per[:, HALO], per[:, HALO + 1], per[:, HALO + 2], per[:, HALO + 3]
    g_n1 = per[:, HALO + 4:HALO + 6].reshape(L, D)
    g_n2 = per[:, HALO + 6:HALO + 8].reshape(L, D)

    gl = [_sum_all(own[l], recv[l]) for l in range(L)]
    g_w_in = jnp.stack([gl[l][0].T for l in range(L)])
    g_w_out = jnp.stack([gl[l][1] for l in range(L)])
    g_w_gate = jnp.stack([gl[l][2].T for l in range(L)])
    g_w_up = jnp.stack([gl[l][3].T for l in range(L)])
    g_w_down = jnp.stack([gl[l][4] for l in range(L)])

    def big(w, g, m, v):
        sh = w.shape
        two = lambda a: a.reshape(-1, sh[-1])
        rows = two(w).shape[0]
        d, mn, vn = _adamw(two(w), two(g), two(m), two(v), rows // 8)
        return d.reshape(sh), mn.reshape(sh), vn.reshape(sh)

    names = ["norm1_g", "conv_w", "conv_b", "conv_ln_g", "conv_ln_b", "ret_gn_g", "norm2_g", "final_g"]
    sw = dict(norm1_g=(norm1_g, g_n1, m_norm1_g, v_norm1_g), conv_w=(conv_w, g_conv_w, m_conv_w, v_conv_w),
              conv_b=(conv_b, g_conv_b, m_conv_b, v_conv_b), conv_ln_g=(conv_ln_g, g_ln_g, m_conv_ln_g, v_conv_ln_g),
              conv_ln_b=(conv_ln_b, g_ln_b, m_conv_ln_b, v_conv_ln_b), ret_gn_g=(ret_gn_g, g_gn, m_ret_gn_g, v_ret_gn_g),
              norm2_g=(norm2_g, g_n2, m_norm2_g, v_norm2_g), final_g=(final_g, g_final, m_final_g, v_final_g))
    lens = [int(math.prod(sw[n][0].shape)) for n in names]
    tot = sum(lens)
    prow = -(-tot // (8 * CW)) * 8

    def packs(j):
        flat = jnp.concatenate([sw[n][j].reshape(-1) for n in names])
        return jnp.pad(flat, (0, prow * CW - tot)).reshape(prow, CW)

    sd, smn, svn = _adamw(packs(0), packs(1), packs(2), packs(3), prow)

    def unpack(a):
        flat = a.reshape(-1)
        out, o = {}, 0
        for n, ln in zip(names, lens):
            out[n] = flat[o:o + ln].reshape(sw[n][0].shape)
            o += ln
        return out

    sd, smn, svn = unpack(sd), unpack(smn), unpack(svn)
    res = {n: (sw[n][1], sd[n], smn[n], svn[n]) for n in names}
    res["w_in"] = (g_w_in,) + big(w_in, g_w_in, m_w_in, v_w_in)
    res["w_out"] = (g_w_out,) + big(w_out, g_w_out, m_w_out, v_w_out)
    res["w_gate"] = (g_w_gate,) + big(w_gate, g_w_gate, m_w_gate, v_w_gate)
    res["w_up"] = (g_w_up,) + big(w_up, g_w_up, m_w_up, v_w_up)
    res["w_down"] = (g_w_down,) + big(w_down, g_w_down, m_w_down, v_w_down)

    order = ["norm1_g", "w_in", "conv_w", "conv_b", "conv_ln_g", "conv_ln_b", "ret_gn_g", "w_out", "norm2_g",
             "w_gate", "w_up", "w_down", "final_g"]
    return (loss, grad_x, *[res[n][0] for n in order], *[res[n][1] for n in order],
            *[res[n][2] for n in order], *[res[n][3] for n in order])
```

```python
import math

import jax
import jax.numpy as jnp
from jax import lax
from jax.experimental import pallas as pl
from jax.experimental.pallas import tpu as pltpu

F32 = jnp.float32
CDT = jnp.bfloat16
EPS = 1e-6
CHUNK = 64
SC = 256
HEADS = 4
HD = 128
CONV_K = 31
HALO = 32
ROPE_BASE = 10000.0
ADAM_LR = 0.001
ADAM_B1 = 0.9
ADAM_B2 = 0.999
ADAM_EPS = 1e-08
ADAM_WD = 0.01
ADAM_STEP = 10
N_DEV = 8
MESH = pl.DeviceIdType.MESH
VMEM_LIMIT = 60 * 1024 * 1024


def _call(body, **kw):
    return pl.pallas_call(body, **kw)


def _cp(sem=None, vmem=VMEM_LIMIT):
    return pltpu.CompilerParams(dimension_semantics=sem, vmem_limit_bytes=vmem)


def _resident(shape):
    nd = len(shape)
    return pl.BlockSpec(shape, lambda *_: (0,) * nd, pipeline_mode=pl.Buffered(1))


def _dot(a, b):
    return jnp.dot(a, b, preferred_element_type=F32)


def _dot_nt(a, b):
    return lax.dot_general(a, b, (((1,), (1,)), ((), ())), preferred_element_type=F32)


def _dot_tn(a, b):
    return lax.dot_general(a, b, (((0,), (0,)), ((), ())), preferred_element_type=F32)


def _sigmoid(x):
    return 1.0 / (1.0 + jnp.exp(-x))


def _rms_bwd(x, r, g, dy):
    xh = x * r
    dyg = dy * g
    dx = r * (dyg - xh * jnp.mean(dyg * xh, axis=-1, keepdims=True))
    return dx, dy * xh


class _Comm:
    def __init__(self, ins, out_shapes, sems, start, mid, finish):
        self.ins, self.out_shapes, self.sems = list(ins), list(out_shapes), list(sems)
        self.start, self.mid, self.finish = start, mid, finish


_ANY = pl.BlockSpec(memory_space=pl.ANY)


def _launch(name, compute, grid, in_specs, out_specs, out_shape, scratch, operands, sem, comm=None):
    n_in, n_out, n_sc = len(in_specs), len(out_specs), len(scratch)
    if comm is None:
        res = _call(compute, name=name, grid=grid, in_specs=in_specs, out_specs=out_specs, out_shape=out_shape,
                    scratch_shapes=scratch, compiler_params=_cp(sem))(*operands)
        return list(res), []
    c_in, c_out = len(comm.ins), len(comm.out_shapes)
    inner = grid[1] if len(grid) > 1 else 1
    steps = grid[0] * inner
    mid_step = (3 * steps) // 4

    def body(*refs):
        ins = refs[:n_in]
        cins = refs[n_in:n_in + c_in]
        o = n_in + c_in
        outs = refs[o:o + n_out]
        couts = refs[o + n_out:o + n_out + c_out]
        o += n_out + c_out
        sc = refs[o:o + n_sc]
        csem = refs[o + n_sc:]
        i = pl.program_id(0)
        if len(grid) > 1:
            i = i * inner + pl.program_id(1)

        @pl.when(i == 0)
        def _():
            comm.start(cins, couts, csem)

        if comm.mid is not None:
            @pl.when(i == mid_step)
            def _():
                comm.mid(cins, couts, csem)

        compute(*ins, *outs, *sc)

        @pl.when(i == steps - 1)
        def _():
            comm.finish(cins, couts, csem)

    res = _call(body, name=name, grid=grid, in_specs=list(in_specs) + [_ANY] * c_in,
                out_specs=list(out_specs) + [_ANY] * c_out, out_shape=list(out_shape) + comm.out_shapes,
                scratch_shapes=list(scratch) + comm.sems,
                compiler_params=_cp(("arbitrary",) * len(grid)))(*operands, *comm.ins)
    return list(res[:n_out]), list(res[n_out:])


def _comm_only(comm, name):
    c_in, c_out = len(comm.ins), len(comm.out_shapes)

    def body(*refs):
        cins, couts, csem = refs[:c_in], refs[c_in:c_in + c_out], refs[c_in + c_out:]
        comm.start(cins, couts, csem)
        if comm.mid is not None:
            comm.mid(cins, couts, csem)
        comm.finish(cins, couts, csem)

    res = _call(body, name=name, in_specs=[_ANY] * c_in, out_specs=[_ANY] * c_out, out_shape=comm.out_shapes,
                scratch_shapes=comm.sems)(*comm.ins)
    return list(res)


def _wres(n, D, part):
    return pl.BlockSpec((N_DEV, n, D), lambda i: (0, part, 0), pipeline_mode=pl.Buffered(1))


def _taps_by_phase(offsets):
    groups = []
    for p in range(8):
        taps = [(k, o - p) for k, o in enumerate(offsets) if o % 8 == p]
        if taps:
            rows = [a for _, a in taps]
            groups.append((p, min(rows), max(rows), taps))
    return groups


def _phases(ext, ph, rows):
    for p in range(8):
        ph[p, :, :] = ext[p:p + rows, :]


def _f_in_conv(x, g, G, n_in, cw, cb, lg, lb, tm, comm=None):
    S, D = x.shape
    N = N_DEV * n_in
    CW = cw.shape[1]
    RB = 64

    def body(x_ref, g_ref, w_ref, cw_ref, cb_ref, lg_ref, lb_ref, h_ref, p_ref, c_ref, u_ref, ext, ph):
        i = pl.program_id(0)

        @pl.when(i == 0)
        def _():
            ext[0:HALO, :] = jnp.zeros((HALO, CW), F32)
            ext[HALO + tm:, :] = jnp.zeros((8, CW), F32)

        @pl.when(i > 0)
        def _():
            ext[0:HALO, :] = ext[tm:tm + HALO, :]

        xv = x_ref[...]
        r = lax.rsqrt(jnp.mean(xv * xv, axis=-1, keepdims=True) + EPS)
        h = ((xv * r) * g_ref[...]).astype(CDT)
        h_ref[...] = h
        w = w_ref[...].reshape(N, D)
        pab = _dot_nt(h, w[0:2 * CW])
        p_ref[:, 0:2 * CW] = pab
        p_ref[:, 2 * CW:] = _dot_nt(h, w[2 * CW:])
        ext[HALO:HALO + tm, :] = pab[:, :CW] * _sigmoid(pab[:, CW:])
        _phases(ext, ph, tm + HALO)
        for rb in range(tm // RB):
            acc = jnp.zeros((RB, CW), F32) + cb_ref[...]
            for k in range(CONV_K):
                o = rb * RB + HALO - (CONV_K - 1) + k
                acc = acc + cw_ref[k:k + 1, :] * ph[o % 8, o - o % 8:o - o % 8 + RB, :]
            c_ref[rb * RB:(rb + 1) * RB, :] = acc
            mu = jnp.mean(acc, axis=-1, keepdims=True)
            var = jnp.mean(jnp.square(acc - mu), axis=-1, keepdims=True)
            z = ((acc - mu) * lax.rsqrt(var + EPS)) * lg_ref[...] + lb_ref[...]
            u_ref[rb * RB:(rb + 1) * RB, :] = (z * _sigmoid(z)).astype(CDT)

    row = lambda n: pl.BlockSpec((tm, n), lambda i: (i, 0))
    return _launch(
        "f_in_conv" if comm is None else "f_in_conv_ag", body, (S // tm,),
        [row(D), _resident((1, D)), _wres(n_in, D, 0),
         _resident((HALO, CW)), _resident((1, CW)), _resident((1, CW)), _resident((1, CW))],
        [row(D), row(N), row(CW), row(CW)],
        [jax.ShapeDtypeStruct((S, D), CDT), jax.ShapeDtypeStruct((S, N), F32),
         jax.ShapeDtypeStruct((S, CW), F32), jax.ShapeDtypeStruct((S, CW), CDT)],
        [pltpu.VMEM((tm + HALO + 8, CW), F32), pltpu.VMEM((8, tm + HALO, CW), F32)],
        (x, g, G, cw, cb, lg, lb), ("arbitrary",), comm)


def _rot(t, c, s):
    return t * c + pltpu.roll(t, HD // 2, 1) * s


def _f_ret(proj, u, cosT, sinT, Mt, qd, kd, gs, gn, comm=None):
    S = proj.shape[0]
    RW = HEADS * HD
    NB = S // SC
    scale = HD ** -0.5

    def body(q_ref, k_ref, v_ref, g_ref, u_ref, c_ref, s_ref, m_ref, qd_ref, kd_ref, gs_ref, gn_ref,
             rraw_ref, st_ref, mix_ref, state):
        @pl.when(pl.program_id(0) == 0)
        def _():
            state[...] = jnp.zeros_like(state)

        mix_ref[:, 0:RW] = u_ref[...]
        cv = c_ref[...]
        sv = s_ref[...]
        for h in range(HEADS):
            cs = slice(h * HD, (h + 1) * HD)
            q = _rot(q_ref[:, cs], cv, sv)
            k = _rot(k_ref[:, cs], cv, sv) * scale
            vb = v_ref[:, cs].astype(CDT)
            qb = q.astype(CDT)
            kb = k.astype(CDT)
            a = _dot_nt(qb, kb) * m_ref[h]
            sp = state[h]
            spb = sp.astype(CDT)
            st_ref[0, h] = spb
            r = _dot(a.astype(CDT), vb) + _dot((q * qd_ref[h]).astype(CDT), spb)
            kv = _dot_tn((k * kd_ref[h]).astype(CDT), vb)
            state[h] = gs_ref[h, 0:1, :] * sp + kv
            rraw_ref[:, cs] = r
            mu = jnp.mean(r, axis=-1, keepdims=True)
            var = jnp.mean(jnp.square(r - mu), axis=-1, keepdims=True)
            n = (r - mu) * lax.rsqrt(var + EPS)
            gv = g_ref[:, cs]
            mix_ref[:, RW + h * HD:RW + (h + 1) * HD] = ((n * gn_ref[:, cs]) * (gv * _sigmoid(gv))).astype(CDT)

    col = lambda j: pl.BlockSpec((SC, RW), lambda i: (i, j))
    return _launch(
        "f_ret" if comm is None else "f_ret_ag", body, (NB,),
        [col(2), col(3), col(4), col(5),
         pl.BlockSpec((SC, RW), lambda i: (i, 0)),
         pl.BlockSpec((SC, HD), lambda i: (i, 0)), pl.BlockSpec((SC, HD), lambda i: (i, 0)),
         _resident((HEADS, SC, SC)), _resident((HEADS, SC, HD)), _resident((HEADS, SC, HD)),
         _resident((HEADS, 8, HD)), _resident((1, RW))],
        [pl.BlockSpec((SC, RW), lambda i: (i, 0)),
         pl.BlockSpec((1, HEADS, HD, HD), lambda i: (i, 0, 0, 0)),
         pl.BlockSpec((SC, 2 * RW), lambda i: (i, 0))],
        [jax.ShapeDtypeStruct((S, RW), F32),
         jax.ShapeDtypeStruct((NB, HEADS, HD, HD), CDT),
         jax.ShapeDtypeStruct((S, 2 * RW), CDT)],
        [pltpu.VMEM((HEADS, HD, HD), F32)],
        (proj, proj, proj, proj, u, cosT, sinT, Mt, qd, kd, gs, gn), ("arbitrary",), comm)


def _f_mlp(x, mixed, g2, G, n_out, n_ff, parts, tm, comm=None):
    S, D = x.shape
    FF = N_DEV * n_ff
    p_wo, p_wg, p_wu, p_wd = parts

    def body(x_ref, m_ref, wo_ref, g_ref, wg_ref, wu_ref, wd_ref,
             xm_ref, h2_ref, gate_ref, up_ref, act_ref, xo_ref):
        xm = x_ref[...] + _dot(m_ref[...], wo_ref[...].reshape(N_DEV * n_out, D))
        xm_ref[...] = xm
        r = lax.rsqrt(jnp.mean(xm * xm, axis=-1, keepdims=True) + EPS)
        h2 = ((xm * r) * g_ref[...]).astype(CDT)
        h2_ref[...] = h2
        gate = _dot_nt(h2, wg_ref[...].reshape(FF, D))
        up = _dot_nt(h2, wu_ref[...].reshape(FF, D))
        gate_ref[...] = gate.astype(CDT)
        up_ref[...] = up.astype(CDT)
        act = ((gate * _sigmoid(gate)) * up).astype(CDT)
        act_ref[...] = act
        xo_ref[...] = xm + _dot(act, wd_ref[...].reshape(FF, D))

    row = lambda n: pl.BlockSpec((tm, n), lambda i: (i, 0))
    return _launch(
        "f_mlp" if comm is None else "f_mlp_ag", body, (S // tm,),
        [row(D), row(D), _wres(n_out, D, p_wo), _resident((1, D)),
         _wres(n_ff, D, p_wg), _wres(n_ff, D, p_wu), _wres(n_ff, D, p_wd)],
        [row(D), row(D), row(FF), row(FF), row(FF), row(D)],
        [jax.ShapeDtypeStruct((S, D), F32), jax.ShapeDtypeStruct((S, D), CDT),
         jax.ShapeDtypeStruct((S, FF), CDT), jax.ShapeDtypeStruct((S, FF), CDT),
         jax.ShapeDtypeStruct((S, FF), CDT), jax.ShapeDtypeStruct((S, D), F32)],
        [], (x, mixed, G[0], g2, G[0], G[0], G[1]), ("parallel",), comm)


def _f_loss(x, fg, tgt, tm):
    S, D = x.shape

    def body(x_ref, g_ref, t_ref, dx_ref, dxb_ref, loss_ref, dg_ref):
        @pl.when(pl.program_id(0) == 0)
        def _():
            loss_ref[...] = jnp.zeros_like(loss_ref)
            dg_ref[...] = jnp.zeros_like(dg_ref)

        xv = x_ref[...]
        r = lax.rsqrt(jnp.mean(xv * xv, axis=-1, keepdims=True) + EPS)
        y = (xv * r) * g_ref[...]
        e = y - t_ref[...]
        loss_ref[...] += 0.5 * jnp.sum(jnp.mean(e * e, axis=-1, keepdims=True))
        dy = e * (1.0 / D)
        dx, dgx = _rms_bwd(xv, r, g_ref[...], dy)
        dg_ref[...] += jnp.sum(dgx, axis=0, keepdims=True)
        dx_ref[...] = dx
        dxb_ref[...] = dx.astype(CDT)

    row = pl.BlockSpec((tm, D), lambda i: (i, 0))
    return _call(
        body, name="f_loss", grid=(S // tm,),
        in_specs=[row, _resident((1, D)), row],
        out_specs=[row, row, pl.BlockSpec((1, 128), lambda i: (0, 0)), pl.BlockSpec((1, D), lambda i: (0, 0))],
        out_shape=[jax.ShapeDtypeStruct((S, D), F32), jax.ShapeDtypeStruct((S, D), CDT),
                   jax.ShapeDtypeStruct((1, 128), F32), jax.ShapeDtypeStruct((1, D), F32)],
        compiler_params=_cp(("arbitrary",)),
    )(x, fg, tgt)


def _b_mlp(dxb, dx, xm, gate, up, g2, G, n_out, n_ff, parts, tm, comm=None):
    S, D = dx.shape
    FF = N_DEV * n_ff
    p_wo, p_wg, p_wu, p_wd = parts

    def body(dxb_ref, dx_ref, xm_ref, gate_ref, up_ref, g_ref, wd_ref, wg_ref, wu_ref, wo_ref,
             dgate_ref, dup_ref, dxm_ref, dxmb_ref, dmix_ref, dg_ref):
        @pl.when(pl.program_id(0) == 0)
        def _():
            dg_ref[...] = jnp.zeros_like(dg_ref)

        dact = _dot_nt(dxb_ref[...], wd_ref[...].reshape(FF, D))
        gate = gate_ref[...].astype(F32)
        up = up_ref[...].astype(F32)
        sg = _sigmoid(gate)
        sil = gate * sg
        dgate = ((dact * up) * (sg * (1.0 + gate * (1.0 - sg)))).astype(CDT)
        dup = (dact * sil).astype(CDT)
        dgate_ref[...] = dgate
        dup_ref[...] = dup
        dh2 = _dot(dgate, wg_ref[...].reshape(FF, D)) + _dot(dup, wu_ref[...].reshape(FF, D))
        xm = xm_ref[...]
        r = lax.rsqrt(jnp.mean(xm * xm, axis=-1, keepdims=True) + EPS)
        dxn, dgx = _rms_bwd(xm, r, g_ref[...], dh2)
        dg_ref[...] += jnp.sum(dgx, axis=0, keepdims=True)
        dxm = dx_ref[...] + dxn
        dxm_ref[...] = dxm
        dxmb = dxm.astype(CDT)
        dxmb_ref[...] = dxmb
        dmix_ref[...] = _dot_nt(dxmb, wo_ref[...].reshape(N_DEV * n_out, D))

    row = lambda n: pl.BlockSpec((tm, n), lambda i: (i, 0))
    return _launch(
        "b_mlp" if comm is None else "b_mlp_rs", body, (S // tm,),
        [row(D), row(D), row(D), row(FF), row(FF), _resident((1, D)),
         _wres(n_ff, D, p_wd), _wres(n_ff, D, p_wg), _wres(n_ff, D, p_wu), _wres(n_out, D, p_wo)],
        [row(FF), row(FF), row(D), row(D), row(D), pl.BlockSpec((1, D), lambda i: (0, 0))],
        [jax.ShapeDtypeStruct((S, FF), CDT), jax.ShapeDtypeStruct((S, FF), CDT),
         jax.ShapeDtypeStruct((S, D), F32), jax.ShapeDtypeStruct((S, D), CDT),
         jax.ShapeDtypeStruct((S, D), F32), jax.ShapeDtypeStruct((1, D), F32)],
        [], (dxb, dx, xm, gate, up, g2, G[1], G[0], G[0], G[0]), ("arbitrary",), comm)


def _b_conv(dmix, c, proj, cw, lg, lb, tm, comm=None):
    S = proj.shape[0]
    CW = cw.shape[1]
    RB = 32
    hb = tm // HALO
    nt = S // tm
    last_h = S // HALO - 1

    def body(du_ref, duh_ref, c_ref, ch_ref, ab_ref, abh_ref, cw_ref, lg_ref, lb_ref,
             dab_ref, dcw_ref, dcb_ref, dlg_ref, dlb_ref, ext_u, ext_dc, ph_u, ph_dc, wacc):
        i = pl.program_id(0)

        @pl.when(i == 0)
        def _():
            wacc[...] = jnp.zeros_like(wacc)
            dcb_ref[...] = jnp.zeros_like(dcb_ref)
            dlg_ref[...] = jnp.zeros_like(dlg_ref)
            dlb_ref[...] = jnp.zeros_like(dlb_ref)

        def ln_bwd(cv, du):
            mu = jnp.mean(cv, axis=-1, keepdims=True)
            var = jnp.mean(jnp.square(cv - mu), axis=-1, keepdims=True)
            rstd = lax.rsqrt(var + EPS)
            n = (cv - mu) * rstd
            z = n * lg_ref[...] + lb_ref[...]
            sz = _sigmoid(z)
            dz = du * (sz * (1.0 + z * (1.0 - sz)))
            dn = dz * lg_ref[...]
            dc = rstd * (dn - jnp.mean(dn, axis=-1, keepdims=True)
                         - n * jnp.mean(dn * n, axis=-1, keepdims=True))
            return dc, dz, n

        hv = abh_ref[...]
        ext_u[0:HALO, :] = jnp.where(i > 0, hv[:, :CW] * _sigmoid(hv[:, CW:]), 0.0)
        av = ab_ref[...]
        sb = _sigmoid(av[:, CW:])
        ext_u[HALO:HALO + tm, :] = av[:, :CW] * sb
        ext_u[HALO + tm:, :] = jnp.zeros((8, CW), F32)
        s_lg = s_lb = s_cb = jnp.zeros((1, CW), F32)
        for b in range(tm // RB):
            bs = slice(b * RB, (b + 1) * RB)
            dc, dz, n = ln_bwd(c_ref[bs, :], du_ref[bs, :])
            ext_dc[bs, :] = dc
            s_lg = s_lg + jnp.sum(dz * n, axis=0, keepdims=True)
            s_lb = s_lb + jnp.sum(dz, axis=0, keepdims=True)
            s_cb = s_cb + jnp.sum(dc, axis=0, keepdims=True)
        dch, _, _ = ln_bwd(ch_ref[...], duh_ref[...])
        ext_dc[tm:tm + HALO, :] = jnp.where(i < nt - 1, dch, 0.0)
        ext_dc[tm + HALO:, :] = jnp.zeros((8, CW), F32)
        dlg_ref[...] += s_lg
        dlb_ref[...] += s_lb
        dcb_ref[...] += s_cb
        _phases(ext_u, ph_u, tm + HALO)
        _phases(ext_dc, ph_dc, tm + HALO)

        for rb in range(tm // RB):
            rs = slice(rb * RB, (rb + 1) * RB)
            dcb = ext_dc[rs, :]
            for p, a0, a1, taps in _taps_by_phase([rb * RB + HALO - (CONV_K - 1) + k for k in range(CONV_K)]):
                win = ph_u[p, a0:a1 + RB, :]
                for k, a in taps:
                    prod = dcb * win[a - a0:a - a0 + RB]
                    part = prod[0:8]
                    for j in range(1, RB // 8):
                        part = part + prod[8 * j:8 * j + 8]
                    wacc[k] += part
            acc = jnp.zeros((RB, CW), F32)
            for p, a0, a1, taps in _taps_by_phase([rb * RB + (CONV_K - 1) - k for k in range(CONV_K)]):
                win = ph_dc[p, a0:a1 + RB, :]
                for k, a in taps:
                    acc = acc + cw_ref[k:k + 1, :] * win[a - a0:a - a0 + RB]
            a_r = ab_ref[rs, 0:CW]
            s_r = _sigmoid(ab_ref[rs, CW:2 * CW])
            dab_ref[rs, 0:CW] = (acc * s_r).astype(CDT)
            dab_ref[rs, CW:2 * CW] = (acc * a_r * (s_r * (1.0 - s_r))).astype(CDT)

        @pl.when(i == nt - 1)
        def _():
            for k in range(CONV_K):
                dcw_ref[k:k + 1, :] = jnp.sum(wacc[k], axis=0, keepdims=True)
            dcw_ref[CONV_K:, :] = jnp.zeros((HALO - CONV_K, CW), F32)

    tile = lambda n, j: pl.BlockSpec((tm, n), lambda i: (i, j))
    nxt = lambda n, j: pl.BlockSpec((HALO, n), lambda i: (jnp.minimum((i + 1) * hb, last_h), j))
    return _launch(
        "b_conv" if comm is None else "b_conv_rs", body, (nt,),
        [tile(CW, 0), nxt(CW, 0), tile(CW, 0), nxt(CW, 0),
         tile(2 * CW, 0),
         pl.BlockSpec((HALO, 2 * CW), lambda i: (jnp.maximum(i * hb - 1, 0), 0)),
         _resident((HALO, CW)), _resident((1, CW)), _resident((1, CW))],
        [tile(2 * CW, 0),
         pl.BlockSpec((HALO, CW), lambda i: (0, 0)), pl.BlockSpec((1, CW), lambda i: (0, 0)),
         pl.BlockSpec((1, CW), lambda i: (0, 0)), pl.BlockSpec((1, CW), lambda i: (0, 0))],
        [jax.ShapeDtypeStruct((S, 2 * CW), CDT),
         jax.ShapeDtypeStruct((HALO, CW), F32), jax.ShapeDtypeStruct((1, CW), F32),
         jax.ShapeDtypeStruct((1, CW), F32), jax.ShapeDtypeStruct((1, CW), F32)],
        [pltpu.VMEM((tm + HALO + 8, CW), F32), pltpu.VMEM((tm + HALO + 8, CW), F32),
         pltpu.VMEM((8, tm + HALO, CW), F32), pltpu.VMEM((8, tm + HALO, CW), F32),
         pltpu.VMEM((HALO, 8, CW), F32)],
        (dmix, dmix, c, c, proj, proj, cw, lg, lb), ("arbitrary",), comm)


def _b_ret(dmix, dab, proj, rraw, states, cosT, sinT, Mt, qd, kd, gs, gn, comm=None):
    S = proj.shape[0]
    RW = HEADS * HD
    NB = S // SC
    scale = HD ** -0.5

    def body(dro_ref, dab_ref, q_ref, k_ref, v_ref, g_ref, rraw_ref, st_ref, c_ref, s_ref,
             m_ref, qd_ref, kd_ref, gs_ref, gn_ref, dp_ref, dgn_ref, G):
        @pl.when(pl.program_id(0) == 0)
        def _():
            G[...] = jnp.zeros_like(G)
            dgn_ref[...] = jnp.zeros_like(dgn_ref)

        dp_ref[:, 0:2 * RW] = dab_ref[...]
        cv = c_ref[...]
        sv = s_ref[...]
        for h in range(HEADS):
            cs = slice(h * HD, (h + 1) * HD)
            q = _rot(q_ref[:, cs], cv, sv)
            k = _rot(k_ref[:, cs], cv, sv) * scale
            qb = q.astype(CDT)
            kb = k.astype(CDT)
            vb = v_ref[:, cs].astype(CDT)
            spb = st_ref[0, h]
            r = rraw_ref[:, cs]
            mu = jnp.mean(r, axis=-1, keepdims=True)
            var = jnp.mean(jnp.square(r - mu), axis=-1, keepdims=True)
            rstd = lax.rsqrt(var + EPS)
            n = (r - mu) * rstd
            gv = g_ref[:, cs]
            sg = _sigmoid(gv)
            sil = gv * sg
            dro = dro_ref[:, cs]
            gnv = gn_ref[:, cs]
            dgn_ref[:, cs] += jnp.sum(dro * n * sil, axis=0, keepdims=True)
            dgate = dro * (n * gnv) * (sg * (1.0 + gv * (1.0 - sg)))
            dn = dro * gnv * sil
            dr = rstd * (dn - jnp.mean(dn, axis=-1, keepdims=True)
                         - n * jnp.mean(dn * n, axis=-1, keepdims=True))
            drb = dr.astype(CDT)
            mh = m_ref[h]
            ab = (_dot_nt(qb, kb) * mh).astype(CDT)
            dab_ = (_dot_nt(drb, vb) * mh).astype(CDT)
            qdb = (q * qd_ref[h]).astype(CDT)
            kdb = (k * kd_ref[h]).astype(CDT)
            gc = G[h]
            gb = gc.astype(CDT)
            dq = _dot(dab_, kb) + _dot_nt(drb, spb) * qd_ref[h]
            dk = _dot_tn(dab_, qb) + _dot_nt(vb, gb) * kd_ref[h]
            dv = _dot_tn(ab, drb) + _dot(kdb, gb)
            G[h] = gs_ref[h, 0:1, :] * gc + _dot_tn(qdb, drb)
            dk = dk * scale
            dqp = dq * cv + pltpu.roll(dq * sv, HD // 2, 1)
            dkp = dk * cv + pltpu.roll(dk * sv, HD // 2, 1)
            base = 2 * RW
            dp_ref[:, base + h * HD:base + (h + 1) * HD] = dqp.astype(CDT)
            dp_ref[:, base + RW + h * HD:base + RW + (h + 1) * HD] = dkp.astype(CDT)
            dp_ref[:, base + 2 * RW + h * HD:base + 2 * RW + (h + 1) * HD] = dv.astype(CDT)
            dp_ref[:, base + 3 * RW + h * HD:base + 3 * RW + (h + 1) * HD] = dgate.astype(CDT)

    rev = lambda n, j: pl.BlockSpec((SC, n), lambda i: (NB - 1 - i, j))
    return _launch(
        "b_ret" if comm is None else "b_ret_rs", body, (NB,),
        [rev(RW, 1), rev(2 * RW, 0), rev(RW, 2), rev(RW, 3), rev(RW, 4), rev(RW, 5), rev(RW, 0),
         pl.BlockSpec((1, HEADS, HD, HD), lambda i: (NB - 1 - i, 0, 0, 0)),
         rev(HD, 0), rev(HD, 0),
         _resident((HEADS, SC, SC)), _resident((HEADS, SC, HD)), _resident((HEADS, SC, HD)),
         _resident((HEADS, 8, HD)), _resident((1, RW))],
        [rev(6 * RW, 0), pl.BlockSpec((1, RW), lambda i: (0, 0))],
        [jax.ShapeDtypeStruct((S, 6 * RW), CDT), jax.ShapeDtypeStruct((1, RW), F32)],
        [pltpu.VMEM((HEADS, HD, HD), F32)],
        (dmix, dab, proj, proj, proj, proj, rraw, states, cosT, sinT, Mt, qd, kd, gs, gn), ("arbitrary",), comm)


def _b_in(dproj, G, n_in, x, g1, dxm, tm, comm=None):
    S, D = x.shape
    N = N_DEV * n_in

    def body(dp_ref, w_ref, x_ref, g_ref, dxm_ref, dx_ref, dxb_ref, dg_ref):
        @pl.when(pl.program_id(0) == 0)
        def _():
            dg_ref[...] = jnp.zeros_like(dg_ref)

        dh = _dot(dp_ref[...], w_ref[...].reshape(N, D))
        xv = x_ref[...]
        r = lax.rsqrt(jnp.mean(xv * xv, axis=-1, keepdims=True) + EPS)
        dxn, dgx = _rms_bwd(xv, r, g_ref[...], dh)
        dg_ref[...] += jnp.sum(dgx, axis=0, keepdims=True)
        dx = dxm_ref[...] + dxn
        dx_ref[...] = dx
        dxb_ref[...] = dx.astype(CDT)

    row = lambda n: pl.BlockSpec((tm, n), lambda i: (i, 0))
    return _launch(
        "b_in" if comm is None else "b_in_rs", body, (S // tm,),
        [row(N), _wres(n_in, D, 0), row(D), _resident((1, D)), row(D)],
        [row(D), row(D), pl.BlockSpec((1, D), lambda i: (0, 0))],
        [jax.ShapeDtypeStruct((S, D), F32), jax.ShapeDtypeStruct((S, D), CDT),
         jax.ShapeDtypeStruct((1, D), F32)],
        [], (dproj, G, x, g1, dxm), ("arbitrary",), comm)


def _dw_tn(a, b, tm, tk):
    S, M = a.shape
    N = b.shape[1]
    nk = S // tk

    def body(a_ref, b_ref, o_ref, ob_ref):
        k = pl.program_id(1)

        @pl.when(k == 0)
        def _():
            o_ref[...] = jnp.zeros_like(o_ref)

        o_ref[...] += _dot_tn(a_ref[...], b_ref[...])

        @pl.when(k == nk - 1)
        def _():
            ob_ref[...] = o_ref[...].astype(CDT)

    out = pl.BlockSpec((tm, N), lambda m, k: (m, 0))
    return _call(
        body, name="dw_tn", grid=(M // tm, nk),
        in_specs=[pl.BlockSpec((tk, tm), lambda m, k: (k, m)), pl.BlockSpec((tk, N), lambda m, k: (k, 0))],
        out_specs=[out, out],
        out_shape=[jax.ShapeDtypeStruct((M, N), F32), jax.ShapeDtypeStruct((M, N), CDT)],
        compiler_params=_cp(("parallel", "arbitrary")),
    )(a, b)


def _adamw(w, g, m, v, tr):
    R, C = w.shape
    c1 = 1.0 - ADAM_B1 ** ADAM_STEP
    c2 = 1.0 - ADAM_B2 ** ADAM_STEP

    def body(w_ref, g_ref, m_ref, v_ref, d_ref, mo_ref, vo_ref):
        gv = g_ref[...]
        mn = ADAM_B1 * m_ref[...] + (1.0 - ADAM_B1) * gv
        vn = ADAM_B2 * v_ref[...] + (1.0 - ADAM_B2) * jnp.square(gv)
        mo_ref[...] = mn
        vo_ref[...] = vn
        d_ref[...] = -ADAM_LR * ((mn / c1) / (jnp.sqrt(vn / c2) + ADAM_EPS) + ADAM_WD * w_ref[...])

    blk = pl.BlockSpec((tr, C), lambda i: (i, 0))
    sh = jax.ShapeDtypeStruct((R, C), F32)
    return _call(
        body, name="adamw", grid=(R // tr,),
        in_specs=[blk, blk, blk, blk], out_specs=[blk, blk, blk], out_shape=[sh, sh, sh],
        compiler_params=_cp(("parallel",)),
    )(w, g, m, v)


def _coords():
    return lax.axis_index("x"), lax.axis_index("y"), lax.axis_index("c")


def _peer(x, y, c, d):
    return (x ^ (d >> 2), y ^ ((d >> 1) & 1), c ^ (d & 1))


def _ag_comm(ps):
    K = len(ps)

    def plan(cins, couts, sems):
        send_sems, recv_sems, local_sems = sems
        x, y, c = _coords()
        me, sibling = (x, y, c), (x, y, 1 - c)
        chips = [(1 - x, y), (x, 1 - y), (1 - x, 1 - y)]
        mine, first, passed, got_ici, got_d2d = [], [], [], [], []
        for a in range(K):
            x_ref, out_ref = cins[a], couts[a]
            R = x_ref.shape[0]

            def rows(px, py, pc, out_ref=out_ref, R=R):
                return out_ref.at[pl.ds((4 * px + 2 * py + pc) * R, R), :]

            def copy(k, block, to, src=None, rows=rows, a=a):
                return pltpu.make_async_remote_copy(
                    src_ref=rows(*block) if src is None else src, dst_ref=rows(*block),
                    send_sem=send_sems.at[7 * a + k], recv_sem=recv_sems.at[7 * a + k],
                    device_id=to, device_id_type=MESH)

            mine.append(pltpu.make_async_copy(x_ref, rows(*me), local_sems.at[a]))
            first.append(copy(0, me, sibling, src=x_ref))
            first += [copy(1 + j, me, (*chip, c), src=x_ref) for j, chip in enumerate(chips)]
            passed += [copy(4 + j, (*chip, c), sibling) for j, chip in enumerate(chips)]
            got_ici += [copy(1 + j, (*chip, c), me) for j, chip in enumerate(chips)]
            got_d2d.append(copy(0, sibling, me))
            got_d2d += [copy(4 + j, (*chip, 1 - c), me) for j, chip in enumerate(chips)]
        return mine, first, passed, got_ici, got_d2d

    def start(*a):
        mine, first, _, _, _ = plan(*a)
        for cp in mine + first:
            cp.start()

    def mid(*a):
        _, _, passed, got_ici, _ = plan(*a)
        for got, fwd in zip(got_ici, passed):
            got.wait_recv()
            fwd.start()

    def finish(*a):
        mine, first, passed, _, got_d2d = plan(*a)
        for got in got_d2d:
            got.wait_recv()
        for cp in first + passed:
            cp.wait_send()
        for cp in mine:
            cp.wait()

    return _Comm(ps, [jax.ShapeDtypeStruct((N_DEV * p.shape[0], p.shape[1]), p.dtype) for p in ps],
                 [pltpu.SemaphoreType.DMA((7 * K,)), pltpu.SemaphoreType.DMA((7 * K,)),
                  pltpu.SemaphoreType.DMA((K,))], start, mid, finish)


def _rs_direct_comm(parts):
    K = len(parts)

    def plan(cins, couts, sems):
        send_sems, recv_sems = sems
        x, y, c = _coords()
        cps = []
        for d in range(1, N_DEV):
            px, py, pc = _peer(x, y, c, d)
            for k in range(K):
                s = (d - 1) * K + k
                cps.append(pltpu.make_async_remote_copy(
                    src_ref=cins[k].at[4 * px + 2 * py + pc], dst_ref=couts[k].at[d - 1],
                    send_sem=send_sems.at[s], recv_sem=recv_sems.at[s], device_id=(px, py, pc),
                    device_id_type=MESH))
        return cps

    def start(*a):
        for cp in plan(*a):
            cp.start()

    def finish(*a):
        cps = plan(*a)
        for cp in cps:
            cp.wait_recv()
        for cp in cps:
            cp.wait_send()

    n_sem = (N_DEV - 1) * K
    return _Comm(parts, [jax.ShapeDtypeStruct((N_DEV - 1,) + p.shape[1:], p.dtype) for p in parts],
                 [pltpu.SemaphoreType.DMA((n_sem,)), pltpu.SemaphoreType.DMA((n_sem,))], start, None, finish)


def _sum_all(parts, recv):
    K = len(parts)

    def body(*refs):
        ins, rcv, outs = refs[:K], refs[K:2 * K], refs[2 * K:3 * K]
        bufs, sem = refs[3 * K:4 * K], refs[4 * K]
        x, y, c = _coords()
        me = 4 * x + 2 * y + c
        cps = [pltpu.make_async_copy(ins[k].at[me], bufs[k], sem.at[k]) for k in range(K)]
        for cp in cps:
            cp.start()
        for k in range(K):
            cps[k].wait()
            acc = bufs[k][...]
            for d in range(N_DEV - 1):
                acc = acc + rcv[k][d].astype(F32)
            outs[k][...] = acc

    vm = pl.BlockSpec(memory_space=pltpu.VMEM)
    res = _call(
        body, name="sum_all", in_specs=[_ANY] * K + [vm] * K, out_specs=[vm] * K,
        out_shape=[jax.ShapeDtypeStruct(p.shape[1:], F32) for p in parts],
        scratch_shapes=[pltpu.VMEM(p.shape[1:], F32) for p in parts] + [pltpu.SemaphoreType.DMA((K,))],
        compiler_params=_cp(),
    )(*parts, *recv)
    return list(res)


def _gather_small(v, reduce):
    R, C = v.shape

    def exchange(v_ref, buf, send_sems, recv_sems):
        x, y, c = _coords()
        me = 4 * x + 2 * y + c
        buf[me] = v_ref[...]
        cps = []
        for d in range(1, N_DEV):
            cp = pltpu.make_async_remote_copy(
                src_ref=v_ref, dst_ref=buf.at[me], send_sem=send_sems.at[d - 1], recv_sem=recv_sems.at[d - 1],
                device_id=_peer(x, y, c, d), device_id_type=MESH)
            cp.start()
            cps.append(cp)
        for cp in cps:
            cp.wait_recv()
        for cp in cps:
            cp.wait_send()

    sems = [pltpu.SemaphoreType.DMA((7,)), pltpu.SemaphoreType.DMA((7,))]
    vm = pl.BlockSpec(memory_space=pltpu.VMEM)
    if reduce:
        def body(v_ref, o_ref, buf, send_sems, recv_sems):
            exchange(v_ref, buf, send_sems, recv_sems)
            acc = buf[0]
            for s in range(1, N_DEV):
                acc = acc + buf[s]
            o_ref[...] = acc

        return _call(body, name="allreduce_small", in_specs=[vm], out_specs=vm,
                     out_shape=jax.ShapeDtypeStruct((R, C), F32),
                     scratch_shapes=[pltpu.VMEM((N_DEV, R, C), F32)] + sems)(v)

    def body(v_ref, o_ref, send_sems, recv_sems):
        exchange(v_ref, o_ref, send_sems, recv_sems)

    return _call(body, name="allgather_small", in_specs=[vm], out_specs=vm,
                 out_shape=jax.ShapeDtypeStruct((N_DEV, R, C), F32), scratch_shapes=sems)(v)


def _tables(S):
    half = HD // 2
    pos = jnp.arange(S, dtype=F32)
    freqs = ROPE_BASE ** (-jnp.arange(half, dtype=F32) / half)
    ang = pos[:, None] * freqs[None, :]
    cos, sin = jnp.cos(ang), jnp.sin(ang)
    cosT = jnp.concatenate([cos, cos], axis=-1)
    sinT = jnp.concatenate([-sin, sin], axis=-1)
    log_g = jnp.log(1.0 - 2.0 ** (-5.0 - jnp.arange(HEADS, dtype=F32)))
    idx = jnp.arange(SC, dtype=F32)
    ci = jnp.arange(SC) // CHUNK
    diff = idx[:, None] - idx[None, :]
    same = ci[:, None] == ci[None, :]
    earlier = ci[None, :] < ci[:, None]
    expo = jnp.where(same, jnp.abs(diff), diff)
    Mt = jnp.where((same | earlier)[None], jnp.exp(log_g[:, None, None] * expo[None]), 0.0)
    ones = jnp.ones((1, 1, HD), F32)
    qd = jnp.exp(log_g[:, None] * (idx + 1.0)[None, :])[:, :, None] * ones
    kd = jnp.exp(log_g[:, None] * (SC - 1.0 - idx)[None, :])[:, :, None] * ones
    gs = jnp.exp(log_g * SC)[:, None, None] * jnp.ones((1, 8, HD), F32)
    return cosT, sinT, Mt, qd, kd, gs


def _pad_rows(a, rows):
    return jnp.pad(a, ((0, rows - a.shape[0]), (0, 0)))


def kernel(x, norm1_g, w_in, conv_w, conv_b, conv_ln_g, conv_ln_b, ret_gn_g, w_out, norm2_g, w_gate, w_up, w_down, final_g, loss_target, m_norm1_g, m_w_in, m_conv_w, m_conv_b, m_conv_ln_g, m_conv_ln_b, m_ret_gn_g, m_w_out, m_norm2_g, m_w_gate, m_w_up, m_w_down, m_final_g, v_norm1_g, v_w_in, v_conv_w, v_conv_b, v_conv_ln_g, v_conv_ln_b, v_ret_gn_g, v_w_out, v_norm2_g, v_w_gate, v_w_up, v_w_down, v_final_g):
    L, D, n_in = w_in.shape
    n_out = w_out.shape[1]
    n_ff = w_gate.shape[2]
    S = x.shape[1]
    CW = conv_b.shape[1]
    IN, FF = N_DEV * n_in, N_DEV * n_ff
    ncw = conv_w.shape[2]
    x0 = x.reshape(S, D)
    tgt = loss_target.reshape(S, D)
    TM = min(512, S)
    TKW = min(2048, S)
    TMI = min(512, S)
    TMM = min(256, S)

    assert n_out <= n_ff
    wparts = (0, 1, 2, 0)
    pack_a = jnp.swapaxes(w_in, 1, 2).astype(CDT)
    pack_b = jnp.concatenate([w_out, jnp.zeros((L, n_ff - n_out, D), F32), jnp.swapaxes(w_gate, 1, 2),
                              jnp.swapaxes(w_up, 1, 2)], axis=1).astype(CDT)
    pack_c = w_down.astype(CDT)
    per_dev = lambda g: g.reshape(N_DEV, -1, D)
    Ga = per_dev(_comm_only(_ag_comm([pack_a[0]]), "ag_first")[0])
    Gb = Gc = None

    cwp = conv_w.reshape(L * CONV_K * ncw // 128, 128)
    cw_rows = -(-cwp.shape[0] // 8) * 8
    cwg = _gather_small(_pad_rows(cwp, cw_rows), reduce=False)[:, :cwp.shape[0], :]
    conv_w_full = jnp.moveaxis(cwg.reshape(N_DEV, L, CONV_K, ncw), 0, 2).reshape(L, CONV_K, CW)

    cosT, sinT, Mt, qd, kd, gs = _tables(S)

    saved = []
    xl = x0
    for l in range(L):
        cw = _pad_rows(conv_w_full[l], HALO)
        (h, proj, c, u), got = _f_in_conv(xl, norm1_g[l][None], Ga, n_in, cw, conv_b[l][None], conv_ln_g[l][None],
                                          conv_ln_b[l][None], TMI, _ag_comm([pack_b[0]]) if l == 0 else None)
        if got:
            Gb = per_dev(got[0])
        more = l + 1 < L
        ret_gather = ([pack_c[0]] if l == 0 else []) + ([pack_a[l + 1]] if more else [])
        (rraw, states, mixed), got = _f_ret(proj, u, cosT, sinT, Mt, qd, kd, gs, ret_gn_g[l][None],
                                            _ag_comm(ret_gather) if ret_gather else None)
        if l == 0:
            Gc = per_dev(got[0])
        (xm, h2, gate, up, act, xo), nxt = _f_mlp(
            xl, mixed, norm2_g[l][None], (Gb, Gc), n_out, n_ff, wparts, TMM,
            _ag_comm([pack_b[l + 1], pack_c[l + 1]]) if more else None)
        saved.append(dict(x=xl, h=h, proj=proj, c=c, rraw=rraw, states=states, mixed=mixed, xm=xm, h2=h2,
                          gate=gate, up=up, act=act, cw=cw, Ga=Ga, Gbc=(Gb, Gc)))
        if more:
            Ga, Gb, Gc = per_dev(got[-1]), per_dev(nxt[0]), per_dev(nxt[1])
        xl = xo

    dx, dxb, loss_p, dfg = _f_loss(xl, final_g[None], tgt, TM)

    small = []
    own = [None] * L
    recv = [None] * L
    blocks = lambda d: d.reshape(N_DEV, -1, D)
    for l in reversed(range(L)):
        sv = saved[l]
        (dgate, dup, dxm, dxmb, dmix, dg2), _ = _b_mlp(
            dxb, dx, sv["xm"], sv["gate"], sv["up"], norm2_g[l][None], sv["Gbc"], n_out, n_ff, wparts, TMM)
        d_wd, d_wd_b = _dw_tn(sv["act"], dxb, FF // 2, TKW)
        d_wgT, d_wgT_b = _dw_tn(dgate, sv["h2"], FF // 2, TKW)
        d_wuT, d_wuT_b = _dw_tn(dup, sv["h2"], FF // 2, TKW)
        d_wo, d_wo_b = _dw_tn(sv["mixed"], dxmb, D, TKW)
        (dab, dcw, dcb, dlg, dlb), r_go = _b_conv(dmix, sv["c"], sv["proj"], sv["cw"], conv_ln_g[l][None],
                                                  conv_ln_b[l][None], TM,
                                                  _rs_direct_comm([blocks(d_wgT_b), blocks(d_wo_b)]))
        (dproj, dgn), r_ud = _b_ret(dmix, dab, sv["proj"], sv["rraw"], sv["states"], cosT, sinT, Mt, qd, kd, gs,
                                    ret_gn_g[l][None], _rs_direct_comm([blocks(d_wuT_b), blocks(d_wd_b)]))
        d_winT, d_winT_b = _dw_tn(dproj, sv["h"], IN // 2, TKW)
        (dx, dxb, dg1), r_i = _b_in(dproj, sv["Ga"], n_in, sv["x"], norm1_g[l][None], dxm, TMI,
                                    _rs_direct_comm([blocks(d_winT_b)]))
        own[l] = [blocks(d) for d in (d_winT, d_wo, d_wgT, d_wuT, d_wd)]
        recv[l] = [r_i[0], r_go[1], r_go[0], r_ud[0], r_ud[1]]
        small.append(jnp.concatenate([dcw, dcb, dlg, dlb, dgn, dg1.reshape(2, CW), dg2.reshape(2, CW)], axis=0))
    small = small[::-1]
    grad_x = dx.reshape(1, S, D)

    rows_l = HALO + 8
    loss_row = jnp.zeros((1, CW), F32).at[0, 0].set(loss_p[0, 0])
    sm = jnp.concatenate(small + [dfg.reshape(2, CW), loss_row], axis=0)
    sm_rows = -(-sm.shape[0] // 8) * 8
    sm = _gather_small(_pad_rows(sm, sm_rows), reduce=True)
    loss = sm[L * rows_l + 2, 0]
    g_final = sm[L * rows_l:L * rows_l + 2].reshape(D)
    per = sm[:L * rows_l].reshape(L, rows_l, CW)
    me = 4 * lax.axis_index("x") + 2 * lax.axis_index("y") + lax.axis_index("c")
    g_conv_w = lax.dynamic_slice_in_dim(per[:, :CONV_K, :], me * ncw, ncw, axis=2)
    g_conv_b, g_ln_g, g_ln_b, g_gn = per[:, HALO], per[:, HALO + 1], per[:, HALO + 2], per[:, HALO + 3]
    g_n1 = per[:, HALO + 4:HALO + 6].reshape(L, D)
    g_n2 = per[:, HALO + 6:HALO + 8].reshape(L, D)

    gl = [_sum_all(own[l], recv[l]) for l in range(L)]
    g_w_in = jnp.stack([gl[l][0].T for l in range(L)])
    g_w_out = jnp.stack([gl[l][1] for l in range(L)])
    g_w_gate = jnp.stack([gl[l][2].T for l in range(L)])
    g_w_up = jnp.stack([gl[l][3].T for l in range(L)])
    g_w_down = jnp.stack([gl[l][4] for l in range(L)])

    def big(w, g, m, v):
        sh = w.shape
        two = lambda a: a.reshape(-1, sh[-1])
        rows = two(w).shape[0]
        d, mn, vn = _adamw(two(w), two(g), two(m), two(v), rows // 8)
        return d.reshape(sh), mn.reshape(sh), vn.reshape(sh)

    names = ["norm1_g", "conv_w", "conv_b", "conv_ln_g", "conv_ln_b", "ret_gn_g", "norm2_g", "final_g"]
    sw = dict(norm1_g=(norm1_g, g_n1, m_norm1_g, v_norm1_g), conv_w=(conv_w, g_conv_w, m_conv_w, v_conv_w),
              conv_b=(conv_b, g_conv_b, m_conv_b, v_conv_b), conv_ln_g=(conv_ln_g, g_ln_g, m_conv_ln_g, v_conv_ln_g),
              conv_ln_b=(conv_ln_b, g_ln_b, m_conv_ln_b, v_conv_ln_b), ret_gn_g=(ret_gn_g, g_gn, m_ret_gn_g, v_ret_gn_g),
              norm2_g=(norm2_g, g_n2, m_norm2_g, v_norm2_g), final_g=(final_g, g_final, m_final_g, v_final_g))
    lens = [int(math.prod(sw[n][0].shape)) for n in names]
    tot = sum(lens)
    prow = -(-tot // (8 * CW)) * 8

    def packs(j):
        flat = jnp.concatenate([sw[n][j].reshape(-1) for n in names])
        return jnp.pad(flat, (0, prow * CW - tot)).reshape(prow, CW)

    sd, smn, svn = _adamw(packs(0), packs(1), packs(2), packs(3), prow)

    def unpack(a):
        flat = a.reshape(-1)
        out, o = {}, 0
        for n, ln in zip(names, lens):
            out[n] = flat[o:o + ln].reshape(sw[n][0].shape)
            o += ln
        return out

    sd, smn, svn = unpack(sd), unpack(smn), unpack(svn)
    res = {n: (sw[n][1], sd[n], smn[n], svn[n]) for n in names}
    res["w_in"] = (g_w_in,) + big(w_in, g_w_in, m_w_in, v_w_in)
    res["w_out"] = (g_w_out,) + big(w_out, g_w_out, m_w_out, v_w_out)
    res["w_gate"] = (g_w_gate,) + big(w_gate, g_w_gate, m_w_gate, v_w_gate)
    res["w_up"] = (g_w_up,) + big(w_up, g_w_up, m_w_up, v_w_up)
    res["w_down"] = (g_w_down,) + big(w_down, g_w_down, m_w_down, v_w_down)

    order = ["norm1_g", "w_in", "conv_w", "conv_b", "conv_ln_g", "conv_ln_b", "ret_gn_g", "w_out", "norm2_g",
             "w_gate", "w_up", "w_down", "final_g"]
    return (loss, grad_x, *[res[n][0] for n in order], *[res[n][1] for n in order],
            *[res[n][2] for n in order], *[res[n][3] for n in order])
```

```python
import math

import jax
import jax.numpy as jnp
from jax import lax
from jax.experimental import pallas as pl
from jax.experimental.pallas import tpu as pltpu

F32 = jnp.float32
CDT = jnp.bfloat16
EPS = 1e-6
CHUNK = 64
SC = 256
HEADS = 4
HD = 128
CONV_K = 31
HALO = 32
ROPE_BASE = 10000.0
ADAM_LR = 0.001
ADAM_B1 = 0.9
ADAM_B2 = 0.999
ADAM_EPS = 1e-08
ADAM_WD = 0.01
ADAM_STEP = 10
N_DEV = 8
MESH = pl.DeviceIdType.MESH
VMEM_LIMIT = 60 * 1024 * 1024


def _call(body, **kw):
    return pl.pallas_call(body, **kw)


def _cp(sem=None, vmem=VMEM_LIMIT):
    return pltpu.CompilerParams(dimension_semantics=sem, vmem_limit_bytes=vmem)


def _resident(shape):
    nd = len(shape)
    return pl.BlockSpec(shape, lambda *_: (0,) * nd, pipeline_mode=pl.Buffered(1))


def _dot(a, b):
    return jnp.dot(a, b, preferred_element_type=F32)


def _dot_nt(a, b):
    return lax.dot_general(a, b, (((1,), (1,)), ((), ())), preferred_element_type=F32)


def _dot_tn(a, b):
    return lax.dot_general(a, b, (((0,), (0,)), ((), ())), preferred_element_type=F32)


def _sigmoid(x):
    return 1.0 / (1.0 + jnp.exp(-x))


def _rms_bwd(x, r, g, dy):
    xh = x * r
    dyg = dy * g
    dx = r * (dyg - xh * jnp.mean(dyg * xh, axis=-1, keepdims=True))
    return dx, dy * xh


class _Comm:
    def __init__(self, ins, out_shapes, sems, start, mid, finish):
        self.ins, self.out_shapes, self.sems = list(ins), list(out_shapes), list(sems)
        self.start, self.mid, self.finish = start, mid, finish


_ANY = pl.BlockSpec(memory_space=pl.ANY)


def _launch(name, compute, grid, in_specs, out_specs, out_shape, scratch, operands, sem, comm=None):
    n_in, n_out, n_sc = len(in_specs), len(out_specs), len(scratch)
    if comm is None:
        res = _call(compute, name=name, grid=grid, in_specs=in_specs, out_specs=out_specs, out_shape=out_shape,
                    scratch_shapes=scratch, compiler_params=_cp(sem))(*operands)
        return list(res), []
    c_in, c_out = len(comm.ins), len(comm.out_shapes)
    inner = grid[1] if len(grid) > 1 else 1
    steps = grid[0] * inner
    mid_step = (3 * steps) // 4

    def body(*refs):
        ins = refs[:n_in]
        cins = refs[n_in:n_in + c_in]
        o = n_in + c_in
        outs = refs[o:o + n_out]
        couts = refs[o + n_out:o + n_out + c_out]
        o += n_out + c_out
        sc = refs[o:o + n_sc]
        csem = refs[o + n_sc:]
        i = pl.program_id(0)
        if len(grid) > 1:
            i = i * inner + pl.program_id(1)

        @pl.when(i == 0)
        def _():
            comm.start(cins, couts, csem)

        if comm.mid is not None:
            @pl.when(i == mid_step)
            def _():
                comm.mid(cins, couts, csem)

        compute(*ins, *outs, *sc)

        @pl.when(i == steps - 1)
        def _():
            comm.finish(cins, couts, csem)

    res = _call(body, name=name, grid=grid, in_specs=list(in_specs) + [_ANY] * c_in,
                out_specs=list(out_specs) + [_ANY] * c_out, out_shape=list(out_shape) + comm.out_shapes,
                scratch_shapes=list(scratch) + comm.sems,
                compiler_params=_cp(("arbitrary",) * len(grid)))(*operands, *comm.ins)
    return list(res[:n_out]), list(res[n_out:])


def _comm_only(comm, name):
    c_in, c_out = len(comm.ins), len(comm.out_shapes)

    def body(*refs):
        cins, couts, csem = refs[:c_in], refs[c_in:c_in + c_out], refs[c_in + c_out:]
        comm.start(cins, couts, csem)
        if comm.mid is not None:
            comm.mid(cins, couts, csem)
        comm.finish(cins, couts, csem)

    res = _call(body, name=name, in_specs=[_ANY] * c_in, out_specs=[_ANY] * c_out, out_shape=comm.out_shapes,
                scratch_shapes=comm.sems)(*comm.ins)
    return list(res)


def _wres(n, D, part):
    return pl.BlockSpec((N_DEV, n, D), lambda i: (0, part, 0), pipeline_mode=pl.Buffered(1))


def _taps_by_phase(offsets):
    groups = []
    for p in range(8):
        taps = [(k, o - p) for k, o in enumerate(offsets) if o % 8 == p]
        if taps:
            rows = [a for _, a in taps]
            groups.append((p, min(rows), max(rows), taps))
    return groups


def _phases(ext, ph, rows):
    for p in range(8):
        ph[p, :, :] = ext[p:p + rows, :]


def _f_in_conv(x, g, G, n_in, cw, cb, lg, lb, tm, comm=None):
    S, D = x.shape
    N = N_DEV * n_in
    CW = cw.shape[1]
    RB = 64

    def body(x_ref, g_ref, w_ref, cw_ref, cb_ref, lg_ref, lb_ref, h_ref, p_ref, c_ref, u_ref, ext, ph):
        i = pl.program_id(0)

        @pl.when(i == 0)
        def _():
            ext[0:HALO, :] = jnp.zeros((HALO, CW), F32)
            ext[HALO + tm:, :] = jnp.zeros((8, CW), F32)

        @pl.when(i > 0)
        def _():
            ext[0:HALO, :] = ext[tm:tm + HALO, :]

        xv = x_ref[...]
        r = lax.rsqrt(jnp.mean(xv * xv, axis=-1, keepdims=True) + EPS)
        h = ((xv * r) * g_ref[...]).astype(CDT)
        h_ref[...] = h
        w = w_ref[...].reshape(N, D)
        pab = _dot_nt(h, w[0:2 * CW])
        p_ref[:, 0:2 * CW] = pab
        p_ref[:, 2 * CW:] = _dot_nt(h, w[2 * CW:])
        ext[HALO:HALO + tm, :] = pab[:, :CW] * _sigmoid(pab[:, CW:])
        _phases(ext, ph, tm + HALO)
        for rb in range(tm // RB):
            acc = jnp.zeros((RB, CW), F32) + cb_ref[...]
            for k in range(CONV_K):
                o = rb * RB + HALO - (CONV_K - 1) + k
                acc = acc + cw_ref[k:k + 1, :] * ph[o % 8, o - o % 8:o - o % 8 + RB, :]
            c_ref[rb * RB:(rb + 1) * RB, :] = acc
            mu = jnp.mean(acc, axis=-1, keepdims=True)
            var = jnp.mean(jnp.square(acc - mu), axis=-1, keepdims=True)
            z = ((acc - mu) * lax.rsqrt(var + EPS)) * lg_ref[...] + lb_ref[...]
            u_ref[rb * RB:(rb + 1) * RB, :] = (z * _sigmoid(z)).astype(CDT)

    row = lambda n: pl.BlockSpec((tm, n), lambda i: (i, 0))
    return _launch(
        "f_in_conv" if comm is None else "f_in_conv_ag", body, (S // tm,),
        [row(D), _resident((1, D)), _wres(n_in, D, 0),
         _resident((HALO, CW)), _resident((1, CW)), _resident((1, CW)), _resident((1, CW))],
        [row(D), row(N), row(CW), row(CW)],
        [jax.ShapeDtypeStruct((S, D), CDT), jax.ShapeDtypeStruct((S, N), F32),
         jax.ShapeDtypeStruct((S, CW), F32), jax.ShapeDtypeStruct((S, CW), CDT)],
        [pltpu.VMEM((tm + HALO + 8, CW), F32), pltpu.VMEM((8, tm + HALO, CW), F32)],
        (x, g, G, cw, cb, lg, lb), ("arbitrary",), comm)


def _rot(t, c, s):
    return t * c + pltpu.roll(t, HD // 2, 1) * s


def _f_ret(proj, u, cosT, sinT, Mt, qd, kd, gs, gn, comm=None):
    S = proj.shape[0]
    RW = HEADS * HD
    NB = S // SC
    scale = HD ** -0.5

    def body(q_ref, k_ref, v_ref, g_ref, u_ref, c_ref, s_ref, m_ref, qd_ref, kd_ref, gs_ref, gn_ref,
             rraw_ref, st_ref, mix_ref, state):
        @pl.when(pl.program_id(0) == 0)
        def _():
            state[...] = jnp.zeros_like(state)

        mix_ref[:, 0:RW] = u_ref[...]
        cv = c_ref[...]
        sv = s_ref[...]
        for h in range(HEADS):
            cs = slice(h * HD, (h + 1) * HD)
            q = _rot(q_ref[:, cs], cv, sv)
            k = _rot(k_ref[:, cs], cv, sv) * scale
            vb = v_ref[:, cs].astype(CDT)
            qb = q.astype(CDT)
            kb = k.astype(CDT)
            a = _dot_nt(qb, kb) * m_ref[h]
            sp = state[h]
            spb = sp.astype(CDT)
            st_ref[0, h] = spb
            r = _dot(a.astype(CDT), vb) + _dot((q * qd_ref[h]).astype(CDT), spb)
            kv = _dot_tn((k * kd_ref[h]).astype(CDT), vb)
            state[h] = gs_ref[h, 0:1, :] * sp + kv
            rraw_ref[:, cs] = r
            mu = jnp.mean(r, axis=-1, keepdims=True)
            var = jnp.mean(jnp.square(r - mu), axis=-1, keepdims=True)
            n = (r - mu) * lax.rsqrt(var + EPS)
            gv = g_ref[:, cs]
            mix_ref[:, RW + h * HD:RW + (h + 1) * HD] = ((n * gn_ref[:, cs]) * (gv * _sigmoid(gv))).astype(CDT)

    col = lambda j: pl.BlockSpec((SC, RW), lambda i: (i, j))
    return _launch(
        "f_ret" if comm is None else "f_ret_ag", body, (NB,),
        [col(2), col(3), col(4), col(5),
         pl.BlockSpec((SC, RW), lambda i: (i, 0)),
         pl.BlockSpec((SC, HD), lambda i: (i, 0)), pl.BlockSpec((SC, HD), lambda i: (i, 0)),
         _resident((HEADS, SC, SC)), _resident((HEADS, SC, HD)), _resident((HEADS, SC, HD)),
         _resident((HEADS, 8, HD)), _resident((1, RW))],
        [pl.BlockSpec((SC, RW), lambda i: (i, 0)),
         pl.BlockSpec((1, HEADS, HD, HD), lambda i: (i, 0, 0, 0)),
         pl.BlockSpec((SC, 2 * RW), lambda i: (i, 0))],
        [jax.ShapeDtypeStruct((S, RW), F32),
         jax.ShapeDtypeStruct((NB, HEADS, HD, HD), CDT),
         jax.ShapeDtypeStruct((S, 2 * RW), CDT)],
        [pltpu.VMEM((HEADS, HD, HD), F32)],
        (proj, proj, proj, proj, u, cosT, sinT, Mt, qd, kd, gs, gn), ("arbitrary",), comm)


def _f_mlp(x, mixed, g2, G, n_out, n_ff, parts, tm, comm=None):
    S, D = x.shape
    FF = N_DEV * n_ff
    p_wo, p_wg, p_wu, p_wd = parts

    def body(x_ref, m_ref, wo_ref, g_ref, wg_ref, wu_ref, wd_ref,
             xm_ref, h2_ref, gate_ref, up_ref, act_ref, xo_ref):
        xm = x_ref[...] + _dot(m_ref[...], wo_ref[...].reshape(N_DEV * n_out, D))
        xm_ref[...] = xm
        r = lax.rsqrt(jnp.mean(xm * xm, axis=-1, keepdims=True) + EPS)
        h2 = ((xm * r) * g_ref[...]).astype(CDT)
        h2_ref[...] = h2
        gate = _dot_nt(h2, wg_ref[...].reshape(FF, D))
        up = _dot_nt(h2, wu_ref[...].reshape(FF, D))
        gate_ref[...] = gate.astype(CDT)
        up_ref[...] = up.astype(CDT)
        act = ((gate * _sigmoid(gate)) * up).astype(CDT)
        act_ref[...] = act
        xo_ref[...] = xm + _dot(act, wd_ref[...].reshape(FF, D))

    row = lambda n: pl.BlockSpec((tm, n), lambda i: (i, 0))
    return _launch(
        "f_mlp" if comm is None else "f_mlp_ag", body, (S // tm,),
        [row(D), row(D), _wres(n_out, D, p_wo), _resident((1, D)),
         _wres(n_ff, D, p_wg), _wres(n_ff, D, p_wu), _wres(n_ff, D, p_wd)],
        [row(D), row(D), row(FF), row(FF), row(FF), row(D)],
        [jax.ShapeDtypeStruct((S, D), F32), jax.ShapeDtypeStruct((S, D), CDT),
         jax.ShapeDtypeStruct((S, FF), CDT), jax.ShapeDtypeStruct((S, FF), CDT),
         jax.ShapeDtypeStruct((S, FF), CDT), jax.ShapeDtypeStruct((S, D), F32)],
        [], (x, mixed, G[0], g2, G[0], G[0], G[1]), ("parallel",), comm)


def _f_loss(x, fg, tgt, tm):
    S, D = x.shape

    def body(x_ref, g_ref, t_ref, dx_ref, dxb_ref, loss_ref, dg_ref):
        @pl.when(pl.program_id(0) == 0)
        def _():
            loss_ref[...] = jnp.zeros_like(loss_ref)
            dg_ref[...] = jnp.zeros_like(dg_ref)

        xv = x_ref[...]
        r = lax.rsqrt(jnp.mean(xv * xv, axis=-1, keepdims=True) + EPS)
        y = (xv * r) * g_ref[...]
        e = y - t_ref[...]
        loss_ref[...] += 0.5 * jnp.sum(jnp.mean(e * e, axis=-1, keepdims=True))
        dy = e * (1.0 / D)
        dx, dgx = _rms_bwd(xv, r, g_ref[...], dy)
        dg_ref[...] += jnp.sum(dgx, axis=0, keepdims=True)
        dx_ref[...] = dx
        dxb_ref[...] = dx.astype(CDT)

    row = pl.BlockSpec((tm, D), lambda i: (i, 0))
    return _call(
        body, name="f_loss", grid=(S // tm,),
        in_specs=[row, _resident((1, D)), row],
        out_specs=[row, row, pl.BlockSpec((1, 128), lambda i: (0, 0)), pl.BlockSpec((1, D), lambda i: (0, 0))],
        out_shape=[jax.ShapeDtypeStruct((S, D), F32), jax.ShapeDtypeStruct((S, D), CDT),
                   jax.ShapeDtypeStruct((1, 128), F32), jax.ShapeDtypeStruct((1, D), F32)],
        compiler_params=_cp(("arbitrary",)),
    )(x, fg, tgt)


def _b_mlp(dxb, dx, xm, gate, up, g2, G, n_out, n_ff, parts, tm, comm=None):
    S, D = dx.shape
    FF = N_DEV * n_ff
    p_wo, p_wg, p_wu, p_wd = parts

    def body(dxb_ref, dx_ref, xm_ref, gate_ref, up_ref, g_ref, wd_ref, wg_ref, wu_ref, wo_ref,
             dgate_ref, dup_ref, dxm_ref, dxmb_ref, dmix_ref, dg_ref):
        @pl.when(pl.program_id(0) == 0)
        def _():
            dg_ref[...] = jnp.zeros_like(dg_ref)

        dact = _dot_nt(dxb_ref[...], wd_ref[...].reshape(FF, D))
        gate = gate_ref[...].astype(F32)
        up = up_ref[...].astype(F32)
        sg = _sigmoid(gate)
        sil = gate * sg
        dgate = ((dact * up) * (sg * (1.0 + gate * (1.0 - sg)))).astype(CDT)
        dup = (dact * sil).astype(CDT)
        dgate_ref[...] = dgate
        dup_ref[...] = dup
        dh2 = _dot(dgate, wg_ref[...].reshape(FF, D)) + _dot(dup, wu_ref[...].reshape(FF, D))
        xm = xm_ref[...]
        r = lax.rsqrt(jnp.mean(xm * xm, axis=-1, keepdims=True) + EPS)
        dxn, dgx = _rms_bwd(xm, r, g_ref[...], dh2)
        dg_ref[...] += jnp.sum(dgx, axis=0, keepdims=True)
        dxm = dx_ref[...] + dxn
        dxm_ref[...] = dxm
        dxmb = dxm.astype(CDT)
        dxmb_ref[...] = dxmb
        dmix_ref[...] = _dot_nt(dxmb, wo_ref[...].reshape(N_DEV * n_out, D))

    row = lambda n: pl.BlockSpec((tm, n), lambda i: (i, 0))
    return _launch(
        "b_mlp" if comm is None else "b_mlp_rs", body, (S // tm,),
        [row(D), row(D), row(D), row(FF), row(FF), _resident((1, D)),
         _wres(n_ff, D, p_wd), _wres(n_ff, D, p_wg), _wres(n_ff, D, p_wu), _wres(n_out, D, p_wo)],
        [row(FF), row(FF), row(D), row(D), row(D), pl.BlockSpec((1, D), lambda i: (0, 0))],
        [jax.ShapeDtypeStruct((S, FF), CDT), jax.ShapeDtypeStruct((S, FF), CDT),
         jax.ShapeDtypeStruct((S, D), F32), jax.ShapeDtypeStruct((S, D), CDT),
         jax.ShapeDtypeStruct((S, D), F32), jax.ShapeDtypeStruct((1, D), F32)],
        [], (dxb, dx, xm, gate, up, g2, G[1], G[0], G[0], G[0]), ("arbitrary",), comm)


def _b_conv(dmix, c, proj, cw, lg, lb, tm, comm=None):
    S = proj.shape[0]
    CW = cw.shape[1]
    RB = 32
    hb = tm // HALO
    nt = S // tm
    last_h = S // HALO - 1

    def body(du_ref, duh_ref, c_ref, ch_ref, ab_ref, abh_ref, cw_ref, lg_ref, lb_ref,
             dab_ref, dcw_ref, dcb_ref, dlg_ref, dlb_ref, ext_u, ext_dc, ph_u, ph_dc, wacc):
        i = pl.program_id(0)

        @pl.when(i == 0)
        def _():
            wacc[...] = jnp.zeros_like(wacc)
            dcb_ref[...] = jnp.zeros_like(dcb_ref)
            dlg_ref[...] = jnp.zeros_like(dlg_ref)
            dlb_ref[...] = jnp.zeros_like(dlb_ref)

        def ln_bwd(cv, du):
            mu = jnp.mean(cv, axis=-1, keepdims=True)
            var = jnp.mean(jnp.square(cv - mu), axis=-1, keepdims=True)
            rstd = lax.rsqrt(var + EPS)
            n = (cv - mu) * rstd
            z = n * lg_ref[...] + lb_ref[...]
            sz = _sigmoid(z)
            dz = du * (sz * (1.0 + z * (1.0 - sz)))
            dn = dz * lg_ref[...]
            dc = rstd * (dn - jnp.mean(dn, axis=-1, keepdims=True)
                         - n * jnp.mean(dn * n, axis=-1, keepdims=True))
            return dc, dz, n

        hv = abh_ref[...]
        ext_u[0:HALO, :] = jnp.where(i > 0, hv[:, :CW] * _sigmoid(hv[:, CW:]), 0.0)
        av = ab_ref[...]
        sb = _sigmoid(av[:, CW:])
        ext_u[HALO:HALO + tm, :] = av[:, :CW] * sb
        ext_u[HALO + tm:, :] = jnp.zeros((8, CW), F32)
        s_lg = s_lb = s_cb = jnp.zeros((1, CW), F32)
        for b in range(tm // RB):
            bs = slice(b * RB, (b + 1) * RB)
            dc, dz, n = ln_bwd(c_ref[bs, :], du_ref[bs, :])
            ext_dc[bs, :] = dc
            s_lg = s_lg + jnp.sum(dz * n, axis=0, keepdims=True)
            s_lb = s_lb + jnp.sum(dz, axis=0, keepdims=True)
            s_cb = s_cb + jnp.sum(dc, axis=0, keepdims=True)
        dch, _, _ = ln_bwd(ch_ref[...], duh_ref[...])
        ext_dc[tm:tm + HALO, :] = jnp.where(i < nt - 1, dch, 0.0)
        ext_dc[tm + HALO:, :] = jnp.zeros((8, CW), F32)
        dlg_ref[...] += s_lg
        dlb_ref[...] += s_lb
        dcb_ref[...] += s_cb
        _phases(ext_u, ph_u, tm + HALO)
        _phases(ext_dc, ph_dc, tm + HALO)

        for rb in range(tm // RB):
            rs = slice(rb * RB, (rb + 1) * RB)
            dcb = ext_dc[rs, :]
            for p, a0, a1, taps in _taps_by_phase([rb * RB + HALO - (CONV_K - 1) + k for k in range(CONV_K)]):
                win = ph_u[p, a0:a1 + RB, :]
                for k, a in taps:
                    prod = dcb * win[a - a0:a - a0 + RB]
                    part = prod[0:8]
                    for j in range(1, RB // 8):
                        part = part + prod[8 * j:8 * j + 8]
                    wacc[k] += part
            acc = jnp.zeros((RB, CW), F32)
            for p, a0, a1, taps in _taps_by_phase([rb * RB + (CONV_K - 1) - k for k in range(CONV_K)]):
                win = ph_dc[p, a0:a1 + RB, :]
                for k, a in taps:
                    acc = acc + cw_ref[k:k + 1, :] * win[a - a0:a - a0 + RB]
            a_r = ab_ref[rs, 0:CW]
            s_r = _sigmoid(ab_ref[rs, CW:2 * CW])
            dab_ref[rs, 0:CW] = (acc * s_r).astype(CDT)
            dab_ref[rs, CW:2 * CW] = (acc * a_r * (s_r * (1.0 - s_r))).astype(CDT)

        @pl.when(i == nt - 1)
        def _():
            for k in range(CONV_K):
                dcw_ref[k:k + 1, :] = jnp.sum(wacc[k], axis=0, keepdims=True)
            dcw_ref[CONV_K:, :] = jnp.zeros((HALO - CONV_K, CW), F32)

    tile = lambda n, j: pl.BlockSpec((tm, n), lambda i: (i, j))
    nxt = lambda n, j: pl.BlockSpec((HALO, n), lambda i: (jnp.minimum((i + 1) * hb, last_h), j))
    return _launch(
        "b_conv" if comm is None else "b_conv_rs", body, (nt,),
        [tile(CW, 0), nxt(CW, 0), tile(CW, 0), nxt(CW, 0),
         tile(2 * CW, 0),
         pl.BlockSpec((HALO, 2 * CW), lambda i: (jnp.maximum(i * hb - 1, 0), 0)),
         _resident((HALO, CW)), _resident((1, CW)), _resident((1, CW))],
        [tile(2 * CW, 0),
         pl.BlockSpec((HALO, CW), lambda i: (0, 0)), pl.BlockSpec((1, CW), lambda i: (0, 0)),
         pl.BlockSpec((1, CW), lambda i: (0, 0)), pl.BlockSpec((1, CW), lambda i: (0, 0))],
        [jax.ShapeDtypeStruct((S, 2 * CW), CDT),
         jax.ShapeDtypeStruct((HALO, CW), F32), jax.ShapeDtypeStruct((1, CW), F32),
         jax.ShapeDtypeStruct((1, CW), F32), jax.ShapeDtypeStruct((1, CW), F32)],
        [pltpu.VMEM((tm + HALO + 8, CW), F32), pltpu.VMEM((tm + HALO + 8, CW), F32),
         pltpu.VMEM((8, tm + HALO, CW), F32), pltpu.VMEM((8, tm + HALO, CW), F32),
         pltpu.VMEM((HALO, 8, CW), F32)],
        (dmix, dmix, c, c, proj, proj, cw, lg, lb), ("arbitrary",), comm)


def _b_ret(dmix, dab, proj, rraw, states, cosT, sinT, Mt, qd, kd, gs, gn, comm=None):
    S = proj.shape[0]
    RW = HEADS * HD
    NB = S // SC
    scale = HD ** -0.5

    def body(dro_ref, dab_ref, q_ref, k_ref, v_ref, g_ref, rraw_ref, st_ref, c_ref, s_ref,
             m_ref, qd_ref, kd_ref, gs_ref, gn_ref, dp_ref, dgn_ref, G):
        @pl.when(pl.program_id(0) == 0)
        def _():
            G[...] = jnp.zeros_like(G)
            dgn_ref[...] = jnp.zeros_like(dgn_ref)

        dp_ref[:, 0:2 * RW] = dab_ref[...]
        cv = c_ref[...]
        sv = s_ref[...]
        for h in range(HEADS):
            cs = slice(h * HD, (h + 1) * HD)
            q = _rot(q_ref[:, cs], cv, sv)
            k = _rot(k_ref[:, cs], cv, sv) * scale
            qb = q.astype(CDT)
            kb = k.astype(CDT)
            vb = v_ref[:, cs].astype(CDT)
            spb = st_ref[0, h]
            r = rraw_ref[:, cs]
            mu = jnp.mean(r, axis=-1, keepdims=True)
            var = jnp.mean(jnp.square(r - mu), axis=-1, keepdims=True)
            rstd = lax.rsqrt(var + EPS)
            n = (r - mu) * rstd
            gv = g_ref[:, cs]
            sg = _sigmoid(gv)
            sil = gv * sg
            dro = dro_ref[:, cs]
            gnv = gn_ref[:, cs]
            dgn_ref[:, cs] += jnp.sum(dro * n * sil, axis=0, keepdims=True)
            dgate = dro * (n * gnv) * (sg * (1.0 + gv * (1.0 - sg)))
            dn = dro * gnv * sil
            dr = rstd * (dn - jnp.mean(dn, axis=-1, keepdims=True)
                         - n * jnp.mean(dn * n, axis=-1, keepdims=True))
            drb = dr.astype(CDT)
            mh = m_ref[h]
            ab = (_dot_nt(qb, kb) * mh).astype(CDT)
            dab_ = (_dot_nt(drb, vb) * mh).astype(CDT)
            qdb = (q * qd_ref[h]).astype(CDT)
            kdb = (k * kd_ref[h]).astype(CDT)
            gc = G[h]
            gb = gc.astype(CDT)
            dq = _dot(dab_, kb) + _dot_nt(drb, spb) * qd_ref[h]
            dk = _dot_tn(dab_, qb) + _dot_nt(vb, gb) * kd_ref[h]
            dv = _dot_tn(ab, drb) + _dot(kdb, gb)
            G[h] = gs_ref[h, 0:1, :] * gc + _dot_tn(qdb, drb)
            dk = dk * scale
            dqp = dq * cv + pltpu.roll(dq * sv, HD // 2, 1)
            dkp = dk * cv + pltpu.roll(dk * sv, HD // 2, 1)
            base = 2 * RW
            dp_ref[:, base + h * HD:base + (h + 1) * HD] = dqp.astype(CDT)
            dp_ref[:, base + RW + h * HD:base + RW + (h + 1) * HD] = dkp.astype(CDT)
            dp_ref[:, base + 2 * RW + h * HD:base + 2 * RW + (h + 1) * HD] = dv.astype(CDT)
            dp_ref[:, base + 3 * RW + h * HD:base + 3 * RW + (h + 1) * HD] = dgate.astype(CDT)

    rev = lambda n, j: pl.BlockSpec((SC, n), lambda i: (NB - 1 - i, j))
    return _launch(
        "b_ret" if comm is None else "b_ret_rs", body, (NB,),
        [rev(RW, 1), rev(2 * RW, 0), rev(RW, 2), rev(RW, 3), rev(RW, 4), rev(RW, 5), rev(RW, 0),
         pl.BlockSpec((1, HEADS, HD, HD), lambda i: (NB - 1 - i, 0, 0, 0)),
         rev(HD, 0), rev(HD, 0),
         _resident((HEADS, SC, SC)), _resident((HEADS, SC, HD)), _resident((HEADS, SC, HD)),
         _resident((HEADS, 8, HD)), _resident((1, RW))],
        [rev(6 * RW, 0), pl.BlockSpec((1, RW), lambda i: (0, 0))],
        [jax.ShapeDtypeStruct((S, 6 * RW), CDT), jax.ShapeDtypeStruct((1, RW), F32)],
        [pltpu.VMEM((HEADS, HD, HD), F32)],
        (dmix, dab, proj, proj, proj, proj, rraw, states, cosT, sinT, Mt, qd, kd, gs, gn), ("arbitrary",), comm)


def _b_in(dproj, G, n_in, x, g1, dxm, tm, comm=None):
    S, D = x.shape
    N = N_DEV * n_in

    def body(dp_ref, w_ref, x_ref, g_ref, dxm_ref, dx_ref, dxb_ref, dg_ref):
        @pl.when(pl.program_id(0) == 0)
        def _():
            dg_ref[...] = jnp.zeros_like(dg_ref)

        dh = _dot(dp_ref[...], w_ref[...].reshape(N, D))
        xv = x_ref[...]
        r = lax.rsqrt(jnp.mean(xv * xv, axis=-1, keepdims=True) + EPS)
        dxn, dgx = _rms_bwd(xv, r, g_ref[...], dh)
        dg_ref[...] += jnp.sum(dgx, axis=0, keepdims=True)
        dx = dxm_ref[...] + dxn
        dx_ref[...] = dx
        dxb_ref[...] = dx.astype(CDT)

    row = lambda n: pl.BlockSpec((tm, n), lambda i: (i, 0))
    return _launch(
        "b_in" if comm is None else "b_in_rs", body, (S // tm,),
        [row(N), _wres(n_in, D, 0), row(D), _resident((1, D)), row(D)],
        [row(D), row(D), pl.BlockSpec((1, D), lambda i: (0, 0))],
        [jax.ShapeDtypeStruct((S, D), F32), jax.ShapeDtypeStruct((S, D), CDT),
         jax.ShapeDtypeStruct((1, D), F32)],
        [], (dproj, G, x, g1, dxm), ("arbitrary",), comm)


def _dw_tn(a, b, tm, tk):
    S, M = a.shape
    N = b.shape[1]
    nk = S // tk

    def body(a_ref, b_ref, o_ref, ob_ref):
        k = pl.program_id(1)

        @pl.when(k == 0)
        def _():
            o_ref[...] = jnp.zeros_like(o_ref)

        o_ref[...] += _dot_tn(a_ref[...], b_ref[...])

        @pl.when(k == nk - 1)
        def _():
            ob_ref[...] = o_ref[...].astype(CDT)

    out = pl.BlockSpec((tm, N), lambda m, k: (m, 0))
    return _call(
        body, name="dw_tn", grid=(M // tm, nk),
        in_specs=[pl.BlockSpec((tk, tm), lambda m, k: (k, m)), pl.BlockSpec((tk, N), lambda m, k: (k, 0))],
        out_specs=[out, out],
        out_shape=[jax.ShapeDtypeStruct((M, N), F32), jax.ShapeDtypeStruct((M, N), CDT)],
        compiler_params=_cp(("parallel", "arbitrary")),
    )(a, b)


def _adamw(w, g, m, v, tr):
    R, C = w.shape
    c1 = 1.0 - ADAM_B1 ** ADAM_STEP
    c2 = 1.0 - ADAM_B2 ** ADAM_STEP

    def body(w_ref, g_ref, m_ref, v_ref, d_ref, mo_ref, vo_ref):
        gv = g_ref[...]
        mn = ADAM_B1 * m_ref[...] + (1.0 - ADAM_B1) * gv
        vn = ADAM_B2 * v_ref[...] + (1.0 - ADAM_B2) * jnp.square(gv)
        mo_ref[...] = mn
        vo_ref[...] = vn
        d_ref[...] = -ADAM_LR * ((mn / c1) / (jnp.sqrt(vn / c2) + ADAM_EPS) + ADAM_WD * w_ref[...])

    blk = pl.BlockSpec((tr, C), lambda i: (i, 0))
    sh = jax.ShapeDtypeStruct((R, C), F32)
    return _call(
        body, name="adamw", grid=(R // tr,),
        in_specs=[blk, blk, blk, blk], out_specs=[blk, blk, blk], out_shape=[sh, sh, sh],
        compiler_params=_cp(("parallel",)),
    )(w, g, m, v)


def _coords():
    return lax.axis_index("x"), lax.axis_index("y"), lax.axis_index("c")


def _peer(x, y, c, d):
    return (x ^ (d >> 2), y ^ ((d >> 1) & 1), c ^ (d & 1))


def _ag_comm(ps):
    K = len(ps)

    def plan(cins, couts, sems):
        send_sems, recv_sems, local_sems = sems
        x, y, c = _coords()
        me, sibling = (x, y, c), (x, y, 1 - c)
        chips = [(1 - x, y), (x, 1 - y), (1 - x, 1 - y)]
        mine, first, passed, got_ici, got_d2d = [], [], [], [], []
        for a in range(K):
            x_ref, out_ref = cins[a], couts[a]
            R = x_ref.shape[0]

            def rows(px, py, pc, out_ref=out_ref, R=R):
                return out_ref.at[pl.ds((4 * px + 2 * py + pc) * R, R), :]

            def copy(k, block, to, src=None, rows=rows, a=a):
                return pltpu.make_async_remote_copy(
                    src_ref=rows(*block) if src is None else src, dst_ref=rows(*block),
                    send_sem=send_sems.at[7 * a + k], recv_sem=recv_sems.at[7 * a + k],
                    device_id=to, device_id_type=MESH)

            mine.append(pltpu.make_async_copy(x_ref, rows(*me), local_sems.at[a]))
            first.append(copy(0, me, sibling, src=x_ref))
            first += [copy(1 + j, me, (*chip, c), src=x_ref) for j, chip in enumerate(chips)]
            passed += [copy(4 + j, (*chip, c), sibling) for j, chip in enumerate(chips)]
            got_ici += [copy(1 + j, (*chip, c), me) for j, chip in enumerate(chips)]
            got_d2d.append(copy(0, sibling, me))
            got_d2d += [copy(4 + j, (*chip, 1 - c), me) for j, chip in enumerate(chips)]
        return mine, first, passed, got_ici, got_d2d

    def start(*a):
        mine, first, _, _, _ = plan(*a)
        for cp in mine + first:
            cp.start()

    def mid(*a):
        _, _, passed, got_ici, _ = plan(*a)
        for got, fwd in zip(got_ici, passed):
            got.wait_recv()
            fwd.start()

    def finish(*a):
        mine, first, passed, _, got_d2d = plan(*a)
        for got in got_d2d:
            got.wait_recv()
        for cp in first + passed:
            cp.wait_send()
        for cp in mine:
            cp.wait()

    return _Comm(ps, [jax.ShapeDtypeStruct((N_DEV * p.shape[0], p.shape[1]), p.dtype) for p in ps],
                 [pltpu.SemaphoreType.DMA((7 * K,)), pltpu.SemaphoreType.DMA((7 * K,)),
                  pltpu.SemaphoreType.DMA((K,))], start, mid, finish)


def _rs_direct_comm(parts):
    K = len(parts)

    def plan(cins, couts, sems):
        send_sems, recv_sems = sems
        x, y, c = _coords()
        cps = []
        for d in range(1, N_DEV):
            px, py, pc = _peer(x, y, c, d)
            for k in range(K):
                s = (d - 1) * K + k
                cps.append(pltpu.make_async_remote_copy(
                    src_ref=cins[k].at[4 * px + 2 * py + pc], dst_ref=couts[k].at[d - 1],
                    send_sem=send_sems.at[s], recv_sem=recv_sems.at[s], device_id=(px, py, pc),
                    device_id_type=MESH))
        return cps

    def start(*a):
        for cp in plan(*a):
            cp.start()

    def finish(*a):
        cps = plan(*a)
        for cp in cps:
            cp.wait_recv()
        for cp in cps:
            cp.wait_send()

    n_sem = (N_DEV - 1) * K
    return _Comm(parts, [jax.ShapeDtypeStruct((N_DEV - 1,) + p.shape[1:], p.dtype) for p in parts],
                 [pltpu.SemaphoreType.DMA((n_sem,)), pltpu.SemaphoreType.DMA((n_sem,))], start, None, finish)


def _sum_all(parts, recv):
    K = len(parts)

    def body(*refs):
        ins, rcv, outs = refs[:K], refs[K:2 * K], refs[2 * K:3 * K]
        bufs, sem = refs[3 * K:4 * K], refs[4 * K]
        x, y, c = _coords()
        me = 4 * x + 2 * y + c
        cps = [pltpu.make_async_copy(ins[k].at[me], bufs[k], sem.at[k]) for k in range(K)]
        for cp in cps:
            cp.start()
        for k in range(K):
            cps[k].wait()
            acc = bufs[k][...]
            for d in range(N_DEV - 1):
                acc = acc + rcv[k][d].astype(F32)
            outs[k][...] = acc

    vm = pl.BlockSpec(memory_space=pltpu.VMEM)
    res = _call(
        body, name="sum_all", in_specs=[_ANY] * K + [vm] * K, out_specs=[vm] * K,
        out_shape=[jax.ShapeDtypeStruct(p.shape[1:], F32) for p in parts],
        scratch_shapes=[pltpu.VMEM(p.shape[1:], F32) for p in parts] + [pltpu.SemaphoreType.DMA((K,))],
        compiler_params=_cp(),
    )(*parts, *recv)
    return list(res)


def _gather_small(v, reduce):
    R, C = v.shape

    def exchange(v_ref, buf, send_sems, recv_sems):
        x, y, c = _coords()
        me = 4 * x + 2 * y + c
        buf[me] = v_ref[...]
        cps = []
        for d in range(1, N_DEV):
            cp = pltpu.make_async_remote_copy(
                src_ref=v_ref, dst_ref=buf.at[me], send_sem=send_sems.at[d - 1], recv_sem=recv_sems.at[d - 1],
                device_id=_peer(x, y, c, d), device_id_type=MESH)
            cp.start()
            cps.append(cp)
        for cp in cps:
            cp.wait_recv()
        for cp in cps:
            cp.wait_send()

    sems = [pltpu.SemaphoreType.DMA((7,)), pltpu.SemaphoreType.DMA((7,))]
    vm = pl.BlockSpec(memory_space=pltpu.VMEM)
    if reduce:
        def body(v_ref, o_ref, buf, send_sems, recv_sems):
            exchange(v_ref, buf, send_sems, recv_sems)
            acc = buf[0]
            for s in range(1, N_DEV):
                acc = acc + buf[s]
            o_ref[...] = acc

        return _call(body, name="allreduce_small", in_specs=[vm], out_specs=vm,
                     out_shape=jax.ShapeDtypeStruct((R, C), F32),
                     scratch_shapes=[pltpu.VMEM((N_DEV, R, C), F32)] + sems)(v)

    def body(v_ref, o_ref, send_sems, recv_sems):
        exchange(v_ref, o_ref, send_sems, recv_sems)

    return _call(body, name="allgather_small", in_specs=[vm], out_specs=vm,
                 out_shape=jax.ShapeDtypeStruct((N_DEV, R, C), F32), scratch_shapes=sems)(v)


def _tables(S):
    half = HD // 2
    pos = jnp.arange(S, dtype=F32)
    freqs = ROPE_BASE ** (-jnp.arange(half, dtype=F32) / half)
    ang = pos[:, None] * freqs[None, :]
    cos, sin = jnp.cos(ang), jnp.sin(ang)
    cosT = jnp.concatenate([cos, cos], axis=-1)
    sinT = jnp.concatenate([-sin, sin], axis=-1)
    log_g = jnp.log(1.0 - 2.0 ** (-5.0 - jnp.arange(HEADS, dtype=F32)))
    idx = jnp.arange(SC, dtype=F32)
    ci = jnp.arange(SC) // CHUNK
    diff = idx[:, None] - idx[None, :]
    same = ci[:, None] == ci[None, :]
    earlier = ci[None, :] < ci[:, None]
    expo = jnp.where(same, jnp.abs(diff), diff)
    Mt = jnp.where((same | earlier)[None], jnp.exp(log_g[:, None, None] * expo[None]), 0.0)
    ones = jnp.ones((1, 1, HD), F32)
    qd = jnp.exp(log_g[:, None] * (idx + 1.0)[None, :])[:, :, None] * ones
    kd = jnp.exp(log_g[:, None] * (SC - 1.0 - idx)[None, :])[:, :, None] * ones
    gs = jnp.exp(log_g * SC)[:, None, None] * jnp.ones((1, 8, HD), F32)
    return cosT, sinT, Mt, qd, kd, gs


def _pad_rows(a, rows):
    return jnp.pad(a, ((0, rows - a.shape[0]), (0, 0)))


def kernel(x, norm1_g, w_in, conv_w, conv_b, conv_ln_g, conv_ln_b, ret_gn_g, w_out, norm2_g, w_gate, w_up, w_down, final_g, loss_target, m_norm1_g, m_w_in, m_conv_w, m_conv_b, m_conv_ln_g, m_conv_ln_b, m_ret_gn_g, m_w_out, m_norm2_g, m_w_gate, m_w_up, m_w_down, m_final_g, v_norm1_g, v_w_in, v_conv_w, v_conv_b, v_conv_ln_g, v_conv_ln_b, v_ret_gn_g, v_w_out, v_norm2_g, v_w_gate, v_w_up, v_w_down, v_final_g):
    L, D, n_in = w_in.shape
    n_out = w_out.shape[1]
    n_ff = w_gate.shape[2]
    S = x.shape[1]
    CW = conv_b.shape[1]
    IN, FF = N_DEV * n_in, N_DEV * n_ff
    ncw = conv_w.shape[2]
    x0 = x.reshape(S, D)
    tgt = loss_target.reshape(S, D)
    TM = min(512, S)
    TKW = min(2048, S)
    TMI = min(512, S)
    TMM = min(256, S)

    assert n_out <= n_ff
    wparts = (0, 1, 2, 0)
    pack_a = jnp.swapaxes(w_in, 1, 2).astype(CDT)
    pack_b = jnp.concatenate([w_out, jnp.zeros((L, n_ff - n_out, D), F32), jnp.swapaxes(w_gate, 1, 2),
                              jnp.swapaxes(w_up, 1, 2)], axis=1).astype(CDT)
    pack_c = w_down.astype(CDT)
    per_dev = lambda g: g.reshape(N_DEV, -1, D)
    Ga = per_dev(_comm_only(_ag_comm([pack_a[0]]), "ag_first")[0])
    Gb = Gc = None

    cwp = conv_w.reshape(L * CONV_K * ncw // 128, 128)
    cw_rows = -(-cwp.shape[0] // 8) * 8
    cwg = _gather_small(_pad_rows(cwp, cw_rows), reduce=False)[:, :cwp.shape[0], :]
    conv_w_full = jnp.moveaxis(cwg.reshape(N_DEV, L, CONV_K, ncw), 0, 2).reshape(L, CONV_K, CW)

    cosT, sinT, Mt, qd, kd, gs = _tables(S)

    saved = []
    xl = x0
    for l in range(L):
        cw = _pad_rows(conv_w_full[l], HALO)
        more = l + 1 < L
        first = [pack_b[0]] if l == 0 else [pack_a[l + 1]] if more else []
        second = [pack_c[0], pack_a[1]] if l == 0 else []
        (h, proj, c, u), got1 = _f_in_conv(xl, norm1_g[l][None], Ga, n_in, cw, conv_b[l][None], conv_ln_g[l][None],
                                           conv_ln_b[l][None], TMI, _ag_comm(first) if first else None)
        (rraw, states, mixed), got2 = _f_ret(proj, u, cosT, sinT, Mt, qd, kd, gs, ret_gn_g[l][None],
                                             _ag_comm(second) if second else None)
        if l == 0:
            Gb, Gc = per_dev(got1[0]), per_dev(got2[0])
        (xm, h2, gate, up, act, xo), nxt = _f_mlp(
            xl, mixed, norm2_g[l][None], (Gb, Gc), n_out, n_ff, wparts, TMM,
            _ag_comm([pack_b[l + 1], pack_c[l + 1]]) if more else None)
        saved.append(dict(x=xl, h=h, proj=proj, c=c, rraw=rraw, states=states, mixed=mixed, xm=xm, h2=h2,
                          gate=gate, up=up, act=act, cw=cw, Ga=Ga, Gbc=(Gb, Gc)))
        if more:
            Ga = per_dev(got2[1] if l == 0 else got1[0])
            Gb, Gc = per_dev(nxt[0]), per_dev(nxt[1])
        xl = xo

    dx, dxb, loss_p, dfg = _f_loss(xl, final_g[None], tgt, TM)

    small = []
    own = [None] * L
    recv = [None] * L
    blocks = lambda d: d.reshape(N_DEV, -1, D)
    for l in reversed(range(L)):
        sv = saved[l]
        (dgate, dup, dxm, dxmb, dmix, dg2), _ = _b_mlp(
            dxb, dx, sv["xm"], sv["gate"], sv["up"], norm2_g[l][None], sv["Gbc"], n_out, n_ff, wparts, TMM)
        d_wd, d_wd_b = _dw_tn(sv["act"], dxb, FF // 2, TKW)
        d_wgT, d_wgT_b = _dw_tn(dgate, sv["h2"], FF // 2, TKW)
        d_wuT, d_wuT_b = _dw_tn(dup, sv["h2"], FF // 2, TKW)
        d_wo, d_wo_b = _dw_tn(sv["mixed"], dxmb, D, TKW)
        (dab, dcw, dcb, dlg, dlb), r_go = _b_conv(dmix, sv["c"], sv["proj"], sv["cw"], conv_ln_g[l][None],
                                                  conv_ln_b[l][None], TM,
                                                  _rs_direct_comm([blocks(d_wgT_b), blocks(d_wo_b)]))
        (dproj, dgn), r_ud = _b_ret(dmix, dab, sv["proj"], sv["rraw"], sv["states"], cosT, sinT, Mt, qd, kd, gs,
                                    ret_gn_g[l][None], _rs_direct_comm([blocks(d_wuT_b), blocks(d_wd_b)]))
        d_winT, d_winT_b = _dw_tn(dproj, sv["h"], IN // 2, TKW)
        (dx, dxb, dg1), r_i = _b_in(dproj, sv["Ga"], n_in, sv["x"], norm1_g[l][None], dxm, TMI,
                                    _rs_direct_comm([blocks(d_winT_b)]))
        own[l] = [blocks(d) for d in (d_winT, d_wo, d_wgT, d_wuT, d_wd)]
        recv[l] = [r_i[0], r_go[1], r_go[0], r_ud[0], r_ud[1]]
        small.append(jnp.concatenate([dcw, dcb, dlg, dlb, dgn, dg1.reshape(2, CW), dg2.reshape(2, CW)], axis=0))
    small = small[::-1]
    grad_x = dx.reshape(1, S, D)

    rows_l = HALO + 8
    loss_row = jnp.zeros((1, CW), F32).at[0, 0].set(loss_p[0, 0])
    sm = jnp.concatenate(small + [dfg.reshape(2, CW), loss_row], axis=0)
    sm_rows = -(-sm.shape[0] // 8) * 8
    sm = _gather_small(_pad_rows(sm, sm_rows), reduce=True)
    loss = sm[L * rows_l + 2, 0]
    g_final = sm[L * rows_l:L * rows_l + 2].reshape(D)
    per = sm[:L * rows_l].reshape(L, rows_l, CW)
    me = 4 * lax.axis_index("x") + 2 * lax.axis_index("y") + lax.axis_index("c")
    g_conv_w = lax.dynamic_slice_in_dim(per[:, :CONV_K, :], me * ncw, ncw, axis=2)
    g_conv_b, g_ln_g, g_ln_b, g_gn = per[:, HALO], per[:, HALO + 1], per[:, HALO + 2], per[:, HALO + 3]
    g_n1 = per[:, HALO + 4:HALO + 6].reshape(L, D)
    g_n2 = per[:, HALO + 6:HALO + 8].reshape(L, D)

    gl = [_sum_all(own[l], recv[l]) for l in range(L)]
    g_w_in = jnp.stack([gl[l][0].T for l in range(L)])
    g_w_out = jnp.stack([gl[l][1] for l in range(L)])
    g_w_gate = jnp.stack([gl[l][2].T for l in range(L)])
    g_w_up = jnp.stack([gl[l][3].T for l in range(L)])
    g_w_down = jnp.stack([gl[l][4] for l in range(L)])

    def big(w, g, m, v):
        sh = w.shape
        two = lambda a: a.reshape(-1, sh[-1])
        rows = two(w).shape[0]
        d, mn, vn = _adamw(two(w), two(g), two(m), two(v), rows // 8)
        return d.reshape(sh), mn.reshape(sh), vn.reshape(sh)

    names = ["norm1_g", "conv_w", "conv_b", "conv_ln_g", "conv_ln_b", "ret_gn_g", "norm2_g", "final_g"]
    sw = dict(norm1_g=(norm1_g, g_n1, m_norm1_g, v_norm1_g), conv_w=(conv_w, g_conv_w, m_conv_w, v_conv_w),
              conv_b=(conv_b, g_conv_b, m_conv_b, v_conv_b), conv_ln_g=(conv_ln_g, g_ln_g, m_conv_ln_g, v_conv_ln_g),
              conv_ln_b=(conv_ln_b, g_ln_b, m_conv_ln_b, v_conv_ln_b), ret_gn_g=(ret_gn_g, g_gn, m_ret_gn_g, v_ret_gn_g),
              norm2_g=(norm2_g, g_n2, m_norm2_g, v_norm2_g), final_g=(final_g, g_final, m_final_g, v_final_g))
    lens = [int(math.prod(sw[n][0].shape)) for n in names]
    tot = sum(lens)
    prow = -(-tot // (8 * CW)) * 8

    def packs(j):
        flat = jnp.concatenate([sw[n][j].reshape(-1) for n in names])
        return jnp.pad(flat, (0, prow * CW - tot)).reshape(prow, CW)

    sd, smn, svn = _adamw(packs(0), packs(1), packs(2), packs(3), prow)

    def unpack(a):
        flat = a.reshape(-1)
        out, o = {}, 0
        for n, ln in zip(names, lens):
            out[n] = flat[o:o + ln].reshape(sw[n][0].shape)
            o += ln
        return out

    sd, smn, svn = unpack(sd), unpack(smn), unpack(svn)
    res = {n: (sw[n][1], sd[n], smn[n], svn[n]) for n in names}
    res["w_in"] = (g_w_in,) + big(w_in, g_w_in, m_w_in, v_w_in)
    res["w_out"] = (g_w_out,) + big(w_out, g_w_out, m_w_out, v_w_out)
    res["w_gate"] = (g_w_gate,) + big(w_gate, g_w_gate, m_w_gate, v_w_gate)
    res["w_up"] = (g_w_up,) + big(w_up, g_w_up, m_w_up, v_w_up)
    res["w_down"] = (g_w_down,) + big(w_down, g_w_down, m_w_down, v_w_down)

    order = ["norm1_g", "w_in", "conv_w", "conv_b", "conv_ln_g", "conv_ln_b", "ret_gn_g", "w_out", "norm2_g",
             "w_gate", "w_up", "w_down", "final_g"]
    return (loss, grad_x, *[res[n][0] for n in order], *[res[n][1] for n in order],
            *[res[n][2] for n in order], *[res[n][3] for n in order])
```

```python
import math

import jax
import jax.numpy as jnp
from jax import lax
from jax.experimental import pallas as pl
from jax.experimental.pallas import tpu as pltpu

F32 = jnp.float32
CDT = jnp.bfloat16
EPS = 1e-6
CHUNK = 64
SC = 256
HEADS = 4
HD = 128
CONV_K = 31
HALO = 32
ROPE_BASE = 10000.0
ADAM_LR = 0.001
ADAM_B1 = 0.9
ADAM_B2 = 0.999
ADAM_EPS = 1e-08
ADAM_WD = 0.01
ADAM_STEP = 10
N_DEV = 8
MESH = pl.DeviceIdType.MESH
VMEM_LIMIT = 60 * 1024 * 1024


def _call(body, **kw):
    return pl.pallas_call(body, **kw)


def _cp(sem=None, vmem=VMEM_LIMIT):
    return pltpu.CompilerParams(dimension_semantics=sem, vmem_limit_bytes=vmem)


def _resident(shape):
    nd = len(shape)
    return pl.BlockSpec(shape, lambda *_: (0,) * nd, pipeline_mode=pl.Buffered(1))


def _dot(a, b):
    return jnp.dot(a, b, preferred_element_type=F32)


def _dot_nt(a, b):
    return lax.dot_general(a, b, (((1,), (1,)), ((), ())), preferred_element_type=F32)


def _dot_tn(a, b):
    return lax.dot_general(a, b, (((0,), (0,)), ((), ())), preferred_element_type=F32)


def _sigmoid(x):
    return 1.0 / (1.0 + jnp.exp(-x))


def _rms_bwd(x, r, g, dy):
    xh = x * r
    dyg = dy * g
    dx = r * (dyg - xh * jnp.mean(dyg * xh, axis=-1, keepdims=True))
    return dx, dy * xh


class _Comm:
    def __init__(self, ins, out_shapes, sems, start, mid, finish):
        self.ins, self.out_shapes, self.sems = list(ins), list(out_shapes), list(sems)
        self.start, self.mid, self.finish = start, mid, finish


_ANY = pl.BlockSpec(memory_space=pl.ANY)


def _launch(name, compute, grid, in_specs, out_specs, out_shape, scratch, operands, sem, comm=None):
    n_in, n_out, n_sc = len(in_specs), len(out_specs), len(scratch)
    if comm is None:
        res = _call(compute, name=name, grid=grid, in_specs=in_specs, out_specs=out_specs, out_shape=out_shape,
                    scratch_shapes=scratch, compiler_params=_cp(sem))(*operands)
        return list(res), []
    c_in, c_out = len(comm.ins), len(comm.out_shapes)
    inner = grid[1] if len(grid) > 1 else 1
    steps = grid[0] * inner
    mid_step = (3 * steps) // 4

    def body(*refs):
        ins = refs[:n_in]
        cins = refs[n_in:n_in + c_in]
        o = n_in + c_in
        outs = refs[o:o + n_out]
        couts = refs[o + n_out:o + n_out + c_out]
        o += n_out + c_out
        sc = refs[o:o + n_sc]
        csem = refs[o + n_sc:]
        i = pl.program_id(0)
        if len(grid) > 1:
            i = i * inner + pl.program_id(1)

        @pl.when(i == 0)
        def _():
            comm.start(cins, couts, csem)

        if comm.mid is not None:
            @pl.when(i == mid_step)
            def _():
                comm.mid(cins, couts, csem)

        compute(*ins, *outs, *sc)

        @pl.when(i == steps - 1)
        def _():
            comm.finish(cins, couts, csem)

    res = _call(body, name=name, grid=grid, in_specs=list(in_specs) + [_ANY] * c_in,
                out_specs=list(out_specs) + [_ANY] * c_out, out_shape=list(out_shape) + comm.out_shapes,
                scratch_shapes=list(scratch) + comm.sems,
                compiler_params=_cp(("arbitrary",) * len(grid)))(*operands, *comm.ins)
    return list(res[:n_out]), list(res[n_out:])


def _comm_only(comm, name):
    c_in, c_out = len(comm.ins), len(comm.out_shapes)

    def body(*refs):
        cins, couts, csem = refs[:c_in], refs[c_in:c_in + c_out], refs[c_in + c_out:]
        comm.start(cins, couts, csem)
        if comm.mid is not None:
            comm.mid(cins, couts, csem)
        comm.finish(cins, couts, csem)

    res = _call(body, name=name, in_specs=[_ANY] * c_in, out_specs=[_ANY] * c_out, out_shape=comm.out_shapes,
                scratch_shapes=comm.sems)(*comm.ins)
    return list(res)


def _wres(n, D, part):
    return pl.BlockSpec((N_DEV, n, D), lambda i: (0, part, 0), pipeline_mode=pl.Buffered(1))


def _taps_by_phase(offsets):
    groups = []
    for p in range(8):
        taps = [(k, o - p) for k, o in enumerate(offsets) if o % 8 == p]
        if taps:
            rows = [a for _, a in taps]
            groups.append((p, min(rows), max(rows), taps))
    return groups


def _phases(ext, ph, rows):
    for p in range(8):
        ph[p, :, :] = ext[p:p + rows, :]


def _f_in_conv(x, g, G, n_in, cw, cb, lg, lb, tm, comm=None):
    S, D = x.shape
    N = N_DEV * n_in
    CW = cw.shape[1]
    RB = 64

    def body(x_ref, g_ref, w_ref, cw_ref, cb_ref, lg_ref, lb_ref, h_ref, p_ref, c_ref, u_ref, ext, ph):
        i = pl.program_id(0)

        @pl.when(i == 0)
        def _():
            ext[0:HALO, :] = jnp.zeros((HALO, CW), F32)
            ext[HALO + tm:, :] = jnp.zeros((8, CW), F32)

        @pl.when(i > 0)
        def _():
            ext[0:HALO, :] = ext[tm:tm + HALO, :]

        xv = x_ref[...]
        r = lax.rsqrt(jnp.mean(xv * xv, axis=-1, keepdims=True) + EPS)
        h = ((xv * r) * g_ref[...]).astype(CDT)
        h_ref[...] = h
        w = w_ref[...].reshape(N, D)
        pab = _dot_nt(h, w[0:2 * CW])
        p_ref[:, 0:2 * CW] = pab
        p_ref[:, 2 * CW:] = _dot_nt(h, w[2 * CW:])
        ext[HALO:HALO + tm, :] = pab[:, :CW] * _sigmoid(pab[:, CW:])
        _phases(ext, ph, tm + HALO)
        for rb in range(tm // RB):
            acc = jnp.zeros((RB, CW), F32) + cb_ref[...]
            for k in range(CONV_K):
                o = rb * RB + HALO - (CONV_K - 1) + k
                acc = acc + cw_ref[k:k + 1, :] * ph[o % 8, o - o % 8:o - o % 8 + RB, :]
            c_ref[rb * RB:(rb + 1) * RB, :] = acc
            mu = jnp.mean(acc, axis=-1, keepdims=True)
            var = jnp.mean(jnp.square(acc - mu), axis=-1, keepdims=True)
            z = ((acc - mu) * lax.rsqrt(var + EPS)) * lg_ref[...] + lb_ref[...]
            u_ref[rb * RB:(rb + 1) * RB, :] = (z * _sigmoid(z)).astype(CDT)

    row = lambda n: pl.BlockSpec((tm, n), lambda i: (i, 0))
    return _launch(
        "f_in_conv" if comm is None else "f_in_conv_ag", body, (S // tm,),
        [row(D), _resident((1, D)), _wres(n_in, D, 0),
         _resident((HALO, CW)), _resident((1, CW)), _resident((1, CW)), _resident((1, CW))],
        [row(D), row(N), row(CW), row(CW)],
        [jax.ShapeDtypeStruct((S, D), CDT), jax.ShapeDtypeStruct((S, N), F32),
         jax.ShapeDtypeStruct((S, CW), F32), jax.ShapeDtypeStruct((S, CW), CDT)],
        [pltpu.VMEM((tm + HALO + 8, CW), F32), pltpu.VMEM((8, tm + HALO, CW), F32)],
        (x, g, G, cw, cb, lg, lb), ("arbitrary",), comm)


def _rot(t, c, s):
    return t * c + pltpu.roll(t, HD // 2, 1) * s


def _f_ret(proj, u, cosT, sinT, Mt, qd, kd, gs, gn, comm=None):
    S = proj.shape[0]
    RW = HEADS * HD
    NB = S // SC
    scale = HD ** -0.5

    def body(q_ref, k_ref, v_ref, g_ref, u_ref, c_ref, s_ref, m_ref, qd_ref, kd_ref, gs_ref, gn_ref,
             rraw_ref, st_ref, mix_ref, state):
        @pl.when(pl.program_id(0) == 0)
        def _():
            state[...] = jnp.zeros_like(state)

        mix_ref[:, 0:RW] = u_ref[...]
        cv = c_ref[...]
        sv = s_ref[...]
        for h in range(HEADS):
            cs = slice(h * HD, (h + 1) * HD)
            q = _rot(q_ref[:, cs], cv, sv)
            k = _rot(k_ref[:, cs], cv, sv) * scale
            vb = v_ref[:, cs].astype(CDT)
            qb = q.astype(CDT)
            kb = k.astype(CDT)
            a = _dot_nt(qb, kb) * m_ref[h]
            sp = state[h]
            spb = sp.astype(CDT)
            st_ref[0, h] = spb
            r = _dot(a.astype(CDT), vb) + _dot((q * qd_ref[h]).astype(CDT), spb)
            kv = _dot_tn((k * kd_ref[h]).astype(CDT), vb)
            state[h] = gs_ref[h, 0:1, :] * sp + kv
            rraw_ref[:, cs] = r
            mu = jnp.mean(r, axis=-1, keepdims=True)
            var = jnp.mean(jnp.square(r - mu), axis=-1, keepdims=True)
            n = (r - mu) * lax.rsqrt(var + EPS)
            gv = g_ref[:, cs]
            mix_ref[:, RW + h * HD:RW + (h + 1) * HD] = ((n * gn_ref[:, cs]) * (gv * _sigmoid(gv))).astype(CDT)

    col = lambda j: pl.BlockSpec((SC, RW), lambda i: (i, j))
    return _launch(
        "f_ret" if comm is None else "f_ret_ag", body, (NB,),
        [col(2), col(3), col(4), col(5),
         pl.BlockSpec((SC, RW), lambda i: (i, 0)),
         pl.BlockSpec((SC, HD), lambda i: (i, 0)), pl.BlockSpec((SC, HD), lambda i: (i, 0)),
         _resident((HEADS, SC, SC)), _resident((HEADS, SC, HD)), _resident((HEADS, SC, HD)),
         _resident((HEADS, 8, HD)), _resident((1, RW))],
        [pl.BlockSpec((SC, RW), lambda i: (i, 0)),
         pl.BlockSpec((1, HEADS, HD, HD), lambda i: (i, 0, 0, 0)),
         pl.BlockSpec((SC, 2 * RW), lambda i: (i, 0))],
        [jax.ShapeDtypeStruct((S, RW), F32),
         jax.ShapeDtypeStruct((NB, HEADS, HD, HD), CDT),
         jax.ShapeDtypeStruct((S, 2 * RW), CDT)],
        [pltpu.VMEM((HEADS, HD, HD), F32)],
        (proj, proj, proj, proj, u, cosT, sinT, Mt, qd, kd, gs, gn), ("arbitrary",), comm)


def _f_mlp(x, mixed, g2, G, n_out, n_ff, parts, tm, comm=None):
    S, D = x.shape
    FF = N_DEV * n_ff
    p_wo, p_wg, p_wu, p_wd = parts

    def body(x_ref, m_ref, wo_ref, g_ref, wg_ref, wu_ref, wd_ref,
             xm_ref, h2_ref, gate_ref, up_ref, act_ref, xo_ref):
        xm = x_ref[...] + _dot(m_ref[...], wo_ref[...].reshape(N_DEV * n_out, D))
        xm_ref[...] = xm
        r = lax.rsqrt(jnp.mean(xm * xm, axis=-1, keepdims=True) + EPS)
        h2 = ((xm * r) * g_ref[...]).astype(CDT)
        h2_ref[...] = h2
        gate = _dot_nt(h2, wg_ref[...].reshape(FF, D))
        up = _dot_nt(h2, wu_ref[...].reshape(FF, D))
        gate_ref[...] = gate.astype(CDT)
        up_ref[...] = up.astype(CDT)
        act = ((gate * _sigmoid(gate)) * up).astype(CDT)
        act_ref[...] = act
        xo_ref[...] = xm + _dot(act, wd_ref[...].reshape(FF, D))

    row = lambda n: pl.BlockSpec((tm, n), lambda i: (i, 0))
    return _launch(
        "f_mlp" if comm is None else "f_mlp_ag", body, (S // tm,),
        [row(D), row(D), _wres(n_out, D, p_wo), _resident((1, D)),
         _wres(n_ff, D, p_wg), _wres(n_ff, D, p_wu), _wres(n_ff, D, p_wd)],
        [row(D), row(D), row(FF), row(FF), row(FF), row(D)],
        [jax.ShapeDtypeStruct((S, D), F32), jax.ShapeDtypeStruct((S, D), CDT),
         jax.ShapeDtypeStruct((S, FF), CDT), jax.ShapeDtypeStruct((S, FF), CDT),
         jax.ShapeDtypeStruct((S, FF), CDT), jax.ShapeDtypeStruct((S, D), F32)],
        [], (x, mixed, G[0], g2, G[0], G[0], G[1]), ("parallel",), comm)


def _f_loss(x, fg, tgt, tm):
    S, D = x.shape

    def body(x_ref, g_ref, t_ref, dx_ref, dxb_ref, loss_ref, dg_ref):
        @pl.when(pl.program_id(0) == 0)
        def _():
            loss_ref[...] = jnp.zeros_like(loss_ref)
            dg_ref[...] = jnp.zeros_like(dg_ref)

        xv = x_ref[...]
        r = lax.rsqrt(jnp.mean(xv * xv, axis=-1, keepdims=True) + EPS)
        y = (xv * r) * g_ref[...]
        e = y - t_ref[...]
        loss_ref[...] += 0.5 * jnp.sum(jnp.mean(e * e, axis=-1, keepdims=True))
        dy = e * (1.0 / D)
        dx, dgx = _rms_bwd(xv, r, g_ref[...], dy)
        dg_ref[...] += jnp.sum(dgx, axis=0, keepdims=True)
        dx_ref[...] = dx
        dxb_ref[...] = dx.astype(CDT)

    row = pl.BlockSpec((tm, D), lambda i: (i, 0))
    return _call(
        body, name="f_loss", grid=(S // tm,),
        in_specs=[row, _resident((1, D)), row],
        out_specs=[row, row, pl.BlockSpec((1, 128), lambda i: (0, 0)), pl.BlockSpec((1, D), lambda i: (0, 0))],
        out_shape=[jax.ShapeDtypeStruct((S, D), F32), jax.ShapeDtypeStruct((S, D), CDT),
                   jax.ShapeDtypeStruct((1, 128), F32), jax.ShapeDtypeStruct((1, D), F32)],
        compiler_params=_cp(("arbitrary",)),
    )(x, fg, tgt)


def _b_mlp(dxb, dx, xm, gate, up, g2, G, n_out, n_ff, parts, tm, comm=None):
    S, D = dx.shape
    FF = N_DEV * n_ff
    p_wo, p_wg, p_wu, p_wd = parts

    def body(dxb_ref, dx_ref, xm_ref, gate_ref, up_ref, g_ref, wd_ref, wg_ref, wu_ref, wo_ref,
             dgate_ref, dup_ref, dxm_ref, dxmb_ref, dmix_ref, dg_ref):
        @pl.when(pl.program_id(0) == 0)
        def _():
            dg_ref[...] = jnp.zeros_like(dg_ref)

        dact = _dot_nt(dxb_ref[...], wd_ref[...].reshape(FF, D))
        gate = gate_ref[...].astype(F32)
        up = up_ref[...].astype(F32)
        sg = _sigmoid(gate)
        sil = gate * sg
        dgate = ((dact * up) * (sg * (1.0 + gate * (1.0 - sg)))).astype(CDT)
        dup = (dact * sil).astype(CDT)
        dgate_ref[...] = dgate
        dup_ref[...] = dup
        dh2 = _dot(dgate, wg_ref[...].reshape(FF, D)) + _dot(dup, wu_ref[...].reshape(FF, D))
        xm = xm_ref[...]
        r = lax.rsqrt(jnp.mean(xm * xm, axis=-1, keepdims=True) + EPS)
        dxn, dgx = _rms_bwd(xm, r, g_ref[...], dh2)
        dg_ref[...] += jnp.sum(dgx, axis=0, keepdims=True)
        dxm = dx_ref[...] + dxn
        dxm_ref[...] = dxm
        dxmb = dxm.astype(CDT)
        dxmb_ref[...] = dxmb
        dmix_ref[...] = _dot_nt(dxmb, wo_ref[...].reshape(N_DEV * n_out, D))

    row = lambda n: pl.BlockSpec((tm, n), lambda i: (i, 0))
    return _launch(
        "b_mlp" if comm is None else "b_mlp_rs", body, (S // tm,),
        [row(D), row(D), row(D), row(FF), row(FF), _resident((1, D)),
         _wres(n_ff, D, p_wd), _wres(n_ff, D, p_wg), _wres(n_ff, D, p_wu), _wres(n_out, D, p_wo)],
        [row(FF), row(FF), row(D), row(D), row(D), pl.BlockSpec((1, D), lambda i: (0, 0))],
        [jax.ShapeDtypeStruct((S, FF), CDT), jax.ShapeDtypeStruct((S, FF), CDT),
         jax.ShapeDtypeStruct((S, D), F32), jax.ShapeDtypeStruct((S, D), CDT),
         jax.ShapeDtypeStruct((S, D), F32), jax.ShapeDtypeStruct((1, D), F32)],
        [], (dxb, dx, xm, gate, up, g2, G[1], G[0], G[0], G[0]), ("arbitrary",), comm)


def _b_conv(dmix, c, proj, cw, lg, lb, tm, comm=None):
    S = proj.shape[0]
    CW = cw.shape[1]
    RB = 32
    hb = tm // HALO
    nt = S // tm
    last_h = S // HALO - 1

    def body(du_ref, duh_ref, c_ref, ch_ref, ab_ref, abh_ref, cw_ref, lg_ref, lb_ref,
             dab_ref, dcw_ref, dcb_ref, dlg_ref, dlb_ref, ext_u, ext_dc, ph_u, ph_dc, wacc):
        i = pl.program_id(0)

        @pl.when(i == 0)
        def _():
            wacc[...] = jnp.zeros_like(wacc)
            dcb_ref[...] = jnp.zeros_like(dcb_ref)
            dlg_ref[...] = jnp.zeros_like(dlg_ref)
            dlb_ref[...] = jnp.zeros_like(dlb_ref)

        def ln_bwd(cv, du):
            mu = jnp.mean(cv, axis=-1, keepdims=True)
            var = jnp.mean(jnp.square(cv - mu), axis=-1, keepdims=True)
            rstd = lax.rsqrt(var + EPS)
            n = (cv - mu) * rstd
            z = n * lg_ref[...] + lb_ref[...]
            sz = _sigmoid(z)
            dz = du * (sz * (1.0 + z * (1.0 - sz)))
            dn = dz * lg_ref[...]
            dc = rstd * (dn - jnp.mean(dn, axis=-1, keepdims=True)
                         - n * jnp.mean(dn * n, axis=-1, keepdims=True))
            return dc, dz, n

        hv = abh_ref[...]
        ext_u[0:HALO, :] = jnp.where(i > 0, hv[:, :CW] * _sigmoid(hv[:, CW:]), 0.0)
        av = ab_ref[...]
        sb = _sigmoid(av[:, CW:])
        ext_u[HALO:HALO + tm, :] = av[:, :CW] * sb
        ext_u[HALO + tm:, :] = jnp.zeros((8, CW), F32)
        s_lg = s_lb = s_cb = jnp.zeros((1, CW), F32)
        for b in range(tm // RB):
            bs = slice(b * RB, (b + 1) * RB)
            dc, dz, n = ln_bwd(c_ref[bs, :], du_ref[bs, :])
            ext_dc[bs, :] = dc
            s_lg = s_lg + jnp.sum(dz * n, axis=0, keepdims=True)
            s_lb = s_lb + jnp.sum(dz, axis=0, keepdims=True)
            s_cb = s_cb + jnp.sum(dc, axis=0, keepdims=True)
        dch, _, _ = ln_bwd(ch_ref[...], duh_ref[...])
        ext_dc[tm:tm + HALO, :] = jnp.where(i < nt - 1, dch, 0.0)
        ext_dc[tm + HALO:, :] = jnp.zeros((8, CW), F32)
        dlg_ref[...] += s_lg
        dlb_ref[...] += s_lb
        dcb_ref[...] += s_cb
        _phases(ext_u, ph_u, tm + HALO)
        _phases(ext_dc, ph_dc, tm + HALO)

        for rb in range(tm // RB):
            rs = slice(rb * RB, (rb + 1) * RB)
            dcb = ext_dc[rs, :]
            for p, a0, a1, taps in _taps_by_phase([rb * RB + HALO - (CONV_K - 1) + k for k in range(CONV_K)]):
                win = ph_u[p, a0:a1 + RB, :]
                for k, a in taps:
                    prod = dcb * win[a - a0:a - a0 + RB]
                    part = prod[0:8]
                    for j in range(1, RB // 8):
                        part = part + prod[8 * j:8 * j + 8]
                    wacc[k] += part
            acc = jnp.zeros((RB, CW), F32)
            for p, a0, a1, taps in _taps_by_phase([rb * RB + (CONV_K - 1) - k for k in range(CONV_K)]):
                win = ph_dc[p, a0:a1 + RB, :]
                for k, a in taps:
                    acc = acc + cw_ref[k:k + 1, :] * win[a - a0:a - a0 + RB]
            a_r = ab_ref[rs, 0:CW]
            s_r = _sigmoid(ab_ref[rs, CW:2 * CW])
            dab_ref[rs, 0:CW] = (acc * s_r).astype(CDT)
            dab_ref[rs, CW:2 * CW] = (acc * a_r * (s_r * (1.0 - s_r))).astype(CDT)

        @pl.when(i == nt - 1)
        def _():
            for k in range(CONV_K):
                dcw_ref[k:k + 1, :] = jnp.sum(wacc[k], axis=0, keepdims=True)
            dcw_ref[CONV_K:, :] = jnp.zeros((HALO - CONV_K, CW), F32)

    tile = lambda n, j: pl.BlockSpec((tm, n), lambda i: (i, j))
    nxt = lambda n, j: pl.BlockSpec((HALO, n), lambda i: (jnp.minimum((i + 1) * hb, last_h), j))
    return _launch(
        "b_conv" if comm is None else "b_conv_rs", body, (nt,),
        [tile(CW, 0), nxt(CW, 0), tile(CW, 0), nxt(CW, 0),
         tile(2 * CW, 0),
         pl.BlockSpec((HALO, 2 * CW), lambda i: (jnp.maximum(i * hb - 1, 0), 0)),
         _resident((HALO, CW)), _resident((1, CW)), _resident((1, CW))],
        [tile(2 * CW, 0),
         pl.BlockSpec((HALO, CW), lambda i: (0, 0)), pl.BlockSpec((1, CW), lambda i: (0, 0)),
         pl.BlockSpec((1, CW), lambda i: (0, 0)), pl.BlockSpec((1, CW), lambda i: (0, 0))],
        [jax.ShapeDtypeStruct((S, 2 * CW), CDT),
         jax.ShapeDtypeStruct((HALO, CW), F32), jax.ShapeDtypeStruct((1, CW), F32),
         jax.ShapeDtypeStruct((1, CW), F32), jax.ShapeDtypeStruct((1, CW), F32)],
        [pltpu.VMEM((tm + HALO + 8, CW), F32), pltpu.VMEM((tm + HALO + 8, CW), F32),
         pltpu.VMEM((8, tm + HALO, CW), F32), pltpu.VMEM((8, tm + HALO, CW), F32),
         pltpu.VMEM((HALO, 8, CW), F32)],
        (dmix, dmix, c, c, proj, proj, cw, lg, lb), ("arbitrary",), comm)


def _b_ret(dmix, dab, proj, rraw, states, cosT, sinT, Mt, qd, kd, gs, gn, comm=None):
    S = proj.shape[0]
    RW = HEADS * HD
    NB = S // SC
    scale = HD ** -0.5

    def body(dro_ref, dab_ref, q_ref, k_ref, v_ref, g_ref, rraw_ref, st_ref, c_ref, s_ref,
             m_ref, qd_ref, kd_ref, gs_ref, gn_ref, dp_ref, dgn_ref, G):
        @pl.when(pl.program_id(0) == 0)
        def _():
            G[...] = jnp.zeros_like(G)
            dgn_ref[...] = jnp.zeros_like(dgn_ref)

        dp_ref[:, 0:2 * RW] = dab_ref[...]
        cv = c_ref[...]
        sv = s_ref[...]
        for h in range(HEADS):
            cs = slice(h * HD, (h + 1) * HD)
            q = _rot(q_ref[:, cs], cv, sv)
            k = _rot(k_ref[:, cs], cv, sv) * scale
            qb = q.astype(CDT)
            kb = k.astype(CDT)
            vb = v_ref[:, cs].astype(CDT)
            spb = st_ref[0, h]
            r = rraw_ref[:, cs]
            mu = jnp.mean(r, axis=-1, keepdims=True)
            var = jnp.mean(jnp.square(r - mu), axis=-1, keepdims=True)
            rstd = lax.rsqrt(var + EPS)
            n = (r - mu) * rstd
            gv = g_ref[:, cs]
            sg = _sigmoid(gv)
            sil = gv * sg
            dro = dro_ref[:, cs]
            gnv = gn_ref[:, cs]
            dgn_ref[:, cs] += jnp.sum(dro * n * sil, axis=0, keepdims=True)
            dgate = dro * (n * gnv) * (sg * (1.0 + gv * (1.0 - sg)))
            dn = dro * gnv * sil
            dr = rstd * (dn - jnp.mean(dn, axis=-1, keepdims=True)
                         - n * jnp.mean(dn * n, axis=-1, keepdims=True))
            drb = dr.astype(CDT)
            mh = m_ref[h]
            ab = (_dot_nt(qb, kb) * mh).astype(CDT)
            dab_ = (_dot_nt(drb, vb) * mh).astype(CDT)
            qdb = (q * qd_ref[h]).astype(CDT)
            kdb = (k * kd_ref[h]).astype(CDT)
            gc = G[h]
            gb = gc.astype(CDT)
            dq = _dot(dab_, kb) + _dot_nt(drb, spb) * qd_ref[h]
            dk = _dot_tn(dab_, qb) + _dot_nt(vb, gb) * kd_ref[h]
            dv = _dot_tn(ab, drb) + _dot(kdb, gb)
            G[h] = gs_ref[h, 0:1, :] * gc + _dot_tn(qdb, drb)
            dk = dk * scale
            dqp = dq * cv + pltpu.roll(dq * sv, HD // 2, 1)
            dkp = dk * cv + pltpu.roll(dk * sv, HD // 2, 1)
            base = 2 * RW
            dp_ref[:, base + h * HD:base + (h + 1) * HD] = dqp.astype(CDT)
            dp_ref[:, base + RW + h * HD:base + RW + (h + 1) * HD] = dkp.astype(CDT)
            dp_ref[:, base + 2 * RW + h * HD:base + 2 * RW + (h + 1) * HD] = dv.astype(CDT)
            dp_ref[:, base + 3 * RW + h * HD:base + 3 * RW + (h + 1) * HD] = dgate.astype(CDT)

    rev = lambda n, j: pl.BlockSpec((SC, n), lambda i: (NB - 1 - i, j))
    return _launch(
        "b_ret" if comm is None else "b_ret_rs", body, (NB,),
        [rev(RW, 1), rev(2 * RW, 0), rev(RW, 2), rev(RW, 3), rev(RW, 4), rev(RW, 5), rev(RW, 0),
         pl.BlockSpec((1, HEADS, HD, HD), lambda i: (NB - 1 - i, 0, 0, 0)),
         rev(HD, 0), rev(HD, 0),
         _resident((HEADS, SC, SC)), _resident((HEADS, SC, HD)), _resident((HEADS, SC, HD)),
         _resident((HEADS, 8, HD)), _resident((1, RW))],
        [rev(6 * RW, 0), pl.BlockSpec((1, RW), lambda i: (0, 0))],
        [jax.ShapeDtypeStruct((S, 6 * RW), CDT), jax.ShapeDtypeStruct((1, RW), F32)],
        [pltpu.VMEM((HEADS, HD, HD), F32)],
        (dmix, dab, proj, proj, proj, proj, rraw, states, cosT, sinT, Mt, qd, kd, gs, gn), ("arbitrary",), comm)


def _b_in(dproj, G, n_in, x, g1, dxm, tm, comm=None):
    S, D = x.shape
    N = N_DEV * n_in

    def body(dp_ref, w_ref, x_ref, g_ref, dxm_ref, dx_ref, dxb_ref, dg_ref):
        @pl.when(pl.program_id(0) == 0)
        def _():
            dg_ref[...] = jnp.zeros_like(dg_ref)

        dh = _dot(dp_ref[...], w_ref[...].reshape(N, D))
        xv = x_ref[...]
        r = lax.rsqrt(jnp.mean(xv * xv, axis=-1, keepdims=True) + EPS)
        dxn, dgx = _rms_bwd(xv, r, g_ref[...], dh)
        dg_ref[...] += jnp.sum(dgx, axis=0, keepdims=True)
        dx = dxm_ref[...] + dxn
        dx_ref[...] = dx
        dxb_ref[...] = dx.astype(CDT)

    row = lambda n: pl.BlockSpec((tm, n), lambda i: (i, 0))
    return _launch(
        "b_in" if comm is None else "b_in_rs", body, (S // tm,),
        [row(N), _wres(n_in, D, 0), row(D), _resident((1, D)), row(D)],
        [row(D), row(D), pl.BlockSpec((1, D), lambda i: (0, 0))],
        [jax.ShapeDtypeStruct((S, D), F32), jax.ShapeDtypeStruct((S, D), CDT),
         jax.ShapeDtypeStruct((1, D), F32)],
        [], (dproj, G, x, g1, dxm), ("arbitrary",), comm)


def _dw_tn(a, b, tm, tk):
    S, M = a.shape
    N = b.shape[1]
    nk = S // tk

    def body(a_ref, b_ref, o_ref, ob_ref):
        k = pl.program_id(1)

        @pl.when(k == 0)
        def _():
            o_ref[...] = jnp.zeros_like(o_ref)

        o_ref[...] += _dot_tn(a_ref[...], b_ref[...])

        @pl.when(k == nk - 1)
        def _():
            ob_ref[...] = o_ref[...].astype(CDT)

    out = pl.BlockSpec((tm, N), lambda m, k: (m, 0))
    return _call(
        body, name="dw_tn", grid=(M // tm, nk),
        in_specs=[pl.BlockSpec((tk, tm), lambda m, k: (k, m)), pl.BlockSpec((tk, N), lambda m, k: (k, 0))],
        out_specs=[out, out],
        out_shape=[jax.ShapeDtypeStruct((M, N), F32), jax.ShapeDtypeStruct((M, N), CDT)],
        compiler_params=_cp(("parallel", "arbitrary")),
    )(a, b)


def _adamw(w, g, m, v, tr):
    R, C = w.shape
    c1 = 1.0 - ADAM_B1 ** ADAM_STEP
    c2 = 1.0 - ADAM_B2 ** ADAM_STEP

    def body(w_ref, g_ref, m_ref, v_ref, d_ref, mo_ref, vo_ref):
        gv = g_ref[...]
        mn = ADAM_B1 * m_ref[...] + (1.0 - ADAM_B1) * gv
        vn = ADAM_B2 * v_ref[...] + (1.0 - ADAM_B2) * jnp.square(gv)
        mo_ref[...] = mn
        vo_ref[...] = vn
        d_ref[...] = -ADAM_LR * ((mn / c1) / (jnp.sqrt(vn / c2) + ADAM_EPS) + ADAM_WD * w_ref[...])

    blk = pl.BlockSpec((tr, C), lambda i: (i, 0))
    sh = jax.ShapeDtypeStruct((R, C), F32)
    return _call(
        body, name="adamw", grid=(R // tr,),
        in_specs=[blk, blk, blk, blk], out_specs=[blk, blk, blk], out_shape=[sh, sh, sh],
        compiler_params=_cp(("parallel",)),
    )(w, g, m, v)


def _coords():
    return lax.axis_index("x"), lax.axis_index("y"), lax.axis_index("c")


def _peer(x, y, c, d):
    return (x ^ (d >> 2), y ^ ((d >> 1) & 1), c ^ (d & 1))


def _ag_comm(ps):
    K = len(ps)

    def plan(cins, couts, sems):
        send_sems, recv_sems, local_sems = sems
        x, y, c = _coords()
        me, sibling = (x, y, c), (x, y, 1 - c)
        chips = [(1 - x, y), (x, 1 - y), (1 - x, 1 - y)]
        mine, first, passed, got_ici, got_d2d = [], [], [], [], []
        for a in range(K):
            x_ref, out_ref = cins[a], couts[a]
            R = x_ref.shape[0]

            def rows(px, py, pc, out_ref=out_ref, R=R):
                return out_ref.at[pl.ds((4 * px + 2 * py + pc) * R, R), :]

            def copy(k, block, to, src=None, rows=rows, a=a):
                return pltpu.make_async_remote_copy(
                    src_ref=rows(*block) if src is None else src, dst_ref=rows(*block),
                    send_sem=send_sems.at[7 * a + k], recv_sem=recv_sems.at[7 * a + k],
                    device_id=to, device_id_type=MESH)

            mine.append(pltpu.make_async_copy(x_ref, rows(*me), local_sems.at[a]))
            first.append(copy(0, me, sibling, src=x_ref))
            first += [copy(1 + j, me, (*chip, c), src=x_ref) for j, chip in enumerate(chips)]
            passed += [copy(4 + j, (*chip, c), sibling) for j, chip in enumerate(chips)]
            got_ici += [copy(1 + j, (*chip, c), me) for j, chip in enumerate(chips)]
            got_d2d.append(copy(0, sibling, me))
            got_d2d += [copy(4 + j, (*chip, 1 - c), me) for j, chip in enumerate(chips)]
        return mine, first, passed, got_ici, got_d2d

    def start(*a):
        mine, first, _, _, _ = plan(*a)
        for cp in mine + first:
            cp.start()

    def mid(*a):
        _, _, passed, got_ici, _ = plan(*a)
        for got, fwd in zip(got_ici, passed):
            got.wait_recv()
            fwd.start()

    def finish(*a):
        mine, first, passed, _, got_d2d = plan(*a)
        for got in got_d2d:
            got.wait_recv()
        for cp in first + passed:
            cp.wait_send()
        for cp in mine:
            cp.wait()

    return _Comm(ps, [jax.ShapeDtypeStruct((N_DEV * p.shape[0], p.shape[1]), p.dtype) for p in ps],
                 [pltpu.SemaphoreType.DMA((7 * K,)), pltpu.SemaphoreType.DMA((7 * K,)),
                  pltpu.SemaphoreType.DMA((K,))], start, mid, finish)


def _rs_direct_comm(parts):
    K = len(parts)

    def plan(cins, couts, sems):
        send_sems, recv_sems = sems
        x, y, c = _coords()
        cps = []
        for d in range(1, N_DEV):
            px, py, pc = _peer(x, y, c, d)
            for k in range(K):
                s = (d - 1) * K + k
                cps.append(pltpu.make_async_remote_copy(
                    src_ref=cins[k].at[4 * px + 2 * py + pc], dst_ref=couts[k].at[d - 1],
                    send_sem=send_sems.at[s], recv_sem=recv_sems.at[s], device_id=(px, py, pc),
                    device_id_type=MESH))
        return cps

    def start(*a):
        for cp in plan(*a):
            cp.start()

    def finish(*a):
        cps = plan(*a)
        for cp in cps:
            cp.wait_recv()
        for cp in cps:
            cp.wait_send()

    n_sem = (N_DEV - 1) * K
    return _Comm(parts, [jax.ShapeDtypeStruct((N_DEV - 1,) + p.shape[1:], p.dtype) for p in parts],
                 [pltpu.SemaphoreType.DMA((n_sem,)), pltpu.SemaphoreType.DMA((n_sem,))], start, None, finish)


def _sum_all(parts, recv):
    K = len(parts)

    def body(*refs):
        ins, rcv, outs = refs[:K], refs[K:2 * K], refs[2 * K:3 * K]
        bufs, sem = refs[3 * K:4 * K], refs[4 * K]
        x, y, c = _coords()
        me = 4 * x + 2 * y + c
        cps = [pltpu.make_async_copy(ins[k].at[me], bufs[k], sem.at[k]) for k in range(K)]
        for cp in cps:
            cp.start()
        for k in range(K):
            cps[k].wait()
            acc = bufs[k][...]
            for d in range(N_DEV - 1):
                acc = acc + rcv[k][d].astype(F32)
            outs[k][...] = acc

    vm = pl.BlockSpec(memory_space=pltpu.VMEM)
    res = _call(
        body, name="sum_all", in_specs=[_ANY] * K + [vm] * K, out_specs=[vm] * K,
        out_shape=[jax.ShapeDtypeStruct(p.shape[1:], F32) for p in parts],
        scratch_shapes=[pltpu.VMEM(p.shape[1:], F32) for p in parts] + [pltpu.SemaphoreType.DMA((K,))],
        compiler_params=_cp(),
    )(*parts, *recv)
    return list(res)


def _gather_small(v, reduce):
    R, C = v.shape

    def exchange(v_ref, buf, send_sems, recv_sems):
        x, y, c = _coords()
        me = 4 * x + 2 * y + c
        buf[me] = v_ref[...]
        cps = []
        for d in range(1, N_DEV):
            cp = pltpu.make_async_remote_copy(
                src_ref=v_ref, dst_ref=buf.at[me], send_sem=send_sems.at[d - 1], recv_sem=recv_sems.at[d - 1],
                device_id=_peer(x, y, c, d), device_id_type=MESH)
            cp.start()
            cps.append(cp)
        for cp in cps:
            cp.wait_recv()
        for cp in cps:
            cp.wait_send()

    sems = [pltpu.SemaphoreType.DMA((7,)), pltpu.SemaphoreType.DMA((7,))]
    vm = pl.BlockSpec(memory_space=pltpu.VMEM)
    if reduce:
        def body(v_ref, o_ref, buf, send_sems, recv_sems):
            exchange(v_ref, buf, send_sems, recv_sems)
            acc = buf[0]
            for s in range(1, N_DEV):
                acc = acc + buf[s]
            o_ref[...] = acc

        return _call(body, name="allreduce_small", in_specs=[vm], out_specs=vm,
                     out_shape=jax.ShapeDtypeStruct((R, C), F32),
                     scratch_shapes=[pltpu.VMEM((N_DEV, R, C), F32)] + sems)(v)

    def body(v_ref, o_ref, send_sems, recv_sems):
        exchange(v_ref, o_ref, send_sems, recv_sems)

    return _call(body, name="allgather_small", in_specs=[vm], out_specs=vm,
                 out_shape=jax.ShapeDtypeStruct((N_DEV, R, C), F32), scratch_shapes=sems)(v)


def _tables(S):
    half = HD // 2
    pos = jnp.arange(S, dtype=F32)
    freqs = ROPE_BASE ** (-jnp.arange(half, dtype=F32) / half)
    ang = pos[:, None] * freqs[None, :]
    cos, sin = jnp.cos(ang), jnp.sin(ang)
    cosT = jnp.concatenate([cos, cos], axis=-1)
    sinT = jnp.concatenate([-sin, sin], axis=-1)
    log_g = jnp.log(1.0 - 2.0 ** (-5.0 - jnp.arange(HEADS, dtype=F32)))
    idx = jnp.arange(SC, dtype=F32)
    ci = jnp.arange(SC) // CHUNK
    diff = idx[:, None] - idx[None, :]
    same = ci[:, None] == ci[None, :]
    earlier = ci[None, :] < ci[:, None]
    expo = jnp.where(same, jnp.abs(diff), diff)
    Mt = jnp.where((same | earlier)[None], jnp.exp(log_g[:, None, None] * expo[None]), 0.0)
    ones = jnp.ones((1, 1, HD), F32)
    qd = jnp.exp(log_g[:, None] * (idx + 1.0)[None, :])[:, :, None] * ones
    kd = jnp.exp(log_g[:, None] * (SC - 1.0 - idx)[None, :])[:, :, None] * ones
    gs = jnp.exp(log_g * SC)[:, None, None] * jnp.ones((1, 8, HD), F32)
    return cosT, sinT, Mt, qd, kd, gs


def _pad_rows(a, rows):
    return jnp.pad(a, ((0, rows - a.shape[0]), (0, 0)))


def kernel(x, norm1_g, w_in, conv_w, conv_b, conv_ln_g, conv_ln_b, ret_gn_g, w_out, norm2_g, w_gate, w_up, w_down, final_g, loss_target, m_norm1_g, m_w_in, m_conv_w, m_conv_b, m_conv_ln_g, m_conv_ln_b, m_ret_gn_g, m_w_out, m_norm2_g, m_w_gate, m_w_up, m_w_down, m_final_g, v_norm1_g, v_w_in, v_conv_w, v_conv_b, v_conv_ln_g, v_conv_ln_b, v_ret_gn_g, v_w_out, v_norm2_g, v_w_gate, v_w_up, v_w_down, v_final_g):
    L, D, n_in = w_in.shape
    n_out = w_out.shape[1]
    n_ff = w_gate.shape[2]
    S = x.shape[1]
    CW = conv_b.shape[1]
    IN, FF = N_DEV * n_in, N_DEV * n_ff
    ncw = conv_w.shape[2]
    x0 = x.reshape(S, D)
    tgt = loss_target.reshape(S, D)
    TM = min(512, S)
    TKW = min(2048, S)
    TMI = min(512, S)
    TMM = min(256, S)

    assert n_out <= n_ff
    wparts = (0, 1, 2, 0)
    pack_a = jnp.swapaxes(w_in, 1, 2).astype(CDT)
    pack_b = jnp.concatenate([w_out, jnp.zeros((L, n_ff - n_out, D), F32), jnp.swapaxes(w_gate, 1, 2),
                              jnp.swapaxes(w_up, 1, 2)], axis=1).astype(CDT)
    pack_c = w_down.astype(CDT)
    per_dev = lambda g: g.reshape(N_DEV, -1, D)
    Ga = per_dev(_comm_only(_ag_comm([pack_a[0]]), "ag_first")[0])
    Gb = Gc = None

    cwp = conv_w.reshape(L * CONV_K * ncw // 128, 128)
    cw_rows = -(-cwp.shape[0] // 8) * 8
    cwg = _gather_small(_pad_rows(cwp, cw_rows), reduce=False)[:, :cwp.shape[0], :]
    conv_w_full = jnp.moveaxis(cwg.reshape(N_DEV, L, CONV_K, ncw), 0, 2).reshape(L, CONV_K, CW)

    cosT, sinT, Mt, qd, kd, gs = _tables(S)

    saved = []
    xl = x0
    for l in range(L):
        cw = _pad_rows(conv_w_full[l], HALO)
        (h, proj, c, u), got = _f_in_conv(xl, norm1_g[l][None], Ga, n_in, cw, conv_b[l][None], conv_ln_g[l][None],
                                          conv_ln_b[l][None], TMI, _ag_comm([pack_b[0]]) if l == 0 else None)
        if got:
            Gb = per_dev(got[0])
        more = l + 1 < L
        ret_gather = ([pack_c[0]] if l == 0 else []) + ([pack_a[l + 1]] if more else [])
        (rraw, states, mixed), got = _f_ret(proj, u, cosT, sinT, Mt, qd, kd, gs, ret_gn_g[l][None],
                                            _ag_comm(ret_gather) if ret_gather else None)
        if l == 0:
            Gc = per_dev(got[0])
        (xm, h2, gate, up, act, xo), nxt = _f_mlp(
            xl, mixed, norm2_g[l][None], (Gb, Gc), n_out, n_ff, wparts, TMM,
            _ag_comm([pack_b[l + 1], pack_c[l + 1]]) if more else None)
        saved.append(dict(x=xl, h=h, proj=proj, c=c, rraw=rraw, states=states, mixed=mixed, xm=xm, h2=h2,
                          gate=gate, up=up, act=act, cw=cw, Ga=Ga, Gbc=(Gb, Gc)))
        if more:
            Ga, Gb, Gc = per_dev(got[-1]), per_dev(nxt[0]), per_dev(nxt[1])
        xl = xo

    dx, dxb, loss_p, dfg = _f_loss(xl, final_g[None], tgt, TM)

    small = []
    own = [None] * L
    recv = [None] * L
    blocks = lambda d: d.reshape(N_DEV, -1, D)
    pending = None
    for l in reversed(range(L)):
        sv = saved[l]
        (dgate, dup, dxm, dxmb, dmix, dg2), got = _b_mlp(
            dxb, dx, sv["xm"], sv["gate"], sv["up"], norm2_g[l][None], sv["Gbc"], n_out, n_ff, wparts, TMM, pending)
        if got:
            recv[l + 1][0] = got[0]
        d_wd, d_wd_b = _dw_tn(sv["act"], dxb, FF // 2, TKW)
        d_wgT, d_wgT_b = _dw_tn(dgate, sv["h2"], FF // 2, TKW)
        d_wuT, d_wuT_b = _dw_tn(dup, sv["h2"], FF // 2, TKW)
        d_wo, d_wo_b = _dw_tn(sv["mixed"], dxmb, D, TKW)
        (dab, dcw, dcb, dlg, dlb), r_go = _b_conv(dmix, sv["c"], sv["proj"], sv["cw"], conv_ln_g[l][None],
                                                  conv_ln_b[l][None], TM,
                                                  _rs_direct_comm([blocks(d_wgT_b), blocks(d_wo_b)]))
        (dproj, dgn), r_ud = _b_ret(dmix, dab, sv["proj"], sv["rraw"], sv["states"], cosT, sinT, Mt, qd, kd, gs,
                                    ret_gn_g[l][None], _rs_direct_comm([blocks(d_wuT_b), blocks(d_wd_b)]))
        d_winT, d_winT_b = _dw_tn(dproj, sv["h"], IN // 2, TKW)
        pending = _rs_direct_comm([blocks(d_winT_b)])
        (dx, dxb, dg1), r_i = _b_in(dproj, sv["Ga"], n_in, sv["x"], norm1_g[l][None], dxm, TMI,
                                    pending if l == 0 else None)
        own[l] = [blocks(d) for d in (d_winT, d_wo, d_wgT, d_wuT, d_wd)]
        recv[l] = [r_i[0] if l == 0 else None, r_go[1], r_go[0], r_ud[0], r_ud[1]]
        small.append(jnp.concatenate([dcw, dcb, dlg, dlb, dgn, dg1.reshape(2, CW), dg2.reshape(2, CW)], axis=0))
    small = small[::-1]
    grad_x = dx.reshape(1, S, D)

    rows_l = HALO + 8
    loss_row = jnp.zeros((1, CW), F32).at[0, 0].set(loss_p[0, 0])
    sm = jnp.concatenate(small + [dfg.reshape(2, CW), loss_row], axis=0)
    sm_rows = -(-sm.shape[0] // 8) * 8
    sm = _gather_small(_pad_rows(sm, sm_rows), reduce=True)
    loss = sm[L * rows_l + 2, 0]
    g_final = sm[L * rows_l:L * rows_l + 2].reshape(D)
    per = sm[:L * rows_l].reshape(L, rows_l, CW)
    me = 4 * lax.axis_index("x") + 2 * lax.axis_index("y") + lax.axis_index("c")
    g_conv_w = lax.dynamic_slice_in_dim(per[:, :CONV_K, :], me * ncw, ncw, axis=2)
    g_conv_b, g_ln_g, g_ln_b, g_gn = per[:, HALO], per[:, HALO + 1], per[:, HALO + 2], per[:, HALO + 3]
    g_n1 = per[:, HALO + 4:HALO + 6].reshape(L, D)
    g_n2 = per[:, HALO + 6:HALO + 8].reshape(L, D)

    gl = [_sum_all(own[l], recv[l]) for l in range(L)]
    g_w_in = jnp.stack([gl[l][0].T for l in range(L)])
    g_w_out = jnp.stack([gl[l][1] for l in range(L)])
    g_w_gate = jnp.stack([gl[l][2].T for l in range(L)])
    g_w_up = jnp.stack([gl[l][3].T for l in range(L)])
    g_w_down = jnp.stack([gl[l][4] for l in range(L)])

    def big(w, g, m, v):
        sh = w.shape
        two = lambda a: a.reshape(-1, sh[-1])
        rows = two(w).shape[0]
        d, mn, vn = _adamw(two(w), two(g), two(m), two(v), rows // 8)
        return d.reshape(sh), mn.reshape(sh), vn.reshape(sh)

    names = ["norm1_g", "conv_w", "conv_b", "conv_ln_g", "conv_ln_b", "ret_gn_g", "norm2_g", "final_g"]
    sw = dict(norm1_g=(norm1_g, g_n1, m_norm1_g, v_norm1_g), conv_w=(conv_w, g_conv_w, m_conv_w, v_conv_w),
              conv_b=(conv_b, g_conv_b, m_conv_b, v_conv_b), conv_ln_g=(conv_ln_g, g_ln_g, m_conv_ln_g, v_conv_ln_g),
              conv_ln_b=(conv_ln_b, g_ln_b, m_conv_ln_b, v_conv_ln_b), ret_gn_g=(ret_gn_g, g_gn, m_ret_gn_g, v_ret_gn_g),
              norm2_g=(norm2_g, g_n2, m_norm2_g, v_norm2_g), final_g=(final_g, g_final, m_final_g, v_final_g))
    lens = [int(math.prod(sw[n][0].shape)) for n in names]
    tot = sum(lens)
    prow = -(-tot // (8 * CW)) * 8

    def packs(j):
        flat = jnp.concatenate([sw[n][j].reshape(-1) for n in names])
        return jnp.pad(flat, (0, prow * CW - tot)).reshape(prow, CW)

    sd, smn, svn = _adamw(packs(0), packs(1), packs(2), packs(3), prow)

    def unpack(a):
        flat = a.reshape(-1)
        out, o = {}, 0
        for n, ln in zip(names, lens):
            out[n] = flat[o:o + ln].reshape(sw[n][0].shape)
            o += ln
        return out

    sd, smn, svn = unpack(sd), unpack(smn), unpack(svn)
    res = {n: (sw[n][1], sd[n], smn[n], svn[n]) for n in names}
    res["w_in"] = (g_w_in,) + big(w_in, g_w_in, m_w_in, v_w_in)
    res["w_out"] = (g_w_out,) + big(w_out, g_w_out, m_w_out, v_w_out)
    res["w_gate"] = (g_w_gate,) + big(w_gate, g_w_gate, m_w_gate, v_w_gate)
    res["w_up"] = (g_w_up,) + big(w_up, g_w_up, m_w_up, v_w_up)
    res["w_down"] = (g_w_down,) + big(w_down, g_w_down, m_w_down, v_w_down)

    order = ["norm1_g", "w_in", "conv_w", "conv_b", "conv_ln_g", "conv_ln_b", "ret_gn_g", "w_out", "norm2_g",
             "w_gate", "w_up", "w_down", "final_g"]
    return (loss, grad_x, *[res[n][0] for n in order], *[res[n][1] for n in order],
            *[res[n][2] for n in order], *[res[n][3] for n in order])
```

```python
import math

import jax
import jax.numpy as jnp
from jax import lax
from jax.experimental import pallas as pl
from jax.experimental.pallas import tpu as pltpu

F32 = jnp.float32
CDT = jnp.bfloat16
EPS = 1e-6
CHUNK = 64
SC = 256
HEADS = 4
HD = 128
CONV_K = 31
HALO = 32
ROPE_BASE = 10000.0
ADAM_LR = 0.001
ADAM_B1 = 0.9
ADAM_B2 = 0.999
ADAM_EPS = 1e-08
ADAM_WD = 0.01
ADAM_STEP = 10
N_DEV = 8
MESH = pl.DeviceIdType.MESH
VMEM_LIMIT = 60 * 1024 * 1024


def _call(body, **kw):
    return pl.pallas_call(body, **kw)


def _cp(sem=None, vmem=VMEM_LIMIT):
    return pltpu.CompilerParams(dimension_semantics=sem, vmem_limit_bytes=vmem)


def _resident(shape):
    nd = len(shape)
    return pl.BlockSpec(shape, lambda *_: (0,) * nd, pipeline_mode=pl.Buffered(1))


def _dot(a, b):
    return jnp.dot(a, b, preferred_element_type=F32)


def _dot_nt(a, b):
    return lax.dot_general(a, b, (((1,), (1,)), ((), ())), preferred_element_type=F32)


def _dot_tn(a, b):
    return lax.dot_general(a, b, (((0,), (0,)), ((), ())), preferred_element_type=F32)


def _sigmoid(x):
    return 1.0 / (1.0 + jnp.exp(-x))


def _rms_bwd(x, r, g, dy):
    xh = x * r
    dyg = dy * g
    dx = r * (dyg - xh * jnp.mean(dyg * xh, axis=-1, keepdims=True))
    return dx, dy * xh


class _Comm:
    def __init__(self, ins, out_shapes, sems, start, mid, finish):
        self.ins, self.out_shapes, self.sems = list(ins), list(out_shapes), list(sems)
        self.start, self.mid, self.finish = start, mid, finish


_ANY = pl.BlockSpec(memory_space=pl.ANY)


def _launch(name, compute, grid, in_specs, out_specs, out_shape, scratch, operands, sem, comm=None):
    n_in, n_out, n_sc = len(in_specs), len(out_specs), len(scratch)
    if comm is None:
        res = _call(compute, name=name, grid=grid, in_specs=in_specs, out_specs=out_specs, out_shape=out_shape,
                    scratch_shapes=scratch, compiler_params=_cp(sem))(*operands)
        return list(res), []
    c_in, c_out = len(comm.ins), len(comm.out_shapes)
    inner = grid[1] if len(grid) > 1 else 1
    steps = grid[0] * inner
    mid_step = (3 * steps) // 4

    def body(*refs):
        ins = refs[:n_in]
        cins = refs[n_in:n_in + c_in]
        o = n_in + c_in
        outs = refs[o:o + n_out]
        couts = refs[o + n_out:o + n_out + c_out]
        o += n_out + c_out
        sc = refs[o:o + n_sc]
        csem = refs[o + n_sc:]
        i = pl.program_id(0)
        if len(grid) > 1:
            i = i * inner + pl.program_id(1)

        @pl.when(i == 0)
        def _():
            comm.start(cins, couts, csem)

        if comm.mid is not None:
            @pl.when(i == mid_step)
            def _():
                comm.mid(cins, couts, csem)

        compute(*ins, *outs, *sc)

        @pl.when(i == steps - 1)
        def _():
            comm.finish(cins, couts, csem)

    res = _call(body, name=name, grid=grid, in_specs=list(in_specs) + [_ANY] * c_in,
                out_specs=list(out_specs) + [_ANY] * c_out, out_shape=list(out_shape) + comm.out_shapes,
                scratch_shapes=list(scratch) + comm.sems,
                compiler_params=_cp(("arbitrary",) * len(grid)))(*operands, *comm.ins)
    return list(res[:n_out]), list(res[n_out:])


def _comm_only(comm, name):
    c_in, c_out = len(comm.ins), len(comm.out_shapes)

    def body(*refs):
        cins, couts, csem = refs[:c_in], refs[c_in:c_in + c_out], refs[c_in + c_out:]
        comm.start(cins, couts, csem)
        if comm.mid is not None:
            comm.mid(cins, couts, csem)
        comm.finish(cins, couts, csem)

    res = _call(body, name=name, in_specs=[_ANY] * c_in, out_specs=[_ANY] * c_out, out_shape=comm.out_shapes,
                scratch_shapes=comm.sems)(*comm.ins)
    return list(res)


def _wres(n, D, part):
    return pl.BlockSpec((N_DEV, n, D), lambda i: (0, part, 0), pipeline_mode=pl.Buffered(1))


def _taps_by_phase(offsets):
    groups = []
    for p in range(8):
        taps = [(k, o - p) for k, o in enumerate(offsets) if o % 8 == p]
        if taps:
            rows = [a for _, a in taps]
            groups.append((p, min(rows), max(rows), taps))
    return groups


def _phases(ext, ph, rows):
    for p in range(8):
        ph[p, :, :] = ext[p:p + rows, :]


def _f_in_conv(x, g, G, n_in, cw, cb, lg, lb, tm, comm=None):
    S, D = x.shape
    N = N_DEV * n_in
    CW = cw.shape[1]
    RB = 64

    def body(x_ref, g_ref, w_ref, cw_ref, cb_ref, lg_ref, lb_ref, h_ref, p_ref, c_ref, u_ref, ext, ph):
        i = pl.program_id(0)

        @pl.when(i == 0)
        def _():
            ext[0:HALO, :] = jnp.zeros((HALO, CW), F32)
            ext[HALO + tm:, :] = jnp.zeros((8, CW), F32)

        @pl.when(i > 0)
        def _():
            ext[0:HALO, :] = ext[tm:tm + HALO, :]

        xv = x_ref[...]
        r = lax.rsqrt(jnp.mean(xv * xv, axis=-1, keepdims=True) + EPS)
        h = ((xv * r) * g_ref[...]).astype(CDT)
        h_ref[...] = h
        w = w_ref[...].reshape(N, D)
        pab = _dot_nt(h, w[0:2 * CW])
        p_ref[:, 0:2 * CW] = pab
        p_ref[:, 2 * CW:] = _dot_nt(h, w[2 * CW:])
        ext[HALO:HALO + tm, :] = pab[:, :CW] * _sigmoid(pab[:, CW:])
        _phases(ext, ph, tm + HALO)
        for rb in range(tm // RB):
            acc = jnp.zeros((RB, CW), F32) + cb_ref[...]
            for k in range(CONV_K):
                o = rb * RB + HALO - (CONV_K - 1) + k
                acc = acc + cw_ref[k:k + 1, :] * ph[o % 8, o - o % 8:o - o % 8 + RB, :]
            c_ref[rb * RB:(rb + 1) * RB, :] = acc
            mu = jnp.mean(acc, axis=-1, keepdims=True)
            var = jnp.mean(jnp.square(acc - mu), axis=-1, keepdims=True)
            z = ((acc - mu) * lax.rsqrt(var + EPS)) * lg_ref[...] + lb_ref[...]
            u_ref[rb * RB:(rb + 1) * RB, :] = (z * _sigmoid(z)).astype(CDT)

    row = lambda n: pl.BlockSpec((tm, n), lambda i: (i, 0))
    return _launch(
        "f_in_conv" if comm is None else "f_in_conv_ag", body, (S // tm,),
        [row(D), _resident((1, D)), _wres(n_in, D, 0),
         _resident((HALO, CW)), _resident((1, CW)), _resident((1, CW)), _resident((1, CW))],
        [row(D), row(N), row(CW), row(CW)],
        [jax.ShapeDtypeStruct((S, D), CDT), jax.ShapeDtypeStruct((S, N), F32),
         jax.ShapeDtypeStruct((S, CW), F32), jax.ShapeDtypeStruct((S, CW), CDT)],
        [pltpu.VMEM((tm + HALO + 8, CW), F32), pltpu.VMEM((8, tm + HALO, CW), F32)],
        (x, g, G, cw, cb, lg, lb), ("arbitrary",), comm)


def _rot(t, c, s):
    return t * c + pltpu.roll(t, HD // 2, 1) * s


def _f_ret(proj, u, cosT, sinT, Mt, qd, kd, gs, gn, comm=None):
    S = proj.shape[0]
    RW = HEADS * HD
    NB = S // SC
    scale = HD ** -0.5

    def body(q_ref, k_ref, v_ref, g_ref, u_ref, c_ref, s_ref, m_ref, qd_ref, kd_ref, gs_ref, gn_ref,
             rraw_ref, st_ref, mix_ref, state):
        @pl.when(pl.program_id(0) == 0)
        def _():
            state[...] = jnp.zeros_like(state)

        mix_ref[:, 0:RW] = u_ref[...]
        cv = c_ref[...]
        sv = s_ref[...]
        for h in range(HEADS):
            cs = slice(h * HD, (h + 1) * HD)
            q = _rot(q_ref[:, cs], cv, sv)
            k = _rot(k_ref[:, cs], cv, sv) * scale
            vb = v_ref[:, cs].astype(CDT)
            qb = q.astype(CDT)
            kb = k.astype(CDT)
            a = _dot_nt(qb, kb) * m_ref[h]
            sp = state[h]
            spb = sp.astype(CDT)
            st_ref[0, h] = spb
            r = _dot(a.astype(CDT), vb) + _dot((q * qd_ref[h]).astype(CDT), spb)
            kv = _dot_tn((k * kd_ref[h]).astype(CDT), vb)
            state[h] = gs_ref[h, 0:1, :] * sp + kv
            rraw_ref[:, cs] = r
            mu = jnp.mean(r, axis=-1, keepdims=True)
            var = jnp.mean(jnp.square(r - mu), axis=-1, keepdims=True)
            n = (r - mu) * lax.rsqrt(var + EPS)
            gv = g_ref[:, cs]
            mix_ref[:, RW + h * HD:RW + (h + 1) * HD] = ((n * gn_ref[:, cs]) * (gv * _sigmoid(gv))).astype(CDT)

    col = lambda j: pl.BlockSpec((SC, RW), lambda i: (i, j))
    return _launch(
        "f_ret" if comm is None else "f_ret_ag", body, (NB,),
        [col(2), col(3), col(4), col(5),
         pl.BlockSpec((SC, RW), lambda i: (i, 0)),
         pl.BlockSpec((SC, HD), lambda i: (i, 0)), pl.BlockSpec((SC, HD), lambda i: (i, 0)),
         _resident((HEADS, SC, SC)), _resident((HEADS, SC, HD)), _resident((HEADS, SC, HD)),
         _resident((HEADS, 8, HD)), _resident((1, RW))],
        [pl.BlockSpec((SC, RW), lambda i: (i, 0)),
         pl.BlockSpec((1, HEADS, HD, HD), lambda i: (i, 0, 0, 0)),
         pl.BlockSpec((SC, 2 * RW), lambda i: (i, 0))],
        [jax.ShapeDtypeStruct((S, RW), F32),
         jax.ShapeDtypeStruct((NB, HEADS, HD, HD), CDT),
         jax.ShapeDtypeStruct((S, 2 * RW), CDT)],
        [pltpu.VMEM((HEADS, HD, HD), F32)],
        (proj, proj, proj, proj, u, cosT, sinT, Mt, qd, kd, gs, gn), ("arbitrary",), comm)


def _f_mlp(x, mixed, g2, G, n_out, n_ff, parts, tm, comm=None):
    S, D = x.shape
    FF = N_DEV * n_ff
    p_wo, p_wg, p_wu, p_wd = parts

    def body(x_ref, m_ref, wo_ref, g_ref, wg_ref, wu_ref, wd_ref,
             xm_ref, h2_ref, gate_ref, up_ref, act_ref, xo_ref):
        xm = x_ref[...] + _dot(m_ref[...], wo_ref[...].reshape(N_DEV * n_out, D))
        xm_ref[...] = xm
        r = lax.rsqrt(jnp.mean(xm * xm, axis=-1, keepdims=True) + EPS)
        h2 = ((xm * r) * g_ref[...]).astype(CDT)
        h2_ref[...] = h2
        gate = _dot_nt(h2, wg_ref[...].reshape(FF, D))
        up = _dot_nt(h2, wu_ref[...].reshape(FF, D))
        gate_ref[...] = gate.astype(CDT)
        up_ref[...] = up.astype(CDT)
        act = ((gate * _sigmoid(gate)) * up).astype(CDT)
        act_ref[...] = act
        xo_ref[...] = xm + _dot(act, wd_ref[...].reshape(FF, D))

    row = lambda n: pl.BlockSpec((tm, n), lambda i: (i, 0))
    return _launch(
        "f_mlp" if comm is None else "f_mlp_ag", body, (S // tm,),
        [row(D), row(D), _wres(n_out, D, p_wo), _resident((1, D)),
         _wres(n_ff, D, p_wg), _wres(n_ff, D, p_wu), _wres(n_ff, D, p_wd)],
        [row(D), row(D), row(FF), row(FF), row(FF), row(D)],
        [jax.ShapeDtypeStruct((S, D), F32), jax.ShapeDtypeStruct((S, D), CDT),
         jax.ShapeDtypeStruct((S, FF), CDT), jax.ShapeDtypeStruct((S, FF), CDT),
         jax.ShapeDtypeStruct((S, FF), CDT), jax.ShapeDtypeStruct((S, D), F32)],
        [], (x, mixed, G[0], g2, G[0], G[0], G[1]), ("parallel",), comm)


def _f_loss(x, fg, tgt, tm):
    S, D = x.shape

    def body(x_ref, g_ref, t_ref, dx_ref, dxb_ref, loss_ref, dg_ref):
        @pl.when(pl.program_id(0) == 0)
        def _():
            loss_ref[...] = jnp.zeros_like(loss_ref)
            dg_ref[...] = jnp.zeros_like(dg_ref)

        xv = x_ref[...]
        r = lax.rsqrt(jnp.mean(xv * xv, axis=-1, keepdims=True) + EPS)
        y = (xv * r) * g_ref[...]
        e = y - t_ref[...]
        loss_ref[...] += 0.5 * jnp.sum(jnp.mean(e * e, axis=-1, keepdims=True))
        dy = e * (1.0 / D)
        dx, dgx = _rms_bwd(xv, r, g_ref[...], dy)
        dg_ref[...] += jnp.sum(dgx, axis=0, keepdims=True)
        dx_ref[...] = dx
        dxb_ref[...] = dx.astype(CDT)

    row = pl.BlockSpec((tm, D), lambda i: (i, 0))
    return _call(
        body, name="f_loss", grid=(S // tm,),
        in_specs=[row, _resident((1, D)), row],
        out_specs=[row, row, pl.BlockSpec((1, 128), lambda i: (0, 0)), pl.BlockSpec((1, D), lambda i: (0, 0))],
        out_shape=[jax.ShapeDtypeStruct((S, D), F32), jax.ShapeDtypeStruct((S, D), CDT),
                   jax.ShapeDtypeStruct((1, 128), F32), jax.ShapeDtypeStruct((1, D), F32)],
        compiler_params=_cp(("arbitrary",)),
    )(x, fg, tgt)


def _b_mlp(dxb, dx, xm, gate, up, g2, G, n_out, n_ff, parts, tm, comm=None):
    S, D = dx.shape
    FF = N_DEV * n_ff
    p_wo, p_wg, p_wu, p_wd = parts

    def body(dxb_ref, dx_ref, xm_ref, gate_ref, up_ref, g_ref, wd_ref, wg_ref, wu_ref, wo_ref,
             dgate_ref, dup_ref, dxm_ref, dxmb_ref, dmix_ref, dg_ref):
        @pl.when(pl.program_id(0) == 0)
        def _():
            dg_ref[...] = jnp.zeros_like(dg_ref)

        dact = _dot_nt(dxb_ref[...], wd_ref[...].reshape(FF, D))
        gate = gate_ref[...].astype(F32)
        up = up_ref[...].astype(F32)
        sg = _sigmoid(gate)
        sil = gate * sg
        dgate = ((dact * up) * (sg * (1.0 + gate * (1.0 - sg)))).astype(CDT)
        dup = (dact * sil).astype(CDT)
        dgate_ref[...] = dgate
        dup_ref[...] = dup
        dh2 = _dot(dgate, wg_ref[...].reshape(FF, D)) + _dot(dup, wu_ref[...].reshape(FF, D))
        xm = xm_ref[...]
        r = lax.rsqrt(jnp.mean(xm * xm, axis=-1, keepdims=True) + EPS)
        dxn, dgx = _rms_bwd(xm, r, g_ref[...], dh2)
        dg_ref[...] += jnp.sum(dgx, axis=0, keepdims=True)
        dxm = dx_ref[...] + dxn
        dxm_ref[...] = dxm
        dxmb = dxm.astype(CDT)
        dxmb_ref[...] = dxmb
        dmix_ref[...] = _dot_nt(dxmb, wo_ref[...].reshape(N_DEV * n_out, D))

    row = lambda n: pl.BlockSpec((tm, n), lambda i: (i, 0))
    return _launch(
        "b_mlp" if comm is None else "b_mlp_rs", body, (S // tm,),
        [row(D), row(D), row(D), row(FF), row(FF), _resident((1, D)),
         _wres(n_ff, D, p_wd), _wres(n_ff, D, p_wg), _wres(n_ff, D, p_wu), _wres(n_out, D, p_wo)],
        [row(FF), row(FF), row(D), row(D), row(D), pl.BlockSpec((1, D), lambda i: (0, 0))],
        [jax.ShapeDtypeStruct((S, FF), CDT), jax.ShapeDtypeStruct((S, FF), CDT),
         jax.ShapeDtypeStruct((S, D), F32), jax.ShapeDtypeStruct((S, D), CDT),
         jax.ShapeDtypeStruct((S, D), F32), jax.ShapeDtypeStruct((1, D), F32)],
        [], (dxb, dx, xm, gate, up, g2, G[1], G[0], G[0], G[0]), ("arbitrary",), comm)


def _b_conv(dmix, c, proj, cw, lg, lb, tm, comm=None):
    S = proj.shape[0]
    CW = cw.shape[1]
    RB = 32
    hb = tm // HALO
    nt = S // tm
    last_h = S // HALO - 1

    def body(du_ref, duh_ref, c_ref, ch_ref, ab_ref, abh_ref, cw_ref, lg_ref, lb_ref,
             dab_ref, dcw_ref, dcb_ref, dlg_ref, dlb_ref, ext_u, ext_dc, ph_u, ph_dc, wacc):
        i = pl.program_id(0)

        @pl.when(i == 0)
        def _():
            wacc[...] = jnp.zeros_like(wacc)
            dcb_ref[...] = jnp.zeros_like(dcb_ref)
            dlg_ref[...] = jnp.zeros_like(dlg_ref)
            dlb_ref[...] = jnp.zeros_like(dlb_ref)

        def ln_bwd(cv, du):
            mu = jnp.mean(cv, axis=-1, keepdims=True)
            var = jnp.mean(jnp.square(cv - mu), axis=-1, keepdims=True)
            rstd = lax.rsqrt(var + EPS)
            n = (cv - mu) * rstd
            z = n * lg_ref[...] + lb_ref[...]
            sz = _sigmoid(z)
            dz = du * (sz * (1.0 + z * (1.0 - sz)))
            dn = dz * lg_ref[...]
            dc = rstd * (dn - jnp.mean(dn, axis=-1, keepdims=True)
                         - n * jnp.mean(dn * n, axis=-1, keepdims=True))
            return dc, dz, n

        hv = abh_ref[...]
        ext_u[0:HALO, :] = jnp.where(i > 0, hv[:, :CW] * _sigmoid(hv[:, CW:]), 0.0)
        av = ab_ref[...]
        sb = _sigmoid(av[:, CW:])
        ext_u[HALO:HALO + tm, :] = av[:, :CW] * sb
        ext_u[HALO + tm:, :] = jnp.zeros((8, CW), F32)
        s_lg = s_lb = s_cb = jnp.zeros((1, CW), F32)
        for b in range(tm // RB):
            bs = slice(b * RB, (b + 1) * RB)
            dc, dz, n = ln_bwd(c_ref[bs, :], du_ref[bs, :])
            ext_dc[bs, :] = dc
            s_lg = s_lg + jnp.sum(dz * n, axis=0, keepdims=True)
            s_lb = s_lb + jnp.sum(dz, axis=0, keepdims=True)
            s_cb = s_cb + jnp.sum(dc, axis=0, keepdims=True)
        dch, _, _ = ln_bwd(ch_ref[...], duh_ref[...])
        ext_dc[tm:tm + HALO, :] = jnp.where(i < nt - 1, dch, 0.0)
        ext_dc[tm + HALO:, :] = jnp.zeros((8, CW), F32)
        dlg_ref[...] += s_lg
        dlb_ref[...] += s_lb
        dcb_ref[...] += s_cb
        _phases(ext_u, ph_u, tm + HALO)
        _phases(ext_dc, ph_dc, tm + HALO)

        for rb in range(tm // RB):
            rs = slice(rb * RB, (rb + 1) * RB)
            dcb = ext_dc[rs, :]
            for p, a0, a1, taps in _taps_by_phase([rb * RB + HALO - (CONV_K - 1) + k for k in range(CONV_K)]):
                win = ph_u[p, a0:a1 + RB, :]
                for k, a in taps:
                    prod = dcb * win[a - a0:a - a0 + RB]
                    part = prod[0:8]
                    for j in range(1, RB // 8):
                        part = part + prod[8 * j:8 * j + 8]
                    wacc[k] += part
            acc = jnp.zeros((RB, CW), F32)
            for p, a0, a1, taps in _taps_by_phase([rb * RB + (CONV_K - 1) - k for k in range(CONV_K)]):
                win = ph_dc[p, a0:a1 + RB, :]
                for k, a in taps:
                    acc = acc + cw_ref[k:k + 1, :] * win[a - a0:a - a0 + RB]
            a_r = ab_ref[rs, 0:CW]
            s_r = _sigmoid(ab_ref[rs, CW:2 * CW])
            dab_ref[rs, 0:CW] = (acc * s_r).astype(CDT)
            dab_ref[rs, CW:2 * CW] = (acc * a_r * (s_r * (1.0 - s_r))).astype(CDT)

        @pl.when(i == nt - 1)
        def _():
            for k in range(CONV_K):
                dcw_ref[k:k + 1, :] = jnp.sum(wacc[k], axis=0, keepdims=True)
            dcw_ref[CONV_K:, :] = jnp.zeros((HALO - CONV_K, CW), F32)

    tile = lambda n, j: pl.BlockSpec((tm, n), lambda i: (i, j))
    nxt = lambda n, j: pl.BlockSpec((HALO, n), lambda i: (jnp.minimum((i + 1) * hb, last_h), j))
    return _launch(
        "b_conv" if comm is None else "b_conv_rs", body, (nt,),
        [tile(CW, 0), nxt(CW, 0), tile(CW, 0), nxt(CW, 0),
         tile(2 * CW, 0),
         pl.BlockSpec((HALO, 2 * CW), lambda i: (jnp.maximum(i * hb - 1, 0), 0)),
         _resident((HALO, CW)), _resident((1, CW)), _resident((1, CW))],
        [tile(2 * CW, 0),
         pl.BlockSpec((HALO, CW), lambda i: (0, 0)), pl.BlockSpec((1, CW), lambda i: (0, 0)),
         pl.BlockSpec((1, CW), lambda i: (0, 0)), pl.BlockSpec((1, CW), lambda i: (0, 0))],
        [jax.ShapeDtypeStruct((S, 2 * CW), CDT),
         jax.ShapeDtypeStruct((HALO, CW), F32), jax.ShapeDtypeStruct((1, CW), F32),
         jax.ShapeDtypeStruct((1, CW), F32), jax.ShapeDtypeStruct((1, CW), F32)],
        [pltpu.VMEM((tm + HALO + 8, CW), F32), pltpu.VMEM((tm + HALO + 8, CW), F32),
         pltpu.VMEM((8, tm + HALO, CW), F32), pltpu.VMEM((8, tm + HALO, CW), F32),
         pltpu.VMEM((HALO, 8, CW), F32)],
        (dmix, dmix, c, c, proj, proj, cw, lg, lb), ("arbitrary",), comm)


def _b_ret(dmix, dab, proj, rraw, states, cosT, sinT, Mt, qd, kd, gs, gn, comm=None):
    S = proj.shape[0]
    RW = HEADS * HD
    NB = S // SC
    scale = HD ** -0.5

    def body(dro_ref, dab_ref, q_ref, k_ref, v_ref, g_ref, rraw_ref, st_ref, c_ref, s_ref,
             m_ref, qd_ref, kd_ref, gs_ref, gn_ref, dp_ref, dgn_ref, G):
        @pl.when(pl.program_id(0) == 0)
        def _():
            G[...] = jnp.zeros_like(G)
            dgn_ref[...] = jnp.zeros_like(dgn_ref)

        dp_ref[:, 0:2 * RW] = dab_ref[...]
        cv = c_ref[...]
        sv = s_ref[...]
        for h in range(HEADS):
            cs = slice(h * HD, (h + 1) * HD)
            q = _rot(q_ref[:, cs], cv, sv)
            k = _rot(k_ref[:, cs], cv, sv) * scale
            qb = q.astype(CDT)
            kb = k.astype(CDT)
            vb = v_ref[:, cs].astype(CDT)
            spb = st_ref[0, h]
            r = rraw_ref[:, cs]
            mu = jnp.mean(r, axis=-1, keepdims=True)
            var = jnp.mean(jnp.square(r - mu), axis=-1, keepdims=True)
            rstd = lax.rsqrt(var + EPS)
            n = (r - mu) * rstd
            gv = g_ref[:, cs]
            sg = _sigmoid(gv)
            sil = gv * sg
            dro = dro_ref[:, cs]
            gnv = gn_ref[:, cs]
            dgn_ref[:, cs] += jnp.sum(dro * n * sil, axis=0, keepdims=True)
            dgate = dro * (n * gnv) * (sg * (1.0 + gv * (1.0 - sg)))
            dn = dro * gnv * sil
            dr = rstd * (dn - jnp.mean(dn, axis=-1, keepdims=True)
                         - n * jnp.mean(dn * n, axis=-1, keepdims=True))
            drb = dr.astype(CDT)
            mh = m_ref[h]
            ab = (_dot_nt(qb, kb) * mh).astype(CDT)
            dab_ = (_dot_nt(drb, vb) * mh).astype(CDT)
            qdb = (q * qd_ref[h]).astype(CDT)
            kdb = (k * kd_ref[h]).astype(CDT)
            gc = G[h]
            gb = gc.astype(CDT)
            dq = _dot(dab_, kb) + _dot_nt(drb, spb) * qd_ref[h]
            dk = _dot_tn(dab_, qb) + _dot_nt(vb, gb) * kd_ref[h]
            dv = _dot_tn(ab, drb) + _dot(kdb, gb)
            G[h] = gs_ref[h, 0:1, :] * gc + _dot_tn(qdb, drb)
            dk = dk * scale
            dqp = dq * cv + pltpu.roll(dq * sv, HD // 2, 1)
            dkp = dk * cv + pltpu.roll(dk * sv, HD // 2, 1)
            base = 2 * RW
            dp_ref[:, base + h * HD:base + (h + 1) * HD] = dqp.astype(CDT)
            dp_ref[:, base + RW + h * HD:base + RW + (h + 1) * HD] = dkp.astype(CDT)
            dp_ref[:, base + 2 * RW + h * HD:base + 2 * RW + (h + 1) * HD] = dv.astype(CDT)
            dp_ref[:, base + 3 * RW + h * HD:base + 3 * RW + (h + 1) * HD] = dgate.astype(CDT)

    rev = lambda n, j: pl.BlockSpec((SC, n), lambda i: (NB - 1 - i, j))
    return _launch(
        "b_ret" if comm is None else "b_ret_rs", body, (NB,),
        [rev(RW, 1), rev(2 * RW, 0), rev(RW, 2), rev(RW, 3), rev(RW, 4), rev(RW, 5), rev(RW, 0),
         pl.BlockSpec((1, HEADS, HD, HD), lambda i: (NB - 1 - i, 0, 0, 0)),
         rev(HD, 0), rev(HD, 0),
         _resident((HEADS, SC, SC)), _resident((HEADS, SC, HD)), _resident((HEADS, SC, HD)),
         _resident((HEADS, 8, HD)), _resident((1, RW))],
        [rev(6 * RW, 0), pl.BlockSpec((1, RW), lambda i: (0, 0))],
        [jax.ShapeDtypeStruct((S, 6 * RW), CDT), jax.ShapeDtypeStruct((1, RW), F32)],
        [pltpu.VMEM((HEADS, HD, HD), F32)],
        (dmix, dab, proj, proj, proj, proj, rraw, states, cosT, sinT, Mt, qd, kd, gs, gn), ("arbitrary",), comm)


def _b_in(dproj, G, n_in, x, g1, dxm, tm, comm=None):
    S, D = x.shape
    N = N_DEV * n_in

    def body(dp_ref, w_ref, x_ref, g_ref, dxm_ref, dx_ref, dxb_ref, dg_ref):
        @pl.when(pl.program_id(0) == 0)
        def _():
            dg_ref[...] = jnp.zeros_like(dg_ref)

        dh = _dot(dp_ref[...], w_ref[...].reshape(N, D))
        xv = x_ref[...]
        r = lax.rsqrt(jnp.mean(xv * xv, axis=-1, keepdims=True) + EPS)
        dxn, dgx = _rms_bwd(xv, r, g_ref[...], dh)
        dg_ref[...] += jnp.sum(dgx, axis=0, keepdims=True)
        dx = dxm_ref[...] + dxn
        dx_ref[...] = dx
        dxb_ref[...] = dx.astype(CDT)

    row = lambda n: pl.BlockSpec((tm, n), lambda i: (i, 0))
    return _launch(
        "b_in" if comm is None else "b_in_rs", body, (S // tm,),
        [row(N), _wres(n_in, D, 0), row(D), _resident((1, D)), row(D)],
        [row(D), row(D), pl.BlockSpec((1, D), lambda i: (0, 0))],
        [jax.ShapeDtypeStruct((S, D), F32), jax.ShapeDtypeStruct((S, D), CDT),
         jax.ShapeDtypeStruct((1, D), F32)],
        [], (dproj, G, x, g1, dxm), ("arbitrary",), comm)


def _dw_tn(a, b, tm, tk):
    S, M = a.shape
    N = b.shape[1]
    nk = S // tk

    def body(a_ref, b_ref, o_ref, ob_ref):
        k = pl.program_id(1)

        @pl.when(k == 0)
        def _():
            o_ref[...] = jnp.zeros_like(o_ref)

        o_ref[...] += _dot_tn(a_ref[...], b_ref[...])

        @pl.when(k == nk - 1)
        def _():
            ob_ref[...] = o_ref[...].astype(CDT)

    out = pl.BlockSpec((tm, N), lambda m, k: (m, 0))
    return _call(
        body, name="dw_tn", grid=(M // tm, nk),
        in_specs=[pl.BlockSpec((tk, tm), lambda m, k: (k, m)), pl.BlockSpec((tk, N), lambda m, k: (k, 0))],
        out_specs=[out, out],
        out_shape=[jax.ShapeDtypeStruct((M, N), F32), jax.ShapeDtypeStruct((M, N), CDT)],
        compiler_params=_cp(("parallel", "arbitrary")),
    )(a, b)


def _adamw(w, g, m, v, tr):
    R, C = w.shape
    c1 = 1.0 - ADAM_B1 ** ADAM_STEP
    c2 = 1.0 - ADAM_B2 ** ADAM_STEP

    def body(w_ref, g_ref, m_ref, v_ref, d_ref, mo_ref, vo_ref):
        gv = g_ref[...]
        mn = ADAM_B1 * m_ref[...] + (1.0 - ADAM_B1) * gv
        vn = ADAM_B2 * v_ref[...] + (1.0 - ADAM_B2) * jnp.square(gv)
        mo_ref[...] = mn
        vo_ref[...] = vn
        d_ref[...] = -ADAM_LR * ((mn / c1) / (jnp.sqrt(vn / c2) + ADAM_EPS) + ADAM_WD * w_ref[...])

    blk = pl.BlockSpec((tr, C), lambda i: (i, 0))
    sh = jax.ShapeDtypeStruct((R, C), F32)
    return _call(
        body, name="adamw", grid=(R // tr,),
        in_specs=[blk, blk, blk, blk], out_specs=[blk, blk, blk], out_shape=[sh, sh, sh],
        compiler_params=_cp(("parallel",)),
    )(w, g, m, v)


def _coords():
    return lax.axis_index("x"), lax.axis_index("y"), lax.axis_index("c")


def _peer(x, y, c, d):
    return (x ^ (d >> 2), y ^ ((d >> 1) & 1), c ^ (d & 1))


def _ag_comm(ps):
    K = len(ps)

    def plan(cins, couts, sems):
        send_sems, recv_sems, local_sems = sems
        x, y, c = _coords()
        me, sibling = (x, y, c), (x, y, 1 - c)
        chips = [(1 - x, y), (x, 1 - y), (1 - x, 1 - y)]
        mine, first, passed, got_ici, got_d2d = [], [], [], [], []
        for a in range(K):
            x_ref, out_ref = cins[a], couts[a]
            R = x_ref.shape[0]

            def rows(px, py, pc, out_ref=out_ref, R=R):
                return out_ref.at[pl.ds((4 * px + 2 * py + pc) * R, R), :]

            def copy(k, block, to, src=None, rows=rows, a=a):
                return pltpu.make_async_remote_copy(
                    src_ref=rows(*block) if src is None else src, dst_ref=rows(*block),
                    send_sem=send_sems.at[7 * a + k], recv_sem=recv_sems.at[7 * a + k],
                    device_id=to, device_id_type=MESH)

            mine.append(pltpu.make_async_copy(x_ref, rows(*me), local_sems.at[a]))
            first.append(copy(0, me, sibling, src=x_ref))
            first += [copy(1 + j, me, (*chip, c), src=x_ref) for j, chip in enumerate(chips)]
            passed += [copy(4 + j, (*chip, c), sibling) for j, chip in enumerate(chips)]
            got_ici += [copy(1 + j, (*chip, c), me) for j, chip in enumerate(chips)]
            got_d2d.append(copy(0, sibling, me))
            got_d2d += [copy(4 + j, (*chip, 1 - c), me) for j, chip in enumerate(chips)]
        return mine, first, passed, got_ici, got_d2d

    def start(*a):
        mine, first, _, _, _ = plan(*a)
        for cp in mine + first:
            cp.start()

    def mid(*a):
        _, _, passed, got_ici, _ = plan(*a)
        for got, fwd in zip(got_ici, passed):
            got.wait_recv()
            fwd.start()

    def finish(*a):
        mine, first, passed, _, got_d2d = plan(*a)
        for got in got_d2d:
            got.wait_recv()
        for cp in first + passed:
            cp.wait_send()
        for cp in mine:
            cp.wait()

    return _Comm(ps, [jax.ShapeDtypeStruct((N_DEV * p.shape[0], p.shape[1]), p.dtype) for p in ps],
                 [pltpu.SemaphoreType.DMA((7 * K,)), pltpu.SemaphoreType.DMA((7 * K,)),
                  pltpu.SemaphoreType.DMA((K,))], start, mid, finish)


def _rs_direct_comm(parts):
    K = len(parts)

    def plan(cins, couts, sems):
        send_sems, recv_sems = sems
        x, y, c = _coords()
        cps = []
        for d in range(1, N_DEV):
            px, py, pc = _peer(x, y, c, d)
            for k in range(K):
                s = (d - 1) * K + k
                cps.append(pltpu.make_async_remote_copy(
                    src_ref=cins[k].at[4 * px + 2 * py + pc], dst_ref=couts[k].at[d - 1],
                    send_sem=send_sems.at[s], recv_sem=recv_sems.at[s], device_id=(px, py, pc),
                    device_id_type=MESH))
        return cps

    def start(*a):
        for cp in plan(*a):
            cp.start()

    def finish(*a):
        cps = plan(*a)
        for cp in cps:
            cp.wait_recv()
        for cp in cps:
            cp.wait_send()

    n_sem = (N_DEV - 1) * K
    return _Comm(parts, [jax.ShapeDtypeStruct((N_DEV - 1,) + p.shape[1:], p.dtype) for p in parts],
                 [pltpu.SemaphoreType.DMA((n_sem,)), pltpu.SemaphoreType.DMA((n_sem,))], start, None, finish)


def _sum_all(parts, recv):
    K = len(parts)

    def body(*refs):
        ins, rcv, outs = refs[:K], refs[K:2 * K], refs[2 * K:3 * K]
        bufs, sem = refs[3 * K:4 * K], refs[4 * K]
        x, y, c = _coords()
        me = 4 * x + 2 * y + c
        cps = [pltpu.make_async_copy(ins[k].at[me], bufs[k], sem.at[k]) for k in range(K)]
        for cp in cps:
            cp.start()
        for k in range(K):
            cps[k].wait()
            acc = bufs[k][...]
            for d in range(N_DEV - 1):
                acc = acc + rcv[k][d].astype(F32)
            outs[k][...] = acc

    vm = pl.BlockSpec(memory_space=pltpu.VMEM)
    res = _call(
        body, name="sum_all", in_specs=[_ANY] * K + [vm] * K, out_specs=[vm] * K,
        out_shape=[jax.ShapeDtypeStruct(p.shape[1:], F32) for p in parts],
        scratch_shapes=[pltpu.VMEM(p.shape[1:], F32) for p in parts] + [pltpu.SemaphoreType.DMA((K,))],
        compiler_params=_cp(),
    )(*parts, *recv)
    return list(res)


def _gather_small(v, reduce):
    R, C = v.shape

    def exchange(v_ref, buf, send_sems, recv_sems):
        x, y, c = _coords()
        me = 4 * x + 2 * y + c
        buf[me] = v_ref[...]
        cps = []
        for d in range(1, N_DEV):
            cp = pltpu.make_async_remote_copy(
                src_ref=v_ref, dst_ref=buf.at[me], send_sem=send_sems.at[d - 1], recv_sem=recv_sems.at[d - 1],
                device_id=_peer(x, y, c, d), device_id_type=MESH)
            cp.start()
            cps.append(cp)
        for cp in cps:
            cp.wait_recv()
        for cp in cps:
            cp.wait_send()

    sems = [pltpu.SemaphoreType.DMA((7,)), pltpu.SemaphoreType.DMA((7,))]
    vm = pl.BlockSpec(memory_space=pltpu.VMEM)
    if reduce:
        def body(v_ref, o_ref, buf, send_sems, recv_sems):
            exchange(v_ref, buf, send_sems, recv_sems)
            acc = buf[0]
            for s in range(1, N_DEV):
                acc = acc + buf[s]
            o_ref[...] = acc

        return _call(body, name="allreduce_small", in_specs=[vm], out_specs=vm,
                     out_shape=jax.ShapeDtypeStruct((R, C), F32),
                     scratch_shapes=[pltpu.VMEM((N_DEV, R, C), F32)] + sems)(v)

    def body(v_ref, o_ref, send_sems, recv_sems):
        exchange(v_ref, o_ref, send_sems, recv_sems)

    return _call(body, name="allgather_small", in_specs=[vm], out_specs=vm,
                 out_shape=jax.ShapeDtypeStruct((N_DEV, R, C), F32), scratch_shapes=sems)(v)


def _tables(S):
    half = HD // 2
    pos = jnp.arange(S, dtype=F32)
    freqs = ROPE_BASE ** (-jnp.arange(half, dtype=F32) / half)
    ang = pos[:, None] * freqs[None, :]
    cos, sin = jnp.cos(ang), jnp.sin(ang)
    cosT = jnp.concatenate([cos, cos], axis=-1)
    sinT = jnp.concatenate([-sin, sin], axis=-1)
    log_g = jnp.log(1.0 - 2.0 ** (-5.0 - jnp.arange(HEADS, dtype=F32)))
    idx = jnp.arange(SC, dtype=F32)
    ci = jnp.arange(SC) // CHUNK
    diff = idx[:, None] - idx[None, :]
    same = ci[:, None] == ci[None, :]
    earlier = ci[None, :] < ci[:, None]
    expo = jnp.where(same, jnp.abs(diff), diff)
    Mt = jnp.where((same | earlier)[None], jnp.exp(log_g[:, None, None] * expo[None]), 0.0)
    ones = jnp.ones((1, 1, HD), F32)
    qd = jnp.exp(log_g[:, None] * (idx + 1.0)[None, :])[:, :, None] * ones
    kd = jnp.exp(log_g[:, None] * (SC - 1.0 - idx)[None, :])[:, :, None] * ones
    gs = jnp.exp(log_g * SC)[:, None, None] * jnp.ones((1, 8, HD), F32)
    return cosT, sinT, Mt, qd, kd, gs


def _pad_rows(a, rows):
    return jnp.pad(a, ((0, rows - a.shape[0]), (0, 0)))


def kernel(x, norm1_g, w_in, conv_w, conv_b, conv_ln_g, conv_ln_b, ret_gn_g, w_out, norm2_g, w_gate, w_up, w_down, final_g, loss_target, m_norm1_g, m_w_in, m_conv_w, m_conv_b, m_conv_ln_g, m_conv_ln_b, m_ret_gn_g, m_w_out, m_norm2_g, m_w_gate, m_w_up, m_w_down, m_final_g, v_norm1_g, v_w_in, v_conv_w, v_conv_b, v_conv_ln_g, v_conv_ln_b, v_ret_gn_g, v_w_out, v_norm2_g, v_w_gate, v_w_up, v_w_down, v_final_g):
    L, D, n_in = w_in.shape
    n_out = w_out.shape[1]
    n_ff = w_gate.shape[2]
    S = x.shape[1]
    CW = conv_b.shape[1]
    IN, FF = N_DEV * n_in, N_DEV * n_ff
    ncw = conv_w.shape[2]
    x0 = x.reshape(S, D)
    tgt = loss_target.reshape(S, D)
    TM = min(512, S)
    TKW = min(2048, S)
    TMI = min(512, S)
    TMM = min(256, S)

    assert n_out <= n_ff
    wparts = (0, 1, 2, 0)
    pack_a = jnp.swapaxes(w_in, 1, 2).astype(CDT)
    pack_b = jnp.concatenate([w_out, jnp.zeros((L, n_ff - n_out, D), F32), jnp.swapaxes(w_gate, 1, 2),
                              jnp.swapaxes(w_up, 1, 2)], axis=1).astype(CDT)
    pack_c = w_down.astype(CDT)
    per_dev = lambda g: g.reshape(N_DEV, -1, D)
    Ga = per_dev(_comm_only(_ag_comm([pack_a[0]]), "ag_first")[0])
    Gb = Gc = None

    cwp = conv_w.reshape(L * CONV_K * ncw // 128, 128)
    cw_rows = -(-cwp.shape[0] // 8) * 8
    cwg = _gather_small(_pad_rows(cwp, cw_rows), reduce=False)[:, :cwp.shape[0], :]
    conv_w_full = jnp.moveaxis(cwg.reshape(N_DEV, L, CONV_K, ncw), 0, 2).reshape(L, CONV_K, CW)

    cosT, sinT, Mt, qd, kd, gs = _tables(S)

    saved = []
    xl = x0
    for l in range(L):
        cw = _pad_rows(conv_w_full[l], HALO)
        (h, proj, c, u), got = _f_in_conv(xl, norm1_g[l][None], Ga, n_in, cw, conv_b[l][None], conv_ln_g[l][None],
                                          conv_ln_b[l][None], TMI, _ag_comm([pack_b[0]]) if l == 0 else None)
        if got:
            Gb = per_dev(got[0])
        more = l + 1 < L
        ret_gather = ([pack_c[0]] if l == 0 else []) + ([pack_a[l + 1]] if more else [])
        (rraw, states, mixed), got = _f_ret(proj, u, cosT, sinT, Mt, qd, kd, gs, ret_gn_g[l][None],
                                            _ag_comm(ret_gather) if ret_gather else None)
        if l == 0:
            Gc = per_dev(got[0])
        (xm, h2, gate, up, act, xo), nxt = _f_mlp(
            xl, mixed, norm2_g[l][None], (Gb, Gc), n_out, n_ff, wparts, 2 * TMM,
            _ag_comm([pack_b[l + 1], pack_c[l + 1]]) if more else None)
        saved.append(dict(x=xl, h=h, proj=proj, c=c, rraw=rraw, states=states, mixed=mixed, xm=xm, h2=h2,
                          gate=gate, up=up, act=act, cw=cw, Ga=Ga, Gbc=(Gb, Gc)))
        if more:
            Ga, Gb, Gc = per_dev(got[-1]), per_dev(nxt[0]), per_dev(nxt[1])
        xl = xo

    dx, dxb, loss_p, dfg = _f_loss(xl, final_g[None], tgt, TM)

    small = []
    own = [None] * L
    recv = [None] * L
    blocks = lambda d: d.reshape(N_DEV, -1, D)
    for l in reversed(range(L)):
        sv = saved[l]
        (dgate, dup, dxm, dxmb, dmix, dg2), _ = _b_mlp(
            dxb, dx, sv["xm"], sv["gate"], sv["up"], norm2_g[l][None], sv["Gbc"], n_out, n_ff, wparts, TMM)
        d_wd, d_wd_b = _dw_tn(sv["act"], dxb, FF // 2, TKW)
        d_wgT, d_wgT_b = _dw_tn(dgate, sv["h2"], FF // 2, TKW)
        d_wuT, d_wuT_b = _dw_tn(dup, sv["h2"], FF // 2, TKW)
        d_wo, d_wo_b = _dw_tn(sv["mixed"], dxmb, D, TKW)
        (dab, dcw, dcb, dlg, dlb), r_go = _b_conv(dmix, sv["c"], sv["proj"], sv["cw"], conv_ln_g[l][None],
                                                  conv_ln_b[l][None], TM,
                                                  _rs_direct_comm([blocks(d_wgT_b), blocks(d_wo_b)]))
        (dproj, dgn), r_ud = _b_ret(dmix, dab, sv["proj"], sv["rraw"], sv["states"], cosT, sinT, Mt, qd, kd, gs,
                                    ret_gn_g[l][None], _rs_direct_comm([blocks(d_wuT_b), blocks(d_wd_b)]))
        d_winT, d_winT_b = _dw_tn(dproj, sv["h"], IN // 2, TKW)
        (dx, dxb, dg1), r_i = _b_in(dproj, sv["Ga"], n_in, sv["x"], norm1_g[l][None], dxm, TMI,
                                    _rs_direct_comm([blocks(d_winT_b)]))
        own[l] = [blocks(d) for d in (d_winT, d_wo, d_wgT, d_wuT, d_wd)]
        recv[l] = [r_i[0], r_go[1], r_go[0], r_ud[0], r_ud[1]]
        small.append(jnp.concatenate([dcw, dcb, dlg, dlb, dgn, dg1.reshape(2, CW), dg2.reshape(2, CW)], axis=0))
    small = small[::-1]
    grad_x = dx.reshape(1, S, D)

    rows_l = HALO + 8
    loss_row = jnp.zeros((1, CW), F32).at[0, 0].set(loss_p[0, 0])
    sm = jnp.concatenate(small + [dfg.reshape(2, CW), loss_row], axis=0)
    sm_rows = -(-sm.shape[0] // 8) * 8
    sm = _gather_small(_pad_rows(sm, sm_rows), reduce=True)
    loss = sm[L * rows_l + 2, 0]
    g_final = sm[L * rows_l:L * rows_l + 2].reshape(D)
    per = sm[:L * rows_l].reshape(L, rows_l, CW)
    me = 4 * lax.axis_index("x") + 2 * lax.axis_index("y") + lax.axis_index("c")
    g_conv_w = lax.dynamic_slice_in_dim(per[:, :CONV_K, :], me * ncw, ncw, axis=2)
    g_conv_b, g_ln_g, g_ln_b, g_gn = per[:, HALO], per[:, HALO + 1], per[:, HALO + 2], per[:, HALO + 3]
    g_n1 = per[:, HALO + 4:HALO + 6].reshape(L, D)
    g_n2 = per[:, HALO + 6:HALO + 8].reshape(L, D)

    gl = [_sum_all(own[l], recv[l]) for l in range(L)]
    g_w_in = jnp.stack([gl[l][0].T for l in range(L)])
    g_w_out = jnp.stack([gl[l][1] for l in range(L)])
    g_w_gate = jnp.stack([gl[l][2].T for l in range(L)])
    g_w_up = jnp.stack([gl[l][3].T for l in range(L)])
    g_w_down = jnp.stack([gl[l][4] for l in range(L)])

    def big(w, g, m, v):
        sh = w.shape
        two = lambda a: a.reshape(-1, sh[-1])
        rows = two(w).shape[0]
        d, mn, vn = _adamw(two(w), two(g), two(m), two(v), rows // 8)
        return d.reshape(sh), mn.reshape(sh), vn.reshape(sh)

    names = ["norm1_g", "conv_w", "conv_b", "conv_ln_g", "conv_ln_b", "ret_gn_g", "norm2_g", "final_g"]
    sw = dict(norm1_g=(norm1_g, g_n1, m_norm1_g, v_norm1_g), conv_w=(conv_w, g_conv_w, m_conv_w, v_conv_w),
              conv_b=(conv_b, g_conv_b, m_conv_b, v_conv_b), conv_ln_g=(conv_ln_g, g_ln_g, m_conv_ln_g, v_conv_ln_g),
              conv_ln_b=(conv_ln_b, g_ln_b, m_conv_ln_b, v_conv_ln_b), ret_gn_g=(ret_gn_g, g_gn, m_ret_gn_g, v_ret_gn_g),
              norm2_g=(norm2_g, g_n2, m_norm2_g, v_norm2_g), final_g=(final_g, g_final, m_final_g, v_final_g))
    lens = [int(math.prod(sw[n][0].shape)) for n in names]
    tot = sum(lens)
    prow = -(-tot // (8 * CW)) * 8

    def packs(j):
        flat = jnp.concatenate([sw[n][j].reshape(-1) for n in names])
        return jnp.pad(flat, (0, prow * CW - tot)).reshape(prow, CW)

    sd, smn, svn = _adamw(packs(0), packs(1), packs(2), packs(3), prow)

    def unpack(a):
        flat = a.reshape(-1)
        out, o = {}, 0
        for n, ln in zip(names, lens):
            out[n] = flat[o:o + ln].reshape(sw[n][0].shape)
            o += ln
        return out

    sd, smn, svn = unpack(sd), unpack(smn), unpack(svn)
    res = {n: (sw[n][1], sd[n], smn[n], svn[n]) for n in names}
    res["w_in"] = (g_w_in,) + big(w_in, g_w_in, m_w_in, v_w_in)
    res["w_out"] = (g_w_out,) + big(w_out, g_w_out, m_w_out, v_w_out)
    res["w_gate"] = (g_w_gate,) + big(w_gate, g_w_gate, m_w_gate, v_w_gate)
    res["w_up"] = (g_w_up,) + big(w_up, g_w_up, m_w_up, v_w_up)
    res["w_down"] = (g_w_down,) + big(w_down, g_w_down, m_w_down, v_w_down)

    order = ["norm1_g", "w_in", "conv_w", "conv_b", "conv_ln_g", "conv_ln_b", "ret_gn_g", "w_out", "norm2_g",
             "w_gate", "w_up", "w_down", "final_g"]
    return (loss, grad_x, *[res[n][0] for n in order], *[res[n][1] for n in order],
            *[res[n][2] for n in order], *[res[n][3] for n in order])
```

```python
import math

import jax
import jax.numpy as jnp
from jax import lax
from jax.experimental import pallas as pl
from jax.experimental.pallas import tpu as pltpu

F32 = jnp.float32
CDT = jnp.bfloat16
EPS = 1e-6
CHUNK = 64
SC = 256
HEADS = 4
HD = 128
CONV_K = 31
HALO = 32
ROPE_BASE = 10000.0
ADAM_LR = 0.001
ADAM_B1 = 0.9
ADAM_B2 = 0.999
ADAM_EPS = 1e-08
ADAM_WD = 0.01
ADAM_STEP = 10
N_DEV = 8
MESH = pl.DeviceIdType.MESH
VMEM_LIMIT = 60 * 1024 * 1024


def _call(body, **kw):
    return pl.pallas_call(body, **kw)


def _cp(sem=None, vmem=VMEM_LIMIT):
    return pltpu.CompilerParams(dimension_semantics=sem, vmem_limit_bytes=vmem)


def _resident(shape):
    nd = len(shape)
    return pl.BlockSpec(shape, lambda *_: (0,) * nd, pipeline_mode=pl.Buffered(1))


def _dot(a, b):
    return jnp.dot(a, b, preferred_element_type=F32)


def _dot_nt(a, b):
    return lax.dot_general(a, b, (((1,), (1,)), ((), ())), preferred_element_type=F32)


def _dot_tn(a, b):
    return lax.dot_general(a, b, (((0,), (0,)), ((), ())), preferred_element_type=F32)


def _sigmoid(x):
    return 1.0 / (1.0 + jnp.exp(-x))


def _rms_bwd(x, r, g, dy):
    xh = x * r
    dyg = dy * g
    dx = r * (dyg - xh * jnp.mean(dyg * xh, axis=-1, keepdims=True))
    return dx, dy * xh


class _Comm:
    def __init__(self, ins, out_shapes, sems, start, mid, finish):
        self.ins, self.out_shapes, self.sems = list(ins), list(out_shapes), list(sems)
        self.start, self.mid, self.finish = start, mid, finish


_ANY = pl.BlockSpec(memory_space=pl.ANY)


def _launch(name, compute, grid, in_specs, out_specs, out_shape, scratch, operands, sem, comm=None):
    n_in, n_out, n_sc = len(in_specs), len(out_specs), len(scratch)
    if comm is None:
        res = _call(compute, name=name, grid=grid, in_specs=in_specs, out_specs=out_specs, out_shape=out_shape,
                    scratch_shapes=scratch, compiler_params=_cp(sem))(*operands)
        return list(res), []
    c_in, c_out = len(comm.ins), len(comm.out_shapes)
    inner = grid[1] if len(grid) > 1 else 1
    steps = grid[0] * inner
    mid_step = (3 * steps) // 4

    def body(*refs):
        ins = refs[:n_in]
        cins = refs[n_in:n_in + c_in]
        o = n_in + c_in
        outs = refs[o:o + n_out]
        couts = refs[o + n_out:o + n_out + c_out]
        o += n_out + c_out
        sc = refs[o:o + n_sc]
        csem = refs[o + n_sc:]
        i = pl.program_id(0)
        if len(grid) > 1:
            i = i * inner + pl.program_id(1)

        @pl.when(i == 0)
        def _():
            comm.start(cins, couts, csem)

        if comm.mid is not None:
            @pl.when(i == mid_step)
            def _():
                comm.mid(cins, couts, csem)

        compute(*ins, *outs, *sc)

        @pl.when(i == steps - 1)
        def _():
            comm.finish(cins, couts, csem)

    res = _call(body, name=name, grid=grid, in_specs=list(in_specs) + [_ANY] * c_in,
                out_specs=list(out_specs) + [_ANY] * c_out, out_shape=list(out_shape) + comm.out_shapes,
                scratch_shapes=list(scratch) + comm.sems,
                compiler_params=_cp(("arbitrary",) * len(grid)))(*operands, *comm.ins)
    return list(res[:n_out]), list(res[n_out:])


def _comm_only(comm, name):
    c_in, c_out = len(comm.ins), len(comm.out_shapes)

    def body(*refs):
        cins, couts, csem = refs[:c_in], refs[c_in:c_in + c_out], refs[c_in + c_out:]
        comm.start(cins, couts, csem)
        if comm.mid is not None:
            comm.mid(cins, couts, csem)
        comm.finish(cins, couts, csem)

    res = _call(body, name=name, in_specs=[_ANY] * c_in, out_specs=[_ANY] * c_out, out_shape=comm.out_shapes,
                scratch_shapes=comm.sems)(*comm.ins)
    return list(res)


def _wres(n, D, part):
    return pl.BlockSpec((N_DEV, n, D), lambda i: (0, part, 0), pipeline_mode=pl.Buffered(1))


def _taps_by_phase(offsets):
    groups = []
    for p in range(8):
        taps = [(k, o - p) for k, o in enumerate(offsets) if o % 8 == p]
        if taps:
            rows = [a for _, a in taps]
            groups.append((p, min(rows), max(rows), taps))
    return groups


def _phases(ext, ph, rows):
    for p in range(8):
        ph[p, :, :] = ext[p:p + rows, :]


def _f_in_conv(x, g, G, n_in, cw, cb, lg, lb, tm, comm=None):
    S, D = x.shape
    N = N_DEV * n_in
    CW = cw.shape[1]
    RB = 64

    def body(x_ref, g_ref, w_ref, cw_ref, cb_ref, lg_ref, lb_ref, h_ref, p_ref, c_ref, u_ref, ext, ph):
        i = pl.program_id(0)

        @pl.when(i == 0)
        def _():
            ext[0:HALO, :] = jnp.zeros((HALO, CW), F32)
            ext[HALO + tm:, :] = jnp.zeros((8, CW), F32)

        @pl.when(i > 0)
        def _():
            ext[0:HALO, :] = ext[tm:tm + HALO, :]

        xv = x_ref[...]
        r = lax.rsqrt(jnp.mean(xv * xv, axis=-1, keepdims=True) + EPS)
        h = ((xv * r) * g_ref[...]).astype(CDT)
        h_ref[...] = h
        w = w_ref[...].reshape(N, D)
        pab = _dot_nt(h, w[0:2 * CW])
        p_ref[:, 0:2 * CW] = pab
        p_ref[:, 2 * CW:] = _dot_nt(h, w[2 * CW:])
        ext[HALO:HALO + tm, :] = pab[:, :CW] * _sigmoid(pab[:, CW:])
        _phases(ext, ph, tm + HALO)
        for rb in range(tm // RB):
            acc = jnp.zeros((RB, CW), F32) + cb_ref[...]
            for k in range(CONV_K):
                o = rb * RB + HALO - (CONV_K - 1) + k
                acc = acc + cw_ref[k:k + 1, :] * ph[o % 8, o - o % 8:o - o % 8 + RB, :]
            c_ref[rb * RB:(rb + 1) * RB, :] = acc
            mu = jnp.mean(acc, axis=-1, keepdims=True)
            var = jnp.mean(jnp.square(acc - mu), axis=-1, keepdims=True)
            z = ((acc - mu) * lax.rsqrt(var + EPS)) * lg_ref[...] + lb_ref[...]
            u_ref[rb * RB:(rb + 1) * RB, :] = (z * _sigmoid(z)).astype(CDT)

    row = lambda n: pl.BlockSpec((tm, n), lambda i: (i, 0))
    return _launch(
        "f_in_conv" if comm is None else "f_in_conv_ag", body, (S // tm,),
        [row(D), _resident((1, D)), _wres(n_in, D, 0),
         _resident((HALO, CW)), _resident((1, CW)), _resident((1, CW)), _resident((1, CW))],
        [row(D), row(N), row(CW), row(CW)],
        [jax.ShapeDtypeStruct((S, D), CDT), jax.ShapeDtypeStruct((S, N), F32),
         jax.ShapeDtypeStruct((S, CW), F32), jax.ShapeDtypeStruct((S, CW), CDT)],
        [pltpu.VMEM((tm + HALO + 8, CW), F32), pltpu.VMEM((8, tm + HALO, CW), F32)],
        (x, g, G, cw, cb, lg, lb), ("arbitrary",), comm)


def _rot(t, c, s):
    return t * c + pltpu.roll(t, HD // 2, 1) * s


def _f_ret(proj, u, cosT, sinT, Mt, qd, kd, gs, gn, comm=None):
    S = proj.shape[0]
    RW = HEADS * HD
    NB = S // SC
    scale = HD ** -0.5

    def body(q_ref, k_ref, v_ref, g_ref, u_ref, c_ref, s_ref, m_ref, qd_ref, kd_ref, gs_ref, gn_ref,
             rraw_ref, st_ref, mix_ref, state):
        @pl.when(pl.program_id(0) == 0)
        def _():
            state[...] = jnp.zeros_like(state)

        mix_ref[:, 0:RW] = u_ref[...]
        cv = c_ref[...]
        sv = s_ref[...]
        for h in range(HEADS):
            cs = slice(h * HD, (h + 1) * HD)
            q = _rot(q_ref[:, cs], cv, sv)
            k = _rot(k_ref[:, cs], cv, sv) * scale
            vb = v_ref[:, cs].astype(CDT)
            qb = q.astype(CDT)
            kb = k.astype(CDT)
            a = _dot_nt(qb, kb) * m_ref[h]
            sp = state[h]
            spb = sp.astype(CDT)
            st_ref[0, h] = spb
            r = _dot(a.astype(CDT), vb) + _dot((q * qd_ref[h]).astype(CDT), spb)
            kv = _dot_tn((k * kd_ref[h]).astype(CDT), vb)
            state[h] = gs_ref[h, 0:1, :] * sp + kv
            rraw_ref[:, cs] = r
            mu = jnp.mean(r, axis=-1, keepdims=True)
            var = jnp.mean(jnp.square(r - mu), axis=-1, keepdims=True)
            n = (r - mu) * lax.rsqrt(var + EPS)
            gv = g_ref[:, cs]
            mix_ref[:, RW + h * HD:RW + (h + 1) * HD] = ((n * gn_ref[:, cs]) * (gv * _sigmoid(gv))).astype(CDT)

    col = lambda j: pl.BlockSpec((SC, RW), lambda i: (i, j))
    return _launch(
        "f_ret" if comm is None else "f_ret_ag", body, (NB,),
        [col(2), col(3), col(4), col(5),
         pl.BlockSpec((SC, RW), lambda i: (i, 0)),
         pl.BlockSpec((SC, HD), lambda i: (i, 0)), pl.BlockSpec((SC, HD), lambda i: (i, 0)),
         _resident((HEADS, SC, SC)), _resident((HEADS, SC, HD)), _resident((HEADS, SC, HD)),
         _resident((HEADS, 8, HD)), _resident((1, RW))],
        [pl.BlockSpec((SC, RW), lambda i: (i, 0)),
         pl.BlockSpec((1, HEADS, HD, HD), lambda i: (i, 0, 0, 0)),
         pl.BlockSpec((SC, 2 * RW), lambda i: (i, 0))],
        [jax.ShapeDtypeStruct((S, RW), F32),
         jax.ShapeDtypeStruct((NB, HEADS, HD, HD), CDT),
         jax.ShapeDtypeStruct((S, 2 * RW), CDT)],
        [pltpu.VMEM((HEADS, HD, HD), F32)],
        (proj, proj, proj, proj, u, cosT, sinT, Mt, qd, kd, gs, gn), ("arbitrary",), comm)


def _f_mlp(x, mixed, g2, G, n_out, n_ff, parts, tm, comm=None):
    S, D = x.shape
    FF = N_DEV * n_ff
    p_wo, p_wg, p_wu, p_wd = parts

    def body(x_ref, m_ref, wo_ref, g_ref, wg_ref, wu_ref, wd_ref,
             xm_ref, h2_ref, gate_ref, up_ref, act_ref, xo_ref):
        xm = x_ref[...] + _dot(m_ref[...], wo_ref[...].reshape(N_DEV * n_out, D))
        xm_ref[...] = xm
        r = lax.rsqrt(jnp.mean(xm * xm, axis=-1, keepdims=True) + EPS)
        h2 = ((xm * r) * g_ref[...]).astype(CDT)
        h2_ref[...] = h2
        gate = _dot_nt(h2, wg_ref[...].reshape(FF, D))
        up = _dot_nt(h2, wu_ref[...].reshape(FF, D))
        gate_ref[...] = gate.astype(CDT)
        up_ref[...] = up.astype(CDT)
        act = ((gate * _sigmoid(gate)) * up).astype(CDT)
        act_ref[...] = act
        xo_ref[...] = xm + _dot(act, wd_ref[...].reshape(FF, D))

    row = lambda n: pl.BlockSpec((tm, n), lambda i: (i, 0))
    return _launch(
        "f_mlp" if comm is None else "f_mlp_ag", body, (S // tm,),
        [row(D), row(D), _wres(n_out, D, p_wo), _resident((1, D)),
         _wres(n_ff, D, p_wg), _wres(n_ff, D, p_wu), _wres(n_ff, D, p_wd)],
        [row(D), row(D), row(FF), row(FF), row(FF), row(D)],
        [jax.ShapeDtypeStruct((S, D), F32), jax.ShapeDtypeStruct((S, D), CDT),
         jax.ShapeDtypeStruct((S, FF), CDT), jax.ShapeDtypeStruct((S, FF), CDT),
         jax.ShapeDtypeStruct((S, FF), CDT), jax.ShapeDtypeStruct((S, D), F32)],
        [], (x, mixed, G[0], g2, G[0], G[0], G[1]), ("parallel",), comm)


def _f_loss(x, fg, tgt, tm):
    S, D = x.shape

    def body(x_ref, g_ref, t_ref, dx_ref, dxb_ref, loss_ref, dg_ref):
        @pl.when(pl.program_id(0) == 0)
        def _():
            loss_ref[...] = jnp.zeros_like(loss_ref)
            dg_ref[...] = jnp.zeros_like(dg_ref)

        xv = x_ref[...]
        r = lax.rsqrt(jnp.mean(xv * xv, axis=-1, keepdims=True) + EPS)
        y = (xv * r) * g_ref[...]
        e = y - t_ref[...]
        loss_ref[...] += 0.5 * jnp.sum(jnp.mean(e * e, axis=-1, keepdims=True))
        dy = e * (1.0 / D)
        dx, dgx = _rms_bwd(xv, r, g_ref[...], dy)
        dg_ref[...] += jnp.sum(dgx, axis=0, keepdims=True)
        dx_ref[...] = dx
        dxb_ref[...] = dx.astype(CDT)

    row = pl.BlockSpec((tm, D), lambda i: (i, 0))
    return _call(
        body, name="f_loss", grid=(S // tm,),
        in_specs=[row, _resident((1, D)), row],
        out_specs=[row, row, pl.BlockSpec((1, 128), lambda i: (0, 0)), pl.BlockSpec((1, D), lambda i: (0, 0))],
        out_shape=[jax.ShapeDtypeStruct((S, D), F32), jax.ShapeDtypeStruct((S, D), CDT),
                   jax.ShapeDtypeStruct((1, 128), F32), jax.ShapeDtypeStruct((1, D), F32)],
        compiler_params=_cp(("arbitrary",)),
    )(x, fg, tgt)


def _b_mlp(dxb, dx, xm, gate, up, g2, G, n_out, n_ff, parts, tm, comm=None):
    S, D = dx.shape
    FF = N_DEV * n_ff
    p_wo, p_wg, p_wu, p_wd = parts

    def body(dxb_ref, dx_ref, xm_ref, gate_ref, up_ref, g_ref, wd_ref, wg_ref, wu_ref, wo_ref,
             dgate_ref, dup_ref, dxm_ref, dxmb_ref, dmix_ref, dg_ref):
        @pl.when(pl.program_id(0) == 0)
        def _():
            dg_ref[...] = jnp.zeros_like(dg_ref)

        dact = _dot_nt(dxb_ref[...], wd_ref[...].reshape(FF, D))
        gate = gate_ref[...].astype(F32)
        up = up_ref[...].astype(F32)
        sg = _sigmoid(gate)
        sil = gate * sg
        dgate = ((dact * up) * (sg * (1.0 + gate * (1.0 - sg)))).astype(CDT)
        dup = (dact * sil).astype(CDT)
        dgate_ref[...] = dgate
        dup_ref[...] = dup
        dh2 = _dot(dgate, wg_ref[...].reshape(FF, D)) + _dot(dup, wu_ref[...].reshape(FF, D))
        xm = xm_ref[...]
        r = lax.rsqrt(jnp.mean(xm * xm, axis=-1, keepdims=True) + EPS)
        dxn, dgx = _rms_bwd(xm, r, g_ref[...], dh2)
        dg_ref[...] += jnp.sum(dgx, axis=0, keepdims=True)
        dxm = dx_ref[...] + dxn
        dxm_ref[...] = dxm
        dxmb = dxm.astype(CDT)
        dxmb_ref[...] = dxmb
        dmix_ref[...] = _dot_nt(dxmb, wo_ref[...].reshape(N_DEV * n_out, D))

    row = lambda n: pl.BlockSpec((tm, n), lambda i: (i, 0))
    return _launch(
        "b_mlp" if comm is None else "b_mlp_rs", body, (S // tm,),
        [row(D), row(D), row(D), row(FF), row(FF), _resident((1, D)),
         _wres(n_ff, D, p_wd), _wres(n_ff, D, p_wg), _wres(n_ff, D, p_wu), _wres(n_out, D, p_wo)],
        [row(FF), row(FF), row(D), row(D), row(D), pl.BlockSpec((1, D), lambda i: (0, 0))],
        [jax.ShapeDtypeStruct((S, FF), CDT), jax.ShapeDtypeStruct((S, FF), CDT),
         jax.ShapeDtypeStruct((S, D), F32), jax.ShapeDtypeStruct((S, D), CDT),
         jax.ShapeDtypeStruct((S, D), F32), jax.ShapeDtypeStruct((1, D), F32)],
        [], (dxb, dx, xm, gate, up, g2, G[1], G[0], G[0], G[0]), ("arbitrary",), comm)


def _b_conv(dmix, c, proj, cw, lg, lb, tm, comm=None):
    S = proj.shape[0]
    CW = cw.shape[1]
    RB = 32
    hb = tm // HALO
    nt = S // tm
    last_h = S // HALO - 1

    def body(du_ref, duh_ref, c_ref, ch_ref, ab_ref, abh_ref, cw_ref, lg_ref, lb_ref,
             dab_ref, dcw_ref, dcb_ref, dlg_ref, dlb_ref, ext_u, ext_dc, ph_u, ph_dc, wacc):
        i = pl.program_id(0)

        @pl.when(i == 0)
        def _():
            wacc[...] = jnp.zeros_like(wacc)
            dcb_ref[...] = jnp.zeros_like(dcb_ref)
            dlg_ref[...] = jnp.zeros_like(dlg_ref)
            dlb_ref[...] = jnp.zeros_like(dlb_ref)

        def ln_bwd(cv, du):
            mu = jnp.mean(cv, axis=-1, keepdims=True)
            var = jnp.mean(jnp.square(cv - mu), axis=-1, keepdims=True)
            rstd = lax.rsqrt(var + EPS)
            n = (cv - mu) * rstd
            z = n * lg_ref[...] + lb_ref[...]
            sz = _sigmoid(z)
            dz = du * (sz * (1.0 + z * (1.0 - sz)))
            dn = dz * lg_ref[...]
            dc = rstd * (dn - jnp.mean(dn, axis=-1, keepdims=True)
                         - n * jnp.mean(dn * n, axis=-1, keepdims=True))
            return dc, dz, n

        hv = abh_ref[...]
        ext_u[0:HALO, :] = jnp.where(i > 0, hv[:, :CW] * _sigmoid(hv[:, CW:]), 0.0)
        av = ab_ref[...]
        sb = _sigmoid(av[:, CW:])
        ext_u[HALO:HALO + tm, :] = av[:, :CW] * sb
        ext_u[HALO + tm:, :] = jnp.zeros((8, CW), F32)
        s_lg = s_lb = s_cb = jnp.zeros((1, CW), F32)
        for b in range(tm // RB):
            bs = slice(b * RB, (b + 1) * RB)
            dc, dz, n = ln_bwd(c_ref[bs, :], du_ref[bs, :])
            ext_dc[bs, :] = dc
            s_lg = s_lg + jnp.sum(dz * n, axis=0, keepdims=True)
            s_lb = s_lb + jnp.sum(dz, axis=0, keepdims=True)
            s_cb = s_cb + jnp.sum(dc, axis=0, keepdims=True)
        dch, _, _ = ln_bwd(ch_ref[...], duh_ref[...])
        ext_dc[tm:tm + HALO, :] = jnp.where(i < nt - 1, dch, 0.0)
        ext_dc[tm + HALO:, :] = jnp.zeros((8, CW), F32)
        dlg_ref[...] += s_lg
        dlb_ref[...] += s_lb
        dcb_ref[...] += s_cb
        _phases(ext_u, ph_u, tm + HALO)
        _phases(ext_dc, ph_dc, tm + HALO)

        for rb in range(tm // RB):
            rs = slice(rb * RB, (rb + 1) * RB)
            dcb = ext_dc[rs, :]
            for p, a0, a1, taps in _taps_by_phase([rb * RB + HALO - (CONV_K - 1) + k for k in range(CONV_K)]):
                win = ph_u[p, a0:a1 + RB, :]
                for k, a in taps:
                    prod = dcb * win[a - a0:a - a0 + RB]
                    part = prod[0:8]
                    for j in range(1, RB // 8):
                        part = part + prod[8 * j:8 * j + 8]
                    wacc[k] += part
            acc = jnp.zeros((RB, CW), F32)
            for p, a0, a1, taps in _taps_by_phase([rb * RB + (CONV_K - 1) - k for k in range(CONV_K)]):
                win = ph_dc[p, a0:a1 + RB, :]
                for k, a in taps:
                    acc = acc + cw_ref[k:k + 1, :] * win[a - a0:a - a0 + RB]
            a_r = ab_ref[rs, 0:CW]
            s_r = _sigmoid(ab_ref[rs, CW:2 * CW])
            dab_ref[rs, 0:CW] = (acc * s_r).astype(CDT)
            dab_ref[rs, CW:2 * CW] = (acc * a_r * (s_r * (1.0 - s_r))).astype(CDT)

        @pl.when(i == nt - 1)
        def _():
            for k in range(CONV_K):
                dcw_ref[k:k + 1, :] = jnp.sum(wacc[k], axis=0, keepdims=True)
            dcw_ref[CONV_K:, :] = jnp.zeros((HALO - CONV_K, CW), F32)

    tile = lambda n, j: pl.BlockSpec((tm, n), lambda i: (i, j))
    nxt = lambda n, j: pl.BlockSpec((HALO, n), lambda i: (jnp.minimum((i + 1) * hb, last_h), j))
    return _launch(
        "b_conv" if comm is None else "b_conv_rs", body, (nt,),
        [tile(CW, 0), nxt(CW, 0), tile(CW, 0), nxt(CW, 0),
         tile(2 * CW, 0),
         pl.BlockSpec((HALO, 2 * CW), lambda i: (jnp.maximum(i * hb - 1, 0), 0)),
         _resident((HALO, CW)), _resident((1, CW)), _resident((1, CW))],
        [tile(2 * CW, 0),
         pl.BlockSpec((HALO, CW), lambda i: (0, 0)), pl.BlockSpec((1, CW), lambda i: (0, 0)),
         pl.BlockSpec((1, CW), lambda i: (0, 0)), pl.BlockSpec((1, CW), lambda i: (0, 0))],
        [jax.ShapeDtypeStruct((S, 2 * CW), CDT),
         jax.ShapeDtypeStruct((HALO, CW), F32), jax.ShapeDtypeStruct((1, CW), F32),
         jax.ShapeDtypeStruct((1, CW), F32), jax.ShapeDtypeStruct((1, CW), F32)],
        [pltpu.VMEM((tm + HALO + 8, CW), F32), pltpu.VMEM((tm + HALO + 8, CW), F32),
         pltpu.VMEM((8, tm + HALO, CW), F32), pltpu.VMEM((8, tm + HALO, CW), F32),
         pltpu.VMEM((HALO, 8, CW), F32)],
        (dmix, dmix, c, c, proj, proj, cw, lg, lb), ("arbitrary",), comm)


def _b_ret(dmix, dab, proj, rraw, states, cosT, sinT, Mt, qd, kd, gs, gn, comm=None):
    S = proj.shape[0]
    RW = HEADS * HD
    NB = S // SC
    scale = HD ** -0.5

    def body(dro_ref, dab_ref, q_ref, k_ref, v_ref, g_ref, rraw_ref, st_ref, c_ref, s_ref,
             m_ref, qd_ref, kd_ref, gs_ref, gn_ref, dp_ref, dgn_ref, G):
        @pl.when(pl.program_id(0) == 0)
        def _():
            G[...] = jnp.zeros_like(G)
            dgn_ref[...] = jnp.zeros_like(dgn_ref)

        dp_ref[:, 0:2 * RW] = dab_ref[...]
        cv = c_ref[...]
        sv = s_ref[...]
        for h in range(HEADS):
            cs = slice(h * HD, (h + 1) * HD)
            q = _rot(q_ref[:, cs], cv, sv)
            k = _rot(k_ref[:, cs], cv, sv) * scale
            qb = q.astype(CDT)
            kb = k.astype(CDT)
            vb = v_ref[:, cs].astype(CDT)
            spb = st_ref[0, h]
            r = rraw_ref[:, cs]
            mu = jnp.mean(r, axis=-1, keepdims=True)
            var = jnp.mean(jnp.square(r - mu), axis=-1, keepdims=True)
            rstd = lax.rsqrt(var + EPS)
            n = (r - mu) * rstd
            gv = g_ref[:, cs]
            sg = _sigmoid(gv)
            sil = gv * sg
            dro = dro_ref[:, cs]
            gnv = gn_ref[:, cs]
            dgn_ref[:, cs] += jnp.sum(dro * n * sil, axis=0, keepdims=True)
            dgate = dro * (n * gnv) * (sg * (1.0 + gv * (1.0 - sg)))
            dn = dro * gnv * sil
            dr = rstd * (dn - jnp.mean(dn, axis=-1, keepdims=True)
                         - n * jnp.mean(dn * n, axis=-1, keepdims=True))
            drb = dr.astype(CDT)
            mh = m_ref[h]
            ab = (_dot_nt(qb, kb) * mh).astype(CDT)
            dab_ = (_dot_nt(drb, vb) * mh).astype(CDT)
            qdb = (q * qd_ref[h]).astype(CDT)
            kdb = (k * kd_ref[h]).astype(CDT)
            gc = G[h]
            gb = gc.astype(CDT)
            dq = _dot(dab_, kb) + _dot_nt(drb, spb) * qd_ref[h]
            dk = _dot_tn(dab_, qb) + _dot_nt(vb, gb) * kd_ref[h]
            dv = _dot_tn(ab, drb) + _dot(kdb, gb)
            G[h] = gs_ref[h, 0:1, :] * gc + _dot_tn(qdb, drb)
            dk = dk * scale
            dqp = dq * cv + pltpu.roll(dq * sv, HD // 2, 1)
            dkp = dk * cv + pltpu.roll(dk * sv, HD // 2, 1)
            base = 2 * RW
            dp_ref[:, base + h * HD:base + (h + 1) * HD] = dqp.astype(CDT)
            dp_ref[:, base + RW + h * HD:base + RW + (h + 1) * HD] = dkp.astype(CDT)
            dp_ref[:, base + 2 * RW + h * HD:base + 2 * RW + (h + 1) * HD] = dv.astype(CDT)
            dp_ref[:, base + 3 * RW + h * HD:base + 3 * RW + (h + 1) * HD] = dgate.astype(CDT)

    rev = lambda n, j: pl.BlockSpec((SC, n), lambda i: (NB - 1 - i, j))
    return _launch(
        "b_ret" if comm is None else "b_ret_rs", body, (NB,),
        [rev(RW, 1), rev(2 * RW, 0), rev(RW, 2), rev(RW, 3), rev(RW, 4), rev(RW, 5), rev(RW, 0),
         pl.BlockSpec((1, HEADS, HD, HD), lambda i: (NB - 1 - i, 0, 0, 0)),
         rev(HD, 0), rev(HD, 0),
         _resident((HEADS, SC, SC)), _resident((HEADS, SC, HD)), _resident((HEADS, SC, HD)),
         _resident((HEADS, 8, HD)), _resident((1, RW))],
        [rev(6 * RW, 0), pl.BlockSpec((1, RW), lambda i: (0, 0))],
        [jax.ShapeDtypeStruct((S, 6 * RW), CDT), jax.ShapeDtypeStruct((1, RW), F32)],
        [pltpu.VMEM((HEADS, HD, HD), F32)],
        (dmix, dab, proj, proj, proj, proj, rraw, states, cosT, sinT, Mt, qd, kd, gs, gn), ("arbitrary",), comm)


def _b_in(dproj, G, n_in, x, g1, dxm, tm, comm=None):
    S, D = x.shape
    N = N_DEV * n_in

    def body(dp_ref, w_ref, x_ref, g_ref, dxm_ref, dx_ref, dxb_ref, dg_ref):
        @pl.when(pl.program_id(0) == 0)
        def _():
            dg_ref[...] = jnp.zeros_like(dg_ref)

        dh = _dot(dp_ref[...], w_ref[...].reshape(N, D))
        xv = x_ref[...]
        r = lax.rsqrt(jnp.mean(xv * xv, axis=-1, keepdims=True) + EPS)
        dxn, dgx = _rms_bwd(xv, r, g_ref[...], dh)
        dg_ref[...] += jnp.sum(dgx, axis=0, keepdims=True)
        dx = dxm_ref[...] + dxn
        dx_ref[...] = dx
        dxb_ref[...] = dx.astype(CDT)

    row = lambda n: pl.BlockSpec((tm, n), lambda i: (i, 0))
    return _launch(
        "b_in" if comm is None else "b_in_rs", body, (S // tm,),
        [row(N), _wres(n_in, D, 0), row(D), _resident((1, D)), row(D)],
        [row(D), row(D), pl.BlockSpec((1, D), lambda i: (0, 0))],
        [jax.ShapeDtypeStruct((S, D), F32), jax.ShapeDtypeStruct((S, D), CDT),
         jax.ShapeDtypeStruct((1, D), F32)],
        [], (dproj, G, x, g1, dxm), ("arbitrary",), comm)


def _dw_tn(a, b, tm, tk):
    S, M = a.shape
    N = b.shape[1]
    nk = S // tk

    def body(a_ref, b_ref, o_ref, ob_ref):
        k = pl.program_id(1)

        @pl.when(k == 0)
        def _():
            o_ref[...] = jnp.zeros_like(o_ref)

        o_ref[...] += _dot_tn(a_ref[...], b_ref[...])

        @pl.when(k == nk - 1)
        def _():
            ob_ref[...] = o_ref[...].astype(CDT)

    out = pl.BlockSpec((tm, N), lambda m, k: (m, 0))
    return _call(
        body, name="dw_tn", grid=(M // tm, nk),
        in_specs=[pl.BlockSpec((tk, tm), lambda m, k: (k, m)), pl.BlockSpec((tk, N), lambda m, k: (k, 0))],
        out_specs=[out, out],
        out_shape=[jax.ShapeDtypeStruct((M, N), F32), jax.ShapeDtypeStruct((M, N), CDT)],
        compiler_params=_cp(("parallel", "arbitrary")),
    )(a, b)


def _adamw(w, g, m, v, tr):
    R, C = w.shape
    c1 = 1.0 - ADAM_B1 ** ADAM_STEP
    c2 = 1.0 - ADAM_B2 ** ADAM_STEP

    def body(w_ref, g_ref, m_ref, v_ref, d_ref, mo_ref, vo_ref):
        gv = g_ref[...]
        mn = ADAM_B1 * m_ref[...] + (1.0 - ADAM_B1) * gv
        vn = ADAM_B2 * v_ref[...] + (1.0 - ADAM_B2) * jnp.square(gv)
        mo_ref[...] = mn
        vo_ref[...] = vn
        d_ref[...] = -ADAM_LR * ((mn / c1) / (jnp.sqrt(vn / c2) + ADAM_EPS) + ADAM_WD * w_ref[...])

    blk = pl.BlockSpec((tr, C), lambda i: (i, 0))
    sh = jax.ShapeDtypeStruct((R, C), F32)
    return _call(
        body, name="adamw", grid=(R // tr,),
        in_specs=[blk, blk, blk, blk], out_specs=[blk, blk, blk], out_shape=[sh, sh, sh],
        compiler_params=_cp(("parallel",)),
    )(w, g, m, v)


def _coords():
    return lax.axis_index("x"), lax.axis_index("y"), lax.axis_index("c")


def _peer(x, y, c, d):
    return (x ^ (d >> 2), y ^ ((d >> 1) & 1), c ^ (d & 1))


def _ag_comm(ps):
    K = len(ps)

    def plan(cins, couts, sems):
        send_sems, recv_sems, local_sems = sems
        x, y, c = _coords()
        me, sibling = (x, y, c), (x, y, 1 - c)
        chips = [(1 - x, y), (x, 1 - y), (1 - x, 1 - y)]
        mine, first, passed, got_ici, got_d2d = [], [], [], [], []
        for a in range(K):
            x_ref, out_ref = cins[a], couts[a]
            R = x_ref.shape[0]

            def rows(px, py, pc, out_ref=out_ref, R=R):
                return out_ref.at[pl.ds((4 * px + 2 * py + pc) * R, R), :]

            def copy(k, block, to, src=None, rows=rows, a=a):
                return pltpu.make_async_remote_copy(
                    src_ref=rows(*block) if src is None else src, dst_ref=rows(*block),
                    send_sem=send_sems.at[7 * a + k], recv_sem=recv_sems.at[7 * a + k],
                    device_id=to, device_id_type=MESH)

            mine.append(pltpu.make_async_copy(x_ref, rows(*me), local_sems.at[a]))
            first.append(copy(0, me, sibling, src=x_ref))
            first += [copy(1 + j, me, (*chip, c), src=x_ref) for j, chip in enumerate(chips)]
            passed += [copy(4 + j, (*chip, c), sibling) for j, chip in enumerate(chips)]
            got_ici += [copy(1 + j, (*chip, c), me) for j, chip in enumerate(chips)]
            got_d2d.append(copy(0, sibling, me))
            got_d2d += [copy(4 + j, (*chip, 1 - c), me) for j, chip in enumerate(chips)]
        return mine, first, passed, got_ici, got_d2d

    def start(*a):
        mine, first, _, _, _ = plan(*a)
        for cp in mine + first:
            cp.start()

    def mid(*a):
        _, _, passed, got_ici, _ = plan(*a)
        for got, fwd in zip(got_ici, passed):
            got.wait_recv()
            fwd.start()

    def finish(*a):
        mine, first, passed, _, got_d2d = plan(*a)
        for got in got_d2d:
            got.wait_recv()
        for cp in first + passed:
            cp.wait_send()
        for cp in mine:
            cp.wait()

    return _Comm(ps, [jax.ShapeDtypeStruct((N_DEV * p.shape[0], p.shape[1]), p.dtype) for p in ps],
                 [pltpu.SemaphoreType.DMA((7 * K,)), pltpu.SemaphoreType.DMA((7 * K,)),
                  pltpu.SemaphoreType.DMA((K,))], start, mid, finish)


def _rs_direct_comm(parts):
    K = len(parts)

    def plan(cins, couts, sems):
        send_sems, recv_sems = sems
        x, y, c = _coords()
        cps = []
        for d in range(1, N_DEV):
            px, py, pc = _peer(x, y, c, d)
            for k in range(K):
                s = (d - 1) * K + k
                cps.append(pltpu.make_async_remote_copy(
                    src_ref=cins[k].at[4 * px + 2 * py + pc], dst_ref=couts[k].at[d - 1],
                    send_sem=send_sems.at[s], recv_sem=recv_sems.at[s], device_id=(px, py, pc),
                    device_id_type=MESH))
        return cps

    def start(*a):
        for cp in plan(*a):
            cp.start()

    def finish(*a):
        cps = plan(*a)
        for cp in cps:
            cp.wait_recv()
        for cp in cps:
            cp.wait_send()

    n_sem = (N_DEV - 1) * K
    return _Comm(parts, [jax.ShapeDtypeStruct((N_DEV - 1,) + p.shape[1:], p.dtype) for p in parts],
                 [pltpu.SemaphoreType.DMA((n_sem,)), pltpu.SemaphoreType.DMA((n_sem,))], start, None, finish)


def _sum_all(parts, recv):
    K = len(parts)

    def body(*refs):
        ins, rcv, outs = refs[:K], refs[K:2 * K], refs[2 * K:3 * K]
        bufs, sem = refs[3 * K:4 * K], refs[4 * K]
        x, y, c = _coords()
        me = 4 * x + 2 * y + c
        cps = [pltpu.make_async_copy(ins[k].at[me], bufs[k], sem.at[k]) for k in range(K)]
        for cp in cps:
            cp.start()
        for k in range(K):
            cps[k].wait()
            acc = bufs[k][...]
            for d in range(N_DEV - 1):
                acc = acc + rcv[k][d].astype(F32)
            outs[k][...] = acc

    vm = pl.BlockSpec(memory_space=pltpu.VMEM)
    res = _call(
        body, name="sum_all", in_specs=[_ANY] * K + [vm] * K, out_specs=[vm] * K,
        out_shape=[jax.ShapeDtypeStruct(p.shape[1:], F32) for p in parts],
        scratch_shapes=[pltpu.VMEM(p.shape[1:], F32) for p in parts] + [pltpu.SemaphoreType.DMA((K,))],
        compiler_params=_cp(),
    )(*parts, *recv)
    return list(res)


def _gather_small(v, reduce):
    R, C = v.shape

    def exchange(v_ref, buf, send_sems, recv_sems):
        x, y, c = _coords()
        me = 4 * x + 2 * y + c
        buf[me] = v_ref[...]
        cps = []
        for d in range(1, N_DEV):
            cp = pltpu.make_async_remote_copy(
                src_ref=v_ref, dst_ref=buf.at[me], send_sem=send_sems.at[d - 1], recv_sem=recv_sems.at[d - 1],
                device_id=_peer(x, y, c, d), device_id_type=MESH)
            cp.start()
            cps.append(cp)
        for cp in cps:
            cp.wait_recv()
        for cp in cps:
            cp.wait_send()

    sems = [pltpu.SemaphoreType.DMA((7,)), pltpu.SemaphoreType.DMA((7,))]
    vm = pl.BlockSpec(memory_space=pltpu.VMEM)
    if reduce:
        def body(v_ref, o_ref, buf, send_sems, recv_sems):
            exchange(v_ref, buf, send_sems, recv_sems)
            acc = buf[0]
            for s in range(1, N_DEV):
                acc = acc + buf[s]
            o_ref[...] = acc

        return _call(body, name="allreduce_small", in_specs=[vm], out_specs=vm,
                     out_shape=jax.ShapeDtypeStruct((R, C), F32),
                     scratch_shapes=[pltpu.VMEM((N_DEV, R, C), F32)] + sems)(v)

    def body(v_ref, o_ref, send_sems, recv_sems):
        exchange(v_ref, o_ref, send_sems, recv_sems)

    return _call(body, name="allgather_small", in_specs=[vm], out_specs=vm,
                 out_shape=jax.ShapeDtypeStruct((N_DEV, R, C), F32), scratch_shapes=sems)(v)


def _tables(S):
    half = HD // 2
    pos = jnp.arange(S, dtype=F32)
    freqs = ROPE_BASE ** (-jnp.arange(half, dtype=F32) / half)
    ang = pos[:, None] * freqs[None, :]
    cos, sin = jnp.cos(ang), jnp.sin(ang)
    cosT = jnp.concatenate([cos, cos], axis=-1)
    sinT = jnp.concatenate([-sin, sin], axis=-1)
    log_g = jnp.log(1.0 - 2.0 ** (-5.0 - jnp.arange(HEADS, dtype=F32)))
    idx = jnp.arange(SC, dtype=F32)
    ci = jnp.arange(SC) // CHUNK
    diff = idx[:, None] - idx[None, :]
    same = ci[:, None] == ci[None, :]
    earlier = ci[None, :] < ci[:, None]
    expo = jnp.where(same, jnp.abs(diff), diff)
    Mt = jnp.where((same | earlier)[None], jnp.exp(log_g[:, None, None] * expo[None]), 0.0)
    ones = jnp.ones((1, 1, HD), F32)
    qd = jnp.exp(log_g[:, None] * (idx + 1.0)[None, :])[:, :, None] * ones
    kd = jnp.exp(log_g[:, None] * (SC - 1.0 - idx)[None, :])[:, :, None] * ones
    gs = jnp.exp(log_g * SC)[:, None, None] * jnp.ones((1, 8, HD), F32)
    return cosT, sinT, Mt, qd, kd, gs


def _pad_rows(a, rows):
    return jnp.pad(a, ((0, rows - a.shape[0]), (0, 0)))


def kernel(x, norm1_g, w_in, conv_w, conv_b, conv_ln_g, conv_ln_b, ret_gn_g, w_out, norm2_g, w_gate, w_up, w_down, final_g, loss_target, m_norm1_g, m_w_in, m_conv_w, m_conv_b, m_conv_ln_g, m_conv_ln_b, m_ret_gn_g, m_w_out, m_norm2_g, m_w_gate, m_w_up, m_w_down, m_final_g, v_norm1_g, v_w_in, v_conv_w, v_conv_b, v_conv_ln_g, v_conv_ln_b, v_ret_gn_g, v_w_out, v_norm2_g, v_w_gate, v_w_up, v_w_down, v_final_g):
    L, D, n_in = w_in.shape
    n_out = w_out.shape[1]
    n_ff = w_gate.shape[2]
    S = x.shape[1]
    CW = conv_b.shape[1]
    IN, FF = N_DEV * n_in, N_DEV * n_ff
    ncw = conv_w.shape[2]
    x0 = x.reshape(S, D)
    tgt = loss_target.reshape(S, D)
    TM = min(512, S)
    TKW = min(2048, S)
    TMI = min(512, S)
    TMM = min(256, S)

    assert n_out <= n_ff
    wparts = (0, 1, 2, 0)
    pack_a = jnp.swapaxes(w_in, 1, 2).astype(CDT)
    pack_b = jnp.concatenate([w_out, jnp.zeros((L, n_ff - n_out, D), F32), jnp.swapaxes(w_gate, 1, 2),
                              jnp.swapaxes(w_up, 1, 2)], axis=1).astype(CDT)
    pack_c = w_down.astype(CDT)
    per_dev = lambda g: g.reshape(N_DEV, -1, D)
    Ga = per_dev(_comm_only(_ag_comm([pack_a[0]]), "ag_first")[0])
    Gb = Gc = None

    cwp = conv_w.reshape(L * CONV_K * ncw // 128, 128)
    cw_rows = -(-cwp.shape[0] // 8) * 8
    cwg = _gather_small(_pad_rows(cwp, cw_rows), reduce=False)[:, :cwp.shape[0], :]
    conv_w_full = jnp.moveaxis(cwg.reshape(N_DEV, L, CONV_K, ncw), 0, 2).reshape(L, CONV_K, CW)

    cosT, sinT, Mt, qd, kd, gs = _tables(S)

    saved = []
    xl = x0
    for l in range(L):
        cw = _pad_rows(conv_w_full[l], HALO)
        (h, proj, c, u), got = _f_in_conv(xl, norm1_g[l][None], Ga, n_in, cw, conv_b[l][None], conv_ln_g[l][None],
                                          conv_ln_b[l][None], TMI, _ag_comm([pack_b[0]]) if l == 0 else None)
        if got:
            Gb = per_dev(got[0])
        more = l + 1 < L
        ret_gather = ([pack_c[0]] if l == 0 else []) + ([pack_a[l + 1]] if more else [])
        (rraw, states, mixed), got = _f_ret(proj, u, cosT, sinT, Mt, qd, kd, gs, ret_gn_g[l][None],
                                            _ag_comm(ret_gather) if ret_gather else None)
        if l == 0:
            Gc = per_dev(got[0])
        (xm, h2, gate, up, act, xo), nxt = _f_mlp(
            xl, mixed, norm2_g[l][None], (Gb, Gc), n_out, n_ff, wparts, 2 * TMM,
            _ag_comm([pack_b[l + 1], pack_c[l + 1]]) if more else None)
        saved.append(dict(x=xl, h=h, proj=proj, c=c, rraw=rraw, states=states, mixed=mixed, xm=xm, h2=h2,
                          gate=gate, up=up, act=act, cw=cw, Ga=Ga, Gbc=(Gb, Gc)))
        if more:
            Ga, Gb, Gc = per_dev(got[-1]), per_dev(nxt[0]), per_dev(nxt[1])
        xl = xo

    dx, dxb, loss_p, dfg = _f_loss(xl, final_g[None], tgt, TM)

    small = []
    own = [None] * L
    recv = [None] * L
    blocks = lambda d: d.reshape(N_DEV, -1, D)
    for l in reversed(range(L)):
        sv = saved[l]
        (dgate, dup, dxm, dxmb, dmix, dg2), _ = _b_mlp(
            dxb, dx, sv["xm"], sv["gate"], sv["up"], norm2_g[l][None], sv["Gbc"], n_out, n_ff, wparts, TMM)
        d_wd, d_wd_b = _dw_tn(sv["act"], dxb, FF // 2, TKW)
        d_wgT, d_wgT_b = _dw_tn(dgate, sv["h2"], FF // 2, TKW)
        d_wuT, d_wuT_b = _dw_tn(dup, sv["h2"], FF // 2, TKW)
        d_wo, d_wo_b = _dw_tn(sv["mixed"], dxmb, D, TKW)
        (dab, dcw, dcb, dlg, dlb), r_go = _b_conv(dmix, sv["c"], sv["proj"], sv["cw"], conv_ln_g[l][None],
                                                  conv_ln_b[l][None], TM,
                                                  _rs_direct_comm([blocks(d_wgT_b), blocks(d_wo_b)]))
        (dproj, dgn), r_ud = _b_ret(dmix, dab, sv["proj"], sv["rraw"], sv["states"], cosT, sinT, Mt, qd, kd, gs,
                                    ret_gn_g[l][None], _rs_direct_comm([blocks(d_wuT_b), blocks(d_wd_b)]))
        d_winT, d_winT_b = _dw_tn(dproj, sv["h"], IN // 2, TKW)
        (dx, dxb, dg1), r_i = _b_in(dproj, sv["Ga"], n_in, sv["x"], norm1_g[l][None], dxm, 2 * TMI,
                                    _rs_direct_comm([blocks(d_winT_b)]))
        own[l] = [blocks(d) for d in (d_winT, d_wo, d_wgT, d_wuT, d_wd)]
        recv[l] = [r_i[0], r_go[1], r_go[0], r_ud[0], r_ud[1]]
        small.append(jnp.concatenate([dcw, dcb, dlg, dlb, dgn, dg1.reshape(2, CW), dg2.reshape(2, CW)], axis=0))
    small = small[::-1]
    grad_x = dx.reshape(1, S, D)

    rows_l = HALO + 8
    loss_row = jnp.zeros((1, CW), F32).at[0, 0].set(loss_p[0, 0])
    sm = jnp.concatenate(small + [dfg.reshape(2, CW), loss_row], axis=0)
    sm_rows = -(-sm.shape[0] // 8) * 8
    sm = _gather_small(_pad_rows(sm, sm_rows), reduce=True)
    loss = sm[L * rows_l + 2, 0]
    g_final = sm[L * rows_l:L * rows_l + 2].reshape(D)
    per = sm[:L * rows_l].reshape(L, rows_l, CW)
    me = 4 * lax.axis_index("x") + 2 * lax.axis_index("y") + lax.axis_index("c")
    g_conv_w = lax.dynamic_slice_in_dim(per[:, :CONV_K, :], me * ncw, ncw, axis=2)
    g_conv_b, g_ln_g, g_ln_b, g_gn = per[:, HALO], per[:, HALO + 1], per[:, HALO + 2], per[:, HALO + 3]
    g_n1 = per[:, HALO + 4:HALO + 6].reshape(L, D)
    g_n2 = per[:, HALO + 6:HALO + 8].reshape(L, D)

    gl = [_sum_all(own[l], recv[l]) for l in range(L)]
    g_w_in = jnp.stack([gl[l][0].T for l in range(L)])
    g_w_out = jnp.stack([gl[l][1] for l in range(L)])
    g_w_gate = jnp.stack([gl[l][2].T for l in range(L)])
    g_w_up = jnp.stack([gl[l][3].T for l in range(L)])
    g_w_down = jnp.stack([gl[l][4] for l in range(L)])

    def big(w, g, m, v):
        sh = w.shape
        two = lambda a: a.reshape(-1, sh[-1])
        rows = two(w).shape[0]
        d, mn, vn = _adamw(two(w), two(g), two(m), two(v), rows // 8)
        return d.reshape(sh), mn.reshape(sh), vn.reshape(sh)

    names = ["norm1_g", "conv_w", "conv_b", "conv_ln_g", "conv_ln_b", "ret_gn_g", "norm2_g", "final_g"]
    sw = dict(norm1_g=(norm1_g, g_n1, m_norm1_g, v_norm1_g), conv_w=(conv_w, g_conv_w, m_conv_w, v_conv_w),
              conv_b=(conv_b, g_conv_b, m_conv_b, v_conv_b), conv_ln_g=(conv_ln_g, g_ln_g, m_conv_ln_g, v_conv_ln_g),
              conv_ln_b=(conv_ln_b, g_ln_b, m_conv_ln_b, v_conv_ln_b), ret_gn_g=(ret_gn_g, g_gn, m_ret_gn_g, v_ret_gn_g),
              norm2_g=(norm2_g, g_n2, m_norm2_g, v_norm2_g), final_g=(final_g, g_final, m_final_g, v_final_g))
    lens = [int(math.prod(sw[n][0].shape)) for n in names]
    tot = sum(lens)
    prow = -(-tot // (8 * CW)) * 8

    def packs(j):
        flat = jnp.concatenate([sw[n][j].reshape(-1) for n in names])
        return jnp.pad(flat, (0, prow * CW - tot)).reshape(prow, CW)

    sd, smn, svn = _adamw(packs(0), packs(1), packs(2), packs(3), prow)

    def unpack(a):
        flat = a.reshape(-1)
        out, o = {}, 0
        for n, ln in zip(names, lens):
            out[n] = flat[o:o + ln].reshape(sw[n][0].shape)
            o += ln
        return out

    sd, smn, svn = unpack(sd), unpack(smn), unpack(svn)
    res = {n: (sw[n][1], sd[n], smn[n], svn[n]) for n in names}
    res["w_in"] = (g_w_in,) + big(w_in, g_w_in, m_w_in, v_w_in)
    res["w_out"] = (g_w_out,) + big(w_out, g_w_out, m_w_out, v_w_out)
    res["w_gate"] = (g_w_gate,) + big(w_gate, g_w_gate, m_w_gate, v_w_gate)
    res["w_up"] = (g_w_up,) + big(w_up, g_w_up, m_w_up, v_w_up)
    res["w_down"] = (g_w_down,) + big(w_down, g_w_down, m_w_down, v_w_down)

    order = ["norm1_g", "w_in", "conv_w", "conv_b", "conv_ln_g", "conv_ln_b", "ret_gn_g", "w_out", "norm2_g",
             "w_gate", "w_up", "w_down", "final_g"]
    return (loss, grad_x, *[res[n][0] for n in order], *[res[n][1] for n in order],
            *[res[n][2] for n in order], *[res[n][3] for n in order])
```
